```python
import jax, jax.numpy as jnp
from jax import lax
import numpy as np

D_MODEL = 1024
BATCH = 16
SEQ = 2048
DEPTH = 1

CHUNK = 64
Q_BLOCK = 128
EPS = 1e-6

CONV_WIDTH = D_MODEL
CONV_K = 3

N_HEADS = D_MODEL // 128
QK_NOPE = 128
QK_ROPE = 64
QK_HEAD = QK_NOPE + QK_ROPE
V_HEAD = D_MODEL // N_HEADS
Q_LORA = (3 * D_MODEL) // 8
KV_LORA = D_MODEL // 4
ROPE_BASE = 10000.0

COL_SIZES = (CONV_WIDTH, CONV_WIDTH, CONV_WIDTH, Q_LORA, KV_LORA, QK_ROPE, D_MODEL, D_MODEL)
D_IN_ALL = sum(COL_SIZES)
COL_SPLITS = tuple(int(v) for v in np.cumsum(COL_SIZES)[:-1])

N_GROUPS = 4
EXPERTS_PER_GROUP = 8
N_EXPERTS = N_GROUPS * EXPERTS_PER_GROUP
TOP_K = 2
D_EXPERT = D_MODEL // 4

kernel_name = 'hybrid_conv_mla_hmoe_block'


def rms_norm(x, g):
    xf = x.astype(jnp.float32)
    y = xf * lax.rsqrt(jnp.mean(xf * xf, axis=-1, keepdims=True) + EPS)
    return (y * g.astype(jnp.float32)).astype(x.dtype)


def rope_tables(positions):
    inv = ROPE_BASE ** (-jnp.arange(0, QK_ROPE, 2, dtype=jnp.float32) / QK_ROPE)
    ang = positions.astype(jnp.float32)[..., None] * inv
    return jnp.cos(ang)[:, :, None, :], jnp.sin(ang)[:, :, None, :]


def apply_rope(x, cos, sin):
    xf = x.astype(jnp.float32)
    x1, x2 = xf[..., :QK_ROPE // 2], xf[..., QK_ROPE // 2:]
    out = jnp.concatenate([x1 * cos - x2 * sin, x2 * cos + x1 * sin], axis=-1)
    return out.astype(x.dtype)


def short_causal_conv(u, w):
    S = u.shape[1]
    up = jnp.pad(u, ((0, 0), (CONV_K - 1, 0), (0, 0)))
    y = up[:, 0:S] * w[0]
    for k in range(1, CONV_K):
        y = y + up[:, k:k + S] * w[k]
    return y


def chunk_causal_attention(q, k, v):
    S = q.shape[1]
    scale = QK_HEAD ** -0.5
    outs = []
    for i in range(S // Q_BLOCK):
        q0 = i * Q_BLOCK
        kend = q0 + Q_BLOCK
        qb = q[:, q0:kend]
        kb = k[:, :kend]
        vb = v[:, :kend]
        s = jnp.einsum('bqhd,bkhd->bhqk', qb, kb).astype(jnp.float32) * scale
        q_chunk = (q0 + jnp.arange(Q_BLOCK)) // CHUNK
        k_chunk = jnp.arange(kend) // CHUNK
        mask = k_chunk[None, :] <= q_chunk[:, None]
        s = jnp.where(mask, s, jnp.finfo(jnp.float32).min)
        p = jax.nn.softmax(s, axis=-1).astype(v.dtype)
        outs.append(jnp.einsum('bhqk,bkhd->bqhd', p, vb))
    return jnp.concatenate(outs, axis=1)


def mla_branch(q_lat, kv_lat, k_rope_raw, cos, sin, q_a_norm_g, w_q_b, kv_a_norm_g, w_kv_b, q_norm_g, k_norm_g):
    B, S = q_lat.shape[0], q_lat.shape[1]
    q = (rms_norm(q_lat, q_a_norm_g) @ w_q_b).reshape(B, S, N_HEADS, QK_HEAD)
    q = rms_norm(q, q_norm_g)
    q = jnp.concatenate([q[..., :QK_NOPE], apply_rope(q[..., QK_NOPE:], cos, sin)], axis=-1)
    kv = (rms_norm(kv_lat, kv_a_norm_g) @ w_kv_b).reshape(B, S, N_HEADS, QK_NOPE + V_HEAD)
    k_nope, v = kv[..., :QK_NOPE], kv[..., QK_NOPE:]
    k_rope = jnp.broadcast_to(k_rope_raw[:, :, None, :], (B, S, N_HEADS, QK_ROPE))
    k = rms_norm(jnp.concatenate([k_nope, k_rope], axis=-1), k_norm_g)
    k = jnp.concatenate([k[..., :QK_NOPE], apply_rope(k[..., QK_NOPE:], cos, sin)], axis=-1)
    o = chunk_causal_attention(q, k, v)
    return o.reshape(B, S, N_HEADS * V_HEAD)


def hierarchical_moe(h, w_rg, b_rg, w_re, b_re, w_gate, w_up, w_down):
    B, S = h.shape[0], h.shape[1]
    g_logits = (h @ w_rg).astype(jnp.float32) + b_rg.astype(jnp.float32)
    g_prob = jax.nn.softmax(g_logits, axis=-1)
    g_onehot = jax.nn.one_hot(jnp.argmax(g_logits, axis=-1), N_GROUPS, dtype=jnp.float32)
    p_group = jnp.sum(g_prob * g_onehot, axis=-1)
    e_logits = ((h @ w_re).astype(jnp.float32) + b_re.astype(jnp.float32)).reshape(B, S, N_GROUPS, EXPERTS_PER_GROUP)
    e_sel = jnp.sum(e_logits * g_onehot[..., None], axis=-2)
    e_prob = jax.nn.softmax(e_sel, axis=-1)
    top_v, top_i = lax.top_k(e_prob, TOP_K)
    top_v = top_v / jnp.sum(top_v, axis=-1, keepdims=True)
    w_local = jnp.sum(jax.nn.one_hot(top_i, EXPERTS_PER_GROUP, dtype=jnp.float32) * top_v[..., None], axis=-2)
    comb = (g_onehot[..., :, None] * w_local[..., None, :] * p_group[..., None, None]).astype(h.dtype)
    out = jnp.zeros_like(h)
    for g in range(N_GROUPS):
        a = jax.nn.silu(jnp.einsum('bsd,edf->bsef', h, w_gate[g])) * jnp.einsum('bsd,edf->bsef', h, w_up[g])
        a = a * comb[:, :, g, :, None]
        out = out + jnp.einsum('bsef,efd->bsd', a, w_down[g])
    return out


def setup_inputs(seed: int = 0) -> dict:
    key = jax.random.key(seed)
    ks = jax.random.split(key, 24)
    f32 = jnp.float32
    nrm = lambda k, shape, s: jax.random.normal(k, shape, f32) * s
    L = DEPTH
    x = jax.random.normal(ks[0], (BATCH, SEQ, D_MODEL), f32)
    c = jax.random.normal(ks[1], (BATCH, D_MODEL), f32)
    offsets = jax.random.randint(ks[2], (BATCH, 1), 0, 4096, dtype=jnp.int32)
    positions = (offsets + jnp.arange(SEQ, dtype=jnp.int32)[None, :]).astype(jnp.int32)
    return {
        'x': x,
        'c': c,
        'positions': positions,
        'w_ada': nrm(ks[3], (L, D_MODEL, 6 * D_MODEL), D_MODEL ** -0.5),
        'b_ada': nrm(ks[4], (L, 6 * D_MODEL), 0.01),
        'norm1_g': 1.0 + nrm(ks[5], (L, D_MODEL), 0.02),
        'w_in': nrm(ks[6], (L, D_MODEL, D_IN_ALL), D_MODEL ** -0.5),
        'conv_w': nrm(ks[7], (L, CONV_K, CONV_WIDTH), CONV_K ** -0.5),
        'q_a_norm_g': 1.0 + nrm(ks[8], (L, Q_LORA), 0.02),
        'w_q_b': nrm(ks[9], (L, Q_LORA, N_HEADS * QK_HEAD), Q_LORA ** -0.5),
        'kv_a_norm_g': 1.0 + nrm(ks[10], (L, KV_LORA), 0.02),
        'w_kv_b': nrm(ks[11], (L, KV_LORA, N_HEADS * (QK_NOPE + V_HEAD)), KV_LORA ** -0.5),
        'q_norm_g': 1.0 + nrm(ks[12], (L, QK_HEAD), 0.02),
        'k_norm_g': 1.0 + nrm(ks[13], (L, QK_HEAD), 0.02),
        'w_o': nrm(ks[14], (L, D_MODEL, D_MODEL), D_MODEL ** -0.5),
        'norm2_g': 1.0 + nrm(ks[15], (L, D_MODEL), 0.02),
        'w_router_group': nrm(ks[16], (L, D_MODEL, N_GROUPS), D_MODEL ** -0.5),
        'b_router_group': nrm(ks[17], (L, N_GROUPS), 0.01),
        'w_router_expert': nrm(ks[18], (L, D_MODEL, N_EXPERTS), D_MODEL ** -0.5),
        'b_router_expert': nrm(ks[19], (L, N_EXPERTS), 0.01),
        'w_exp_gate': nrm(ks[20], (L, N_GROUPS, EXPERTS_PER_GROUP, D_MODEL, D_EXPERT), D_MODEL ** -0.5),
        'w_exp_up': nrm(ks[21], (L, N_GROUPS, EXPERTS_PER_GROUP, D_MODEL, D_EXPERT), D_MODEL ** -0.5),
        'w_exp_down': nrm(ks[22], (L, N_GROUPS, EXPERTS_PER_GROUP, D_EXPERT, D_MODEL), D_EXPERT ** -0.5),
    }


def reference(x, c, positions, w_ada, b_ada, norm1_g, w_in, conv_w, q_a_norm_g, w_q_b, kv_a_norm_g, w_kv_b,
              q_norm_g, k_norm_g, w_o, norm2_g, w_router_group, b_router_group, w_router_expert, b_router_expert,
              w_exp_gate, w_exp_up, w_exp_down):
    cos, sin = rope_tables(positions)
    c_act = jax.nn.silu(c)
    for l in range(DEPTH):
        mod = (c_act @ w_ada[l] + b_ada[l])[:, None, :]
        shift1, scale1, gate1, shift2, scale2, gate2 = jnp.split(mod, 6, axis=-1)

        h = rms_norm(x, norm1_g[l]) * (1.0 + scale1) + shift1
        z = h @ w_in[l]
        zx, zb, zc, q_lat, kv_lat, k_rope_raw, zg_conv, zg_mla = jnp.split(z, COL_SPLITS, axis=-1)
        y_conv = zb * short_causal_conv(zc * zx, conv_w[l])
        y_mla = mla_branch(q_lat, kv_lat, k_rope_raw, cos, sin, q_a_norm_g[l], w_q_b[l], kv_a_norm_g[l],
                           w_kv_b[l], q_norm_g[l], k_norm_g[l])
        merged = jax.nn.sigmoid(zg_conv) * y_conv + jax.nn.sigmoid(zg_mla) * y_mla
        x = x + gate1 * (merged @ w_o[l])

        h2 = rms_norm(x, norm2_g[l]) * (1.0 + scale2) + shift2
        y_moe = hierarchical_moe(h2, w_router_group[l], b_router_group[l], w_router_expert[l], b_router_expert[l],
                                 w_exp_gate[l], w_exp_up[l], w_exp_down[l])
        x = x + gate2 * y_moe
    return x
```

```python
import functools
import math

import jax
import jax.numpy as jnp
from jax import lax
from jax.experimental import pallas as pl
from jax.experimental.pallas import tpu as pltpu

F32 = jnp.float32
BF16 = jnp.bfloat16

D_MODEL = 1024
N_HEADS = 8
QK_NOPE = 128
QK_ROPE = 64
QK_HEAD = QK_NOPE + QK_ROPE
V_HEAD = 128
Q_LORA = 384
KV_LORA = 256
CHUNK = 64
EPS = 1e-6
ROPE_BASE = 10000.0
N_GROUPS = 4
EXPERTS_PER_GROUP = 8
N_EXPERTS = N_GROUPS * EXPERTS_PER_GROUP
D_EXPERT = 256
CONV_K = 3

LANES = 128
SUBLANES = 8
VMEM_LIMIT = 56 * 1024 * 1024

ROW_TILE = 256
Q_TILE = 256
MOE_TILE = 256
ADA_COLS = 1536
ROWS_PER_TOKEN = D_MODEL // LANES
DMA_UNROLL = 8
ROUTER_ROWS = 40


def _sigmoid(v):
    return 1.0 / (1.0 + jnp.exp(-v))


def _dot(a, b):
    return jnp.dot(a, b, preferred_element_type=F32)


def _dot_nt(a, b):
    return lax.dot_general(a, b, (((1,), (1,)), ((), ())), preferred_element_type=F32)


def _ada_kernel(c_ref, w_ref, b_ref, o_ref):
    c = c_ref[...]
    act = (c * _sigmoid(c)).astype(BF16)
    o_ref[...] = _dot(act, w_ref[...].astype(BF16)) + b_ref[...]


def _ada(c, w_ada, b_ada):
    nb, d = c.shape
    n = w_ada.shape[1]
    return pl.pallas_call(
        _ada_kernel,
        grid=(n // ADA_COLS,),
        in_specs=[
            pl.BlockSpec((nb, d), lambda j: (0, 0)),
            pl.BlockSpec((d, ADA_COLS), lambda j: (0, j)),
            pl.BlockSpec((1, ADA_COLS), lambda j: (0, j)),
        ],
        out_specs=pl.BlockSpec((nb, ADA_COLS), lambda j: (0, j)),
        out_shape=jax.ShapeDtypeStruct((nb, n), F32),
        compiler_params=pltpu.CompilerParams(
            dimension_semantics=("arbitrary",), vmem_limit_bytes=VMEM_LIMIT),
        name="ada",
    )(c, w_ada, b_ada.reshape(1, n))


_C_ZX = 0
_C_ZB = D_MODEL
_C_ZC = 2 * D_MODEL
_C_GC = 3 * D_MODEL
_C_GM = 4 * D_MODEL
_C_QL = 5 * D_MODEL
_C_KV = _C_QL + Q_LORA
_C_KR = _C_KV + KV_LORA
_C_END = _C_KR + 2 * QK_ROPE


def _inproj_kernel(tiles_per_seq, x_ref, mod_ref, g1_ref, w_ref, cw_ref, gq_ref, gkv_ref,
                   conv_ref, gm_ref, qn_ref, kvn_ref, kr_ref, ubuf):
    tm = x_ref.shape[0]

    @pl.when(pl.program_id(0) % tiles_per_seq == 0)
    def _():
        ubuf[0:SUBLANES, :] = jnp.zeros((SUBLANES, D_MODEL), F32)

    x = x_ref[...]
    xn = x * lax.rsqrt(jnp.mean(x * x, axis=-1, keepdims=True) + EPS) * g1_ref[...]
    shift = mod_ref[0, :, 0:D_MODEL]
    scale = mod_ref[0, :, D_MODEL:2 * D_MODEL]
    h = (xn * (1.0 + scale) + shift).astype(BF16)

    def proj(lo, hi):
        return _dot(h, w_ref[:, lo:hi])

    u = proj(_C_ZC, _C_GC) * proj(_C_ZX, _C_ZB)
    ubuf[SUBLANES:SUBLANES + tm, :] = u
    conv = (ubuf[SUBLANES - 2:SUBLANES - 2 + tm, :] * cw_ref[0:1, :]
            + ubuf[SUBLANES - 1:SUBLANES - 1 + tm, :] * cw_ref[1:2, :]
            + u * cw_ref[2:3, :])
    ubuf[0:SUBLANES, :] = ubuf[tm:tm + SUBLANES, :]
    y_conv = proj(_C_ZB, _C_ZC) * conv
    conv_ref[...] = (_sigmoid(proj(_C_GC, _C_GM)) * y_conv).astype(BF16)
    gm_ref[...] = _sigmoid(proj(_C_GM, _C_QL)).astype(BF16)

    ql = proj(_C_QL, _C_KV)
    qn_ref[...] = (ql * lax.rsqrt(jnp.mean(ql * ql, axis=-1, keepdims=True) + EPS)
                   * gq_ref[...]).astype(BF16)
    kl = proj(_C_KV, _C_KR)
    kvn_ref[...] = (kl * lax.rsqrt(jnp.mean(kl * kl, axis=-1, keepdims=True) + EPS)
                    * gkv_ref[...]).astype(BF16)
    kr_ref[...] = proj(_C_KR, _C_END)


def _inproj(x2, mod3, g1, w_cat, conv_w, gq, gkv, seq):
    t = x2.shape[0]
    tm = ROW_TILE
    tiles_per_seq = seq // tm
    row = lambda i: (i, 0)
    const = lambda i: (0, 0)
    return pl.pallas_call(
        functools.partial(_inproj_kernel, tiles_per_seq),
        grid=(t // tm,),
        in_specs=[
            pl.BlockSpec((tm, D_MODEL), row),
            pl.BlockSpec((1, 1, 6 * D_MODEL), lambda i: (i // tiles_per_seq, 0, 0)),
            pl.BlockSpec((1, D_MODEL), const),
            pl.BlockSpec((D_MODEL, _C_END), const),
            pl.BlockSpec((CONV_K, D_MODEL), const),
            pl.BlockSpec((1, Q_LORA), const),
            pl.BlockSpec((1, KV_LORA), const),
        ],
        out_specs=[
            pl.BlockSpec((tm, D_MODEL), row),
            pl.BlockSpec((tm, D_MODEL), row),
            pl.BlockSpec((tm, Q_LORA), row),
            pl.BlockSpec((tm, KV_LORA), row),
            pl.BlockSpec((tm, 2 * QK_ROPE), row),
        ],
        out_shape=[
            jax.ShapeDtypeStruct((t, D_MODEL), BF16),
            jax.ShapeDtypeStruct((t, D_MODEL), BF16),
            jax.ShapeDtypeStruct((t, Q_LORA), BF16),
            jax.ShapeDtypeStruct((t, KV_LORA), BF16),
            jax.ShapeDtypeStruct((t, 2 * QK_ROPE), F32),
        ],
        scratch_shapes=[pltpu.VMEM((tm + SUBLANES, D_MODEL), F32)],
        compiler_params=pltpu.CompilerParams(
            dimension_semantics=("arbitrary",), vmem_limit_bytes=VMEM_LIMIT),
        name="inproj",
    )(x2, mod3, g1, w_cat, conv_w, gq, gkv)


def _attn_kernel(qn_ref, kvn_ref, kr_ref, cs_ref, wq_ref, wkv_ref, gq_ref, gk_ref,
                 o_ref, q_s, k_s, v_s):
    seq = qn_ref.shape[0]
    lane = lax.broadcasted_iota(jnp.int32, (seq, LANES), 1)
    low_half = lane < QK_ROPE
    cs = cs_ref[...]

    def rope_pair(blk, gain):
        prod = blk * (cs * gain)
        return prod + pltpu.roll(prod, QK_ROPE, 1)

    def inv_rms(nope, rope_blk):
        ss = (jnp.sum(nope * nope, axis=-1, keepdims=True)
              + jnp.sum(jnp.where(low_half, rope_blk * rope_blk, 0.0), axis=-1, keepdims=True))
        return lax.rsqrt(ss * (1.0 / QK_HEAD) + EPS)

    qf = _dot(qn_ref[...], wq_ref[0])
    q_nope = qf[:, 0:QK_NOPE]
    q_rblk = qf[:, QK_NOPE:]
    q_scale = inv_rms(q_nope, q_rblk) * (QK_HEAD ** -0.5 * math.log2(math.e))
    q_s[:, 0:QK_NOPE] = (q_nope * gq_ref[:, 0:QK_NOPE] * q_scale).astype(BF16)
    q_s[:, QK_NOPE:] = (rope_pair(q_rblk, gq_ref[:, QK_NOPE:]) * q_scale).astype(BF16)

    kvf = _dot(kvn_ref[...], wkv_ref[0])
    k_nope = kvf[:, 0:QK_NOPE]
    k_rblk = kr_ref[...]
    k_scale = inv_rms(k_nope, k_rblk)
    k_s[:, 0:QK_NOPE] = (k_nope * gk_ref[:, 0:QK_NOPE] * k_scale).astype(BF16)
    k_rope = jnp.where(low_half, rope_pair(k_rblk, gk_ref[:, QK_NOPE:]), 0.0)
    k_s[:, QK_NOPE:] = (k_rope * k_scale).astype(BF16)
    v_s[:, 0:V_HEAD] = kvf[:, QK_NOPE:].astype(BF16)
    v_s[:, V_HEAD:] = jnp.ones((seq, V_HEAD), BF16)

    tq = Q_TILE
    rchunk = lax.broadcasted_iota(jnp.int32, (tq, tq), 0) // CHUNK
    cchunk = lax.broadcasted_iota(jnp.int32, (tq, tq), 1) // CHUNK
    diag_ok = cchunk <= rchunk
    neg = jnp.finfo(F32).min
    for i in range(seq // tq):
        q0 = i * tq
        q = q_s[q0:q0 + tq, :]
        s_d = jnp.where(diag_ok, _dot_nt(q, k_s[q0:q0 + tq, :]), neg)
        m = jnp.max(s_d, axis=-1, keepdims=True)
        if i > 0:
            s_p = _dot_nt(q, k_s[0:q0, :])
            m = jnp.maximum(m, jnp.max(s_p, axis=-1, keepdims=True))
        acc = _dot(jnp.exp2(s_d - m).astype(BF16), v_s[q0:q0 + tq, :])
        if i > 0:
            acc = acc + _dot(jnp.exp2(s_p - m).astype(BF16), v_s[0:q0, :])
        o_ref[q0:q0 + tq, :] = (acc[:, 0:V_HEAD] / acc[:, V_HEAD:]).astype(o_ref.dtype)


def _attn(qn, kvn, kr, cs, wq, wkv, gq, gk, nb, seq):
    t = qn.shape[0]
    per_b = lambda b, h: (b, 0)
    per_h = lambda b, h: (h, 0, 0)
    const = lambda b, h: (0, 0)
    return pl.pallas_call(
        _attn_kernel,
        grid=(nb, N_HEADS),
        in_specs=[
            pl.BlockSpec((seq, Q_LORA), per_b),
            pl.BlockSpec((seq, KV_LORA), per_b),
            pl.BlockSpec((seq, 2 * QK_ROPE), per_b),
            pl.BlockSpec((seq, 2 * QK_ROPE), per_b),
            pl.BlockSpec((1, Q_LORA, QK_NOPE + 2 * QK_ROPE), per_h),
            pl.BlockSpec((1, KV_LORA, QK_NOPE + V_HEAD), per_h),
            pl.BlockSpec((1, QK_NOPE + 2 * QK_ROPE), const),
            pl.BlockSpec((1, QK_NOPE + 2 * QK_ROPE), const),
        ],
        out_specs=pl.BlockSpec((seq, V_HEAD), lambda b, h: (b, h)),
        out_shape=jax.ShapeDtypeStruct((t, N_HEADS * V_HEAD), BF16),
        scratch_shapes=[
            pltpu.VMEM((seq, QK_NOPE + 2 * QK_ROPE), BF16),
            pltpu.VMEM((seq, QK_NOPE + 2 * QK_ROPE), BF16),
            pltpu.VMEM((seq, 2 * V_HEAD), BF16),
        ],
        compiler_params=pltpu.CompilerParams(
            dimension_semantics=("arbitrary", "arbitrary"), vmem_limit_bytes=VMEM_LIMIT),
        name="attn",
    )(qn, kvn, kr, cs, wq, wkv, gq, gk)


def _oproj_kernel(conv_ref, gm_ref, y_ref, x_ref, mod_ref, wo_ref, g2_ref, wr_ref, br_ref,
                  x1_ref, h2_ref, eid_ref, cwt_ref):
    tm = x_ref.shape[0]
    merged = conv_ref[...].astype(F32) + gm_ref[...].astype(F32) * y_ref[...].astype(F32)
    att = _dot(merged.astype(BF16), wo_ref[...])
    gate1 = mod_ref[0, :, 2 * D_MODEL:3 * D_MODEL]
    shift2 = mod_ref[0, :, 3 * D_MODEL:4 * D_MODEL]
    scale2 = mod_ref[0, :, 4 * D_MODEL:5 * D_MODEL]
    x1 = x_ref[...] + gate1 * att
    x1_ref[...] = x1
    xn = x1 * lax.rsqrt(jnp.mean(x1 * x1, axis=-1, keepdims=True) + EPS) * g2_ref[...]
    h2 = xn * (1.0 + scale2) + shift2
    for j in range(ROWS_PER_TOKEN):
        h2_ref[pl.ds(j, tm, stride=ROWS_PER_TOKEN), :] = h2[:, j * LANES:(j + 1) * LANES]

    lt = _dot_nt(wr_ref[...], h2.astype(BF16)) + br_ref[...]
    gl = [lt[N_EXPERTS + r:N_EXPERTS + r + 1, :] for r in range(N_GROUPS)]
    gmax = jnp.maximum(jnp.maximum(gl[0], gl[1]), jnp.maximum(gl[2], gl[3]))
    gidx = jnp.full(gmax.shape, N_GROUPS - 1, jnp.int32)
    for r in range(N_GROUPS - 2, -1, -1):
        gidx = jnp.where(gl[r] == gmax, r, gidx)
    gsum = jnp.exp(gl[0] - gmax)
    for r in range(1, N_GROUPS):
        gsum = gsum + jnp.exp(gl[r] - gmax)
    p_group = 1.0 / gsum
    es = lt[(N_GROUPS - 1) * EXPERTS_PER_GROUP:N_GROUPS * EXPERTS_PER_GROUP, :]
    for r in range(N_GROUPS - 2, -1, -1):
        es = jnp.where(gidx == r, lt[r * EXPERTS_PER_GROUP:(r + 1) * EXPERTS_PER_GROUP, :], es)
    row = lax.broadcasted_iota(jnp.int32, es.shape, 0)
    m1 = jnp.max(es, axis=0, keepdims=True)
    i1 = jnp.min(jnp.where(es == m1, row, EXPERTS_PER_GROUP), axis=0, keepdims=True)
    es2 = jnp.where(row == i1, -jnp.inf, es)
    m2 = jnp.max(es2, axis=0, keepdims=True)
    i2 = jnp.min(jnp.where(es2 == m2, row, EXPERTS_PER_GROUP), axis=0, keepdims=True)
    e2 = jnp.exp(m2 - m1)
    w1 = p_group / (1.0 + e2)
    w2 = w1 * e2
    eid_ref[0:1, :] = gidx * EXPERTS_PER_GROUP + i1
    eid_ref[1:2, :] = gidx * EXPERTS_PER_GROUP + i2
    wrow = lax.broadcasted_iota(jnp.int32, (LANES, tm), 0)
    wmat = jnp.where(wrow == 0, w1, jnp.where(wrow == 1, w2, 0.0))
    cwt_ref[...] = wmat.T


def _oproj(conv_p, gm, y_mla, x2, mod3, wo, g2, wr_t, br, seq):
    t = x2.shape[0]
    tm = ROW_TILE
    tiles_per_seq = seq // tm
    row = lambda i: (i, 0)
    const = lambda i: (0, 0)
    return pl.pallas_call(
        _oproj_kernel,
        grid=(t // tm,),
        in_specs=[
            pl.BlockSpec((tm, D_MODEL), row),
            pl.BlockSpec((tm, D_MODEL), row),
            pl.BlockSpec((tm, D_MODEL), row),
            pl.BlockSpec((tm, D_MODEL), row),
            pl.BlockSpec((1, 1, 6 * D_MODEL), lambda i: (i // tiles_per_seq, 0, 0)),
            pl.BlockSpec((D_MODEL, D_MODEL), const),
            pl.BlockSpec((1, D_MODEL), const),
            pl.BlockSpec((ROUTER_ROWS, D_MODEL), const),
            pl.BlockSpec((ROUTER_ROWS, 1), const),
        ],
        out_specs=[
            pl.BlockSpec((tm, D_MODEL), row),
            pl.BlockSpec((tm * ROWS_PER_TOKEN, LANES), row),
            pl.BlockSpec((2, tm), lambda i: (0, i)),
            pl.BlockSpec((tm, LANES), row),
        ],
        out_shape=[
            jax.ShapeDtypeStruct((t, D_MODEL), F32),
            jax.ShapeDtypeStruct((t * ROWS_PER_TOKEN, LANES), F32),
            jax.ShapeDtypeStruct((2, t), jnp.int32),
            jax.ShapeDtypeStruct((t, LANES), F32),
        ],
        compiler_params=pltpu.CompilerParams(
            dimension_semantics=("arbitrary",), vmem_limit_bytes=VMEM_LIMIT),
        name="oproj",
    )(conv_p, gm, y_mla, x2, mod3, wo, g2, wr_t, br)


def _moe_kernel(dump_row, te_ref, na_ref, src0_ref, srcn_ref, dst_ref,
                h2_hbm, wg_ref, wu_ref, wd_ref, y_hbm,
                xbuf, ybuf, wgu_s, wd_s, gsem, ssem):
    i = pl.program_id(0)
    n_active = na_ref[0]
    slot = i % 2
    tile_rows = MOE_TILE * ROWS_PER_TOKEN

    def gather(idx_ref, dst_slot):
        def body(c, carry):
            for k in range(DMA_UNROLL):
                r = c * DMA_UNROLL + k
                src_row = pl.multiple_of(idx_ref[0, 0, r] * ROWS_PER_TOKEN, ROWS_PER_TOKEN)
                dst_row = pl.multiple_of(r * ROWS_PER_TOKEN, ROWS_PER_TOKEN)
                pltpu.make_async_copy(
                    h2_hbm.at[pl.ds(src_row, ROWS_PER_TOKEN), :],
                    xbuf.at[dst_slot, pl.ds(dst_row, ROWS_PER_TOKEN), :],
                    gsem.at[dst_slot]).start()
            return carry
        lax.fori_loop(0, MOE_TILE // DMA_UNROLL, body, 0)

    def scatter(src_slot):
        def body(c, carry):
            for k in range(DMA_UNROLL):
                r = c * DMA_UNROLL + k
                dst_row = pl.multiple_of(dst_ref[0, 0, r] * ROWS_PER_TOKEN, ROWS_PER_TOKEN)
                src_row = pl.multiple_of(r * ROWS_PER_TOKEN, ROWS_PER_TOKEN)
                pltpu.make_async_copy(
                    ybuf.at[src_slot, pl.ds(src_row, ROWS_PER_TOKEN), :],
                    y_hbm.at[pl.ds(dst_row, ROWS_PER_TOKEN), :],
                    ssem.at[src_slot]).start()
            return carry
        lax.fori_loop(0, MOE_TILE // DMA_UNROLL, body, 0)

    def wait_gather(s):
        pltpu.make_async_copy(h2_hbm.at[pl.ds(0, tile_rows), :], xbuf.at[s], gsem.at[s]).wait()

    def wait_scatter(s):
        pltpu.make_async_copy(ybuf.at[s], y_hbm.at[pl.ds(0, tile_rows), :], ssem.at[s]).wait()

    @pl.when(i == 0)
    def _():
        ybuf[1] = jnp.zeros((tile_rows, LANES), F32)
        init = pltpu.make_async_copy(ybuf.at[1], y_hbm.at[pl.ds(dump_row, tile_rows), :], ssem.at[1])
        init.start()
        init.wait()
        gather(src0_ref, 0)

    @pl.when(i + 1 < n_active)
    def _():
        gather(srcn_ref, 1 - slot)

    new_expert = jnp.logical_or(i == 0, te_ref[i] != te_ref[jnp.maximum(i - 1, 0)])

    @pl.when(jnp.logical_and(i < n_active, new_expert))
    def _():
        wgu_s[:, 0:D_EXPERT] = wg_ref[0].astype(BF16)
        wgu_s[:, D_EXPERT:] = wu_ref[0].astype(BF16)
        wd_s[...] = wd_ref[0].astype(BF16)

    @pl.when(i < n_active)
    def _():
        wait_gather(slot)
        xs = xbuf.at[slot]
        x = jnp.concatenate(
            [xs[pl.ds(j, MOE_TILE, stride=ROWS_PER_TOKEN), :].astype(BF16)
             for j in range(ROWS_PER_TOKEN)], axis=1)
        gu = _dot(x, wgu_s[...])
        g = gu[:, 0:D_EXPERT]
        a = (g * _sigmoid(g)) * gu[:, D_EXPERT:]
        y = _dot(a.astype(BF16), wd_s[...])

        @pl.when(i > 0)
        def _():
            wait_scatter(1 - slot)

        ys = ybuf.at[slot]
        for j in range(ROWS_PER_TOKEN):
            ys[pl.ds(j, MOE_TILE, stride=ROWS_PER_TOKEN), :] = y[:, j * LANES:(j + 1) * LANES]
        scatter(slot)

        @pl.when(i == n_active - 1)
        def _():
            wait_scatter(slot)


def _moe(te, na, src, dst, h2r, wg, wu, wd, n_tok):
    nt = te.shape[0]
    tile_rows = MOE_TILE * ROWS_PER_TOKEN
    y_rows = (2 * n_tok + MOE_TILE) * ROWS_PER_TOKEN
    smem_blk = lambda f: pl.BlockSpec((1, 1, MOE_TILE), f, memory_space=pltpu.SMEM)
    wspec = lambda shape: pl.BlockSpec((1,) + shape, lambda i, te_r, na_r: (te_r[i], 0, 0))
    grid_spec = pltpu.PrefetchScalarGridSpec(
        num_scalar_prefetch=2,
        grid=(nt,),
        in_specs=[
            smem_blk(lambda i, te_r, na_r: (0, 0, 0)),
            smem_blk(lambda i, te_r, na_r: (jnp.minimum(i + 1, nt - 1), 0, 0)),
            smem_blk(lambda i, te_r, na_r: (i, 0, 0)),
            pl.BlockSpec(memory_space=pl.ANY),
            wspec((D_MODEL, D_EXPERT)),
            wspec((D_MODEL, D_EXPERT)),
            wspec((D_EXPERT, D_MODEL)),
        ],
        out_specs=pl.BlockSpec(memory_space=pl.ANY),
        scratch_shapes=[
            pltpu.VMEM((2, tile_rows, LANES), F32),
            pltpu.VMEM((2, tile_rows, LANES), F32),
            pltpu.VMEM((D_MODEL, 2 * D_EXPERT), BF16),
            pltpu.VMEM((D_EXPERT, D_MODEL), BF16),
            pltpu.SemaphoreType.DMA((2,)),
            pltpu.SemaphoreType.DMA((2,)),
        ],
    )
    return pl.pallas_call(
        functools.partial(_moe_kernel, 2 * n_tok * ROWS_PER_TOKEN),
        grid_spec=grid_spec,
        out_shape=jax.ShapeDtypeStruct((y_rows, LANES), F32),
        compiler_params=pltpu.CompilerParams(
            dimension_semantics=("arbitrary",), vmem_limit_bytes=VMEM_LIMIT),
        name="moe",
    )(te, na, src, src, dst, h2r, wg, wu, wd)


def _comb_kernel(x1_ref, y0_ref, y1_ref, cwt_ref, mod_ref, o_ref):
    tm = x1_ref.shape[0]
    c0 = cwt_ref[:, 0:1]
    c1 = cwt_ref[:, 1:2]
    for j in range(ROWS_PER_TOKEN):
        cols = slice(j * LANES, (j + 1) * LANES)
        y0 = y0_ref[pl.ds(j, tm, stride=ROWS_PER_TOKEN), :]
        y1 = y1_ref[pl.ds(j, tm, stride=ROWS_PER_TOKEN), :]
        gate2 = mod_ref[0, :, 5 * D_MODEL + j * LANES:5 * D_MODEL + (j + 1) * LANES]
        o_ref[:, cols] = x1_ref[:, cols] + gate2 * (c0 * y0 + c1 * y1)


def _comb(x1, yr, cwt, mod3, seq):
    t = x1.shape[0]
    tm = ROW_TILE
    tiles_per_seq = seq // tm
    n_row_tiles = t // tm
    row = lambda i: (i, 0)
    return pl.pallas_call(
        _comb_kernel,
        grid=(n_row_tiles,),
        in_specs=[
            pl.BlockSpec((tm, D_MODEL), row),
            pl.BlockSpec((tm * ROWS_PER_TOKEN, LANES), row),
            pl.BlockSpec((tm * ROWS_PER_TOKEN, LANES), lambda i: (n_row_tiles + i, 0)),
            pl.BlockSpec((tm, LANES), row),
            pl.BlockSpec((1, 1, 6 * D_MODEL), lambda i: (i // tiles_per_seq, 0, 0)),
        ],
        out_specs=pl.BlockSpec((tm, D_MODEL), row),
        out_shape=jax.ShapeDtypeStruct((t, D_MODEL), F32),
        compiler_params=pltpu.CompilerParams(
            dimension_semantics=("arbitrary",), vmem_limit_bytes=VMEM_LIMIT),
        name="comb",
    )(x1, yr, yr, cwt, mod3)


def _route_plan(eid, n_tok):
    n_assign = 2 * n_tok
    n_tiles = n_assign // MOE_TILE + N_EXPERTS
    e = eid.reshape(n_assign)
    order = jnp.argsort(e, stable=True).astype(jnp.int32)
    counts = jnp.sum((e[:, None] == jnp.arange(N_EXPERTS, dtype=jnp.int32)[None, :]).astype(jnp.int32),
                     axis=0)
    ntile = (counts + MOE_TILE - 1) // MOE_TILE
    tend = jnp.cumsum(ntile)
    tstart = tend - ntile
    cstart = jnp.cumsum(counts) - counts
    n_active = tend[-1]
    tj = jnp.arange(n_tiles, dtype=jnp.int32)
    te_raw = jnp.minimum(jnp.sum((tj[:, None] >= tend[None, :]).astype(jnp.int32), axis=1),
                         N_EXPERTS - 1)
    te_last = te_raw[jnp.maximum(n_active - 1, 0)]
    te = jnp.where(tj < n_active, te_raw, te_last).astype(jnp.int32)
    p = jnp.arange(n_tiles * MOE_TILE, dtype=jnp.int32)
    ptile = p // MOE_TILE
    pe = te[ptile]
    r = p - tstart[pe] * MOE_TILE
    valid = jnp.logical_and(r < counts[pe], ptile < n_active)
    a = order[jnp.clip(cstart[pe] + r, 0, n_assign - 1)]
    src = jnp.where(valid, a % n_tok, 0).astype(jnp.int32)
    dst = jnp.where(valid, a, n_assign + p % MOE_TILE).astype(jnp.int32)
    return (te, n_active.reshape(1).astype(jnp.int32),
            src.reshape(n_tiles, 1, MOE_TILE), dst.reshape(n_tiles, 1, MOE_TILE))


def _rotate_half_cols(w):
    half = QK_ROPE // 2
    return jnp.concatenate([w[..., half:], w[..., :half]], axis=-1)


def kernel(x, c, positions, w_ada, b_ada, norm1_g, w_in, conv_w, q_a_norm_g, w_q_b, kv_a_norm_g, w_kv_b, q_norm_g, k_norm_g, w_o, norm2_g, w_router_group, b_router_group, w_router_expert, b_router_expert, w_exp_gate, w_exp_up, w_exp_down):
    nb, seq, d = x.shape
    depth = w_ada.shape[0]
    n_tok = nb * seq
    assert d == D_MODEL and seq % ROW_TILE == 0 and seq % Q_TILE == 0 and Q_TILE % CHUNK == 0
    assert (2 * n_tok) % MOE_TILE == 0

    inv = ROPE_BASE ** (-jnp.arange(0, QK_ROPE, 2, dtype=F32) / QK_ROPE)
    ang = positions.astype(F32)[..., None] * inv
    cos, sin = jnp.cos(ang), jnp.sin(ang)
    cs = jnp.concatenate([cos, cos, -sin, sin], axis=-1).reshape(n_tok, 2 * QK_ROPE)

    x2 = x.reshape(n_tok, d)
    for l in range(depth):
        wi = w_in[l]
        o_q = 3 * D_MODEL
        o_kv = o_q + Q_LORA
        o_kr = o_kv + KV_LORA
        o_gc = o_kr + QK_ROPE
        o_gm = o_gc + D_MODEL
        w_kr = wi[:, o_kr:o_gc]
        w_cat = jnp.concatenate(
            [wi[:, 0:o_q], wi[:, o_gc:o_gm], wi[:, o_gm:], wi[:, o_q:o_kv], wi[:, o_kv:o_kr],
             w_kr, _rotate_half_cols(w_kr)], axis=1).astype(BF16)
        wq3 = w_q_b[l].reshape(Q_LORA, N_HEADS, QK_HEAD)
        wq = jnp.concatenate([wq3, _rotate_half_cols(wq3[..., QK_NOPE:])], axis=-1)
        wq = wq.transpose(1, 0, 2).astype(BF16)
        wkv = w_kv_b[l].reshape(KV_LORA, N_HEADS, QK_NOPE + V_HEAD).transpose(1, 0, 2).astype(BF16)
        gq = jnp.concatenate([q_norm_g[l], _rotate_half_cols(q_norm_g[l][QK_NOPE:])]).reshape(1, -1)
        gk = jnp.concatenate([k_norm_g[l], _rotate_half_cols(k_norm_g[l][QK_NOPE:])]).reshape(1, -1)
        wr_t = jnp.concatenate(
            [w_router_expert[l].T, w_router_group[l].T,
             jnp.zeros((ROUTER_ROWS - N_EXPERTS - N_GROUPS, d), F32)], axis=0).astype(BF16)
        br = jnp.concatenate(
            [b_router_expert[l], b_router_group[l],
             jnp.zeros((ROUTER_ROWS - N_EXPERTS - N_GROUPS,), F32)]).reshape(ROUTER_ROWS, 1)

        mod3 = _ada(c, w_ada[l], b_ada[l]).reshape(nb, 1, 6 * d)
        conv_p, gm, qn, kvn, kr = _inproj(
            x2, mod3, norm1_g[l].reshape(1, d), w_cat, conv_w[l],
            q_a_norm_g[l].reshape(1, -1), kv_a_norm_g[l].reshape(1, -1), seq)
        y_mla = _attn(qn, kvn, kr, cs, wq, wkv, gq, gk, nb, seq)
        x1, h2r, eid, cwt = _oproj(conv_p, gm, y_mla, x2, mod3, w_o[l].astype(BF16),
                                   norm2_g[l].reshape(1, d), wr_t, br, seq)
        te, na, src, dst = _route_plan(eid, n_tok)
        yr = _moe(te, na, src, dst, h2r,
                  w_exp_gate[l].reshape(N_EXPERTS, d, D_EXPERT),
                  w_exp_up[l].reshape(N_EXPERTS, d, D_EXPERT),
                  w_exp_down[l].reshape(N_EXPERTS, D_EXPERT, d), n_tok)
        x2 = _comb(x1, yr, cwt, mod3, seq)
    return x2.reshape(nb, seq, d)
```

```python
import functools
import math

import jax
import jax.numpy as jnp
from jax import lax
from jax.experimental import pallas as pl
from jax.experimental.pallas import tpu as pltpu

F32 = jnp.float32
BF16 = jnp.bfloat16

D_MODEL = 1024
N_HEADS = 8
QK_NOPE = 128
QK_ROPE = 64
QK_HEAD = QK_NOPE + QK_ROPE
V_HEAD = 128
Q_LORA = 384
KV_LORA = 256
CHUNK = 64
EPS = 1e-6
ROPE_BASE = 10000.0
N_GROUPS = 4
EXPERTS_PER_GROUP = 8
N_EXPERTS = N_GROUPS * EXPERTS_PER_GROUP
D_EXPERT = 256
CONV_K = 3

LANES = 128
SUBLANES = 8
VMEM_LIMIT = 56 * 1024 * 1024

ROW_TILE = 256
Q_TILE = 512
MOE_TILE = 256
ADA_COLS = 1536
ROWS_PER_TOKEN = D_MODEL // LANES
DMA_UNROLL = 8
ROUTER_ROWS = 40


def _sigmoid(v):
    return 1.0 / (1.0 + jnp.exp(-v))


def _dot(a, b):
    return jnp.dot(a, b, preferred_element_type=F32)


def _dot_nt(a, b):
    return lax.dot_general(a, b, (((1,), (1,)), ((), ())), preferred_element_type=F32)


def _ada_kernel(c_ref, w_ref, b_ref, o_ref):
    c = c_ref[...]
    act = (c * _sigmoid(c)).astype(BF16)
    o_ref[...] = _dot(act, w_ref[...].astype(BF16)) + b_ref[...]


def _ada(c, w_ada, b_ada):
    nb, d = c.shape
    n = w_ada.shape[1]
    return pl.pallas_call(
        _ada_kernel,
        grid=(n // ADA_COLS,),
        in_specs=[
            pl.BlockSpec((nb, d), lambda j: (0, 0)),
            pl.BlockSpec((d, ADA_COLS), lambda j: (0, j)),
            pl.BlockSpec((1, ADA_COLS), lambda j: (0, j)),
        ],
        out_specs=pl.BlockSpec((nb, ADA_COLS), lambda j: (0, j)),
        out_shape=jax.ShapeDtypeStruct((nb, n), F32),
        compiler_params=pltpu.CompilerParams(
            dimension_semantics=("arbitrary",), vmem_limit_bytes=VMEM_LIMIT),
        name="ada",
    )(c, w_ada, b_ada.reshape(1, n))


_C_ZX = 0
_C_ZB = D_MODEL
_C_ZC = 2 * D_MODEL
_C_GC = 3 * D_MODEL
_C_GM = 4 * D_MODEL
_C_QL = 5 * D_MODEL
_C_KV = _C_QL + Q_LORA
_C_END = _C_KV + KV_LORA


def _inproj_kernel(tiles_per_seq, x_ref, mod_ref, g1_ref, w_ref, wkr_ref, cw_ref, gq_ref, gkv_ref,
                   conv_ref, gm_ref, qn_ref, kvn_ref, krt_ref, ubuf):
    tm = x_ref.shape[0]

    @pl.when(pl.program_id(0) % tiles_per_seq == 0)
    def _():
        ubuf[0:SUBLANES, :] = jnp.zeros((SUBLANES, D_MODEL), F32)

    x = x_ref[...]
    xn = x * lax.rsqrt(jnp.mean(x * x, axis=-1, keepdims=True) + EPS) * g1_ref[...]
    shift = mod_ref[0, :, 0:D_MODEL]
    scale = mod_ref[0, :, D_MODEL:2 * D_MODEL]
    h = (xn * (1.0 + scale) + shift).astype(BF16)

    def proj(lo, hi):
        return _dot(h, w_ref[:, lo:hi])

    u = proj(_C_ZC, _C_GC) * proj(_C_ZX, _C_ZB)
    ubuf[SUBLANES:SUBLANES + tm, :] = u
    conv = (ubuf[SUBLANES - 2:SUBLANES - 2 + tm, :] * cw_ref[0:1, :]
            + ubuf[SUBLANES - 1:SUBLANES - 1 + tm, :] * cw_ref[1:2, :]
            + u * cw_ref[2:3, :])
    ubuf[0:SUBLANES, :] = ubuf[tm:tm + SUBLANES, :]
    y_conv = proj(_C_ZB, _C_ZC) * conv
    conv_ref[...] = (_sigmoid(proj(_C_GC, _C_GM)) * y_conv).astype(BF16)
    gm_ref[...] = _sigmoid(proj(_C_GM, _C_QL)).astype(BF16)

    ql = proj(_C_QL, _C_KV)
    qn_ref[...] = (ql * lax.rsqrt(jnp.mean(ql * ql, axis=-1, keepdims=True) + EPS)
                   * gq_ref[...]).astype(BF16)
    kl = proj(_C_KV, _C_END)
    kvn_ref[...] = (kl * lax.rsqrt(jnp.mean(kl * kl, axis=-1, keepdims=True) + EPS)
                    * gkv_ref[...]).astype(BF16)
    krt_ref[0] = _dot_nt(wkr_ref[...], h)


def _inproj(x2, mod3, g1, w_cat, w_krt, conv_w, gq, gkv, seq):
    t = x2.shape[0]
    nb = t // seq
    tm = ROW_TILE
    tiles_per_seq = seq // tm
    row = lambda i: (i, 0)
    const = lambda i: (0, 0)
    return pl.pallas_call(
        functools.partial(_inproj_kernel, tiles_per_seq),
        grid=(t // tm,),
        in_specs=[
            pl.BlockSpec((tm, D_MODEL), row),
            pl.BlockSpec((1, 1, 6 * D_MODEL), lambda i: (i // tiles_per_seq, 0, 0)),
            pl.BlockSpec((1, D_MODEL), const),
            pl.BlockSpec((D_MODEL, _C_END), const),
            pl.BlockSpec((2 * QK_ROPE, D_MODEL), const),
            pl.BlockSpec((CONV_K, D_MODEL), const),
            pl.BlockSpec((1, Q_LORA), const),
            pl.BlockSpec((1, KV_LORA), const),
        ],
        out_specs=[
            pl.BlockSpec((tm, D_MODEL), row),
            pl.BlockSpec((tm, D_MODEL), row),
            pl.BlockSpec((tm, Q_LORA), row),
            pl.BlockSpec((tm, KV_LORA), row),
            pl.BlockSpec((1, 2 * QK_ROPE, tm),
                         lambda i: (i // tiles_per_seq, 0, i % tiles_per_seq)),
        ],
        out_shape=[
            jax.ShapeDtypeStruct((t, D_MODEL), BF16),
            jax.ShapeDtypeStruct((t, D_MODEL), BF16),
            jax.ShapeDtypeStruct((t, Q_LORA), BF16),
            jax.ShapeDtypeStruct((t, KV_LORA), BF16),
            jax.ShapeDtypeStruct((nb, 2 * QK_ROPE, seq), F32),
        ],
        scratch_shapes=[pltpu.VMEM((tm + SUBLANES, D_MODEL), F32)],
        compiler_params=pltpu.CompilerParams(
            dimension_semantics=("arbitrary",), vmem_limit_bytes=VMEM_LIMIT),
        name="inproj",
    )(x2, mod3, g1, w_cat, w_krt, conv_w, gq, gkv)


def _attn_kernel(qn_ref, kvn_ref, krt_ref, cst_ref, wqt_ref, wkvt_ref, gq_ref, gk_ref,
                 o_ref, qt_s, k_s, vt_s, s_buf0, s_buf1):
    seq = qn_ref.shape[0]
    cos_t = cst_ref[0, 0:QK_ROPE, :]
    sin_t = cst_ref[0, QK_ROPE:, :]

    def normed_rope(nope, r, rr, g, extra_scale):
        ss = jnp.sum(nope * nope, axis=0, keepdims=True) + jnp.sum(r * r, axis=0, keepdims=True)
        scale = lax.rsqrt(ss * (1.0 / QK_HEAD) + EPS) * extra_scale
        rope = r * g[QK_NOPE:QK_HEAD] * cos_t + rr * g[QK_HEAD:] * sin_t
        return (nope * g[0:QK_NOPE] * scale).astype(BF16), (rope * scale).astype(BF16)

    qt = _dot_nt(wqt_ref[0], qn_ref[...])
    q_n, q_r = normed_rope(qt[0:QK_NOPE], qt[QK_NOPE:QK_HEAD], qt[QK_HEAD:], gq_ref[...],
                           QK_HEAD ** -0.5 * math.log2(math.e))
    qt_s[0:QK_NOPE, :] = q_n
    qt_s[QK_NOPE:QK_HEAD, :] = q_r
    qt_s[QK_HEAD:, :] = jnp.zeros((QK_ROPE, seq), BF16)

    kvt = _dot_nt(wkvt_ref[0], kvn_ref[...])
    krt = krt_ref[0]
    k_n, k_r = normed_rope(kvt[0:QK_NOPE], krt[0:QK_ROPE], krt[QK_ROPE:], gk_ref[...], 1.0)
    kt = jnp.concatenate([k_n, k_r, jnp.zeros((QK_ROPE, seq), BF16)], axis=0)
    k_s[...] = kt.T
    vt_s[0:V_HEAD, :] = kvt[QK_NOPE:].astype(BF16)
    vt_s[V_HEAD:, :] = jnp.ones((vt_s.shape[0] - V_HEAD, seq), BF16)

    tq = Q_TILE
    kchunk = lax.broadcasted_iota(jnp.int32, (tq, tq), 0) // CHUNK
    qchunk = lax.broadcasted_iota(jnp.int32, (tq, tq), 1) // CHUNK
    diag_ok = kchunk <= qchunk
    neg = jnp.finfo(F32).min
    nq = seq // tq

    def scores(i):
        q0 = i * tq
        sb = s_buf0 if i % 2 == 0 else s_buf1
        q = qt_s[:, q0:q0 + tq]
        if i > 0:
            sb[0:q0, :] = _dot(k_s[0:q0, :], q)
        sb[q0:q0 + tq, :] = jnp.where(diag_ok, _dot(k_s[q0:q0 + tq, :], q), neg)

    def finish(i):
        q0 = i * tq
        kend = q0 + tq
        sb = s_buf0 if i % 2 == 0 else s_buf1
        m = jnp.max(sb[0:kend, :], axis=0, keepdims=True)
        acc = _dot(vt_s[:, 0:kend], jnp.exp2(sb[0:kend, :] - m).astype(BF16))
        o_t = acc[0:V_HEAD] / acc[V_HEAD:V_HEAD + 1]
        o_ref[q0:q0 + tq, :] = o_t.T.astype(o_ref.dtype)

    scores(nq - 1)
    for i in range(nq - 1, -1, -1):
        if i > 0:
            scores(i - 1)
        finish(i)


def _attn(qn, kvn, krt, cst, wqt, wkvt, gq, gk, nb, seq):
    t = qn.shape[0]
    per_b = lambda b, h: (b, 0)
    per_b3 = lambda b, h: (b, 0, 0)
    per_h = lambda b, h: (h, 0, 0)
    const = lambda b, h: (0, 0)
    qk_rows = QK_NOPE + 2 * QK_ROPE
    return pl.pallas_call(
        _attn_kernel,
        grid=(nb, N_HEADS),
        in_specs=[
            pl.BlockSpec((seq, Q_LORA), per_b),
            pl.BlockSpec((seq, KV_LORA), per_b),
            pl.BlockSpec((1, 2 * QK_ROPE, seq), per_b3),
            pl.BlockSpec((1, 2 * QK_ROPE, seq), per_b3),
            pl.BlockSpec((1, qk_rows, Q_LORA), per_h),
            pl.BlockSpec((1, QK_NOPE + V_HEAD, KV_LORA), per_h),
            pl.BlockSpec((qk_rows, 1), const),
            pl.BlockSpec((qk_rows, 1), const),
        ],
        out_specs=pl.BlockSpec((seq, V_HEAD), lambda b, h: (b, h)),
        out_shape=jax.ShapeDtypeStruct((t, N_HEADS * V_HEAD), BF16),
        scratch_shapes=[
            pltpu.VMEM((qk_rows, seq), BF16),
            pltpu.VMEM((seq, qk_rows), BF16),
            pltpu.VMEM((V_HEAD + 2 * SUBLANES, seq), BF16),
            pltpu.VMEM((seq, Q_TILE), F32),
            pltpu.VMEM((seq, Q_TILE), F32),
        ],
        compiler_params=pltpu.CompilerParams(
            dimension_semantics=("arbitrary", "arbitrary"), vmem_limit_bytes=VMEM_LIMIT),
        name="attn",
    )(qn, kvn, krt, cst, wqt, wkvt, gq, gk)


def _oproj_kernel(conv_ref, gm_ref, y_ref, x_ref, mod_ref, wo_ref, g2_ref, wr_ref, br_ref,
                  x1_ref, h2_ref, eid_ref, cwt_ref):
    tm = x_ref.shape[0]
    merged = conv_ref[...].astype(F32) + gm_ref[...].astype(F32) * y_ref[...].astype(F32)
    att = _dot(merged.astype(BF16), wo_ref[...])
    gate1 = mod_ref[0, :, 2 * D_MODEL:3 * D_MODEL]
    shift2 = mod_ref[0, :, 3 * D_MODEL:4 * D_MODEL]
    scale2 = mod_ref[0, :, 4 * D_MODEL:5 * D_MODEL]
    x1 = x_ref[...] + gate1 * att
    x1_ref[...] = x1
    xn = x1 * lax.rsqrt(jnp.mean(x1 * x1, axis=-1, keepdims=True) + EPS) * g2_ref[...]
    h2 = xn * (1.0 + scale2) + shift2
    for j in range(ROWS_PER_TOKEN):
        h2_ref[pl.ds(j, tm, stride=ROWS_PER_TOKEN), :] = h2[:, j * LANES:(j + 1) * LANES]

    lt = _dot_nt(wr_ref[...], h2.astype(BF16)) + br_ref[...]
    gl = [lt[N_EXPERTS + r:N_EXPERTS + r + 1, :] for r in range(N_GROUPS)]
    gmax = jnp.maximum(jnp.maximum(gl[0], gl[1]), jnp.maximum(gl[2], gl[3]))
    gidx = jnp.full(gmax.shape, N_GROUPS - 1, jnp.int32)
    for r in range(N_GROUPS - 2, -1, -1):
        gidx = jnp.where(gl[r] == gmax, r, gidx)
    gsum = jnp.exp(gl[0] - gmax)
    for r in range(1, N_GROUPS):
        gsum = gsum + jnp.exp(gl[r] - gmax)
    p_group = 1.0 / gsum
    es = lt[(N_GROUPS - 1) * EXPERTS_PER_GROUP:N_GROUPS * EXPERTS_PER_GROUP, :]
    for r in range(N_GROUPS - 2, -1, -1):
        es = jnp.where(gidx == r, lt[r * EXPERTS_PER_GROUP:(r + 1) * EXPERTS_PER_GROUP, :], es)
    row = lax.broadcasted_iota(jnp.int32, es.shape, 0)
    m1 = jnp.max(es, axis=0, keepdims=True)
    i1 = jnp.min(jnp.where(es == m1, row, EXPERTS_PER_GROUP), axis=0, keepdims=True)
    es2 = jnp.where(row == i1, -jnp.inf, es)
    m2 = jnp.max(es2, axis=0, keepdims=True)
    i2 = jnp.min(jnp.where(es2 == m2, row, EXPERTS_PER_GROUP), axis=0, keepdims=True)
    e2 = jnp.exp(m2 - m1)
    w1 = p_group / (1.0 + e2)
    w2 = w1 * e2
    eid_ref[0:1, :] = gidx * EXPERTS_PER_GROUP + i1
    eid_ref[1:2, :] = gidx * EXPERTS_PER_GROUP + i2
    wrow = lax.broadcasted_iota(jnp.int32, (LANES, tm), 0)
    wmat = jnp.where(wrow == 0, w1, jnp.where(wrow == 1, w2, 0.0))
    cwt_ref[...] = wmat.T


def _oproj(conv_p, gm, y_mla, x2, mod3, wo, g2, wr_t, br, seq):
    t = x2.shape[0]
    tm = ROW_TILE
    tiles_per_seq = seq // tm
    row = lambda i: (i, 0)
    const = lambda i: (0, 0)
    return pl.pallas_call(
        _oproj_kernel,
        grid=(t // tm,),
        in_specs=[
            pl.BlockSpec((tm, D_MODEL), row),
            pl.BlockSpec((tm, D_MODEL), row),
            pl.BlockSpec((tm, D_MODEL), row),
            pl.BlockSpec((tm, D_MODEL), row),
            pl.BlockSpec((1, 1, 6 * D_MODEL), lambda i: (i // tiles_per_seq, 0, 0)),
            pl.BlockSpec((D_MODEL, D_MODEL), const),
            pl.BlockSpec((1, D_MODEL), const),
            pl.BlockSpec((ROUTER_ROWS, D_MODEL), const),
            pl.BlockSpec((ROUTER_ROWS, 1), const),
        ],
        out_specs=[
            pl.BlockSpec((tm, D_MODEL), row),
            pl.BlockSpec((tm * ROWS_PER_TOKEN, LANES), row),
            pl.BlockSpec((2, tm), lambda i: (0, i)),
            pl.BlockSpec((tm, LANES), row),
        ],
        out_shape=[
            jax.ShapeDtypeStruct((t, D_MODEL), F32),
            jax.ShapeDtypeStruct((t * ROWS_PER_TOKEN, LANES), F32),
            jax.ShapeDtypeStruct((2, t), jnp.int32),
            jax.ShapeDtypeStruct((t, LANES), F32),
        ],
        compiler_params=pltpu.CompilerParams(
            dimension_semantics=("arbitrary",), vmem_limit_bytes=VMEM_LIMIT),
        name="oproj",
    )(conv_p, gm, y_mla, x2, mod3, wo, g2, wr_t, br)


def _moe_kernel(dump_row, te_ref, na_ref, src0_ref, srcn_ref, dst_ref,
                h2_hbm, wg_ref, wu_ref, wd_ref, y_hbm,
                xbuf, ybuf, wgu_s, wd_s, gsem, ssem):
    i = pl.program_id(0)
    n_active = na_ref[0]
    slot = i % 2
    tile_rows = MOE_TILE * ROWS_PER_TOKEN

    def gather(idx_ref, dst_slot):
        def body(c, carry):
            for k in range(DMA_UNROLL):
                r = c * DMA_UNROLL + k
                src_row = pl.multiple_of(idx_ref[0, 0, r] * ROWS_PER_TOKEN, ROWS_PER_TOKEN)
                dst_row = pl.multiple_of(r * ROWS_PER_TOKEN, ROWS_PER_TOKEN)
                pltpu.make_async_copy(
                    h2_hbm.at[pl.ds(src_row, ROWS_PER_TOKEN), :],
                    xbuf.at[dst_slot, pl.ds(dst_row, ROWS_PER_TOKEN), :],
                    gsem.at[dst_slot]).start()
            return carry
        lax.fori_loop(0, MOE_TILE // DMA_UNROLL, body, 0)

    def scatter(src_slot):
        def body(c, carry):
            for k in range(DMA_UNROLL):
                r = c * DMA_UNROLL + k
                dst_row = pl.multiple_of(dst_ref[0, 0, r] * ROWS_PER_TOKEN, ROWS_PER_TOKEN)
                src_row = pl.multiple_of(r * ROWS_PER_TOKEN, ROWS_PER_TOKEN)
                pltpu.make_async_copy(
                    ybuf.at[src_slot, pl.ds(src_row, ROWS_PER_TOKEN), :],
                    y_hbm.at[pl.ds(dst_row, ROWS_PER_TOKEN), :],
                    ssem.at[src_slot]).start()
            return carry
        lax.fori_loop(0, MOE_TILE // DMA_UNROLL, body, 0)

    def wait_gather(s):
        pltpu.make_async_copy(h2_hbm.at[pl.ds(0, tile_rows), :], xbuf.at[s], gsem.at[s]).wait()

    def wait_scatter(s):
        pltpu.make_async_copy(ybuf.at[s], y_hbm.at[pl.ds(0, tile_rows), :], ssem.at[s]).wait()

    @pl.when(i == 0)
    def _():
        ybuf[1] = jnp.zeros((tile_rows, LANES), F32)
        init = pltpu.make_async_copy(ybuf.at[1], y_hbm.at[pl.ds(dump_row, tile_rows), :], ssem.at[1])
        init.start()
        init.wait()
        gather(src0_ref, 0)

    @pl.when(i + 1 < n_active)
    def _():
        gather(srcn_ref, 1 - slot)

    new_expert = jnp.logical_or(i == 0, te_ref[i] != te_ref[jnp.maximum(i - 1, 0)])

    @pl.when(jnp.logical_and(i < n_active, new_expert))
    def _():
        wgu_s[:, 0:D_EXPERT] = wg_ref[0].astype(BF16)
        wgu_s[:, D_EXPERT:] = wu_ref[0].astype(BF16)
        wd_s[...] = wd_ref[0].astype(BF16)

    @pl.when(i < n_active)
    def _():
        wait_gather(slot)
        xs = xbuf.at[slot]
        x = jnp.concatenate(
            [xs[pl.ds(j, MOE_TILE, stride=ROWS_PER_TOKEN), :].astype(BF16)
             for j in range(ROWS_PER_TOKEN)], axis=1)
        gu = _dot(x, wgu_s[...])
        g = gu[:, 0:D_EXPERT]
        a = (g * _sigmoid(g)) * gu[:, D_EXPERT:]
        y = _dot(a.astype(BF16), wd_s[...])

        @pl.when(i > 0)
        def _():
            wait_scatter(1 - slot)

        ys = ybuf.at[slot]
        for j in range(ROWS_PER_TOKEN):
            ys[pl.ds(j, MOE_TILE, stride=ROWS_PER_TOKEN), :] = y[:, j * LANES:(j + 1) * LANES]
        scatter(slot)

        @pl.when(i == n_active - 1)
        def _():
            wait_scatter(slot)


def _moe(te, na, src, dst, h2r, wg, wu, wd, n_tok):
    nt = te.shape[0]
    tile_rows = MOE_TILE * ROWS_PER_TOKEN
    y_rows = (2 * n_tok + MOE_TILE) * ROWS_PER_TOKEN
    smem_blk = lambda f: pl.BlockSpec((1, 1, MOE_TILE), f, memory_space=pltpu.SMEM)
    wspec = lambda shape: pl.BlockSpec((1,) + shape, lambda i, te_r, na_r: (te_r[i], 0, 0))
    grid_spec = pltpu.PrefetchScalarGridSpec(
        num_scalar_prefetch=2,
        grid=(nt,),
        in_specs=[
            smem_blk(lambda i, te_r, na_r: (0, 0, 0)),
            smem_blk(lambda i, te_r, na_r: (jnp.minimum(i + 1, nt - 1), 0, 0)),
            smem_blk(lambda i, te_r, na_r: (i, 0, 0)),
            pl.BlockSpec(memory_space=pl.ANY),
            wspec((D_MODEL, D_EXPERT)),
            wspec((D_MODEL, D_EXPERT)),
            wspec((D_EXPERT, D_MODEL)),
        ],
        out_specs=pl.BlockSpec(memory_space=pl.ANY),
        scratch_shapes=[
            pltpu.VMEM((2, tile_rows, LANES), F32),
            pltpu.VMEM((2, tile_rows, LANES), F32),
            pltpu.VMEM((D_MODEL, 2 * D_EXPERT), BF16),
            pltpu.VMEM((D_EXPERT, D_MODEL), BF16),
            pltpu.SemaphoreType.DMA((2,)),
            pltpu.SemaphoreType.DMA((2,)),
        ],
    )
    return pl.pallas_call(
        functools.partial(_moe_kernel, 2 * n_tok * ROWS_PER_TOKEN),
        grid_spec=grid_spec,
        out_shape=jax.ShapeDtypeStruct((y_rows, LANES), F32),
        compiler_params=pltpu.CompilerParams(
            dimension_semantics=("arbitrary",), vmem_limit_bytes=VMEM_LIMIT),
        name="moe",
    )(te, na, src, src, dst, h2r, wg, wu, wd)


def _comb_kernel(x1_ref, y0_ref, y1_ref, cwt_ref, mod_ref, o_ref):
    tm = x1_ref.shape[0]
    c0 = cwt_ref[:, 0:1]
    c1 = cwt_ref[:, 1:2]
    for j in range(ROWS_PER_TOKEN):
        cols = slice(j * LANES, (j + 1) * LANES)
        y0 = y0_ref[pl.ds(j, tm, stride=ROWS_PER_TOKEN), :]
        y1 = y1_ref[pl.ds(j, tm, stride=ROWS_PER_TOKEN), :]
        gate2 = mod_ref[0, :, 5 * D_MODEL + j * LANES:5 * D_MODEL + (j + 1) * LANES]
        o_ref[:, cols] = x1_ref[:, cols] + gate2 * (c0 * y0 + c1 * y1)


def _comb(x1, yr, cwt, mod3, seq):
    t = x1.shape[0]
    tm = ROW_TILE
    tiles_per_seq = seq // tm
    n_row_tiles = t // tm
    row = lambda i: (i, 0)
    return pl.pallas_call(
        _comb_kernel,
        grid=(n_row_tiles,),
        in_specs=[
            pl.BlockSpec((tm, D_MODEL), row),
            pl.BlockSpec((tm * ROWS_PER_TOKEN, LANES), row),
            pl.BlockSpec((tm * ROWS_PER_TOKEN, LANES), lambda i: (n_row_tiles + i, 0)),
            pl.BlockSpec((tm, LANES), row),
            pl.BlockSpec((1, 1, 6 * D_MODEL), lambda i: (i // tiles_per_seq, 0, 0)),
        ],
        out_specs=pl.BlockSpec((tm, D_MODEL), row),
        out_shape=jax.ShapeDtypeStruct((t, D_MODEL), F32),
        compiler_params=pltpu.CompilerParams(
            dimension_semantics=("arbitrary",), vmem_limit_bytes=VMEM_LIMIT),
        name="comb",
    )(x1, yr, yr, cwt, mod3)


def _route_plan(eid, n_tok):
    n_assign = 2 * n_tok
    n_tiles = n_assign // MOE_TILE + N_EXPERTS
    e = eid.reshape(n_assign)
    order = jnp.argsort(e, stable=True).astype(jnp.int32)
    experts = jnp.arange(N_EXPERTS, dtype=jnp.int32)
    counts = jnp.sum((e[:, None] == experts[None, :]).astype(jnp.int32), axis=0)
    ntile = (counts + MOE_TILE - 1) // MOE_TILE
    tend = jnp.cumsum(ntile)
    tstart = tend - ntile
    cstart = jnp.cumsum(counts) - counts
    n_active = tend[-1]
    tj = jnp.arange(n_tiles, dtype=jnp.int32)
    te_raw = jnp.minimum(jnp.sum((tj[:, None] >= tend[None, :]).astype(jnp.int32), axis=1),
                         N_EXPERTS - 1)
    te_last = jnp.sum(jnp.where(tj == n_active - 1, te_raw, 0))
    te = jnp.where(tj < n_active, te_raw, te_last).astype(jnp.int32)
    onehot = (te[:, None] == experts[None, :]).astype(jnp.int32)
    lookup = lambda table: jnp.sum(onehot * table[None, :], axis=1)
    row0 = (tj - lookup(tstart)) * MOE_TILE
    n_valid = jnp.where(tj < n_active, jnp.clip(lookup(counts) - row0, 0, MOE_TILE), 0)
    start = jnp.clip(lookup(cstart) + row0, 0, n_assign)
    order_pad = jnp.concatenate([order, jnp.zeros((MOE_TILE,), jnp.int32)])
    a = jax.vmap(lambda s0: lax.dynamic_slice(order_pad, (s0,), (MOE_TILE,)))(start)
    r = jnp.arange(MOE_TILE, dtype=jnp.int32)[None, :]
    valid = r < n_valid[:, None]
    src = jnp.where(valid, a % n_tok, 0).astype(jnp.int32)
    dst = jnp.where(valid, a, n_assign + r).astype(jnp.int32)
    return (te, n_active.reshape(1).astype(jnp.int32),
            src.reshape(n_tiles, 1, MOE_TILE), dst.reshape(n_tiles, 1, MOE_TILE))


def _rotate_half_cols(w):
    half = QK_ROPE // 2
    return jnp.concatenate([w[..., half:], w[..., :half]], axis=-1)


def kernel(x, c, positions, w_ada, b_ada, norm1_g, w_in, conv_w, q_a_norm_g, w_q_b, kv_a_norm_g, w_kv_b, q_norm_g, k_norm_g, w_o, norm2_g, w_router_group, b_router_group, w_router_expert, b_router_expert, w_exp_gate, w_exp_up, w_exp_down):
    nb, seq, d = x.shape
    depth = w_ada.shape[0]
    n_tok = nb * seq
    assert d == D_MODEL and seq % ROW_TILE == 0 and seq % Q_TILE == 0 and Q_TILE % CHUNK == 0
    assert (2 * n_tok) % MOE_TILE == 0

    inv = ROPE_BASE ** (-jnp.arange(0, QK_ROPE, 2, dtype=F32) / QK_ROPE)
    ang = inv[None, :, None] * positions.astype(F32)[:, None, :]
    cos, sin = jnp.cos(ang), jnp.sin(ang)
    cst = jnp.concatenate([cos, cos, -sin, sin], axis=1)

    x2 = x.reshape(n_tok, d)
    for l in range(depth):
        wi = w_in[l]
        o_q = 3 * D_MODEL
        o_kv = o_q + Q_LORA
        o_kr = o_kv + KV_LORA
        o_gc = o_kr + QK_ROPE
        o_gm = o_gc + D_MODEL
        w_kr = wi[:, o_kr:o_gc]
        w_cat = jnp.concatenate(
            [wi[:, 0:o_q], wi[:, o_gc:o_gm], wi[:, o_gm:], wi[:, o_q:o_kv], wi[:, o_kv:o_kr]],
            axis=1).astype(BF16)
        w_krt = jnp.concatenate([w_kr, _rotate_half_cols(w_kr)], axis=1).T.astype(BF16)
        wq3 = w_q_b[l].reshape(Q_LORA, N_HEADS, QK_HEAD)
        wq = jnp.concatenate([wq3, _rotate_half_cols(wq3[..., QK_NOPE:])], axis=-1)
        wqt = wq.transpose(1, 2, 0).astype(BF16)
        wkvt = w_kv_b[l].reshape(KV_LORA, N_HEADS, QK_NOPE + V_HEAD).transpose(1, 2, 0).astype(BF16)
        gq = jnp.concatenate([q_norm_g[l], _rotate_half_cols(q_norm_g[l][QK_NOPE:])]).reshape(-1, 1)
        gk = jnp.concatenate([k_norm_g[l], _rotate_half_cols(k_norm_g[l][QK_NOPE:])]).reshape(-1, 1)
        wr_t = jnp.concatenate(
            [w_router_expert[l].T, w_router_group[l].T,
             jnp.zeros((ROUTER_ROWS - N_EXPERTS - N_GROUPS, d), F32)], axis=0).astype(BF16)
        br = jnp.concatenate(
            [b_router_expert[l], b_router_group[l],
             jnp.zeros((ROUTER_ROWS - N_EXPERTS - N_GROUPS,), F32)]).reshape(ROUTER_ROWS, 1)

        mod3 = _ada(c, w_ada[l], b_ada[l]).reshape(nb, 1, 6 * d)
        conv_p, gm, qn, kvn, krt = _inproj(
            x2, mod3, norm1_g[l].reshape(1, d), w_cat, w_krt, conv_w[l],
            q_a_norm_g[l].reshape(1, -1), kv_a_norm_g[l].reshape(1, -1), seq)
        y_mla = _attn(qn, kvn, krt, cst, wqt, wkvt, gq, gk, nb, seq)
        x1, h2r, eid, cwt = _oproj(conv_p, gm, y_mla, x2, mod3, w_o[l].astype(BF16),
                                   norm2_g[l].reshape(1, d), wr_t, br, seq)
        te, na, src, dst = _route_plan(eid, n_tok)
        yr = _moe(te, na, src, dst, h2r,
                  w_exp_gate[l].reshape(N_EXPERTS, d, D_EXPERT),
                  w_exp_up[l].reshape(N_EXPERTS, d, D_EXPERT),
                  w_exp_down[l].reshape(N_EXPERTS, D_EXPERT, d), n_tok)
        x2 = _comb(x1, yr, cwt, mod3, seq)
    return x2.reshape(nb, seq, d)
```

```python
import functools
import math

import jax
import jax.numpy as jnp
from jax import lax
from jax.experimental import pallas as pl
from jax.experimental.pallas import tpu as pltpu

F32 = jnp.float32
BF16 = jnp.bfloat16

D_MODEL = 1024
N_HEADS = 8
QK_NOPE = 128
QK_ROPE = 64
QK_HEAD = QK_NOPE + QK_ROPE
V_HEAD = 128
Q_LORA = 384
KV_LORA = 256
CHUNK = 64
EPS = 1e-6
ROPE_BASE = 10000.0
N_GROUPS = 4
EXPERTS_PER_GROUP = 8
N_EXPERTS = N_GROUPS * EXPERTS_PER_GROUP
D_EXPERT = 256
CONV_K = 3

LANES = 128
SUBLANES = 8
VMEM_LIMIT = 56 * 1024 * 1024

ROW_TILE = 256
Q_TILE = 512
MOE_TILE = 256
ADA_COLS = 1536
ROWS_PER_TOKEN = D_MODEL // LANES
DMA_UNROLL = 8
ROUTER_ROWS = 40


def _sigmoid(v):
    return 1.0 / (1.0 + jnp.exp(-v))


def _dot(a, b):
    return jnp.dot(a, b, preferred_element_type=F32)


def _dot_nt(a, b):
    return lax.dot_general(a, b, (((1,), (1,)), ((), ())), preferred_element_type=F32)


def _ada_kernel(c_ref, w_ref, b_ref, o_ref):
    c = c_ref[...]
    act = (c * _sigmoid(c)).astype(BF16)
    o_ref[...] = _dot(act, w_ref[...].astype(BF16)) + b_ref[...]


def _ada(c, w_ada, b_ada):
    nb, d = c.shape
    n = w_ada.shape[1]
    return pl.pallas_call(
        _ada_kernel,
        grid=(n // ADA_COLS,),
        in_specs=[
            pl.BlockSpec((nb, d), lambda j: (0, 0)),
            pl.BlockSpec((d, ADA_COLS), lambda j: (0, j)),
            pl.BlockSpec((1, ADA_COLS), lambda j: (0, j)),
        ],
        out_specs=pl.BlockSpec((nb, ADA_COLS), lambda j: (0, j)),
        out_shape=jax.ShapeDtypeStruct((nb, n), F32),
        compiler_params=pltpu.CompilerParams(
            dimension_semantics=("arbitrary",), vmem_limit_bytes=VMEM_LIMIT),
        name="ada",
    )(c, w_ada, b_ada.reshape(1, n))


_C_ZX = 0
_C_ZB = D_MODEL
_C_ZC = 2 * D_MODEL
_C_GC = 3 * D_MODEL
_C_GM = 4 * D_MODEL
_C_QL = 5 * D_MODEL
_C_KV = _C_QL + Q_LORA
_C_END = _C_KV + KV_LORA


def _inproj_kernel(tiles_per_seq, x_ref, mod_ref, g1_ref, w_ref, wkr_ref, cw_ref, gq_ref, gkv_ref,
                   conv_ref, gm_ref, qn_ref, kvn_ref, krt_ref, ubuf):
    tm = x_ref.shape[0]

    @pl.when(pl.program_id(0) % tiles_per_seq == 0)
    def _():
        ubuf[0:SUBLANES, :] = jnp.zeros((SUBLANES, D_MODEL), F32)

    x = x_ref[...]
    xn = x * lax.rsqrt(jnp.mean(x * x, axis=-1, keepdims=True) + EPS) * g1_ref[...]
    shift = mod_ref[0, :, 0:D_MODEL]
    scale = mod_ref[0, :, D_MODEL:2 * D_MODEL]
    h = (xn * (1.0 + scale) + shift).astype(BF16)

    def proj(lo, hi):
        return _dot(h, w_ref[:, lo:hi])

    u = proj(_C_ZC, _C_GC) * proj(_C_ZX, _C_ZB)
    ubuf[SUBLANES:SUBLANES + tm, :] = u
    conv = (ubuf[SUBLANES - 2:SUBLANES - 2 + tm, :] * cw_ref[0:1, :]
            + ubuf[SUBLANES - 1:SUBLANES - 1 + tm, :] * cw_ref[1:2, :]
            + u * cw_ref[2:3, :])
    ubuf[0:SUBLANES, :] = ubuf[tm:tm + SUBLANES, :]
    y_conv = proj(_C_ZB, _C_ZC) * conv
    conv_ref[...] = (_sigmoid(proj(_C_GC, _C_GM)) * y_conv).astype(BF16)
    gm_ref[...] = _sigmoid(proj(_C_GM, _C_QL)).astype(BF16)

    ql = proj(_C_QL, _C_KV)
    qn_ref[...] = (ql * lax.rsqrt(jnp.mean(ql * ql, axis=-1, keepdims=True) + EPS)
                   * gq_ref[...]).astype(BF16)
    kl = proj(_C_KV, _C_END)
    kvn_ref[...] = (kl * lax.rsqrt(jnp.mean(kl * kl, axis=-1, keepdims=True) + EPS)
                    * gkv_ref[...]).astype(BF16)
    krt_ref[0] = _dot_nt(wkr_ref[...], h)


def _inproj(x2, mod3, g1, w_cat, w_krt, conv_w, gq, gkv, seq):
    t = x2.shape[0]
    nb = t // seq
    tm = ROW_TILE
    tiles_per_seq = seq // tm
    row = lambda i: (i, 0)
    const = lambda i: (0, 0)
    return pl.pallas_call(
        functools.partial(_inproj_kernel, tiles_per_seq),
        grid=(t // tm,),
        in_specs=[
            pl.BlockSpec((tm, D_MODEL), row),
            pl.BlockSpec((1, 1, 6 * D_MODEL), lambda i: (i // tiles_per_seq, 0, 0)),
            pl.BlockSpec((1, D_MODEL), const),
            pl.BlockSpec((D_MODEL, _C_END), const),
            pl.BlockSpec((2 * QK_ROPE, D_MODEL), const),
            pl.BlockSpec((CONV_K, D_MODEL), const),
            pl.BlockSpec((1, Q_LORA), const),
            pl.BlockSpec((1, KV_LORA), const),
        ],
        out_specs=[
            pl.BlockSpec((tm, D_MODEL), row),
            pl.BlockSpec((tm, D_MODEL), row),
            pl.BlockSpec((tm, Q_LORA), row),
            pl.BlockSpec((tm, KV_LORA), row),
            pl.BlockSpec((1, 2 * QK_ROPE, tm),
                         lambda i: (i // tiles_per_seq, 0, i % tiles_per_seq)),
        ],
        out_shape=[
            jax.ShapeDtypeStruct((t, D_MODEL), BF16),
            jax.ShapeDtypeStruct((t, D_MODEL), BF16),
            jax.ShapeDtypeStruct((t, Q_LORA), BF16),
            jax.ShapeDtypeStruct((t, KV_LORA), BF16),
            jax.ShapeDtypeStruct((nb, 2 * QK_ROPE, seq), F32),
        ],
        scratch_shapes=[pltpu.VMEM((tm + SUBLANES, D_MODEL), F32)],
        compiler_params=pltpu.CompilerParams(
            dimension_semantics=("arbitrary",), vmem_limit_bytes=VMEM_LIMIT),
        name="inproj",
    )(x2, mod3, g1, w_cat, w_krt, conv_w, gq, gkv)


def _attn_kernel(qn_ref, kvn_ref, krt_ref, cst_ref, wqt_ref, wkvt_ref, gq_ref, gk_ref,
                 o_ref, qt_s, k_s, vt_s, s_buf0, s_buf1):
    seq = qn_ref.shape[0]
    cos_t = cst_ref[0, 0:QK_ROPE, :]
    sin_t = cst_ref[0, QK_ROPE:, :]

    def normed_rope(nope, r, rr, g, extra_scale):
        ss = jnp.sum(nope * nope, axis=0, keepdims=True) + jnp.sum(r * r, axis=0, keepdims=True)
        scale = lax.rsqrt(ss * (1.0 / QK_HEAD) + EPS) * extra_scale
        rope = r * g[QK_NOPE:QK_HEAD] * cos_t + rr * g[QK_HEAD:] * sin_t
        return (nope * g[0:QK_NOPE] * scale).astype(BF16), (rope * scale).astype(BF16)

    qt = _dot_nt(wqt_ref[0], qn_ref[...])
    q_n, q_r = normed_rope(qt[0:QK_NOPE], qt[QK_NOPE:QK_HEAD], qt[QK_HEAD:], gq_ref[...],
                           QK_HEAD ** -0.5 * math.log2(math.e))
    qt_s[0:QK_NOPE, :] = q_n
    qt_s[QK_NOPE:QK_HEAD, :] = q_r
    qt_s[QK_HEAD:, :] = jnp.zeros((QK_ROPE, seq), BF16)

    kvt = _dot_nt(wkvt_ref[0], kvn_ref[...])
    krt = krt_ref[0]
    k_n, k_r = normed_rope(kvt[0:QK_NOPE], krt[0:QK_ROPE], krt[QK_ROPE:], gk_ref[...], 1.0)
    kt = jnp.concatenate([k_n, k_r, jnp.zeros((QK_ROPE, seq), BF16)], axis=0)
    k_s[...] = kt.T
    vt_s[0:V_HEAD, :] = kvt[QK_NOPE:].astype(BF16)
    vt_s[V_HEAD:, :] = jnp.ones((vt_s.shape[0] - V_HEAD, seq), BF16)

    tq = Q_TILE
    kchunk = lax.broadcasted_iota(jnp.int32, (tq, tq), 0) // CHUNK
    qchunk = lax.broadcasted_iota(jnp.int32, (tq, tq), 1) // CHUNK
    diag_ok = kchunk <= qchunk
    neg = jnp.finfo(F32).min
    nq = seq // tq

    def scores(i):
        q0 = i * tq
        sb = s_buf0 if i % 2 == 0 else s_buf1
        q = qt_s[:, q0:q0 + tq]
        if i > 0:
            sb[0:q0, :] = _dot(k_s[0:q0, :], q)
        sb[q0:q0 + tq, :] = jnp.where(diag_ok, _dot(k_s[q0:q0 + tq, :], q), neg)

    def finish(i):
        q0 = i * tq
        kend = q0 + tq
        sb = s_buf0 if i % 2 == 0 else s_buf1
        m = jnp.max(sb[0:kend, :], axis=0, keepdims=True)
        acc = _dot(vt_s[:, 0:kend], jnp.exp2(sb[0:kend, :] - m).astype(BF16))
        o_t = acc[0:V_HEAD] / acc[V_HEAD:V_HEAD + 1]
        o_ref[q0:q0 + tq, :] = o_t.T.astype(o_ref.dtype)

    scores(nq - 1)
    for i in range(nq - 1, -1, -1):
        if i > 0:
            scores(i - 1)
        finish(i)


def _attn(qn, kvn, krt, cst, wqt, wkvt, gq, gk, nb, seq):
    t = qn.shape[0]
    per_b = lambda b, h: (b, 0)
    per_b3 = lambda b, h: (b, 0, 0)
    per_h = lambda b, h: (h, 0, 0)
    const = lambda b, h: (0, 0)
    qk_rows = QK_NOPE + 2 * QK_ROPE
    return pl.pallas_call(
        _attn_kernel,
        grid=(nb, N_HEADS),
        in_specs=[
            pl.BlockSpec((seq, Q_LORA), per_b),
            pl.BlockSpec((seq, KV_LORA), per_b),
            pl.BlockSpec((1, 2 * QK_ROPE, seq), per_b3),
            pl.BlockSpec((1, 2 * QK_ROPE, seq), per_b3),
            pl.BlockSpec((1, qk_rows, Q_LORA), per_h),
            pl.BlockSpec((1, QK_NOPE + V_HEAD, KV_LORA), per_h),
            pl.BlockSpec((qk_rows, 1), const),
            pl.BlockSpec((qk_rows, 1), const),
        ],
        out_specs=pl.BlockSpec((seq, V_HEAD), lambda b, h: (b, h)),
        out_shape=jax.ShapeDtypeStruct((t, N_HEADS * V_HEAD), BF16),
        scratch_shapes=[
            pltpu.VMEM((qk_rows, seq), BF16),
            pltpu.VMEM((seq, qk_rows), BF16),
            pltpu.VMEM((V_HEAD + 2 * SUBLANES, seq), BF16),
            pltpu.VMEM((seq, Q_TILE), F32),
            pltpu.VMEM((seq, Q_TILE), F32),
        ],
        compiler_params=pltpu.CompilerParams(
            dimension_semantics=("arbitrary", "arbitrary"), vmem_limit_bytes=VMEM_LIMIT),
        name="attn",
    )(qn, kvn, krt, cst, wqt, wkvt, gq, gk)


def _oproj_kernel(conv_ref, gm_ref, y_ref, x_ref, mod_ref, wo_ref, g2_ref, wr_ref, br_ref,
                  x1_ref, h2_ref, eid_ref, rank_ref, cnt_ref, cwt_ref, base):
    tm = x_ref.shape[0]

    @pl.when(pl.program_id(0) == 0)
    def _():
        base[...] = jnp.zeros(base.shape, F32)

    merged = conv_ref[...].astype(F32) + gm_ref[...].astype(F32) * y_ref[...].astype(F32)
    att = _dot(merged.astype(BF16), wo_ref[...])
    gate1 = mod_ref[0, :, 2 * D_MODEL:3 * D_MODEL]
    shift2 = mod_ref[0, :, 3 * D_MODEL:4 * D_MODEL]
    scale2 = mod_ref[0, :, 4 * D_MODEL:5 * D_MODEL]
    x1 = x_ref[...] + gate1 * att
    x1_ref[...] = x1
    xn = x1 * lax.rsqrt(jnp.mean(x1 * x1, axis=-1, keepdims=True) + EPS) * g2_ref[...]
    h2 = xn * (1.0 + scale2) + shift2
    for j in range(ROWS_PER_TOKEN):
        h2_ref[pl.ds(j, tm, stride=ROWS_PER_TOKEN), :] = h2[:, j * LANES:(j + 1) * LANES]

    lt = _dot_nt(wr_ref[...], h2.astype(BF16)) + br_ref[...]
    gl = [lt[N_EXPERTS + r:N_EXPERTS + r + 1, :] for r in range(N_GROUPS)]
    gmax = jnp.maximum(jnp.maximum(gl[0], gl[1]), jnp.maximum(gl[2], gl[3]))
    gidx = jnp.full(gmax.shape, N_GROUPS - 1, jnp.int32)
    for r in range(N_GROUPS - 2, -1, -1):
        gidx = jnp.where(gl[r] == gmax, r, gidx)
    gsum = jnp.exp(gl[0] - gmax)
    for r in range(1, N_GROUPS):
        gsum = gsum + jnp.exp(gl[r] - gmax)
    p_group = 1.0 / gsum
    es = lt[(N_GROUPS - 1) * EXPERTS_PER_GROUP:N_GROUPS * EXPERTS_PER_GROUP, :]
    for r in range(N_GROUPS - 2, -1, -1):
        es = jnp.where(gidx == r, lt[r * EXPERTS_PER_GROUP:(r + 1) * EXPERTS_PER_GROUP, :], es)
    row = lax.broadcasted_iota(jnp.int32, es.shape, 0)
    m1 = jnp.max(es, axis=0, keepdims=True)
    i1 = jnp.min(jnp.where(es == m1, row, EXPERTS_PER_GROUP), axis=0, keepdims=True)
    es2 = jnp.where(row == i1, -jnp.inf, es)
    m2 = jnp.max(es2, axis=0, keepdims=True)
    i2 = jnp.min(jnp.where(es2 == m2, row, EXPERTS_PER_GROUP), axis=0, keepdims=True)
    e2 = jnp.exp(m2 - m1)
    w1 = p_group / (1.0 + e2)
    w2 = w1 * e2
    eid0 = gidx * EXPERTS_PER_GROUP + i1
    eid1 = gidx * EXPERTS_PER_GROUP + i2
    eid_ref[0:1, :] = eid0
    eid_ref[1:2, :] = eid1
    erow = lax.broadcasted_iota(jnp.int32, (N_EXPERTS, tm), 0)
    oh0 = erow == eid0
    oh1 = erow == eid1
    both = jnp.where(oh0, 1.0, jnp.where(oh1, 1.0, 0.0))
    earlier = (lax.broadcasted_iota(jnp.int32, (tm, tm), 0)
               < lax.broadcasted_iota(jnp.int32, (tm, tm), 1))
    seen = base[...] + _dot(both.astype(BF16), jnp.where(earlier, 1.0, 0.0).astype(BF16))
    rank_ref[0:1, :] = jnp.sum(jnp.where(oh0, seen, 0.0), axis=0, keepdims=True).astype(jnp.int32)
    rank_ref[1:2, :] = jnp.sum(jnp.where(oh1, seen, 0.0), axis=0, keepdims=True).astype(jnp.int32)
    base[...] = base[...] + jnp.sum(both, axis=1, keepdims=True)
    cnt_ref[...] = base[...]
    wrow = lax.broadcasted_iota(jnp.int32, (LANES, tm), 0)
    wmat = jnp.where(wrow == 0, w1, jnp.where(wrow == 1, w2, 0.0))
    cwt_ref[...] = wmat.T


def _oproj(conv_p, gm, y_mla, x2, mod3, wo, g2, wr_t, br, seq):
    t = x2.shape[0]
    tm = ROW_TILE
    tiles_per_seq = seq // tm
    row = lambda i: (i, 0)
    const = lambda i: (0, 0)
    return pl.pallas_call(
        _oproj_kernel,
        grid=(t // tm,),
        in_specs=[
            pl.BlockSpec((tm, D_MODEL), row),
            pl.BlockSpec((tm, D_MODEL), row),
            pl.BlockSpec((tm, D_MODEL), row),
            pl.BlockSpec((tm, D_MODEL), row),
            pl.BlockSpec((1, 1, 6 * D_MODEL), lambda i: (i // tiles_per_seq, 0, 0)),
            pl.BlockSpec((D_MODEL, D_MODEL), const),
            pl.BlockSpec((1, D_MODEL), const),
            pl.BlockSpec((ROUTER_ROWS, D_MODEL), const),
            pl.BlockSpec((ROUTER_ROWS, 1), const),
        ],
        out_specs=[
            pl.BlockSpec((tm, D_MODEL), row),
            pl.BlockSpec((tm * ROWS_PER_TOKEN, LANES), row),
            pl.BlockSpec((2, tm), lambda i: (0, i)),
            pl.BlockSpec((2, tm), lambda i: (0, i)),
            pl.BlockSpec((N_EXPERTS, 1), const),
            pl.BlockSpec((tm, LANES), row),
        ],
        out_shape=[
            jax.ShapeDtypeStruct((t, D_MODEL), F32),
            jax.ShapeDtypeStruct((t * ROWS_PER_TOKEN, LANES), F32),
            jax.ShapeDtypeStruct((2, t), jnp.int32),
            jax.ShapeDtypeStruct((2, t), jnp.int32),
            jax.ShapeDtypeStruct((N_EXPERTS, 1), F32),
            jax.ShapeDtypeStruct((t, LANES), F32),
        ],
        scratch_shapes=[pltpu.VMEM((N_EXPERTS, 1), F32)],
        compiler_params=pltpu.CompilerParams(
            dimension_semantics=("arbitrary",), vmem_limit_bytes=VMEM_LIMIT),
        name="oproj",
    )(conv_p, gm, y_mla, x2, mod3, wo, g2, wr_t, br)


def _row_copies(n_rows, issue):
    def body(c, carry):
        for k in range(DMA_UNROLL):
            issue(c * DMA_UNROLL + k, k % 2)
        return carry
    lax.fori_loop(0, n_rows // DMA_UNROLL, body, 0)


def _dispatch_kernel(n_tiles, ztile_ref, na_ref, pos_ref, h2_ref, xs_hbm, zbuf, sem):
    tm = h2_ref.shape[0] // ROWS_PER_TOKEN
    tile_rows = MOE_TILE * ROWS_PER_TOKEN

    @pl.when(pl.program_id(0) == 0)
    def _():
        zbuf[...] = jnp.zeros(zbuf.shape, F32)

        def zero_tile(tile):
            row = pl.multiple_of(tile * tile_rows, tile_rows)
            return pltpu.make_async_copy(zbuf, xs_hbm.at[pl.ds(row, tile_rows), :], sem)
        for e in range(N_EXPERTS):
            zero_tile(ztile_ref[e]).start()
        for e in range(N_EXPERTS):
            zero_tile(ztile_ref[e]).wait()

        def zero_tail(k, carry):
            cp = zero_tile(na_ref[0] + k)
            cp.start()
            cp.wait()
            return carry
        lax.fori_loop(0, n_tiles - na_ref[0], zero_tail, 0)

    def issue(r, priority):
        tok_row = pl.multiple_of((r % tm) * ROWS_PER_TOKEN, ROWS_PER_TOKEN)
        dst_row = pl.multiple_of(pos_ref[0, 0, r] * ROWS_PER_TOKEN, ROWS_PER_TOKEN)
        pltpu.make_async_copy(h2_ref.at[pl.ds(tok_row, ROWS_PER_TOKEN), :],
                              xs_hbm.at[pl.ds(dst_row, ROWS_PER_TOKEN), :],
                              sem).start(priority=priority)
    _row_copies(2 * tm, issue)
    for _ in range(2):
        pltpu.make_async_copy(h2_ref, xs_hbm.at[pl.ds(0, tm * ROWS_PER_TOKEN), :], sem).wait()


def _dispatch(ztile, na, pos3, h2r, n_tiles):
    tm = pos3.shape[2] // 2
    n_steps = pos3.shape[0]
    grid_spec = pltpu.PrefetchScalarGridSpec(
        num_scalar_prefetch=2,
        grid=(n_steps,),
        in_specs=[
            pl.BlockSpec((1, 1, 2 * tm), lambda i, z, n: (i, 0, 0), memory_space=pltpu.SMEM),
            pl.BlockSpec((tm * ROWS_PER_TOKEN, LANES), lambda i, z, n: (i, 0)),
        ],
        out_specs=pl.BlockSpec(memory_space=pl.ANY),
        scratch_shapes=[
            pltpu.VMEM((MOE_TILE * ROWS_PER_TOKEN, LANES), F32),
            pltpu.SemaphoreType.DMA(()),
        ],
    )
    return pl.pallas_call(
        functools.partial(_dispatch_kernel, n_tiles),
        grid_spec=grid_spec,
        out_shape=jax.ShapeDtypeStruct((n_tiles * MOE_TILE * ROWS_PER_TOKEN, LANES), F32),
        compiler_params=pltpu.CompilerParams(
            dimension_semantics=("arbitrary",), vmem_limit_bytes=VMEM_LIMIT),
        name="dispatch",
    )(ztile, na, pos3, h2r)


def _moe_kernel(te_ref, na_ref, x_ref, wg_ref, wu_ref, wd_ref, y_ref, wgu_s, wd_s):
    i = pl.program_id(0)
    active = i < na_ref[0]
    new_expert = jnp.logical_or(i == 0, te_ref[i] != te_ref[jnp.maximum(i - 1, 0)])

    @pl.when(jnp.logical_and(active, new_expert))
    def _():
        wgu_s[:, 0:D_EXPERT] = wg_ref[0].astype(BF16)
        wgu_s[:, D_EXPERT:] = wu_ref[0].astype(BF16)
        wd_s[...] = wd_ref[0].astype(BF16)

    @pl.when(active)
    def _():
        x = jnp.concatenate(
            [x_ref[pl.ds(j, MOE_TILE, stride=ROWS_PER_TOKEN), :].astype(BF16)
             for j in range(ROWS_PER_TOKEN)], axis=1)
        gu = _dot(x, wgu_s[...])
        g = gu[:, 0:D_EXPERT]
        a = (g * _sigmoid(g)) * gu[:, D_EXPERT:]
        y = _dot(a.astype(BF16), wd_s[...])
        for j in range(ROWS_PER_TOKEN):
            y_ref[pl.ds(j, MOE_TILE, stride=ROWS_PER_TOKEN), :] = y[:, j * LANES:(j + 1) * LANES]

    @pl.when(jnp.logical_not(active))
    def _():
        y_ref[...] = jnp.zeros(y_ref.shape, F32)


def _moe(te, na, xs, wg, wu, wd):
    nt = te.shape[0]
    tile_rows = MOE_TILE * ROWS_PER_TOKEN
    tile = lambda i, te_r, na_r: (jnp.minimum(i, na_r[0] - 1), 0)
    wspec = lambda shape: pl.BlockSpec((1,) + shape, lambda i, te_r, na_r: (te_r[i], 0, 0))
    grid_spec = pltpu.PrefetchScalarGridSpec(
        num_scalar_prefetch=2,
        grid=(nt,),
        in_specs=[
            pl.BlockSpec((tile_rows, LANES), tile),
            wspec((D_MODEL, D_EXPERT)),
            wspec((D_MODEL, D_EXPERT)),
            wspec((D_EXPERT, D_MODEL)),
        ],
        out_specs=pl.BlockSpec((tile_rows, LANES), lambda i, te_r, na_r: (i, 0)),
        scratch_shapes=[
            pltpu.VMEM((D_MODEL, 2 * D_EXPERT), BF16),
            pltpu.VMEM((D_EXPERT, D_MODEL), BF16),
        ],
    )
    return pl.pallas_call(
        _moe_kernel,
        grid_spec=grid_spec,
        out_shape=jax.ShapeDtypeStruct(xs.shape, F32),
        compiler_params=pltpu.CompilerParams(
            dimension_semantics=("arbitrary",), vmem_limit_bytes=VMEM_LIMIT),
        name="moe",
    )(te, na, xs, wg, wu, wd)


def _comb_kernel(pos0_ref, posn_ref, x1_ref, cwt_ref, mod_ref, y_hbm, o_ref, ybuf, sem):
    tm = x1_ref.shape[0]
    i = pl.program_id(0)
    slot = i % 2

    def gather(pos_ref, dst_slot):
        def issue(r, priority):
            src_row = pl.multiple_of(pos_ref[0, 0, r] * ROWS_PER_TOKEN, ROWS_PER_TOKEN)
            dst_row = pl.multiple_of(r * ROWS_PER_TOKEN, ROWS_PER_TOKEN)
            pltpu.make_async_copy(y_hbm.at[pl.ds(src_row, ROWS_PER_TOKEN), :],
                                  ybuf.at[dst_slot, pl.ds(dst_row, ROWS_PER_TOKEN), :],
                                  sem.at[dst_slot]).start(priority=priority)
        _row_copies(2 * tm, issue)

    @pl.when(i == 0)
    def _():
        gather(pos0_ref, 0)

    @pl.when(i + 1 < pl.num_programs(0))
    def _():
        gather(posn_ref, 1 - slot)

    pltpu.make_async_copy(y_hbm.at[pl.ds(0, 2 * tm * ROWS_PER_TOKEN), :], ybuf.at[slot],
                          sem.at[slot]).wait()
    c0 = cwt_ref[:, 0:1]
    c1 = cwt_ref[:, 1:2]
    ys = ybuf.at[slot]
    for j in range(ROWS_PER_TOKEN):
        cols = slice(j * LANES, (j + 1) * LANES)
        y0 = ys[pl.ds(j, tm, stride=ROWS_PER_TOKEN), :]
        y1 = ys[pl.ds(tm * ROWS_PER_TOKEN + j, tm, stride=ROWS_PER_TOKEN), :]
        gate2 = mod_ref[0, :, 5 * D_MODEL + j * LANES:5 * D_MODEL + (j + 1) * LANES]
        o_ref[:, cols] = x1_ref[:, cols] + gate2 * (c0 * y0 + c1 * y1)


def _comb(x1, yr, pos3, cwt, mod3, seq):
    t = x1.shape[0]
    tm = ROW_TILE
    tiles_per_seq = seq // tm
    n_steps = t // tm
    row = lambda i: (i, 0)
    smem_blk = lambda f: pl.BlockSpec((1, 1, 2 * tm), f, memory_space=pltpu.SMEM)
    return pl.pallas_call(
        _comb_kernel,
        grid=(n_steps,),
        in_specs=[
            smem_blk(lambda i: (0, 0, 0)),
            smem_blk(lambda i: (jnp.minimum(i + 1, n_steps - 1), 0, 0)),
            pl.BlockSpec((tm, D_MODEL), row),
            pl.BlockSpec((tm, LANES), row),
            pl.BlockSpec((1, 1, 6 * D_MODEL), lambda i: (i // tiles_per_seq, 0, 0)),
            pl.BlockSpec(memory_space=pl.ANY),
        ],
        out_specs=pl.BlockSpec((tm, D_MODEL), row),
        out_shape=jax.ShapeDtypeStruct((t, D_MODEL), F32),
        scratch_shapes=[
            pltpu.VMEM((2, 2 * tm * ROWS_PER_TOKEN, LANES), F32),
            pltpu.SemaphoreType.DMA((2,)),
        ],
        compiler_params=pltpu.CompilerParams(
            dimension_semantics=("arbitrary",), vmem_limit_bytes=VMEM_LIMIT),
        name="comb",
    )(pos3, pos3, x1, cwt, mod3, yr)


def _route_plan(eid, rank, cnt, n_tok):
    n_tiles = (2 * n_tok) // MOE_TILE + N_EXPERTS
    experts = jnp.arange(N_EXPERTS, dtype=jnp.int32)
    counts = cnt.reshape(N_EXPERTS).astype(jnp.int32)
    ntile = (counts + MOE_TILE - 1) // MOE_TILE
    tend = jnp.cumsum(ntile)
    tstart = tend - ntile
    n_active = tend[-1]
    tj = jnp.arange(n_tiles, dtype=jnp.int32)
    te_raw = jnp.minimum(jnp.sum((tj[:, None] >= tend[None, :]).astype(jnp.int32), axis=1),
                         N_EXPERTS - 1)
    te_last = jnp.sum(jnp.where(tj == n_active - 1, te_raw, 0))
    te = jnp.where(tj < n_active, te_raw, te_last).astype(jnp.int32)
    first_row = jnp.sum(jnp.where(eid[:, :, None] == experts[None, None, :],
                                  (tstart * MOE_TILE)[None, None, :], 0), axis=-1)
    pos = (first_row + rank).astype(jnp.int32)
    ztile = jnp.maximum(tend - 1, 0).astype(jnp.int32)
    pos3 = pos.reshape(2, n_tok // ROW_TILE, ROW_TILE).transpose(1, 0, 2).reshape(
        n_tok // ROW_TILE, 1, 2 * ROW_TILE)
    return te, n_active.reshape(1).astype(jnp.int32), ztile, pos3, n_tiles


def _rotate_half_cols(w):
    half = QK_ROPE // 2
    return jnp.concatenate([w[..., half:], w[..., :half]], axis=-1)


def kernel(x, c, positions, w_ada, b_ada, norm1_g, w_in, conv_w, q_a_norm_g, w_q_b, kv_a_norm_g, w_kv_b, q_norm_g, k_norm_g, w_o, norm2_g, w_router_group, b_router_group, w_router_expert, b_router_expert, w_exp_gate, w_exp_up, w_exp_down):
    nb, seq, d = x.shape
    depth = w_ada.shape[0]
    n_tok = nb * seq
    assert d == D_MODEL and seq % ROW_TILE == 0 and seq % Q_TILE == 0 and Q_TILE % CHUNK == 0
    assert (2 * n_tok) % MOE_TILE == 0

    inv = ROPE_BASE ** (-jnp.arange(0, QK_ROPE, 2, dtype=F32) / QK_ROPE)
    ang = inv[None, :, None] * positions.astype(F32)[:, None, :]
    cos, sin = jnp.cos(ang), jnp.sin(ang)
    cst = jnp.concatenate([cos, cos, -sin, sin], axis=1)

    x2 = x.reshape(n_tok, d)
    for l in range(depth):
        wi = w_in[l]
        o_q = 3 * D_MODEL
        o_kv = o_q + Q_LORA
        o_kr = o_kv + KV_LORA
        o_gc = o_kr + QK_ROPE
        o_gm = o_gc + D_MODEL
        w_kr = wi[:, o_kr:o_gc]
        w_cat = jnp.concatenate(
            [wi[:, 0:o_q], wi[:, o_gc:o_gm], wi[:, o_gm:], wi[:, o_q:o_kv], wi[:, o_kv:o_kr]],
            axis=1).astype(BF16)
        w_krt = jnp.concatenate([w_kr, _rotate_half_cols(w_kr)], axis=1).T.astype(BF16)
        wq3 = w_q_b[l].reshape(Q_LORA, N_HEADS, QK_HEAD)
        wq = jnp.concatenate([wq3, _rotate_half_cols(wq3[..., QK_NOPE:])], axis=-1)
        wqt = wq.transpose(1, 2, 0).astype(BF16)
        wkvt = w_kv_b[l].reshape(KV_LORA, N_HEADS, QK_NOPE + V_HEAD).transpose(1, 2, 0).astype(BF16)
        gq = jnp.concatenate([q_norm_g[l], _rotate_half_cols(q_norm_g[l][QK_NOPE:])]).reshape(-1, 1)
        gk = jnp.concatenate([k_norm_g[l], _rotate_half_cols(k_norm_g[l][QK_NOPE:])]).reshape(-1, 1)
        wr_t = jnp.concatenate(
            [w_router_expert[l].T, w_router_group[l].T,
             jnp.zeros((ROUTER_ROWS - N_EXPERTS - N_GROUPS, d), F32)], axis=0).astype(BF16)
        br = jnp.concatenate(
            [b_router_expert[l], b_router_group[l],
             jnp.zeros((ROUTER_ROWS - N_EXPERTS - N_GROUPS,), F32)]).reshape(ROUTER_ROWS, 1)

        mod3 = _ada(c, w_ada[l], b_ada[l]).reshape(nb, 1, 6 * d)
        conv_p, gm, qn, kvn, krt = _inproj(
            x2, mod3, norm1_g[l].reshape(1, d), w_cat, w_krt, conv_w[l],
            q_a_norm_g[l].reshape(1, -1), kv_a_norm_g[l].reshape(1, -1), seq)
        y_mla = _attn(qn, kvn, krt, cst, wqt, wkvt, gq, gk, nb, seq)
        x1, h2r, eid, rank, cnt, cwt = _oproj(conv_p, gm, y_mla, x2, mod3, w_o[l].astype(BF16),
                                              norm2_g[l].reshape(1, d), wr_t, br, seq)
        te, na, ztile, pos3, n_tiles = _route_plan(eid, rank, cnt, n_tok)
        xs = _dispatch(ztile, na, pos3, h2r, n_tiles)
        yr = _moe(te, na, xs,
                  w_exp_gate[l].reshape(N_EXPERTS, d, D_EXPERT),
                  w_exp_up[l].reshape(N_EXPERTS, d, D_EXPERT),
                  w_exp_down[l].reshape(N_EXPERTS, D_EXPERT, d))
        x2 = _comb(x1, yr, pos3, cwt, mod3, seq)
    return x2.reshape(nb, seq, d)
```

```python
import functools
import math

import jax
import jax.numpy as jnp
from jax import lax
from jax.experimental import pallas as pl
from jax.experimental.pallas import tpu as pltpu

F32 = jnp.float32
BF16 = jnp.bfloat16

D_MODEL = 1024
N_HEADS = 8
QK_NOPE = 128
QK_ROPE = 64
QK_HEAD = QK_NOPE + QK_ROPE
V_HEAD = 128
Q_LORA = 384
KV_LORA = 256
CHUNK = 64
EPS = 1e-6
ROPE_BASE = 10000.0
N_GROUPS = 4
EXPERTS_PER_GROUP = 8
N_EXPERTS = N_GROUPS * EXPERTS_PER_GROUP
D_EXPERT = 256
CONV_K = 3

LANES = 128
SUBLANES = 8
VMEM_LIMIT = 56 * 1024 * 1024

ROW_TILE = 256
Q_TILE = 512
MOE_TILE = 256
ADA_COLS = 1536
PACK_ROWS = D_MODEL // (2 * LANES)
DMA_UNROLL = 8
ROUTER_ROWS = 40


def _sigmoid(v):
    return 1.0 / (1.0 + jnp.exp(-v))


def _dot(a, b):
    return jnp.dot(a, b, preferred_element_type=F32)


def _pack_pair(lo, hi):
    lo_bits = lax.bitcast_convert_type(lo.astype(BF16).astype(F32), jnp.uint32)
    hi_bits = lax.bitcast_convert_type(hi.astype(BF16).astype(F32), jnp.uint32)
    return (lo_bits >> 16) | hi_bits


def _unpack_pair(w):
    lo = lax.bitcast_convert_type(w << 16, F32)
    hi = lax.bitcast_convert_type(w & jnp.uint32(0xFFFF0000), F32)
    return lo, hi


def _store_packed(ref, val, n):
    for j in range(PACK_ROWS):
        c = 2 * j * LANES
        ref[pl.ds(j, n, stride=PACK_ROWS), :] = _pack_pair(val[:, c:c + LANES], val[:, c + LANES:c + 2 * LANES])


def _load_packed(ref, n, offset=0):
    cols = []
    for j in range(PACK_ROWS):
        cols.extend(_unpack_pair(ref[pl.ds(offset + j, n, stride=PACK_ROWS), :]))
    return cols


def _dot_nt(a, b):
    return lax.dot_general(a, b, (((1,), (1,)), ((), ())), preferred_element_type=F32)


def _ada_kernel(c_ref, w_ref, b_ref, o_ref):
    c = c_ref[...]
    act = (c * _sigmoid(c)).astype(BF16)
    o_ref[...] = _dot(act, w_ref[...].astype(BF16)) + b_ref[...]


def _ada(c, w_ada, b_ada):
    nb, d = c.shape
    n = w_ada.shape[1]
    return pl.pallas_call(
        _ada_kernel,
        grid=(n // ADA_COLS,),
        in_specs=[
            pl.BlockSpec((nb, d), lambda j: (0, 0)),
            pl.BlockSpec((d, ADA_COLS), lambda j: (0, j)),
            pl.BlockSpec((1, ADA_COLS), lambda j: (0, j)),
        ],
        out_specs=pl.BlockSpec((nb, ADA_COLS), lambda j: (0, j)),
        out_shape=jax.ShapeDtypeStruct((nb, n), F32),
        compiler_params=pltpu.CompilerParams(
            dimension_semantics=("arbitrary",), vmem_limit_bytes=VMEM_LIMIT),
        name="ada",
    )(c, w_ada, b_ada.reshape(1, n))


_C_ZX = 0
_C_ZB = D_MODEL
_C_ZC = 2 * D_MODEL
_C_GC = 3 * D_MODEL
_C_GM = 4 * D_MODEL
_C_QL = 5 * D_MODEL
_C_KV = _C_QL + Q_LORA
_C_END = _C_KV + KV_LORA


def _inproj_kernel(tiles_per_seq, x_ref, mod_ref, g1_ref, w_ref, wkr_ref, cw_ref, gq_ref, gkv_ref,
                   conv_ref, gm_ref, qn_ref, kvn_ref, krt_ref, ubuf):
    tm = x_ref.shape[0]

    @pl.when(pl.program_id(0) % tiles_per_seq == 0)
    def _():
        ubuf[0:SUBLANES, :] = jnp.zeros((SUBLANES, D_MODEL), F32)

    x = x_ref[...]
    xn = x * lax.rsqrt(jnp.mean(x * x, axis=-1, keepdims=True) + EPS) * g1_ref[...]
    shift = mod_ref[0, :, 0:D_MODEL]
    scale = mod_ref[0, :, D_MODEL:2 * D_MODEL]
    h = (xn * (1.0 + scale) + shift).astype(BF16)

    def proj(lo, hi):
        return _dot(h, w_ref[:, lo:hi])

    u = proj(_C_ZC, _C_GC) * proj(_C_ZX, _C_ZB)
    ubuf[SUBLANES:SUBLANES + tm, :] = u
    conv = (ubuf[SUBLANES - 2:SUBLANES - 2 + tm, :] * cw_ref[0:1, :]
            + ubuf[SUBLANES - 1:SUBLANES - 1 + tm, :] * cw_ref[1:2, :]
            + u * cw_ref[2:3, :])
    ubuf[0:SUBLANES, :] = ubuf[tm:tm + SUBLANES, :]
    y_conv = proj(_C_ZB, _C_ZC) * conv
    conv_ref[...] = (_sigmoid(proj(_C_GC, _C_GM)) * y_conv).astype(BF16)
    gm_ref[...] = _sigmoid(proj(_C_GM, _C_QL)).astype(BF16)

    ql = proj(_C_QL, _C_KV)
    qn_ref[...] = (ql * lax.rsqrt(jnp.mean(ql * ql, axis=-1, keepdims=True) + EPS)
                   * gq_ref[...]).astype(BF16)
    kl = proj(_C_KV, _C_END)
    kvn_ref[...] = (kl * lax.rsqrt(jnp.mean(kl * kl, axis=-1, keepdims=True) + EPS)
                    * gkv_ref[...]).astype(BF16)
    krt_ref[0] = _dot_nt(wkr_ref[...], h)


def _inproj(x2, mod3, g1, w_cat, w_krt, conv_w, gq, gkv, seq):
    t = x2.shape[0]
    nb = t // seq
    tm = ROW_TILE
    tiles_per_seq = seq // tm
    row = lambda i: (i, 0)
    const = lambda i: (0, 0)
    return pl.pallas_call(
        functools.partial(_inproj_kernel, tiles_per_seq),
        grid=(t // tm,),
        in_specs=[
            pl.BlockSpec((tm, D_MODEL), row),
            pl.BlockSpec((1, 1, 6 * D_MODEL), lambda i: (i // tiles_per_seq, 0, 0)),
            pl.BlockSpec((1, D_MODEL), const),
            pl.BlockSpec((D_MODEL, _C_END), const),
            pl.BlockSpec((2 * QK_ROPE, D_MODEL), const),
            pl.BlockSpec((CONV_K, D_MODEL), const),
            pl.BlockSpec((1, Q_LORA), const),
            pl.BlockSpec((1, KV_LORA), const),
        ],
        out_specs=[
            pl.BlockSpec((tm, D_MODEL), row),
            pl.BlockSpec((tm, D_MODEL), row),
            pl.BlockSpec((tm, Q_LORA), row),
            pl.BlockSpec((tm, KV_LORA), row),
            pl.BlockSpec((1, 2 * QK_ROPE, tm),
                         lambda i: (i // tiles_per_seq, 0, i % tiles_per_seq)),
        ],
        out_shape=[
            jax.ShapeDtypeStruct((t, D_MODEL), BF16),
            jax.ShapeDtypeStruct((t, D_MODEL), BF16),
            jax.ShapeDtypeStruct((t, Q_LORA), BF16),
            jax.ShapeDtypeStruct((t, KV_LORA), BF16),
            jax.ShapeDtypeStruct((nb, 2 * QK_ROPE, seq), F32),
        ],
        scratch_shapes=[pltpu.VMEM((tm + SUBLANES, D_MODEL), F32)],
        compiler_params=pltpu.CompilerParams(
            dimension_semantics=("arbitrary",), vmem_limit_bytes=VMEM_LIMIT),
        name="inproj",
    )(x2, mod3, g1, w_cat, w_krt, conv_w, gq, gkv)


def _attn_kernel(qn_ref, kvn_ref, krt_ref, cst_ref, wqt_ref, wkvt_ref, gq_ref, gk_ref,
                 o_ref, qt_s, k_s, vt_s, s_buf0, s_buf1):
    seq = qn_ref.shape[0]
    cos_t = cst_ref[0, 0:QK_ROPE, :]
    sin_t = cst_ref[0, QK_ROPE:, :]

    def normed_rope(nope, r, rr, g, extra_scale):
        ss = jnp.sum(nope * nope, axis=0, keepdims=True) + jnp.sum(r * r, axis=0, keepdims=True)
        scale = lax.rsqrt(ss * (1.0 / QK_HEAD) + EPS) * extra_scale
        rope = r * g[QK_NOPE:QK_HEAD] * cos_t + rr * g[QK_HEAD:] * sin_t
        return (nope * g[0:QK_NOPE] * scale).astype(BF16), (rope * scale).astype(BF16)

    qt = _dot_nt(wqt_ref[0], qn_ref[...])
    q_n, q_r = normed_rope(qt[0:QK_NOPE], qt[QK_NOPE:QK_HEAD], qt[QK_HEAD:], gq_ref[...],
                           QK_HEAD ** -0.5 * math.log2(math.e))
    qt_s[0:QK_NOPE, :] = q_n
    qt_s[QK_NOPE:QK_HEAD, :] = q_r
    qt_s[QK_HEAD:, :] = jnp.zeros((QK_ROPE, seq), BF16)

    kvt = _dot_nt(wkvt_ref[0], kvn_ref[...])
    krt = krt_ref[0]
    k_n, k_r = normed_rope(kvt[0:QK_NOPE], krt[0:QK_ROPE], krt[QK_ROPE:], gk_ref[...], 1.0)
    kt = jnp.concatenate([k_n, k_r, jnp.zeros((QK_ROPE, seq), BF16)], axis=0)
    k_s[...] = kt.T
    vt_s[0:V_HEAD, :] = kvt[QK_NOPE:].astype(BF16)
    vt_s[V_HEAD:, :] = jnp.ones((vt_s.shape[0] - V_HEAD, seq), BF16)

    tq = Q_TILE
    kchunk = lax.broadcasted_iota(jnp.int32, (tq, tq), 0) // CHUNK
    qchunk = lax.broadcasted_iota(jnp.int32, (tq, tq), 1) // CHUNK
    diag_ok = kchunk <= qchunk
    neg = jnp.finfo(F32).min
    nq = seq // tq

    def scores(i):
        q0 = i * tq
        sb = s_buf0 if i % 2 == 0 else s_buf1
        q = qt_s[:, q0:q0 + tq]
        if i > 0:
            sb[0:q0, :] = _dot(k_s[0:q0, :], q)
        sb[q0:q0 + tq, :] = jnp.where(diag_ok, _dot(k_s[q0:q0 + tq, :], q), neg)

    def finish(i):
        q0 = i * tq
        kend = q0 + tq
        sb = s_buf0 if i % 2 == 0 else s_buf1
        m = jnp.max(sb[0:kend, :], axis=0, keepdims=True)
        acc = _dot(vt_s[:, 0:kend], jnp.exp2(sb[0:kend, :] - m).astype(BF16))
        o_t = acc[0:V_HEAD] / acc[V_HEAD:V_HEAD + 1]
        o_ref[q0:q0 + tq, :] = o_t.T.astype(o_ref.dtype)

    scores(nq - 1)
    for i in range(nq - 1, -1, -1):
        if i > 0:
            scores(i - 1)
        finish(i)


def _attn(qn, kvn, krt, cst, wqt, wkvt, gq, gk, nb, seq):
    t = qn.shape[0]
    per_b = lambda b, h: (b, 0)
    per_b3 = lambda b, h: (b, 0, 0)
    per_h = lambda b, h: (h, 0, 0)
    const = lambda b, h: (0, 0)
    qk_rows = QK_NOPE + 2 * QK_ROPE
    return pl.pallas_call(
        _attn_kernel,
        grid=(nb, N_HEADS),
        in_specs=[
            pl.BlockSpec((seq, Q_LORA), per_b),
            pl.BlockSpec((seq, KV_LORA), per_b),
            pl.BlockSpec((1, 2 * QK_ROPE, seq), per_b3),
            pl.BlockSpec((1, 2 * QK_ROPE, seq), per_b3),
            pl.BlockSpec((1, qk_rows, Q_LORA), per_h),
            pl.BlockSpec((1, QK_NOPE + V_HEAD, KV_LORA), per_h),
            pl.BlockSpec((qk_rows, 1), const),
            pl.BlockSpec((qk_rows, 1), const),
        ],
        out_specs=pl.BlockSpec((seq, V_HEAD), lambda b, h: (b, h)),
        out_shape=jax.ShapeDtypeStruct((t, N_HEADS * V_HEAD), BF16),
        scratch_shapes=[
            pltpu.VMEM((qk_rows, seq), BF16),
            pltpu.VMEM((seq, qk_rows), BF16),
            pltpu.VMEM((V_HEAD + 2 * SUBLANES, seq), BF16),
            pltpu.VMEM((seq, Q_TILE), F32),
            pltpu.VMEM((seq, Q_TILE), F32),
        ],
        compiler_params=pltpu.CompilerParams(
            dimension_semantics=("arbitrary", "arbitrary"), vmem_limit_bytes=VMEM_LIMIT),
        name="attn",
    )(qn, kvn, krt, cst, wqt, wkvt, gq, gk)


def _oproj_kernel(conv_ref, gm_ref, y_ref, x_ref, mod_ref, wo_ref, g2_ref, wr_ref, br_ref,
                  x1_ref, h2_ref, eid_ref, rank_ref, cnt_ref, cwt_ref, base):
    tm = x_ref.shape[0]

    @pl.when(pl.program_id(0) == 0)
    def _():
        base[...] = jnp.zeros(base.shape, F32)

    merged = conv_ref[...].astype(F32) + gm_ref[...].astype(F32) * y_ref[...].astype(F32)
    att = _dot(merged.astype(BF16), wo_ref[...])
    gate1 = mod_ref[0, :, 2 * D_MODEL:3 * D_MODEL]
    shift2 = mod_ref[0, :, 3 * D_MODEL:4 * D_MODEL]
    scale2 = mod_ref[0, :, 4 * D_MODEL:5 * D_MODEL]
    x1 = x_ref[...] + gate1 * att
    x1_ref[...] = x1
    xn = x1 * lax.rsqrt(jnp.mean(x1 * x1, axis=-1, keepdims=True) + EPS) * g2_ref[...]
    h2 = xn * (1.0 + scale2) + shift2
    _store_packed(h2_ref, h2, tm)

    lt = _dot_nt(wr_ref[...], h2.astype(BF16)) + br_ref[...]
    gl = [lt[N_EXPERTS + r:N_EXPERTS + r + 1, :] for r in range(N_GROUPS)]
    gmax = jnp.maximum(jnp.maximum(gl[0], gl[1]), jnp.maximum(gl[2], gl[3]))
    gidx = jnp.full(gmax.shape, N_GROUPS - 1, jnp.int32)
    for r in range(N_GROUPS - 2, -1, -1):
        gidx = jnp.where(gl[r] == gmax, r, gidx)
    gsum = jnp.exp(gl[0] - gmax)
    for r in range(1, N_GROUPS):
        gsum = gsum + jnp.exp(gl[r] - gmax)
    p_group = 1.0 / gsum
    es = lt[(N_GROUPS - 1) * EXPERTS_PER_GROUP:N_GROUPS * EXPERTS_PER_GROUP, :]
    for r in range(N_GROUPS - 2, -1, -1):
        es = jnp.where(gidx == r, lt[r * EXPERTS_PER_GROUP:(r + 1) * EXPERTS_PER_GROUP, :], es)
    row = lax.broadcasted_iota(jnp.int32, es.shape, 0)
    m1 = jnp.max(es, axis=0, keepdims=True)
    i1 = jnp.min(jnp.where(es == m1, row, EXPERTS_PER_GROUP), axis=0, keepdims=True)
    es2 = jnp.where(row == i1, -jnp.inf, es)
    m2 = jnp.max(es2, axis=0, keepdims=True)
    i2 = jnp.min(jnp.where(es2 == m2, row, EXPERTS_PER_GROUP), axis=0, keepdims=True)
    e2 = jnp.exp(m2 - m1)
    w1 = p_group / (1.0 + e2)
    w2 = w1 * e2
    eid0 = gidx * EXPERTS_PER_GROUP + i1
    eid1 = gidx * EXPERTS_PER_GROUP + i2
    eid_ref[0:1, :] = eid0
    eid_ref[1:2, :] = eid1
    erow = lax.broadcasted_iota(jnp.int32, (N_EXPERTS, tm), 0)
    oh0 = erow == eid0
    oh1 = erow == eid1
    both = jnp.where(oh0, 1.0, jnp.where(oh1, 1.0, 0.0))
    earlier = (lax.broadcasted_iota(jnp.int32, (tm, tm), 0)
               < lax.broadcasted_iota(jnp.int32, (tm, tm), 1))
    seen = base[...] + _dot(both.astype(BF16), jnp.where(earlier, 1.0, 0.0).astype(BF16))
    rank_ref[0:1, :] = jnp.sum(jnp.where(oh0, seen, 0.0), axis=0, keepdims=True).astype(jnp.int32)
    rank_ref[1:2, :] = jnp.sum(jnp.where(oh1, seen, 0.0), axis=0, keepdims=True).astype(jnp.int32)
    base[...] = base[...] + jnp.sum(both, axis=1, keepdims=True)
    cnt_ref[...] = base[...]
    wrow = lax.broadcasted_iota(jnp.int32, (LANES, tm), 0)
    wmat = jnp.where(wrow == 0, w1, jnp.where(wrow == 1, w2, 0.0))
    cwt_ref[...] = wmat.T


def _oproj(conv_p, gm, y_mla, x2, mod3, wo, g2, wr_t, br, seq):
    t = x2.shape[0]
    tm = ROW_TILE
    tiles_per_seq = seq // tm
    row = lambda i: (i, 0)
    const = lambda i: (0, 0)
    return pl.pallas_call(
        _oproj_kernel,
        grid=(t // tm,),
        in_specs=[
            pl.BlockSpec((tm, D_MODEL), row),
            pl.BlockSpec((tm, D_MODEL), row),
            pl.BlockSpec((tm, D_MODEL), row),
            pl.BlockSpec((tm, D_MODEL), row),
            pl.BlockSpec((1, 1, 6 * D_MODEL), lambda i: (i // tiles_per_seq, 0, 0)),
            pl.BlockSpec((D_MODEL, D_MODEL), const),
            pl.BlockSpec((1, D_MODEL), const),
            pl.BlockSpec((ROUTER_ROWS, D_MODEL), const),
            pl.BlockSpec((ROUTER_ROWS, 1), const),
        ],
        out_specs=[
            pl.BlockSpec((tm, D_MODEL), row),
            pl.BlockSpec((tm * PACK_ROWS, LANES), row),
            pl.BlockSpec((2, tm), lambda i: (0, i)),
            pl.BlockSpec((2, tm), lambda i: (0, i)),
            pl.BlockSpec((N_EXPERTS, 1), const),
            pl.BlockSpec((tm, LANES), row),
        ],
        out_shape=[
            jax.ShapeDtypeStruct((t, D_MODEL), F32),
            jax.ShapeDtypeStruct((t * PACK_ROWS, LANES), jnp.uint32),
            jax.ShapeDtypeStruct((2, t), jnp.int32),
            jax.ShapeDtypeStruct((2, t), jnp.int32),
            jax.ShapeDtypeStruct((N_EXPERTS, 1), F32),
            jax.ShapeDtypeStruct((t, LANES), F32),
        ],
        scratch_shapes=[pltpu.VMEM((N_EXPERTS, 1), F32)],
        compiler_params=pltpu.CompilerParams(
            dimension_semantics=("arbitrary",), vmem_limit_bytes=VMEM_LIMIT),
        name="oproj",
    )(conv_p, gm, y_mla, x2, mod3, wo, g2, wr_t, br)


def _dispatch_kernel(n_tiles, ztile_ref, na_ref, pos_ref, h2_ref, xs_hbm, zbuf, sem):
    tm = h2_ref.shape[0] // PACK_ROWS
    tile_rows = MOE_TILE * PACK_ROWS

    @pl.when(pl.program_id(0) == 0)
    def _():
        zbuf[...] = jnp.zeros(zbuf.shape, zbuf.dtype)

        def zero_tile(tile):
            row = pl.multiple_of(tile * tile_rows, tile_rows)
            return pltpu.make_async_copy(zbuf, xs_hbm.at[pl.ds(row, tile_rows), :], sem)
        for e in range(N_EXPERTS):
            zero_tile(ztile_ref[e]).start()
        for e in range(N_EXPERTS):
            zero_tile(ztile_ref[e]).wait()

        def zero_tail(k, carry):
            cp = zero_tile(na_ref[0] + k)
            cp.start()
            cp.wait()
            return carry
        lax.fori_loop(0, n_tiles - na_ref[0], zero_tail, 0)

    def body(c, carry):
        for k in range(DMA_UNROLL):
            t = c * DMA_UNROLL + k
            src = h2_ref.at[pl.ds(pl.multiple_of(t * PACK_ROWS, PACK_ROWS), PACK_ROWS), :]
            for s in range(2):
                dst_row = pl.multiple_of(pos_ref[0, 0, s * tm + t] * PACK_ROWS, PACK_ROWS)
                pltpu.make_async_copy(src, xs_hbm.at[pl.ds(dst_row, PACK_ROWS), :],
                                      sem).start(priority=s)
        return carry
    lax.fori_loop(0, tm // DMA_UNROLL, body, 0)
    for _ in range(2):
        pltpu.make_async_copy(h2_ref, xs_hbm.at[pl.ds(0, tm * PACK_ROWS), :], sem).wait()


def _dispatch(ztile, na, pos3, h2r, n_tiles):
    tm = pos3.shape[2] // 2
    n_steps = pos3.shape[0]
    grid_spec = pltpu.PrefetchScalarGridSpec(
        num_scalar_prefetch=2,
        grid=(n_steps,),
        in_specs=[
            pl.BlockSpec((1, 1, 2 * tm), lambda i, z, n: (i, 0, 0), memory_space=pltpu.SMEM),
            pl.BlockSpec((tm * PACK_ROWS, LANES), lambda i, z, n: (i, 0)),
        ],
        out_specs=pl.BlockSpec(memory_space=pl.ANY),
        scratch_shapes=[
            pltpu.VMEM((MOE_TILE * PACK_ROWS, LANES), jnp.uint32),
            pltpu.SemaphoreType.DMA(()),
        ],
    )
    return pl.pallas_call(
        functools.partial(_dispatch_kernel, n_tiles),
        grid_spec=grid_spec,
        out_shape=jax.ShapeDtypeStruct((n_tiles * MOE_TILE * PACK_ROWS, LANES), jnp.uint32),
        compiler_params=pltpu.CompilerParams(
            dimension_semantics=("arbitrary",), vmem_limit_bytes=VMEM_LIMIT),
        name="dispatch",
    )(ztile, na, pos3, h2r)


def _moe_kernel(te_ref, na_ref, x_ref, wg_ref, wu_ref, wd_ref, y_ref, wgu_s, wd_s):
    i = pl.program_id(0)
    active = i < na_ref[0]
    new_expert = jnp.logical_or(i == 0, te_ref[i] != te_ref[jnp.maximum(i - 1, 0)])

    @pl.when(jnp.logical_and(active, new_expert))
    def _():
        wgu_s[:, 0:D_EXPERT] = wg_ref[0].astype(BF16)
        wgu_s[:, D_EXPERT:] = wu_ref[0].astype(BF16)
        wd_s[...] = wd_ref[0].astype(BF16)

    @pl.when(active)
    def _():
        x = jnp.concatenate([col.astype(BF16) for col in _load_packed(x_ref, MOE_TILE)], axis=1)
        gu = _dot(x, wgu_s[...])
        g = gu[:, 0:D_EXPERT]
        a = (g * _sigmoid(g)) * gu[:, D_EXPERT:]
        y = _dot(a.astype(BF16), wd_s[...])
        _store_packed(y_ref, y, MOE_TILE)

    @pl.when(jnp.logical_not(active))
    def _():
        y_ref[...] = jnp.zeros(y_ref.shape, y_ref.dtype)


def _moe(te, na, xs, wg, wu, wd):
    nt = te.shape[0]
    tile_rows = MOE_TILE * PACK_ROWS
    tile = lambda i, te_r, na_r: (jnp.minimum(i, na_r[0] - 1), 0)
    wspec = lambda shape: pl.BlockSpec((1,) + shape, lambda i, te_r, na_r: (te_r[i], 0, 0))
    grid_spec = pltpu.PrefetchScalarGridSpec(
        num_scalar_prefetch=2,
        grid=(nt,),
        in_specs=[
            pl.BlockSpec((tile_rows, LANES), tile),
            wspec((D_MODEL, D_EXPERT)),
            wspec((D_MODEL, D_EXPERT)),
            wspec((D_EXPERT, D_MODEL)),
        ],
        out_specs=pl.BlockSpec((tile_rows, LANES), lambda i, te_r, na_r: (i, 0)),
        scratch_shapes=[
            pltpu.VMEM((D_MODEL, 2 * D_EXPERT), BF16),
            pltpu.VMEM((D_EXPERT, D_MODEL), BF16),
        ],
    )
    return pl.pallas_call(
        _moe_kernel,
        grid_spec=grid_spec,
        out_shape=jax.ShapeDtypeStruct(xs.shape, xs.dtype),
        compiler_params=pltpu.CompilerParams(
            dimension_semantics=("arbitrary",), vmem_limit_bytes=VMEM_LIMIT),
        name="moe",
    )(te, na, xs, wg, wu, wd)


def _comb_kernel(pos0_ref, posn_ref, x1_ref, cwt_ref, mod_ref, y_hbm, o_ref, ybuf, sem):
    tm = x1_ref.shape[0]
    i = pl.program_id(0)
    slot = i % 2

    def gather(pos_ref, dst_slot):
        def body(c, carry):
            for k in range(DMA_UNROLL):
                r = c * DMA_UNROLL + k
                src_row = pl.multiple_of(pos_ref[0, 0, r] * PACK_ROWS, PACK_ROWS)
                dst_row = pl.multiple_of(r * PACK_ROWS, PACK_ROWS)
                pltpu.make_async_copy(y_hbm.at[pl.ds(src_row, PACK_ROWS), :],
                                      ybuf.at[dst_slot, pl.ds(dst_row, PACK_ROWS), :],
                                      sem.at[dst_slot]).start(priority=k % 2)
            return carry
        lax.fori_loop(0, 2 * tm // DMA_UNROLL, body, 0)

    @pl.when(i == 0)
    def _():
        gather(pos0_ref, 0)

    @pl.when(i + 1 < pl.num_programs(0))
    def _():
        gather(posn_ref, 1 - slot)

    pltpu.make_async_copy(y_hbm.at[pl.ds(0, 2 * tm * PACK_ROWS), :], ybuf.at[slot],
                          sem.at[slot]).wait()
    c0 = cwt_ref[:, 0:1]
    c1 = cwt_ref[:, 1:2]
    y0 = _load_packed(ybuf.at[slot], tm)
    y1 = _load_packed(ybuf.at[slot], tm, offset=tm * PACK_ROWS)
    for j in range(2 * PACK_ROWS):
        cols = slice(j * LANES, (j + 1) * LANES)
        gate2 = mod_ref[0, :, 5 * D_MODEL + j * LANES:5 * D_MODEL + (j + 1) * LANES]
        o_ref[:, cols] = x1_ref[:, cols] + gate2 * (c0 * y0[j] + c1 * y1[j])


def _comb(x1, yr, pos3, cwt, mod3, seq):
    t = x1.shape[0]
    tm = ROW_TILE
    tiles_per_seq = seq // tm
    n_steps = t // tm
    row = lambda i: (i, 0)
    smem_blk = lambda f: pl.BlockSpec((1, 1, 2 * tm), f, memory_space=pltpu.SMEM)
    return pl.pallas_call(
        _comb_kernel,
        grid=(n_steps,),
        in_specs=[
            smem_blk(lambda i: (0, 0, 0)),
            smem_blk(lambda i: (jnp.minimum(i + 1, n_steps - 1), 0, 0)),
            pl.BlockSpec((tm, D_MODEL), row),
            pl.BlockSpec((tm, LANES), row),
            pl.BlockSpec((1, 1, 6 * D_MODEL), lambda i: (i // tiles_per_seq, 0, 0)),
            pl.BlockSpec(memory_space=pl.ANY),
        ],
        out_specs=pl.BlockSpec((tm, D_MODEL), row),
        out_shape=jax.ShapeDtypeStruct((t, D_MODEL), F32),
        scratch_shapes=[
            pltpu.VMEM((2, 2 * tm * PACK_ROWS, LANES), jnp.uint32),
            pltpu.SemaphoreType.DMA((2,)),
        ],
        compiler_params=pltpu.CompilerParams(
            dimension_semantics=("arbitrary",), vmem_limit_bytes=VMEM_LIMIT),
        name="comb",
    )(pos3, pos3, x1, cwt, mod3, yr)


def _route_plan(eid, rank, cnt, n_tok):
    n_tiles = (2 * n_tok) // MOE_TILE + N_EXPERTS
    experts = jnp.arange(N_EXPERTS, dtype=jnp.int32)
    counts = cnt.reshape(N_EXPERTS).astype(jnp.int32)
    ntile = (counts + MOE_TILE - 1) // MOE_TILE
    tend = jnp.cumsum(ntile)
    tstart = tend - ntile
    n_active = tend[-1]
    tj = jnp.arange(n_tiles, dtype=jnp.int32)
    te_raw = jnp.minimum(jnp.sum((tj[:, None] >= tend[None, :]).astype(jnp.int32), axis=1),
                         N_EXPERTS - 1)
    te_last = jnp.sum(jnp.where(tj == n_active - 1, te_raw, 0))
    te = jnp.where(tj < n_active, te_raw, te_last).astype(jnp.int32)
    first_row = jnp.sum(jnp.where(eid[:, :, None] == experts[None, None, :],
                                  (tstart * MOE_TILE)[None, None, :], 0), axis=-1)
    pos = (first_row + rank).astype(jnp.int32)
    ztile = jnp.maximum(tend - 1, 0).astype(jnp.int32)
    pos3 = pos.reshape(2, n_tok // ROW_TILE, ROW_TILE).transpose(1, 0, 2).reshape(
        n_tok // ROW_TILE, 1, 2 * ROW_TILE)
    return te, n_active.reshape(1).astype(jnp.int32), ztile, pos3, n_tiles


def _rotate_half_cols(w):
    half = QK_ROPE // 2
    return jnp.concatenate([w[..., half:], w[..., :half]], axis=-1)


def kernel(x, c, positions, w_ada, b_ada, norm1_g, w_in, conv_w, q_a_norm_g, w_q_b, kv_a_norm_g, w_kv_b, q_norm_g, k_norm_g, w_o, norm2_g, w_router_group, b_router_group, w_router_expert, b_router_expert, w_exp_gate, w_exp_up, w_exp_down):
    nb, seq, d = x.shape
    depth = w_ada.shape[0]
    n_tok = nb * seq
    assert d == D_MODEL and seq % ROW_TILE == 0 and seq % Q_TILE == 0 and Q_TILE % CHUNK == 0
    assert (2 * n_tok) % MOE_TILE == 0

    inv = ROPE_BASE ** (-jnp.arange(0, QK_ROPE, 2, dtype=F32) / QK_ROPE)
    ang = inv[None, :, None] * positions.astype(F32)[:, None, :]
    cos, sin = jnp.cos(ang), jnp.sin(ang)
    cst = jnp.concatenate([cos, cos, -sin, sin], axis=1)

    x2 = x.reshape(n_tok, d)
    for l in range(depth):
        wi = w_in[l]
        o_q = 3 * D_MODEL
        o_kv = o_q + Q_LORA
        o_kr = o_kv + KV_LORA
        o_gc = o_kr + QK_ROPE
        o_gm = o_gc + D_MODEL
        w_kr = wi[:, o_kr:o_gc]
        w_cat = jnp.concatenate(
            [wi[:, 0:o_q], wi[:, o_gc:o_gm], wi[:, o_gm:], wi[:, o_q:o_kv], wi[:, o_kv:o_kr]],
            axis=1).astype(BF16)
        w_krt = jnp.concatenate([w_kr, _rotate_half_cols(w_kr)], axis=1).T.astype(BF16)
        wq3 = w_q_b[l].reshape(Q_LORA, N_HEADS, QK_HEAD)
        wq = jnp.concatenate([wq3, _rotate_half_cols(wq3[..., QK_NOPE:])], axis=-1)
        wqt = wq.transpose(1, 2, 0).astype(BF16)
        wkvt = w_kv_b[l].reshape(KV_LORA, N_HEADS, QK_NOPE + V_HEAD).transpose(1, 2, 0).astype(BF16)
        gq = jnp.concatenate([q_norm_g[l], _rotate_half_cols(q_norm_g[l][QK_NOPE:])]).reshape(-1, 1)
        gk = jnp.concatenate([k_norm_g[l], _rotate_half_cols(k_norm_g[l][QK_NOPE:])]).reshape(-1, 1)
        wr_t = jnp.concatenate(
            [w_router_expert[l].T, w_router_group[l].T,
             jnp.zeros((ROUTER_ROWS - N_EXPERTS - N_GROUPS, d), F32)], axis=0).astype(BF16)
        br = jnp.concatenate(
            [b_router_expert[l], b_router_group[l],
             jnp.zeros((ROUTER_ROWS - N_EXPERTS - N_GROUPS,), F32)]).reshape(ROUTER_ROWS, 1)

        mod3 = _ada(c, w_ada[l], b_ada[l]).reshape(nb, 1, 6 * d)
        conv_p, gm, qn, kvn, krt = _inproj(
            x2, mod3, norm1_g[l].reshape(1, d), w_cat, w_krt, conv_w[l],
            q_a_norm_g[l].reshape(1, -1), kv_a_norm_g[l].reshape(1, -1), seq)
        y_mla = _attn(qn, kvn, krt, cst, wqt, wkvt, gq, gk, nb, seq)
        x1, h2r, eid, rank, cnt, cwt = _oproj(conv_p, gm, y_mla, x2, mod3, w_o[l].astype(BF16),
                                              norm2_g[l].reshape(1, d), wr_t, br, seq)
        te, na, ztile, pos3, n_tiles = _route_plan(eid, rank, cnt, n_tok)
        xs = _dispatch(ztile, na, pos3, h2r, n_tiles)
        yr = _moe(te, na, xs,
                  w_exp_gate[l].reshape(N_EXPERTS, d, D_EXPERT),
                  w_exp_up[l].reshape(N_EXPERTS, d, D_EXPERT),
                  w_exp_down[l].reshape(N_EXPERTS, D_EXPERT, d))
        x2 = _comb(x1, yr, pos3, cwt, mod3, seq)
    return x2.reshape(nb, seq, d)
```

```python
import functools
import math

import jax
import jax.numpy as jnp
from jax import lax
from jax.experimental import pallas as pl
from jax.experimental.pallas import tpu as pltpu

F32 = jnp.float32
BF16 = jnp.bfloat16

D_MODEL = 1024
N_HEADS = 8
QK_NOPE = 128
QK_ROPE = 64
QK_HEAD = QK_NOPE + QK_ROPE
V_HEAD = 128
Q_LORA = 384
KV_LORA = 256
CHUNK = 64
EPS = 1e-6
ROPE_BASE = 10000.0
N_GROUPS = 4
EXPERTS_PER_GROUP = 8
N_EXPERTS = N_GROUPS * EXPERTS_PER_GROUP
D_EXPERT = 256
CONV_K = 3

LANES = 128
SUBLANES = 8
VMEM_LIMIT = 56 * 1024 * 1024

ROW_TILE = 256
Q_TILE = 512
MOE_TILE = 512
ADA_COLS = 1536
ROWS_PER_TOKEN = D_MODEL // LANES
DMA_UNROLL = 8
ROUTER_ROWS = 40


def _sigmoid(v):
    return 1.0 / (1.0 + jnp.exp(-v))


def _dot(a, b):
    return jnp.dot(a, b, preferred_element_type=F32)


def _store_rows(ref, val, n):
    for j in range(ROWS_PER_TOKEN):
        ref[pl.ds(j, n, stride=ROWS_PER_TOKEN), :] = val[:, j * LANES:(j + 1) * LANES]


def _load_rows(ref, n, offset=0):
    return [ref[pl.ds(offset + j, n, stride=ROWS_PER_TOKEN), :] for j in range(ROWS_PER_TOKEN)]


def _dot_nt(a, b):
    return lax.dot_general(a, b, (((1,), (1,)), ((), ())), preferred_element_type=F32)


def _ada_kernel(c_ref, w_ref, b_ref, o_ref):
    c = c_ref[...]
    act = (c * _sigmoid(c)).astype(BF16)
    o_ref[...] = _dot(act, w_ref[...].astype(BF16)) + b_ref[...]


def _ada(c, w_ada, b_ada):
    nb, d = c.shape
    n = w_ada.shape[1]
    return pl.pallas_call(
        _ada_kernel,
        grid=(n // ADA_COLS,),
        in_specs=[
            pl.BlockSpec((nb, d), lambda j: (0, 0)),
            pl.BlockSpec((d, ADA_COLS), lambda j: (0, j)),
            pl.BlockSpec((1, ADA_COLS), lambda j: (0, j)),
        ],
        out_specs=pl.BlockSpec((nb, ADA_COLS), lambda j: (0, j)),
        out_shape=jax.ShapeDtypeStruct((nb, n), F32),
        compiler_params=pltpu.CompilerParams(
            dimension_semantics=("arbitrary",), vmem_limit_bytes=VMEM_LIMIT),
        name="ada",
    )(c, w_ada, b_ada.reshape(1, n))


_C_ZX = 0
_C_ZB = D_MODEL
_C_ZC = 2 * D_MODEL
_C_GC = 3 * D_MODEL
_C_GM = 4 * D_MODEL
_C_QL = 5 * D_MODEL
_C_KV = _C_QL + Q_LORA
_C_END = _C_KV + KV_LORA


def _inproj_kernel(tiles_per_seq, x_ref, mod_ref, g1_ref, w_ref, wkr_ref, cw_ref, gq_ref, gkv_ref,
                   conv_ref, gm_ref, qn_ref, kvn_ref, krt_ref, ubuf):
    tm = x_ref.shape[0]

    @pl.when(pl.program_id(0) % tiles_per_seq == 0)
    def _():
        ubuf[0:SUBLANES, :] = jnp.zeros((SUBLANES, D_MODEL), F32)

    x = x_ref[...]
    xn = x * lax.rsqrt(jnp.mean(x * x, axis=-1, keepdims=True) + EPS) * g1_ref[...]
    shift = mod_ref[0, :, 0:D_MODEL]
    scale = mod_ref[0, :, D_MODEL:2 * D_MODEL]
    h = (xn * (1.0 + scale) + shift).astype(BF16)

    def proj(lo, hi):
        return _dot(h, w_ref[:, lo:hi])

    u = proj(_C_ZC, _C_GC) * proj(_C_ZX, _C_ZB)
    ubuf[SUBLANES:SUBLANES + tm, :] = u
    conv = (ubuf[SUBLANES - 2:SUBLANES - 2 + tm, :] * cw_ref[0:1, :]
            + ubuf[SUBLANES - 1:SUBLANES - 1 + tm, :] * cw_ref[1:2, :]
            + u * cw_ref[2:3, :])
    ubuf[0:SUBLANES, :] = ubuf[tm:tm + SUBLANES, :]
    y_conv = proj(_C_ZB, _C_ZC) * conv
    conv_ref[...] = (_sigmoid(proj(_C_GC, _C_GM)) * y_conv).astype(BF16)
    gm_ref[...] = _sigmoid(proj(_C_GM, _C_QL)).astype(BF16)

    ql = proj(_C_QL, _C_KV)
    qn_ref[...] = (ql * lax.rsqrt(jnp.mean(ql * ql, axis=-1, keepdims=True) + EPS)
                   * gq_ref[...]).astype(BF16)
    kl = proj(_C_KV, _C_END)
    kvn_ref[...] = (kl * lax.rsqrt(jnp.mean(kl * kl, axis=-1, keepdims=True) + EPS)
                    * gkv_ref[...]).astype(BF16)
    krt_ref[0] = _dot_nt(wkr_ref[...], h)


def _inproj(x2, mod3, g1, w_cat, w_krt, conv_w, gq, gkv, seq):
    t = x2.shape[0]
    nb = t // seq
    tm = ROW_TILE
    tiles_per_seq = seq // tm
    row = lambda i: (i, 0)
    const = lambda i: (0, 0)
    return pl.pallas_call(
        functools.partial(_inproj_kernel, tiles_per_seq),
        grid=(t // tm,),
        in_specs=[
            pl.BlockSpec((tm, D_MODEL), row),
            pl.BlockSpec((1, 1, 6 * D_MODEL), lambda i: (i // tiles_per_seq, 0, 0)),
            pl.BlockSpec((1, D_MODEL), const),
            pl.BlockSpec((D_MODEL, _C_END), const),
            pl.BlockSpec((2 * QK_ROPE, D_MODEL), const),
            pl.BlockSpec((CONV_K, D_MODEL), const),
            pl.BlockSpec((1, Q_LORA), const),
            pl.BlockSpec((1, KV_LORA), const),
        ],
        out_specs=[
            pl.BlockSpec((tm, D_MODEL), row),
            pl.BlockSpec((tm, D_MODEL), row),
            pl.BlockSpec((tm, Q_LORA), row),
            pl.BlockSpec((tm, KV_LORA), row),
            pl.BlockSpec((1, 2 * QK_ROPE, tm),
                         lambda i: (i // tiles_per_seq, 0, i % tiles_per_seq)),
        ],
        out_shape=[
            jax.ShapeDtypeStruct((t, D_MODEL), BF16),
            jax.ShapeDtypeStruct((t, D_MODEL), BF16),
            jax.ShapeDtypeStruct((t, Q_LORA), BF16),
            jax.ShapeDtypeStruct((t, KV_LORA), BF16),
            jax.ShapeDtypeStruct((nb, 2 * QK_ROPE, seq), F32),
        ],
        scratch_shapes=[pltpu.VMEM((tm + SUBLANES, D_MODEL), F32)],
        compiler_params=pltpu.CompilerParams(
            dimension_semantics=("arbitrary",), vmem_limit_bytes=VMEM_LIMIT),
        name="inproj",
    )(x2, mod3, g1, w_cat, w_krt, conv_w, gq, gkv)


def _attn_kernel(qn_ref, kvn_ref, krt_ref, cst_ref, wqt_ref, wkvt_ref, gq_ref, gk_ref,
                 o_ref, qt_s, k_s, vt_s, s_buf0, s_buf1):
    seq = qn_ref.shape[0]
    cos_t = cst_ref[0, 0:QK_ROPE, :]
    sin_t = cst_ref[0, QK_ROPE:, :]

    def normed_rope(nope, r, rr, g, extra_scale):
        ss = jnp.sum(nope * nope, axis=0, keepdims=True) + jnp.sum(r * r, axis=0, keepdims=True)
        scale = lax.rsqrt(ss * (1.0 / QK_HEAD) + EPS) * extra_scale
        rope = r * g[QK_NOPE:QK_HEAD] * cos_t + rr * g[QK_HEAD:] * sin_t
        return (nope * g[0:QK_NOPE] * scale).astype(BF16), (rope * scale).astype(BF16)

    qt = _dot_nt(wqt_ref[0], qn_ref[...])
    q_n, q_r = normed_rope(qt[0:QK_NOPE], qt[QK_NOPE:QK_HEAD], qt[QK_HEAD:], gq_ref[...],
                           QK_HEAD ** -0.5 * math.log2(math.e))
    qt_s[0:QK_NOPE, :] = q_n
    qt_s[QK_NOPE:QK_HEAD, :] = q_r
    qt_s[QK_HEAD:, :] = jnp.zeros((QK_ROPE, seq), BF16)

    kvt = _dot_nt(wkvt_ref[0], kvn_ref[...])
    krt = krt_ref[0]
    k_n, k_r = normed_rope(kvt[0:QK_NOPE], krt[0:QK_ROPE], krt[QK_ROPE:], gk_ref[...], 1.0)
    kt = jnp.concatenate([k_n, k_r, jnp.zeros((QK_ROPE, seq), BF16)], axis=0)
    k_s[...] = kt.T
    vt_s[0:V_HEAD, :] = kvt[QK_NOPE:].astype(BF16)
    vt_s[V_HEAD:, :] = jnp.ones((vt_s.shape[0] - V_HEAD, seq), BF16)

    tq = Q_TILE
    kchunk = lax.broadcasted_iota(jnp.int32, (tq, tq), 0) // CHUNK
    qchunk = lax.broadcasted_iota(jnp.int32, (tq, tq), 1) // CHUNK
    diag_ok = kchunk <= qchunk
    neg = jnp.finfo(F32).min
    nq = seq // tq

    def scores(i):
        q0 = i * tq
        sb = s_buf0 if i % 2 == 0 else s_buf1
        q = qt_s[:, q0:q0 + tq]
        if i > 0:
            sb[0:q0, :] = _dot(k_s[0:q0, :], q)
        sb[q0:q0 + tq, :] = jnp.where(diag_ok, _dot(k_s[q0:q0 + tq, :], q), neg)

    def finish(i):
        q0 = i * tq
        kend = q0 + tq
        sb = s_buf0 if i % 2 == 0 else s_buf1
        m = jnp.max(sb[0:kend, :], axis=0, keepdims=True)
        acc = _dot(vt_s[:, 0:kend], jnp.exp2(sb[0:kend, :] - m).astype(BF16))
        o_t = acc[0:V_HEAD] / acc[V_HEAD:V_HEAD + 1]
        o_ref[q0:q0 + tq, :] = o_t.T.astype(o_ref.dtype)

    scores(nq - 1)
    for i in range(nq - 1, -1, -1):
        if i > 0:
            scores(i - 1)
        finish(i)


def _attn(qn, kvn, krt, cst, wqt, wkvt, gq, gk, nb, seq):
    t = qn.shape[0]
    per_b = lambda b, h: (b, 0)
    per_b3 = lambda b, h: (b, 0, 0)
    per_h = lambda b, h: (h, 0, 0)
    const = lambda b, h: (0, 0)
    qk_rows = QK_NOPE + 2 * QK_ROPE
    return pl.pallas_call(
        _attn_kernel,
        grid=(nb, N_HEADS),
        in_specs=[
            pl.BlockSpec((seq, Q_LORA), per_b),
            pl.BlockSpec((seq, KV_LORA), per_b),
            pl.BlockSpec((1, 2 * QK_ROPE, seq), per_b3),
            pl.BlockSpec((1, 2 * QK_ROPE, seq), per_b3),
            pl.BlockSpec((1, qk_rows, Q_LORA), per_h),
            pl.BlockSpec((1, QK_NOPE + V_HEAD, KV_LORA), per_h),
            pl.BlockSpec((qk_rows, 1), const),
            pl.BlockSpec((qk_rows, 1), const),
        ],
        out_specs=pl.BlockSpec((seq, V_HEAD), lambda b, h: (b, h)),
        out_shape=jax.ShapeDtypeStruct((t, N_HEADS * V_HEAD), BF16),
        scratch_shapes=[
            pltpu.VMEM((qk_rows, seq), BF16),
            pltpu.VMEM((seq, qk_rows), BF16),
            pltpu.VMEM((V_HEAD + 2 * SUBLANES, seq), BF16),
            pltpu.VMEM((seq, Q_TILE), F32),
            pltpu.VMEM((seq, Q_TILE), F32),
        ],
        compiler_params=pltpu.CompilerParams(
            dimension_semantics=("arbitrary", "arbitrary"), vmem_limit_bytes=VMEM_LIMIT),
        name="attn",
    )(qn, kvn, krt, cst, wqt, wkvt, gq, gk)


def _oproj_kernel(conv_ref, gm_ref, y_ref, x_ref, mod_ref, wo_ref, g2_ref, wr_ref, br_ref,
                  x1_ref, h2_ref, eid_ref, rank_ref, cnt_ref, cwt_ref, base):
    tm = x_ref.shape[0]

    @pl.when(pl.program_id(0) == 0)
    def _():
        base[...] = jnp.zeros(base.shape, F32)

    merged = conv_ref[...].astype(F32) + gm_ref[...].astype(F32) * y_ref[...].astype(F32)
    att = _dot(merged.astype(BF16), wo_ref[...])
    gate1 = mod_ref[0, :, 2 * D_MODEL:3 * D_MODEL]
    shift2 = mod_ref[0, :, 3 * D_MODEL:4 * D_MODEL]
    scale2 = mod_ref[0, :, 4 * D_MODEL:5 * D_MODEL]
    x1 = x_ref[...] + gate1 * att
    x1_ref[...] = x1
    xn = x1 * lax.rsqrt(jnp.mean(x1 * x1, axis=-1, keepdims=True) + EPS) * g2_ref[...]
    h2 = xn * (1.0 + scale2) + shift2
    _store_rows(h2_ref, h2, tm)

    lt = _dot_nt(wr_ref[...], h2.astype(BF16)) + br_ref[...]
    gl = [lt[N_EXPERTS + r:N_EXPERTS + r + 1, :] for r in range(N_GROUPS)]
    gmax = jnp.maximum(jnp.maximum(gl[0], gl[1]), jnp.maximum(gl[2], gl[3]))
    gidx = jnp.full(gmax.shape, N_GROUPS - 1, jnp.int32)
    for r in range(N_GROUPS - 2, -1, -1):
        gidx = jnp.where(gl[r] == gmax, r, gidx)
    gsum = jnp.exp(gl[0] - gmax)
    for r in range(1, N_GROUPS):
        gsum = gsum + jnp.exp(gl[r] - gmax)
    p_group = 1.0 / gsum
    es = lt[(N_GROUPS - 1) * EXPERTS_PER_GROUP:N_GROUPS * EXPERTS_PER_GROUP, :]
    for r in range(N_GROUPS - 2, -1, -1):
        es = jnp.where(gidx == r, lt[r * EXPERTS_PER_GROUP:(r + 1) * EXPERTS_PER_GROUP, :], es)
    row = lax.broadcasted_iota(jnp.int32, es.shape, 0)
    m1 = jnp.max(es, axis=0, keepdims=True)
    i1 = jnp.min(jnp.where(es == m1, row, EXPERTS_PER_GROUP), axis=0, keepdims=True)
    es2 = jnp.where(row == i1, -jnp.inf, es)
    m2 = jnp.max(es2, axis=0, keepdims=True)
    i2 = jnp.min(jnp.where(es2 == m2, row, EXPERTS_PER_GROUP), axis=0, keepdims=True)
    e2 = jnp.exp(m2 - m1)
    w1 = p_group / (1.0 + e2)
    w2 = w1 * e2
    eid0 = gidx * EXPERTS_PER_GROUP + i1
    eid1 = gidx * EXPERTS_PER_GROUP + i2
    eid_ref[0:1, :] = eid0
    eid_ref[1:2, :] = eid1
    erow = lax.broadcasted_iota(jnp.int32, (N_EXPERTS, tm), 0)
    oh0 = erow == eid0
    oh1 = erow == eid1
    both = jnp.where(oh0, 1.0, jnp.where(oh1, 1.0, 0.0))
    earlier = (lax.broadcasted_iota(jnp.int32, (tm, tm), 0)
               < lax.broadcasted_iota(jnp.int32, (tm, tm), 1))
    seen = base[...] + _dot(both.astype(BF16), jnp.where(earlier, 1.0, 0.0).astype(BF16))
    rank_ref[0:1, :] = jnp.sum(jnp.where(oh0, seen, 0.0), axis=0, keepdims=True).astype(jnp.int32)
    rank_ref[1:2, :] = jnp.sum(jnp.where(oh1, seen, 0.0), axis=0, keepdims=True).astype(jnp.int32)
    base[...] = base[...] + jnp.sum(both, axis=1, keepdims=True)
    cnt_ref[...] = base[...]
    wrow = lax.broadcasted_iota(jnp.int32, (LANES, tm), 0)
    wmat = jnp.where(wrow == 0, w1, jnp.where(wrow == 1, w2, 0.0))
    cwt_ref[...] = wmat.T


def _oproj(conv_p, gm, y_mla, x2, mod3, wo, g2, wr_t, br, seq):
    t = x2.shape[0]
    tm = ROW_TILE
    tiles_per_seq = seq // tm
    row = lambda i: (i, 0)
    const = lambda i: (0, 0)
    return pl.pallas_call(
        _oproj_kernel,
        grid=(t // tm,),
        in_specs=[
            pl.BlockSpec((tm, D_MODEL), row),
            pl.BlockSpec((tm, D_MODEL), row),
            pl.BlockSpec((tm, D_MODEL), row),
            pl.BlockSpec((tm, D_MODEL), row),
            pl.BlockSpec((1, 1, 6 * D_MODEL), lambda i: (i // tiles_per_seq, 0, 0)),
            pl.BlockSpec((D_MODEL, D_MODEL), const),
            pl.BlockSpec((1, D_MODEL), const),
            pl.BlockSpec((ROUTER_ROWS, D_MODEL), const),
            pl.BlockSpec((ROUTER_ROWS, 1), const),
        ],
        out_specs=[
            pl.BlockSpec((tm, D_MODEL), row),
            pl.BlockSpec((tm * ROWS_PER_TOKEN, LANES), row),
            pl.BlockSpec((2, tm), lambda i: (0, i)),
            pl.BlockSpec((2, tm), lambda i: (0, i)),
            pl.BlockSpec((N_EXPERTS, 1), const),
            pl.BlockSpec((tm, LANES), row),
        ],
        out_shape=[
            jax.ShapeDtypeStruct((t, D_MODEL), F32),
            jax.ShapeDtypeStruct((t * ROWS_PER_TOKEN, LANES), F32),
            jax.ShapeDtypeStruct((2, t), jnp.int32),
            jax.ShapeDtypeStruct((2, t), jnp.int32),
            jax.ShapeDtypeStruct((N_EXPERTS, 1), F32),
            jax.ShapeDtypeStruct((t, LANES), F32),
        ],
        scratch_shapes=[pltpu.VMEM((N_EXPERTS, 1), F32)],
        compiler_params=pltpu.CompilerParams(
            dimension_semantics=("arbitrary",), vmem_limit_bytes=VMEM_LIMIT),
        name="oproj",
    )(conv_p, gm, y_mla, x2, mod3, wo, g2, wr_t, br)


def _dispatch_kernel(n_tiles, ztile_ref, na_ref, pos_ref, h2_ref, xs_hbm, zbuf, sem):
    tm = h2_ref.shape[0] // ROWS_PER_TOKEN
    tile_rows = MOE_TILE * ROWS_PER_TOKEN

    @pl.when(pl.program_id(0) == 0)
    def _():
        zbuf[...] = jnp.zeros(zbuf.shape, zbuf.dtype)

        def zero_tile(tile):
            row = pl.multiple_of(tile * tile_rows, tile_rows)
            return pltpu.make_async_copy(zbuf, xs_hbm.at[pl.ds(row, tile_rows), :], sem)
        for e in range(N_EXPERTS):
            zero_tile(ztile_ref[e]).start()
        for e in range(N_EXPERTS):
            zero_tile(ztile_ref[e]).wait()

        def zero_tail(k, carry):
            cp = zero_tile(na_ref[0] + k)
            cp.start()
            cp.wait()
            return carry
        lax.fori_loop(0, n_tiles - na_ref[0], zero_tail, 0)

    def body(c, carry):
        for k in range(DMA_UNROLL):
            t = c * DMA_UNROLL + k
            src = h2_ref.at[pl.ds(pl.multiple_of(t * ROWS_PER_TOKEN, ROWS_PER_TOKEN), ROWS_PER_TOKEN), :]
            for s in range(2):
                dst_row = pl.multiple_of(pos_ref[0, 0, s * tm + t] * ROWS_PER_TOKEN, ROWS_PER_TOKEN)
                pltpu.make_async_copy(src, xs_hbm.at[pl.ds(dst_row, ROWS_PER_TOKEN), :],
                                      sem).start(priority=s)
        return carry
    lax.fori_loop(0, tm // DMA_UNROLL, body, 0)
    for _ in range(2):
        pltpu.make_async_copy(h2_ref, xs_hbm.at[pl.ds(0, tm * ROWS_PER_TOKEN), :], sem).wait()


def _dispatch(ztile, na, pos3, h2r, n_tiles):
    tm = pos3.shape[2] // 2
    n_steps = pos3.shape[0]
    grid_spec = pltpu.PrefetchScalarGridSpec(
        num_scalar_prefetch=2,
        grid=(n_steps,),
        in_specs=[
            pl.BlockSpec((1, 1, 2 * tm), lambda i, z, n: (i, 0, 0), memory_space=pltpu.SMEM),
            pl.BlockSpec((tm * ROWS_PER_TOKEN, LANES), lambda i, z, n: (i, 0)),
        ],
        out_specs=pl.BlockSpec(memory_space=pl.ANY),
        scratch_shapes=[
            pltpu.VMEM((MOE_TILE * ROWS_PER_TOKEN, LANES), F32),
            pltpu.SemaphoreType.DMA(()),
        ],
    )
    return pl.pallas_call(
        functools.partial(_dispatch_kernel, n_tiles),
        grid_spec=grid_spec,
        out_shape=jax.ShapeDtypeStruct((n_tiles * MOE_TILE * ROWS_PER_TOKEN, LANES), F32),
        compiler_params=pltpu.CompilerParams(
            dimension_semantics=("arbitrary",), vmem_limit_bytes=VMEM_LIMIT),
        name="dispatch",
    )(ztile, na, pos3, h2r)


def _moe_kernel(te_ref, na_ref, x_ref, wg_ref, wu_ref, wd_ref, y_ref, wgu_s, wd_s):
    i = pl.program_id(0)
    active = i < na_ref[0]
    new_expert = jnp.logical_or(i == 0, te_ref[i] != te_ref[jnp.maximum(i - 1, 0)])

    @pl.when(jnp.logical_and(active, new_expert))
    def _():
        wgu_s[:, 0:D_EXPERT] = wg_ref[0].astype(BF16)
        wgu_s[:, D_EXPERT:] = wu_ref[0].astype(BF16)
        wd_s[...] = wd_ref[0].astype(BF16)

    @pl.when(active)
    def _():
        x = jnp.concatenate([col.astype(BF16) for col in _load_rows(x_ref, MOE_TILE)], axis=1)
        gu = _dot(x, wgu_s[...])
        g = gu[:, 0:D_EXPERT]
        a = (g * _sigmoid(g)) * gu[:, D_EXPERT:]
        y = _dot(a.astype(BF16), wd_s[...])
        _store_rows(y_ref, y, MOE_TILE)

    @pl.when(jnp.logical_not(active))
    def _():
        y_ref[...] = jnp.zeros(y_ref.shape, y_ref.dtype)


def _moe(te, na, xs, wg, wu, wd):
    nt = te.shape[0]
    tile_rows = MOE_TILE * ROWS_PER_TOKEN
    tile = lambda i, te_r, na_r: (jnp.minimum(i, na_r[0] - 1), 0)
    wspec = lambda shape: pl.BlockSpec((1,) + shape, lambda i, te_r, na_r: (te_r[i], 0, 0))
    grid_spec = pltpu.PrefetchScalarGridSpec(
        num_scalar_prefetch=2,
        grid=(nt,),
        in_specs=[
            pl.BlockSpec((tile_rows, LANES), tile),
            wspec((D_MODEL, D_EXPERT)),
            wspec((D_MODEL, D_EXPERT)),
            wspec((D_EXPERT, D_MODEL)),
        ],
        out_specs=pl.BlockSpec((tile_rows, LANES), lambda i, te_r, na_r: (i, 0)),
        scratch_shapes=[
            pltpu.VMEM((D_MODEL, 2 * D_EXPERT), BF16),
            pltpu.VMEM((D_EXPERT, D_MODEL), BF16),
        ],
    )
    return pl.pallas_call(
        _moe_kernel,
        grid_spec=grid_spec,
        out_shape=jax.ShapeDtypeStruct(xs.shape, xs.dtype),
        compiler_params=pltpu.CompilerParams(
            dimension_semantics=("arbitrary",), vmem_limit_bytes=VMEM_LIMIT),
        name="moe",
    )(te, na, xs, wg, wu, wd)


def _comb_kernel(pos0_ref, posn_ref, x1_ref, cwt_ref, mod_ref, y_hbm, o_ref, ybuf, sem):
    tm = x1_ref.shape[0]
    i = pl.program_id(0)
    slot = i % 2

    def gather(pos_ref, dst_slot):
        def body(c, carry):
            for k in range(DMA_UNROLL):
                r = c * DMA_UNROLL + k
                src_row = pl.multiple_of(pos_ref[0, 0, r] * ROWS_PER_TOKEN, ROWS_PER_TOKEN)
                dst_row = pl.multiple_of(r * ROWS_PER_TOKEN, ROWS_PER_TOKEN)
                pltpu.make_async_copy(y_hbm.at[pl.ds(src_row, ROWS_PER_TOKEN), :],
                                      ybuf.at[dst_slot, pl.ds(dst_row, ROWS_PER_TOKEN), :],
                                      sem.at[dst_slot]).start(priority=k % 2)
            return carry
        lax.fori_loop(0, 2 * tm // DMA_UNROLL, body, 0)

    @pl.when(i == 0)
    def _():
        gather(pos0_ref, 0)

    @pl.when(i + 1 < pl.num_programs(0))
    def _():
        gather(posn_ref, 1 - slot)

    pltpu.make_async_copy(y_hbm.at[pl.ds(0, 2 * tm * ROWS_PER_TOKEN), :], ybuf.at[slot],
                          sem.at[slot]).wait()
    c0 = cwt_ref[:, 0:1]
    c1 = cwt_ref[:, 1:2]
    y0 = _load_rows(ybuf.at[slot], tm)
    y1 = _load_rows(ybuf.at[slot], tm, offset=tm * ROWS_PER_TOKEN)
    for j in range(ROWS_PER_TOKEN):
        cols = slice(j * LANES, (j + 1) * LANES)
        gate2 = mod_ref[0, :, 5 * D_MODEL + j * LANES:5 * D_MODEL + (j + 1) * LANES]
        o_ref[:, cols] = x1_ref[:, cols] + gate2 * (c0 * y0[j] + c1 * y1[j])


def _comb(x1, yr, pos3, cwt, mod3, seq):
    t = x1.shape[0]
    tm = ROW_TILE
    tiles_per_seq = seq // tm
    n_steps = t // tm
    row = lambda i: (i, 0)
    smem_blk = lambda f: pl.BlockSpec((1, 1, 2 * tm), f, memory_space=pltpu.SMEM)
    return pl.pallas_call(
        _comb_kernel,
        grid=(n_steps,),
        in_specs=[
            smem_blk(lambda i: (0, 0, 0)),
            smem_blk(lambda i: (jnp.minimum(i + 1, n_steps - 1), 0, 0)),
            pl.BlockSpec((tm, D_MODEL), row),
            pl.BlockSpec((tm, LANES), row),
            pl.BlockSpec((1, 1, 6 * D_MODEL), lambda i: (i // tiles_per_seq, 0, 0)),
            pl.BlockSpec(memory_space=pl.ANY),
        ],
        out_specs=pl.BlockSpec((tm, D_MODEL), row),
        out_shape=jax.ShapeDtypeStruct((t, D_MODEL), F32),
        scratch_shapes=[
            pltpu.VMEM((2, 2 * tm * ROWS_PER_TOKEN, LANES), F32),
            pltpu.SemaphoreType.DMA((2,)),
        ],
        compiler_params=pltpu.CompilerParams(
            dimension_semantics=("arbitrary",), vmem_limit_bytes=VMEM_LIMIT),
        name="comb",
    )(pos3, pos3, x1, cwt, mod3, yr)


def _route_plan(eid, rank, cnt, n_tok):
    n_tiles = (2 * n_tok) // MOE_TILE + N_EXPERTS
    experts = jnp.arange(N_EXPERTS, dtype=jnp.int32)
    counts = cnt.reshape(N_EXPERTS).astype(jnp.int32)
    ntile = (counts + MOE_TILE - 1) // MOE_TILE
    tend = jnp.cumsum(ntile)
    tstart = tend - ntile
    n_active = tend[-1]
    tj = jnp.arange(n_tiles, dtype=jnp.int32)
    te_raw = jnp.minimum(jnp.sum((tj[:, None] >= tend[None, :]).astype(jnp.int32), axis=1),
                         N_EXPERTS - 1)
    te_last = jnp.sum(jnp.where(tj == n_active - 1, te_raw, 0))
    te = jnp.where(tj < n_active, te_raw, te_last).astype(jnp.int32)
    first_row = jnp.sum(jnp.where(eid[:, :, None] == experts[None, None, :],
                                  (tstart * MOE_TILE)[None, None, :], 0), axis=-1)
    pos = (first_row + rank).astype(jnp.int32)
    ztile = jnp.maximum(tend - 1, 0).astype(jnp.int32)
    pos3 = pos.reshape(2, n_tok // ROW_TILE, ROW_TILE).transpose(1, 0, 2).reshape(
        n_tok // ROW_TILE, 1, 2 * ROW_TILE)
    return te, n_active.reshape(1).astype(jnp.int32), ztile, pos3, n_tiles


def _rotate_half_cols(w):
    half = QK_ROPE // 2
    return jnp.concatenate([w[..., half:], w[..., :half]], axis=-1)


def kernel(x, c, positions, w_ada, b_ada, norm1_g, w_in, conv_w, q_a_norm_g, w_q_b, kv_a_norm_g, w_kv_b, q_norm_g, k_norm_g, w_o, norm2_g, w_router_group, b_router_group, w_router_expert, b_router_expert, w_exp_gate, w_exp_up, w_exp_down):
    nb, seq, d = x.shape
    depth = w_ada.shape[0]
    n_tok = nb * seq
    assert d == D_MODEL and seq % ROW_TILE == 0 and seq % Q_TILE == 0 and Q_TILE % CHUNK == 0
    assert (2 * n_tok) % MOE_TILE == 0

    inv = ROPE_BASE ** (-jnp.arange(0, QK_ROPE, 2, dtype=F32) / QK_ROPE)
    ang = inv[None, :, None] * positions.astype(F32)[:, None, :]
    cos, sin = jnp.cos(ang), jnp.sin(ang)
    cst = jnp.concatenate([cos, cos, -sin, sin], axis=1)

    x2 = x.reshape(n_tok, d)
    for l in range(depth):
        wi = w_in[l]
        o_q = 3 * D_MODEL
        o_kv = o_q + Q_LORA
        o_kr = o_kv + KV_LORA
        o_gc = o_kr + QK_ROPE
        o_gm = o_gc + D_MODEL
        w_kr = wi[:, o_kr:o_gc]
        w_cat = jnp.concatenate(
            [wi[:, 0:o_q], wi[:, o_gc:o_gm], wi[:, o_gm:], wi[:, o_q:o_kv], wi[:, o_kv:o_kr]],
            axis=1).astype(BF16)
        w_krt = jnp.concatenate([w_kr, _rotate_half_cols(w_kr)], axis=1).T.astype(BF16)
        wq3 = w_q_b[l].reshape(Q_LORA, N_HEADS, QK_HEAD)
        wq = jnp.concatenate([wq3, _rotate_half_cols(wq3[..., QK_NOPE:])], axis=-1)
        wqt = wq.transpose(1, 2, 0).astype(BF16)
        wkvt = w_kv_b[l].reshape(KV_LORA, N_HEADS, QK_NOPE + V_HEAD).transpose(1, 2, 0).astype(BF16)
        gq = jnp.concatenate([q_norm_g[l], _rotate_half_cols(q_norm_g[l][QK_NOPE:])]).reshape(-1, 1)
        gk = jnp.concatenate([k_norm_g[l], _rotate_half_cols(k_norm_g[l][QK_NOPE:])]).reshape(-1, 1)
        wr_t = jnp.concatenate(
            [w_router_expert[l].T, w_router_group[l].T,
             jnp.zeros((ROUTER_ROWS - N_EXPERTS - N_GROUPS, d), F32)], axis=0).astype(BF16)
        br = jnp.concatenate(
            [b_router_expert[l], b_router_group[l],
             jnp.zeros((ROUTER_ROWS - N_EXPERTS - N_GROUPS,), F32)]).reshape(ROUTER_ROWS, 1)

        mod3 = _ada(c, w_ada[l], b_ada[l]).reshape(nb, 1, 6 * d)
        conv_p, gm, qn, kvn, krt = _inproj(
            x2, mod3, norm1_g[l].reshape(1, d), w_cat, w_krt, conv_w[l],
            q_a_norm_g[l].reshape(1, -1), kv_a_norm_g[l].reshape(1, -1), seq)
        y_mla = _attn(qn, kvn, krt, cst, wqt, wkvt, gq, gk, nb, seq)
        x1, h2r, eid, rank, cnt, cwt = _oproj(conv_p, gm, y_mla, x2, mod3, w_o[l].astype(BF16),
                                              norm2_g[l].reshape(1, d), wr_t, br, seq)
        te, na, ztile, pos3, n_tiles = _route_plan(eid, rank, cnt, n_tok)
        xs = _dispatch(ztile, na, pos3, h2r, n_tiles)
        yr = _moe(te, na, xs,
                  w_exp_gate[l].reshape(N_EXPERTS, d, D_EXPERT),
                  w_exp_up[l].reshape(N_EXPERTS, d, D_EXPERT),
                  w_exp_down[l].reshape(N_EXPERTS, D_EXPERT, d))
        x2 = _comb(x1, yr, pos3, cwt, mod3, seq)
    return x2.reshape(nb, seq, d)
```

```python
import functools
import math

import jax
import jax.numpy as jnp
from jax import lax
from jax.experimental import pallas as pl
from jax.experimental.pallas import tpu as pltpu

F32 = jnp.float32
BF16 = jnp.bfloat16

D_MODEL = 1024
N_HEADS = 8
QK_NOPE = 128
QK_ROPE = 64
QK_HEAD = QK_NOPE + QK_ROPE
V_HEAD = 128
Q_LORA = 384
KV_LORA = 256
CHUNK = 64
EPS = 1e-6
ROPE_BASE = 10000.0
N_GROUPS = 4
EXPERTS_PER_GROUP = 8
N_EXPERTS = N_GROUPS * EXPERTS_PER_GROUP
D_EXPERT = 256
CONV_K = 3

LANES = 128
SUBLANES = 8
VMEM_LIMIT = 56 * 1024 * 1024

ROW_TILE = 512
Q_TILE = 512
MOE_TILE = 512
ADA_COLS = 1536
ROWS_PER_TOKEN = D_MODEL // LANES
DMA_UNROLL = 8
ROUTER_ROWS = 40


def _sigmoid(v):
    return 1.0 / (1.0 + jnp.exp(-v))


def _dot(a, b):
    return jnp.dot(a, b, preferred_element_type=F32)


def _store_rows(ref, val, n):
    for j in range(ROWS_PER_TOKEN):
        ref[pl.ds(j, n, stride=ROWS_PER_TOKEN), :] = val[:, j * LANES:(j + 1) * LANES]


def _load_rows(ref, n, offset=0):
    return [ref[pl.ds(offset + j, n, stride=ROWS_PER_TOKEN), :] for j in range(ROWS_PER_TOKEN)]


def _dot_nt(a, b):
    return lax.dot_general(a, b, (((1,), (1,)), ((), ())), preferred_element_type=F32)


def _ada_kernel(c_ref, w_ref, b_ref, o_ref):
    c = c_ref[...]
    act = (c * _sigmoid(c)).astype(BF16)
    o_ref[...] = _dot(act, w_ref[...].astype(BF16)) + b_ref[...]


def _ada(c, w_ada, b_ada):
    nb, d = c.shape
    n = w_ada.shape[1]
    return pl.pallas_call(
        _ada_kernel,
        grid=(n // ADA_COLS,),
        in_specs=[
            pl.BlockSpec((nb, d), lambda j: (0, 0)),
            pl.BlockSpec((d, ADA_COLS), lambda j: (0, j)),
            pl.BlockSpec((1, ADA_COLS), lambda j: (0, j)),
        ],
        out_specs=pl.BlockSpec((nb, ADA_COLS), lambda j: (0, j)),
        out_shape=jax.ShapeDtypeStruct((nb, n), F32),
        compiler_params=pltpu.CompilerParams(
            dimension_semantics=("arbitrary",), vmem_limit_bytes=VMEM_LIMIT),
        name="ada",
    )(c, w_ada, b_ada.reshape(1, n))


_C_ZX = 0
_C_ZB = D_MODEL
_C_ZC = 2 * D_MODEL
_C_GC = 3 * D_MODEL
_C_GM = 4 * D_MODEL
_C_QL = 5 * D_MODEL
_C_KV = _C_QL + Q_LORA
_C_END = _C_KV + KV_LORA


def _inproj_kernel(tiles_per_seq, x_ref, mod_ref, g1_ref, w_ref, wkr_ref, cw_ref, gq_ref, gkv_ref,
                   conv_ref, gm_ref, qn_ref, kvn_ref, krt_ref, ubuf):
    tm = x_ref.shape[0]

    @pl.when(pl.program_id(0) % tiles_per_seq == 0)
    def _():
        ubuf[0:SUBLANES, :] = jnp.zeros((SUBLANES, D_MODEL), F32)

    x = x_ref[...]
    xn = x * lax.rsqrt(jnp.mean(x * x, axis=-1, keepdims=True) + EPS) * g1_ref[...]
    shift = mod_ref[0, :, 0:D_MODEL]
    scale = mod_ref[0, :, D_MODEL:2 * D_MODEL]
    h = (xn * (1.0 + scale) + shift).astype(BF16)

    def proj(lo, hi):
        return _dot(h, w_ref[:, lo:hi])

    u = proj(_C_ZC, _C_GC) * proj(_C_ZX, _C_ZB)
    ubuf[SUBLANES:SUBLANES + tm, :] = u
    conv = (ubuf[SUBLANES - 2:SUBLANES - 2 + tm, :] * cw_ref[0:1, :]
            + ubuf[SUBLANES - 1:SUBLANES - 1 + tm, :] * cw_ref[1:2, :]
            + u * cw_ref[2:3, :])
    ubuf[0:SUBLANES, :] = ubuf[tm:tm + SUBLANES, :]
    y_conv = proj(_C_ZB, _C_ZC) * conv
    conv_ref[...] = (_sigmoid(proj(_C_GC, _C_GM)) * y_conv).astype(BF16)
    gm_ref[...] = _sigmoid(proj(_C_GM, _C_QL)).astype(BF16)

    ql = proj(_C_QL, _C_KV)
    qn_ref[...] = (ql * lax.rsqrt(jnp.mean(ql * ql, axis=-1, keepdims=True) + EPS)
                   * gq_ref[...]).astype(BF16)
    kl = proj(_C_KV, _C_END)
    kvn_ref[...] = (kl * lax.rsqrt(jnp.mean(kl * kl, axis=-1, keepdims=True) + EPS)
                    * gkv_ref[...]).astype(BF16)
    krt_ref[0] = _dot_nt(wkr_ref[...], h)


def _inproj(x2, mod3, g1, w_cat, w_krt, conv_w, gq, gkv, seq):
    t = x2.shape[0]
    nb = t // seq
    tm = ROW_TILE
    tiles_per_seq = seq // tm
    row = lambda i: (i, 0)
    const = lambda i: (0, 0)
    return pl.pallas_call(
        functools.partial(_inproj_kernel, tiles_per_seq),
        grid=(t // tm,),
        in_specs=[
            pl.BlockSpec((tm, D_MODEL), row),
            pl.BlockSpec((1, 1, 6 * D_MODEL), lambda i: (i // tiles_per_seq, 0, 0)),
            pl.BlockSpec((1, D_MODEL), const),
            pl.BlockSpec((D_MODEL, _C_END), const),
            pl.BlockSpec((2 * QK_ROPE, D_MODEL), const),
            pl.BlockSpec((CONV_K, D_MODEL), const),
            pl.BlockSpec((1, Q_LORA), const),
            pl.BlockSpec((1, KV_LORA), const),
        ],
        out_specs=[
            pl.BlockSpec((tm, D_MODEL), row),
            pl.BlockSpec((tm, D_MODEL), row),
            pl.BlockSpec((tm, Q_LORA), row),
            pl.BlockSpec((tm, KV_LORA), row),
            pl.BlockSpec((1, 2 * QK_ROPE, tm),
                         lambda i: (i // tiles_per_seq, 0, i % tiles_per_seq)),
        ],
        out_shape=[
            jax.ShapeDtypeStruct((t, D_MODEL), BF16),
            jax.ShapeDtypeStruct((t, D_MODEL), BF16),
            jax.ShapeDtypeStruct((t, Q_LORA), BF16),
            jax.ShapeDtypeStruct((t, KV_LORA), BF16),
            jax.ShapeDtypeStruct((nb, 2 * QK_ROPE, seq), F32),
        ],
        scratch_shapes=[pltpu.VMEM((tm + SUBLANES, D_MODEL), F32)],
        compiler_params=pltpu.CompilerParams(
            dimension_semantics=("arbitrary",), vmem_limit_bytes=VMEM_LIMIT),
        name="inproj",
    )(x2, mod3, g1, w_cat, w_krt, conv_w, gq, gkv)


def _attn_kernel(qn_ref, kvn_ref, krt_ref, cst_ref, wqt_ref, wkvt_ref, gq_ref, gk_ref,
                 o_ref, qt_s, k_s, vt_s, s_buf0, s_buf1):
    seq = qn_ref.shape[0]
    cos_t = cst_ref[0, 0:QK_ROPE, :]
    sin_t = cst_ref[0, QK_ROPE:, :]

    def normed_rope(nope, r, rr, g, extra_scale):
        ss = jnp.sum(nope * nope, axis=0, keepdims=True) + jnp.sum(r * r, axis=0, keepdims=True)
        scale = lax.rsqrt(ss * (1.0 / QK_HEAD) + EPS) * extra_scale
        rope = r * g[QK_NOPE:QK_HEAD] * cos_t + rr * g[QK_HEAD:] * sin_t
        return (nope * g[0:QK_NOPE] * scale).astype(BF16), (rope * scale).astype(BF16)

    qt = _dot_nt(wqt_ref[0], qn_ref[...])
    q_n, q_r = normed_rope(qt[0:QK_NOPE], qt[QK_NOPE:QK_HEAD], qt[QK_HEAD:], gq_ref[...],
                           QK_HEAD ** -0.5 * math.log2(math.e))
    qt_s[0:QK_NOPE, :] = q_n
    qt_s[QK_NOPE:QK_HEAD, :] = q_r
    qt_s[QK_HEAD:, :] = jnp.zeros((QK_ROPE, seq), BF16)

    kvt = _dot_nt(wkvt_ref[0], kvn_ref[...])
    krt = krt_ref[0]
    k_n, k_r = normed_rope(kvt[0:QK_NOPE], krt[0:QK_ROPE], krt[QK_ROPE:], gk_ref[...], 1.0)
    kt = jnp.concatenate([k_n, k_r, jnp.zeros((QK_ROPE, seq), BF16)], axis=0)
    k_s[...] = kt.T
    vt_s[0:V_HEAD, :] = kvt[QK_NOPE:].astype(BF16)
    vt_s[V_HEAD:, :] = jnp.ones((vt_s.shape[0] - V_HEAD, seq), BF16)

    tq = Q_TILE
    kchunk = lax.broadcasted_iota(jnp.int32, (tq, tq), 0) // CHUNK
    qchunk = lax.broadcasted_iota(jnp.int32, (tq, tq), 1) // CHUNK
    diag_ok = kchunk <= qchunk
    neg = jnp.finfo(F32).min
    nq = seq // tq

    def scores(i):
        q0 = i * tq
        sb = s_buf0 if i % 2 == 0 else s_buf1
        q = qt_s[:, q0:q0 + tq]
        if i > 0:
            sb[0:q0, :] = _dot(k_s[0:q0, :], q)
        sb[q0:q0 + tq, :] = jnp.where(diag_ok, _dot(k_s[q0:q0 + tq, :], q), neg)

    def finish(i):
        q0 = i * tq
        kend = q0 + tq
        sb = s_buf0 if i % 2 == 0 else s_buf1
        m = jnp.max(sb[0:kend, :], axis=0, keepdims=True)
        acc = _dot(vt_s[:, 0:kend], jnp.exp2(sb[0:kend, :] - m).astype(BF16))
        o_t = acc[0:V_HEAD] / acc[V_HEAD:V_HEAD + 1]
        o_ref[q0:q0 + tq, :] = o_t.T.astype(o_ref.dtype)

    scores(nq - 1)
    for i in range(nq - 1, -1, -1):
        if i > 0:
            scores(i - 1)
        finish(i)


def _attn(qn, kvn, krt, cst, wqt, wkvt, gq, gk, nb, seq):
    t = qn.shape[0]
    per_b = lambda b, h: (b, 0)
    per_b3 = lambda b, h: (b, 0, 0)
    per_h = lambda b, h: (h, 0, 0)
    const = lambda b, h: (0, 0)
    qk_rows = QK_NOPE + 2 * QK_ROPE
    return pl.pallas_call(
        _attn_kernel,
        grid=(nb, N_HEADS),
        in_specs=[
            pl.BlockSpec((seq, Q_LORA), per_b),
            pl.BlockSpec((seq, KV_LORA), per_b),
            pl.BlockSpec((1, 2 * QK_ROPE, seq), per_b3),
            pl.BlockSpec((1, 2 * QK_ROPE, seq), per_b3),
            pl.BlockSpec((1, qk_rows, Q_LORA), per_h),
            pl.BlockSpec((1, QK_NOPE + V_HEAD, KV_LORA), per_h),
            pl.BlockSpec((qk_rows, 1), const),
            pl.BlockSpec((qk_rows, 1), const),
        ],
        out_specs=pl.BlockSpec((seq, V_HEAD), lambda b, h: (b, h)),
        out_shape=jax.ShapeDtypeStruct((t, N_HEADS * V_HEAD), BF16),
        scratch_shapes=[
            pltpu.VMEM((qk_rows, seq), BF16),
            pltpu.VMEM((seq, qk_rows), BF16),
            pltpu.VMEM((V_HEAD + 2 * SUBLANES, seq), BF16),
            pltpu.VMEM((seq, Q_TILE), F32),
            pltpu.VMEM((seq, Q_TILE), F32),
        ],
        compiler_params=pltpu.CompilerParams(
            dimension_semantics=("arbitrary", "arbitrary"), vmem_limit_bytes=VMEM_LIMIT),
        name="attn",
    )(qn, kvn, krt, cst, wqt, wkvt, gq, gk)


def _oproj_kernel(conv_ref, gm_ref, y_ref, x_ref, mod_ref, wo_ref, g2_ref, wr_ref, br_ref,
                  x1_ref, h2_ref, eid_ref, rank_ref, cnt_ref, cwt_ref, base):
    tm = x_ref.shape[0]

    @pl.when(pl.program_id(0) == 0)
    def _():
        base[...] = jnp.zeros(base.shape, F32)

    merged = conv_ref[...].astype(F32) + gm_ref[...].astype(F32) * y_ref[...].astype(F32)
    att = _dot(merged.astype(BF16), wo_ref[...])
    gate1 = mod_ref[0, :, 2 * D_MODEL:3 * D_MODEL]
    shift2 = mod_ref[0, :, 3 * D_MODEL:4 * D_MODEL]
    scale2 = mod_ref[0, :, 4 * D_MODEL:5 * D_MODEL]
    x1 = x_ref[...] + gate1 * att
    x1_ref[...] = x1
    xn = x1 * lax.rsqrt(jnp.mean(x1 * x1, axis=-1, keepdims=True) + EPS) * g2_ref[...]
    h2 = xn * (1.0 + scale2) + shift2
    _store_rows(h2_ref, h2, tm)

    lt = _dot_nt(wr_ref[...], h2.astype(BF16)) + br_ref[...]
    gl = [lt[N_EXPERTS + r:N_EXPERTS + r + 1, :] for r in range(N_GROUPS)]
    gmax = jnp.maximum(jnp.maximum(gl[0], gl[1]), jnp.maximum(gl[2], gl[3]))
    gidx = jnp.full(gmax.shape, N_GROUPS - 1, jnp.int32)
    for r in range(N_GROUPS - 2, -1, -1):
        gidx = jnp.where(gl[r] == gmax, r, gidx)
    gsum = jnp.exp(gl[0] - gmax)
    for r in range(1, N_GROUPS):
        gsum = gsum + jnp.exp(gl[r] - gmax)
    p_group = 1.0 / gsum
    es = lt[(N_GROUPS - 1) * EXPERTS_PER_GROUP:N_GROUPS * EXPERTS_PER_GROUP, :]
    for r in range(N_GROUPS - 2, -1, -1):
        es = jnp.where(gidx == r, lt[r * EXPERTS_PER_GROUP:(r + 1) * EXPERTS_PER_GROUP, :], es)
    row = lax.broadcasted_iota(jnp.int32, es.shape, 0)
    m1 = jnp.max(es, axis=0, keepdims=True)
    i1 = jnp.min(jnp.where(es == m1, row, EXPERTS_PER_GROUP), axis=0, keepdims=True)
    es2 = jnp.where(row == i1, -jnp.inf, es)
    m2 = jnp.max(es2, axis=0, keepdims=True)
    i2 = jnp.min(jnp.where(es2 == m2, row, EXPERTS_PER_GROUP), axis=0, keepdims=True)
    e2 = jnp.exp(m2 - m1)
    w1 = p_group / (1.0 + e2)
    w2 = w1 * e2
    eid0 = gidx * EXPERTS_PER_GROUP + i1
    eid1 = gidx * EXPERTS_PER_GROUP + i2
    eid_ref[0:1, :] = eid0
    eid_ref[1:2, :] = eid1
    erow = lax.broadcasted_iota(jnp.int32, (N_EXPERTS, tm), 0)
    oh0 = erow == eid0
    oh1 = erow == eid1
    both = jnp.where(oh0, 1.0, jnp.where(oh1, 1.0, 0.0))
    earlier = (lax.broadcasted_iota(jnp.int32, (tm, tm), 0)
               < lax.broadcasted_iota(jnp.int32, (tm, tm), 1))
    seen = base[...] + _dot(both.astype(BF16), jnp.where(earlier, 1.0, 0.0).astype(BF16))
    rank_ref[0:1, :] = jnp.sum(jnp.where(oh0, seen, 0.0), axis=0, keepdims=True).astype(jnp.int32)
    rank_ref[1:2, :] = jnp.sum(jnp.where(oh1, seen, 0.0), axis=0, keepdims=True).astype(jnp.int32)
    base[...] = base[...] + jnp.sum(both, axis=1, keepdims=True)
    cnt_ref[...] = base[...]
    wrow = lax.broadcasted_iota(jnp.int32, (LANES, tm), 0)
    wmat = jnp.where(wrow == 0, w1, jnp.where(wrow == 1, w2, 0.0))
    cwt_ref[...] = wmat.T


def _oproj(conv_p, gm, y_mla, x2, mod3, wo, g2, wr_t, br, seq):
    t = x2.shape[0]
    tm = ROW_TILE
    tiles_per_seq = seq // tm
    row = lambda i: (i, 0)
    const = lambda i: (0, 0)
    return pl.pallas_call(
        _oproj_kernel,
        grid=(t // tm,),
        in_specs=[
            pl.BlockSpec((tm, D_MODEL), row),
            pl.BlockSpec((tm, D_MODEL), row),
            pl.BlockSpec((tm, D_MODEL), row),
            pl.BlockSpec((tm, D_MODEL), row),
            pl.BlockSpec((1, 1, 6 * D_MODEL), lambda i: (i // tiles_per_seq, 0, 0)),
            pl.BlockSpec((D_MODEL, D_MODEL), const),
            pl.BlockSpec((1, D_MODEL), const),
            pl.BlockSpec((ROUTER_ROWS, D_MODEL), const),
            pl.BlockSpec((ROUTER_ROWS, 1), const),
        ],
        out_specs=[
            pl.BlockSpec((tm, D_MODEL), row),
            pl.BlockSpec((tm * ROWS_PER_TOKEN, LANES), row),
            pl.BlockSpec((2, tm), lambda i: (0, i)),
            pl.BlockSpec((2, tm), lambda i: (0, i)),
            pl.BlockSpec((N_EXPERTS, 1), const),
            pl.BlockSpec((tm, LANES), row),
        ],
        out_shape=[
            jax.ShapeDtypeStruct((t, D_MODEL), F32),
            jax.ShapeDtypeStruct((t * ROWS_PER_TOKEN, LANES), F32),
            jax.ShapeDtypeStruct((2, t), jnp.int32),
            jax.ShapeDtypeStruct((2, t), jnp.int32),
            jax.ShapeDtypeStruct((N_EXPERTS, 1), F32),
            jax.ShapeDtypeStruct((t, LANES), F32),
        ],
        scratch_shapes=[pltpu.VMEM((N_EXPERTS, 1), F32)],
        compiler_params=pltpu.CompilerParams(
            dimension_semantics=("arbitrary",), vmem_limit_bytes=VMEM_LIMIT),
        name="oproj",
    )(conv_p, gm, y_mla, x2, mod3, wo, g2, wr_t, br)


def _dispatch_kernel(n_tiles, ztile_ref, na_ref, pos_ref, h2_ref, xs_hbm, zbuf, sem):
    tm = h2_ref.shape[0] // ROWS_PER_TOKEN
    tile_rows = MOE_TILE * ROWS_PER_TOKEN

    @pl.when(pl.program_id(0) == 0)
    def _():
        zbuf[...] = jnp.zeros(zbuf.shape, zbuf.dtype)

        def zero_tile(tile):
            row = pl.multiple_of(tile * tile_rows, tile_rows)
            return pltpu.make_async_copy(zbuf, xs_hbm.at[pl.ds(row, tile_rows), :], sem)
        for e in range(N_EXPERTS):
            zero_tile(ztile_ref[e]).start()
        for e in range(N_EXPERTS):
            zero_tile(ztile_ref[e]).wait()

        def zero_tail(k, carry):
            cp = zero_tile(na_ref[0] + k)
            cp.start()
            cp.wait()
            return carry
        lax.fori_loop(0, n_tiles - na_ref[0], zero_tail, 0)

    def body(c, carry):
        for k in range(DMA_UNROLL):
            t = c * DMA_UNROLL + k
            src = h2_ref.at[pl.ds(pl.multiple_of(t * ROWS_PER_TOKEN, ROWS_PER_TOKEN), ROWS_PER_TOKEN), :]
            for s in range(2):
                dst_row = pl.multiple_of(pos_ref[0, 0, s * tm + t] * ROWS_PER_TOKEN, ROWS_PER_TOKEN)
                pltpu.make_async_copy(src, xs_hbm.at[pl.ds(dst_row, ROWS_PER_TOKEN), :],
                                      sem).start(priority=s)
        return carry
    lax.fori_loop(0, tm // DMA_UNROLL, body, 0)
    for _ in range(2):
        pltpu.make_async_copy(h2_ref, xs_hbm.at[pl.ds(0, tm * ROWS_PER_TOKEN), :], sem).wait()


def _dispatch(ztile, na, pos3, h2r, n_tiles):
    tm = pos3.shape[2] // 2
    n_steps = pos3.shape[0]
    grid_spec = pltpu.PrefetchScalarGridSpec(
        num_scalar_prefetch=2,
        grid=(n_steps,),
        in_specs=[
            pl.BlockSpec((1, 1, 2 * tm), lambda i, z, n: (i, 0, 0), memory_space=pltpu.SMEM),
            pl.BlockSpec((tm * ROWS_PER_TOKEN, LANES), lambda i, z, n: (i, 0)),
        ],
        out_specs=pl.BlockSpec(memory_space=pl.ANY),
        scratch_shapes=[
            pltpu.VMEM((MOE_TILE * ROWS_PER_TOKEN, LANES), F32),
            pltpu.SemaphoreType.DMA(()),
        ],
    )
    return pl.pallas_call(
        functools.partial(_dispatch_kernel, n_tiles),
        grid_spec=grid_spec,
        out_shape=jax.ShapeDtypeStruct((n_tiles * MOE_TILE * ROWS_PER_TOKEN, LANES), F32),
        compiler_params=pltpu.CompilerParams(
            dimension_semantics=("arbitrary",), vmem_limit_bytes=VMEM_LIMIT),
        name="dispatch",
    )(ztile, na, pos3, h2r)


def _moe_kernel(te_ref, na_ref, x_ref, wg_ref, wu_ref, wd_ref, y_ref, wgu_s, wd_s):
    i = pl.program_id(0)
    active = i < na_ref[0]
    new_expert = jnp.logical_or(i == 0, te_ref[i] != te_ref[jnp.maximum(i - 1, 0)])

    @pl.when(jnp.logical_and(active, new_expert))
    def _():
        wgu_s[:, 0:D_EXPERT] = wg_ref[0].astype(BF16)
        wgu_s[:, D_EXPERT:] = wu_ref[0].astype(BF16)
        wd_s[...] = wd_ref[0].astype(BF16)

    @pl.when(active)
    def _():
        x = jnp.concatenate([col.astype(BF16) for col in _load_rows(x_ref, MOE_TILE)], axis=1)
        gu = _dot(x, wgu_s[...])
        g = gu[:, 0:D_EXPERT]
        a = (g * _sigmoid(g)) * gu[:, D_EXPERT:]
        y = _dot(a.astype(BF16), wd_s[...])
        _store_rows(y_ref, y, MOE_TILE)

    @pl.when(jnp.logical_not(active))
    def _():
        y_ref[...] = jnp.zeros(y_ref.shape, y_ref.dtype)


def _moe(te, na, xs, wg, wu, wd):
    nt = te.shape[0]
    tile_rows = MOE_TILE * ROWS_PER_TOKEN
    tile = lambda i, te_r, na_r: (jnp.minimum(i, na_r[0] - 1), 0)
    wspec = lambda shape: pl.BlockSpec((1,) + shape, lambda i, te_r, na_r: (te_r[i], 0, 0))
    grid_spec = pltpu.PrefetchScalarGridSpec(
        num_scalar_prefetch=2,
        grid=(nt,),
        in_specs=[
            pl.BlockSpec((tile_rows, LANES), tile),
            wspec((D_MODEL, D_EXPERT)),
            wspec((D_MODEL, D_EXPERT)),
            wspec((D_EXPERT, D_MODEL)),
        ],
        out_specs=pl.BlockSpec((tile_rows, LANES), lambda i, te_r, na_r: (i, 0)),
        scratch_shapes=[
            pltpu.VMEM((D_MODEL, 2 * D_EXPERT), BF16),
            pltpu.VMEM((D_EXPERT, D_MODEL), BF16),
        ],
    )
    return pl.pallas_call(
        _moe_kernel,
        grid_spec=grid_spec,
        out_shape=jax.ShapeDtypeStruct(xs.shape, xs.dtype),
        compiler_params=pltpu.CompilerParams(
            dimension_semantics=("arbitrary",), vmem_limit_bytes=VMEM_LIMIT),
        name="moe",
    )(te, na, xs, wg, wu, wd)


def _comb_kernel(pos0_ref, posn_ref, x1_ref, cwt_ref, mod_ref, y_hbm, o_ref, ybuf, sem):
    tm = x1_ref.shape[0]
    i = pl.program_id(0)
    slot = i % 2

    def gather(pos_ref, dst_slot):
        def body(c, carry):
            for k in range(DMA_UNROLL):
                r = c * DMA_UNROLL + k
                src_row = pl.multiple_of(pos_ref[0, 0, r] * ROWS_PER_TOKEN, ROWS_PER_TOKEN)
                dst_row = pl.multiple_of(r * ROWS_PER_TOKEN, ROWS_PER_TOKEN)
                pltpu.make_async_copy(y_hbm.at[pl.ds(src_row, ROWS_PER_TOKEN), :],
                                      ybuf.at[dst_slot, pl.ds(dst_row, ROWS_PER_TOKEN), :],
                                      sem.at[dst_slot]).start(priority=k % 2)
            return carry
        lax.fori_loop(0, 2 * tm // DMA_UNROLL, body, 0)

    @pl.when(i == 0)
    def _():
        gather(pos0_ref, 0)

    @pl.when(i + 1 < pl.num_programs(0))
    def _():
        gather(posn_ref, 1 - slot)

    pltpu.make_async_copy(y_hbm.at[pl.ds(0, 2 * tm * ROWS_PER_TOKEN), :], ybuf.at[slot],
                          sem.at[slot]).wait()
    c0 = cwt_ref[:, 0:1]
    c1 = cwt_ref[:, 1:2]
    y0 = _load_rows(ybuf.at[slot], tm)
    y1 = _load_rows(ybuf.at[slot], tm, offset=tm * ROWS_PER_TOKEN)
    for j in range(ROWS_PER_TOKEN):
        cols = slice(j * LANES, (j + 1) * LANES)
        gate2 = mod_ref[0, :, 5 * D_MODEL + j * LANES:5 * D_MODEL + (j + 1) * LANES]
        o_ref[:, cols] = x1_ref[:, cols] + gate2 * (c0 * y0[j] + c1 * y1[j])


def _comb(x1, yr, pos3, cwt, mod3, seq):
    t = x1.shape[0]
    tm = ROW_TILE
    tiles_per_seq = seq // tm
    n_steps = t // tm
    row = lambda i: (i, 0)
    smem_blk = lambda f: pl.BlockSpec((1, 1, 2 * tm), f, memory_space=pltpu.SMEM)
    return pl.pallas_call(
        _comb_kernel,
        grid=(n_steps,),
        in_specs=[
            smem_blk(lambda i: (0, 0, 0)),
            smem_blk(lambda i: (jnp.minimum(i + 1, n_steps - 1), 0, 0)),
            pl.BlockSpec((tm, D_MODEL), row),
            pl.BlockSpec((tm, LANES), row),
            pl.BlockSpec((1, 1, 6 * D_MODEL), lambda i: (i // tiles_per_seq, 0, 0)),
            pl.BlockSpec(memory_space=pl.ANY),
        ],
        out_specs=pl.BlockSpec((tm, D_MODEL), row),
        out_shape=jax.ShapeDtypeStruct((t, D_MODEL), F32),
        scratch_shapes=[
            pltpu.VMEM((2, 2 * tm * ROWS_PER_TOKEN, LANES), F32),
            pltpu.SemaphoreType.DMA((2,)),
        ],
        compiler_params=pltpu.CompilerParams(
            dimension_semantics=("arbitrary",), vmem_limit_bytes=VMEM_LIMIT),
        name="comb",
    )(pos3, pos3, x1, cwt, mod3, yr)


def _route_plan(eid, rank, cnt, n_tok):
    n_tiles = (2 * n_tok) // MOE_TILE + N_EXPERTS
    experts = jnp.arange(N_EXPERTS, dtype=jnp.int32)
    counts = cnt.reshape(N_EXPERTS).astype(jnp.int32)
    ntile = (counts + MOE_TILE - 1) // MOE_TILE
    tend = jnp.cumsum(ntile)
    tstart = tend - ntile
    n_active = tend[-1]
    tj = jnp.arange(n_tiles, dtype=jnp.int32)
    te_raw = jnp.minimum(jnp.sum((tj[:, None] >= tend[None, :]).astype(jnp.int32), axis=1),
                         N_EXPERTS - 1)
    te_last = jnp.sum(jnp.where(tj == n_active - 1, te_raw, 0))
    te = jnp.where(tj < n_active, te_raw, te_last).astype(jnp.int32)
    first_row = jnp.sum(jnp.where(eid[:, :, None] == experts[None, None, :],
                                  (tstart * MOE_TILE)[None, None, :], 0), axis=-1)
    pos = (first_row + rank).astype(jnp.int32)
    ztile = jnp.maximum(tend - 1, 0).astype(jnp.int32)
    pos3 = pos.reshape(2, n_tok // ROW_TILE, ROW_TILE).transpose(1, 0, 2).reshape(
        n_tok // ROW_TILE, 1, 2 * ROW_TILE)
    return te, n_active.reshape(1).astype(jnp.int32), ztile, pos3, n_tiles


def _rotate_half_cols(w):
    half = QK_ROPE // 2
    return jnp.concatenate([w[..., half:], w[..., :half]], axis=-1)


def kernel(x, c, positions, w_ada, b_ada, norm1_g, w_in, conv_w, q_a_norm_g, w_q_b, kv_a_norm_g, w_kv_b, q_norm_g, k_norm_g, w_o, norm2_g, w_router_group, b_router_group, w_router_expert, b_router_expert, w_exp_gate, w_exp_up, w_exp_down):
    nb, seq, d = x.shape
    depth = w_ada.shape[0]
    n_tok = nb * seq
    assert d == D_MODEL and seq % ROW_TILE == 0 and seq % Q_TILE == 0 and Q_TILE % CHUNK == 0
    assert (2 * n_tok) % MOE_TILE == 0

    inv = ROPE_BASE ** (-jnp.arange(0, QK_ROPE, 2, dtype=F32) / QK_ROPE)
    ang = inv[None, :, None] * positions.astype(F32)[:, None, :]
    cos, sin = jnp.cos(ang), jnp.sin(ang)
    cst = jnp.concatenate([cos, cos, -sin, sin], axis=1)

    x2 = x.reshape(n_tok, d)
    for l in range(depth):
        wi = w_in[l]
        o_q = 3 * D_MODEL
        o_kv = o_q + Q_LORA
        o_kr = o_kv + KV_LORA
        o_gc = o_kr + QK_ROPE
        o_gm = o_gc + D_MODEL
        w_kr = wi[:, o_kr:o_gc]
        w_cat = jnp.concatenate(
            [wi[:, 0:o_q], wi[:, o_gc:o_gm], wi[:, o_gm:], wi[:, o_q:o_kv], wi[:, o_kv:o_kr]],
            axis=1).astype(BF16)
        w_krt = jnp.concatenate([w_kr, _rotate_half_cols(w_kr)], axis=1).T.astype(BF16)
        wq3 = w_q_b[l].reshape(Q_LORA, N_HEADS, QK_HEAD)
        wq = jnp.concatenate([wq3, _rotate_half_cols(wq3[..., QK_NOPE:])], axis=-1)
        wqt = wq.transpose(1, 2, 0).astype(BF16)
        wkvt = w_kv_b[l].reshape(KV_LORA, N_HEADS, QK_NOPE + V_HEAD).transpose(1, 2, 0).astype(BF16)
        gq = jnp.concatenate([q_norm_g[l], _rotate_half_cols(q_norm_g[l][QK_NOPE:])]).reshape(-1, 1)
        gk = jnp.concatenate([k_norm_g[l], _rotate_half_cols(k_norm_g[l][QK_NOPE:])]).reshape(-1, 1)
        wr_t = jnp.concatenate(
            [w_router_expert[l].T, w_router_group[l].T,
             jnp.zeros((ROUTER_ROWS - N_EXPERTS - N_GROUPS, d), F32)], axis=0).astype(BF16)
        br = jnp.concatenate(
            [b_router_expert[l], b_router_group[l],
             jnp.zeros((ROUTER_ROWS - N_EXPERTS - N_GROUPS,), F32)]).reshape(ROUTER_ROWS, 1)

        mod3 = _ada(c, w_ada[l], b_ada[l]).reshape(nb, 1, 6 * d)
        conv_p, gm, qn, kvn, krt = _inproj(
            x2, mod3, norm1_g[l].reshape(1, d), w_cat, w_krt, conv_w[l],
            q_a_norm_g[l].reshape(1, -1), kv_a_norm_g[l].reshape(1, -1), seq)
        y_mla = _attn(qn, kvn, krt, cst, wqt, wkvt, gq, gk, nb, seq)
        x1, h2r, eid, rank, cnt, cwt = _oproj(conv_p, gm, y_mla, x2, mod3, w_o[l].astype(BF16),
                                              norm2_g[l].reshape(1, d), wr_t, br, seq)
        te, na, ztile, pos3, n_tiles = _route_plan(eid, rank, cnt, n_tok)
        xs = _dispatch(ztile, na, pos3, h2r, n_tiles)
        yr = _moe(te, na, xs,
                  w_exp_gate[l].reshape(N_EXPERTS, d, D_EXPERT),
                  w_exp_up[l].reshape(N_EXPERTS, d, D_EXPERT),
                  w_exp_down[l].reshape(N_EXPERTS, D_EXPERT, d))
        x2 = _comb(x1, yr, pos3, cwt, mod3, seq)
    return x2.reshape(nb, seq, d)
```

```python
import functools
import math

import jax
import jax.numpy as jnp
from jax import lax
from jax.experimental import pallas as pl
from jax.experimental.pallas import tpu as pltpu

F32 = jnp.float32
BF16 = jnp.bfloat16

D_MODEL = 1024
N_HEADS = 8
QK_NOPE = 128
QK_ROPE = 64
QK_HEAD = QK_NOPE + QK_ROPE
V_HEAD = 128
Q_LORA = 384
KV_LORA = 256
CHUNK = 64
EPS = 1e-6
ROPE_BASE = 10000.0
N_GROUPS = 4
EXPERTS_PER_GROUP = 8
N_EXPERTS = N_GROUPS * EXPERTS_PER_GROUP
D_EXPERT = 256
CONV_K = 3

LANES = 128
SUBLANES = 8
VMEM_LIMIT = 56 * 1024 * 1024

ROW_TILE = 512
Q_TILE = 512
HEADS_PER_STEP = 2
MOE_TILE = 512
ADA_COLS = 1536
ROWS_PER_TOKEN = D_MODEL // LANES
DMA_UNROLL = 8
ROUTER_ROWS = 40


def _sigmoid(v):
    return 1.0 / (1.0 + jnp.exp(-v))


def _dot(a, b):
    return jnp.dot(a, b, preferred_element_type=F32)


def _store_rows(ref, val, n):
    for j in range(ROWS_PER_TOKEN):
        ref[pl.ds(j, n, stride=ROWS_PER_TOKEN), :] = val[:, j * LANES:(j + 1) * LANES]


def _load_rows(ref, n, offset=0):
    return [ref[pl.ds(offset + j, n, stride=ROWS_PER_TOKEN), :] for j in range(ROWS_PER_TOKEN)]


def _dot_nt(a, b):
    return lax.dot_general(a, b, (((1,), (1,)), ((), ())), preferred_element_type=F32)


def _ada_kernel(c_ref, w_ref, b_ref, o_ref):
    c = c_ref[...]
    act = (c * _sigmoid(c)).astype(BF16)
    o_ref[...] = _dot(act, w_ref[...].astype(BF16)) + b_ref[...]


def _ada(c, w_ada, b_ada):
    nb, d = c.shape
    n = w_ada.shape[1]
    return pl.pallas_call(
        _ada_kernel,
        grid=(n // ADA_COLS,),
        in_specs=[
            pl.BlockSpec((nb, d), lambda j: (0, 0)),
            pl.BlockSpec((d, ADA_COLS), lambda j: (0, j)),
            pl.BlockSpec((1, ADA_COLS), lambda j: (0, j)),
        ],
        out_specs=pl.BlockSpec((nb, ADA_COLS), lambda j: (0, j)),
        out_shape=jax.ShapeDtypeStruct((nb, n), F32),
        compiler_params=pltpu.CompilerParams(
            dimension_semantics=("arbitrary",), vmem_limit_bytes=VMEM_LIMIT),
        name="ada",
    )(c, w_ada, b_ada.reshape(1, n))


_C_ZX = 0
_C_ZB = D_MODEL
_C_ZC = 2 * D_MODEL
_C_GC = 3 * D_MODEL
_C_GM = 4 * D_MODEL
_C_QL = 5 * D_MODEL
_C_KV = _C_QL + Q_LORA
_C_END = _C_KV + KV_LORA


def _inproj_kernel(tiles_per_seq, x_ref, mod_ref, g1_ref, w_ref, wkr_ref, cw_ref, gq_ref, gkv_ref,
                   conv_ref, gm_ref, qn_ref, kvn_ref, krt_ref, ubuf):
    tm = x_ref.shape[0]

    @pl.when(pl.program_id(0) % tiles_per_seq == 0)
    def _():
        ubuf[0:SUBLANES, :] = jnp.zeros((SUBLANES, D_MODEL), F32)

    x = x_ref[...]
    xn = x * lax.rsqrt(jnp.mean(x * x, axis=-1, keepdims=True) + EPS) * g1_ref[...]
    shift = mod_ref[0, :, 0:D_MODEL]
    scale = mod_ref[0, :, D_MODEL:2 * D_MODEL]
    h = (xn * (1.0 + scale) + shift).astype(BF16)

    def proj(lo, hi):
        return _dot(h, w_ref[:, lo:hi])

    u = proj(_C_ZC, _C_GC) * proj(_C_ZX, _C_ZB)
    ubuf[SUBLANES:SUBLANES + tm, :] = u
    conv = (ubuf[SUBLANES - 2:SUBLANES - 2 + tm, :] * cw_ref[0:1, :]
            + ubuf[SUBLANES - 1:SUBLANES - 1 + tm, :] * cw_ref[1:2, :]
            + u * cw_ref[2:3, :])
    ubuf[0:SUBLANES, :] = ubuf[tm:tm + SUBLANES, :]
    y_conv = proj(_C_ZB, _C_ZC) * conv
    conv_ref[...] = (_sigmoid(proj(_C_GC, _C_GM)) * y_conv).astype(BF16)
    gm_ref[...] = _sigmoid(proj(_C_GM, _C_QL)).astype(BF16)

    ql = proj(_C_QL, _C_KV)
    qn_ref[...] = (ql * lax.rsqrt(jnp.mean(ql * ql, axis=-1, keepdims=True) + EPS)
                   * gq_ref[...]).astype(BF16)
    kl = proj(_C_KV, _C_END)
    kvn_ref[...] = (kl * lax.rsqrt(jnp.mean(kl * kl, axis=-1, keepdims=True) + EPS)
                    * gkv_ref[...]).astype(BF16)
    krt_ref[0] = _dot_nt(wkr_ref[...], h)


def _inproj(x2, mod3, g1, w_cat, w_krt, conv_w, gq, gkv, seq):
    t = x2.shape[0]
    nb = t // seq
    tm = ROW_TILE
    tiles_per_seq = seq // tm
    row = lambda i: (i, 0)
    const = lambda i: (0, 0)
    return pl.pallas_call(
        functools.partial(_inproj_kernel, tiles_per_seq),
        grid=(t // tm,),
        in_specs=[
            pl.BlockSpec((tm, D_MODEL), row),
            pl.BlockSpec((1, 1, 6 * D_MODEL), lambda i: (i // tiles_per_seq, 0, 0)),
            pl.BlockSpec((1, D_MODEL), const),
            pl.BlockSpec((D_MODEL, _C_END), const),
            pl.BlockSpec((2 * QK_ROPE, D_MODEL), const),
            pl.BlockSpec((CONV_K, D_MODEL), const),
            pl.BlockSpec((1, Q_LORA), const),
            pl.BlockSpec((1, KV_LORA), const),
        ],
        out_specs=[
            pl.BlockSpec((tm, D_MODEL), row),
            pl.BlockSpec((tm, D_MODEL), row),
            pl.BlockSpec((tm, Q_LORA), row),
            pl.BlockSpec((tm, KV_LORA), row),
            pl.BlockSpec((1, 2 * QK_ROPE, tm),
                         lambda i: (i // tiles_per_seq, 0, i % tiles_per_seq)),
        ],
        out_shape=[
            jax.ShapeDtypeStruct((t, D_MODEL), BF16),
            jax.ShapeDtypeStruct((t, D_MODEL), BF16),
            jax.ShapeDtypeStruct((t, Q_LORA), BF16),
            jax.ShapeDtypeStruct((t, KV_LORA), BF16),
            jax.ShapeDtypeStruct((nb, 2 * QK_ROPE, seq), F32),
        ],
        scratch_shapes=[pltpu.VMEM((tm + SUBLANES, D_MODEL), F32)],
        compiler_params=pltpu.CompilerParams(
            dimension_semantics=("arbitrary",), vmem_limit_bytes=VMEM_LIMIT),
        name="inproj",
    )(x2, mod3, g1, w_cat, w_krt, conv_w, gq, gkv)


def _attn_kernel(qn_ref, kvn_ref, krt_ref, cst_ref, wqt_ref, wkvt_ref, gq_ref, gk_ref,
                 o_ref, *scratch):
    per_head = len(scratch) // HEADS_PER_STEP
    heads = [_attn_head(hh, qn_ref, kvn_ref, krt_ref, cst_ref, wqt_ref, wkvt_ref, gq_ref, gk_ref, o_ref,
                        *scratch[hh * per_head:(hh + 1) * per_head]) for hh in range(HEADS_PER_STEP)]
    nq = qn_ref.shape[0] // Q_TILE
    for scores, _ in heads:
        scores(nq - 1)
    for i in range(nq - 1, -1, -1):
        for scores, finish in heads:
            if i > 0:
                scores(i - 1)
            finish(i)


def _attn_head(hh, qn_ref, kvn_ref, krt_ref, cst_ref, wqt_ref, wkvt_ref, gq_ref, gk_ref,
               o_ref, qt_s, k_s, vt_s, s_buf0, s_buf1):
    seq = qn_ref.shape[0]
    cos_t = cst_ref[0, 0:QK_ROPE, :]
    sin_t = cst_ref[0, QK_ROPE:, :]

    def normed_rope(nope, r, rr, g, extra_scale):
        ss = jnp.sum(nope * nope, axis=0, keepdims=True) + jnp.sum(r * r, axis=0, keepdims=True)
        scale = lax.rsqrt(ss * (1.0 / QK_HEAD) + EPS) * extra_scale
        rope = r * g[QK_NOPE:QK_HEAD] * cos_t + rr * g[QK_HEAD:] * sin_t
        return (nope * g[0:QK_NOPE] * scale).astype(BF16), (rope * scale).astype(BF16)

    qt = _dot_nt(wqt_ref[hh], qn_ref[...])
    q_n, q_r = normed_rope(qt[0:QK_NOPE], qt[QK_NOPE:QK_HEAD], qt[QK_HEAD:], gq_ref[...],
                           QK_HEAD ** -0.5 * math.log2(math.e))
    qt_s[0:QK_NOPE, :] = q_n
    qt_s[QK_NOPE:QK_HEAD, :] = q_r
    qt_s[QK_HEAD:, :] = jnp.zeros((QK_ROPE, seq), BF16)

    kvt = _dot_nt(wkvt_ref[hh], kvn_ref[...])
    krt = krt_ref[0]
    k_n, k_r = normed_rope(kvt[0:QK_NOPE], krt[0:QK_ROPE], krt[QK_ROPE:], gk_ref[...], 1.0)
    kt = jnp.concatenate([k_n, k_r, jnp.zeros((QK_ROPE, seq), BF16)], axis=0)
    k_s[...] = kt.T
    vt_s[0:V_HEAD, :] = kvt[QK_NOPE:].astype(BF16)
    vt_s[V_HEAD:, :] = jnp.ones((vt_s.shape[0] - V_HEAD, seq), BF16)

    tq = Q_TILE
    kchunk = lax.broadcasted_iota(jnp.int32, (tq, tq), 0) // CHUNK
    qchunk = lax.broadcasted_iota(jnp.int32, (tq, tq), 1) // CHUNK
    diag_ok = kchunk <= qchunk
    neg = jnp.finfo(F32).min

    def scores(i):
        q0 = i * tq
        sb = s_buf0 if i % 2 == 0 else s_buf1
        q = qt_s[:, q0:q0 + tq]
        if i > 0:
            sb[0:q0, :] = _dot(k_s[0:q0, :], q)
        sb[q0:q0 + tq, :] = jnp.where(diag_ok, _dot(k_s[q0:q0 + tq, :], q), neg)

    def finish(i):
        q0 = i * tq
        kend = q0 + tq
        sb = s_buf0 if i % 2 == 0 else s_buf1
        m = jnp.max(sb[0:kend, :], axis=0, keepdims=True)
        acc = _dot(vt_s[:, 0:kend], jnp.exp2(sb[0:kend, :] - m).astype(BF16))
        o_t = acc[0:V_HEAD] / acc[V_HEAD:V_HEAD + 1]
        o_ref[q0:q0 + tq, hh * V_HEAD:(hh + 1) * V_HEAD] = o_t.T.astype(o_ref.dtype)

    return scores, finish


def _attn(qn, kvn, krt, cst, wqt, wkvt, gq, gk, nb, seq):
    t = qn.shape[0]
    per_b = lambda b, h: (b, 0)
    per_b3 = lambda b, h: (b, 0, 0)
    per_h = lambda b, h: (h, 0, 0)
    const = lambda b, h: (0, 0)
    qk_rows = QK_NOPE + 2 * QK_ROPE
    return pl.pallas_call(
        _attn_kernel,
        grid=(nb, N_HEADS // HEADS_PER_STEP),
        in_specs=[
            pl.BlockSpec((seq, Q_LORA), per_b),
            pl.BlockSpec((seq, KV_LORA), per_b),
            pl.BlockSpec((1, 2 * QK_ROPE, seq), per_b3),
            pl.BlockSpec((1, 2 * QK_ROPE, seq), per_b3),
            pl.BlockSpec((HEADS_PER_STEP, qk_rows, Q_LORA), per_h),
            pl.BlockSpec((HEADS_PER_STEP, QK_NOPE + V_HEAD, KV_LORA), per_h),
            pl.BlockSpec((qk_rows, 1), const),
            pl.BlockSpec((qk_rows, 1), const),
        ],
        out_specs=pl.BlockSpec((seq, HEADS_PER_STEP * V_HEAD), lambda b, h: (b, h)),
        out_shape=jax.ShapeDtypeStruct((t, N_HEADS * V_HEAD), BF16),
        scratch_shapes=[
            pltpu.VMEM((qk_rows, seq), BF16),
            pltpu.VMEM((seq, qk_rows), BF16),
            pltpu.VMEM((V_HEAD + 2 * SUBLANES, seq), BF16),
            pltpu.VMEM((seq, Q_TILE), F32),
            pltpu.VMEM((seq, Q_TILE), F32),
        ] * HEADS_PER_STEP,
        compiler_params=pltpu.CompilerParams(
            dimension_semantics=("arbitrary", "arbitrary"), vmem_limit_bytes=VMEM_LIMIT),
        name="attn",
    )(qn, kvn, krt, cst, wqt, wkvt, gq, gk)


def _oproj_kernel(conv_ref, gm_ref, y_ref, x_ref, mod_ref, wo_ref, g2_ref, wr_ref, br_ref,
                  x1_ref, h2_ref, eid_ref, rank_ref, cnt_ref, cwt_ref, base):
    tm = x_ref.shape[0]

    @pl.when(pl.program_id(0) == 0)
    def _():
        base[...] = jnp.zeros(base.shape, F32)

    merged = conv_ref[...].astype(F32) + gm_ref[...].astype(F32) * y_ref[...].astype(F32)
    att = _dot(merged.astype(BF16), wo_ref[...])
    gate1 = mod_ref[0, :, 2 * D_MODEL:3 * D_MODEL]
    shift2 = mod_ref[0, :, 3 * D_MODEL:4 * D_MODEL]
    scale2 = mod_ref[0, :, 4 * D_MODEL:5 * D_MODEL]
    x1 = x_ref[...] + gate1 * att
    x1_ref[...] = x1
    xn = x1 * lax.rsqrt(jnp.mean(x1 * x1, axis=-1, keepdims=True) + EPS) * g2_ref[...]
    h2 = xn * (1.0 + scale2) + shift2
    _store_rows(h2_ref, h2, tm)

    lt = _dot_nt(wr_ref[...], h2.astype(BF16)) + br_ref[...]
    gl = [lt[N_EXPERTS + r:N_EXPERTS + r + 1, :] for r in range(N_GROUPS)]
    gmax = jnp.maximum(jnp.maximum(gl[0], gl[1]), jnp.maximum(gl[2], gl[3]))
    gidx = jnp.full(gmax.shape, N_GROUPS - 1, jnp.int32)
    for r in range(N_GROUPS - 2, -1, -1):
        gidx = jnp.where(gl[r] == gmax, r, gidx)
    gsum = jnp.exp(gl[0] - gmax)
    for r in range(1, N_GROUPS):
        gsum = gsum + jnp.exp(gl[r] - gmax)
    p_group = 1.0 / gsum
    es = lt[(N_GROUPS - 1) * EXPERTS_PER_GROUP:N_GROUPS * EXPERTS_PER_GROUP, :]
    for r in range(N_GROUPS - 2, -1, -1):
        es = jnp.where(gidx == r, lt[r * EXPERTS_PER_GROUP:(r + 1) * EXPERTS_PER_GROUP, :], es)
    row = lax.broadcasted_iota(jnp.int32, es.shape, 0)
    m1 = jnp.max(es, axis=0, keepdims=True)
    i1 = jnp.min(jnp.where(es == m1, row, EXPERTS_PER_GROUP), axis=0, keepdims=True)
    es2 = jnp.where(row == i1, -jnp.inf, es)
    m2 = jnp.max(es2, axis=0, keepdims=True)
    i2 = jnp.min(jnp.where(es2 == m2, row, EXPERTS_PER_GROUP), axis=0, keepdims=True)
    e2 = jnp.exp(m2 - m1)
    w1 = p_group / (1.0 + e2)
    w2 = w1 * e2
    eid0 = gidx * EXPERTS_PER_GROUP + i1
    eid1 = gidx * EXPERTS_PER_GROUP + i2
    eid_ref[0:1, :] = eid0
    eid_ref[1:2, :] = eid1
    erow = lax.broadcasted_iota(jnp.int32, (N_EXPERTS, tm), 0)
    oh0 = erow == eid0
    oh1 = erow == eid1
    both = jnp.where(oh0, 1.0, jnp.where(oh1, 1.0, 0.0))
    earlier = (lax.broadcasted_iota(jnp.int32, (tm, tm), 0)
               < lax.broadcasted_iota(jnp.int32, (tm, tm), 1))
    seen = base[...] + _dot(both.astype(BF16), jnp.where(earlier, 1.0, 0.0).astype(BF16))
    rank_ref[0:1, :] = jnp.sum(jnp.where(oh0, seen, 0.0), axis=0, keepdims=True).astype(jnp.int32)
    rank_ref[1:2, :] = jnp.sum(jnp.where(oh1, seen, 0.0), axis=0, keepdims=True).astype(jnp.int32)
    base[...] = base[...] + jnp.sum(both, axis=1, keepdims=True)
    cnt_ref[...] = base[...]
    wrow = lax.broadcasted_iota(jnp.int32, (LANES, tm), 0)
    wmat = jnp.where(wrow == 0, w1, jnp.where(wrow == 1, w2, 0.0))
    cwt_ref[...] = wmat.T


def _oproj(conv_p, gm, y_mla, x2, mod3, wo, g2, wr_t, br, seq):
    t = x2.shape[0]
    tm = ROW_TILE
    tiles_per_seq = seq // tm
    row = lambda i: (i, 0)
    const = lambda i: (0, 0)
    return pl.pallas_call(
        _oproj_kernel,
        grid=(t // tm,),
        in_specs=[
            pl.BlockSpec((tm, D_MODEL), row),
            pl.BlockSpec((tm, D_MODEL), row),
            pl.BlockSpec((tm, D_MODEL), row),
            pl.BlockSpec((tm, D_MODEL), row),
            pl.BlockSpec((1, 1, 6 * D_MODEL), lambda i: (i // tiles_per_seq, 0, 0)),
            pl.BlockSpec((D_MODEL, D_MODEL), const),
            pl.BlockSpec((1, D_MODEL), const),
            pl.BlockSpec((ROUTER_ROWS, D_MODEL), const),
            pl.BlockSpec((ROUTER_ROWS, 1), const),
        ],
        out_specs=[
            pl.BlockSpec((tm, D_MODEL), row),
            pl.BlockSpec((tm * ROWS_PER_TOKEN, LANES), row),
            pl.BlockSpec((2, tm), lambda i: (0, i)),
            pl.BlockSpec((2, tm), lambda i: (0, i)),
            pl.BlockSpec((N_EXPERTS, 1), const),
            pl.BlockSpec((tm, LANES), row),
        ],
        out_shape=[
            jax.ShapeDtypeStruct((t, D_MODEL), F32),
            jax.ShapeDtypeStruct((t * ROWS_PER_TOKEN, LANES), F32),
            jax.ShapeDtypeStruct((2, t), jnp.int32),
            jax.ShapeDtypeStruct((2, t), jnp.int32),
            jax.ShapeDtypeStruct((N_EXPERTS, 1), F32),
            jax.ShapeDtypeStruct((t, LANES), F32),
        ],
        scratch_shapes=[pltpu.VMEM((N_EXPERTS, 1), F32)],
        compiler_params=pltpu.CompilerParams(
            dimension_semantics=("arbitrary",), vmem_limit_bytes=VMEM_LIMIT),
        name="oproj",
    )(conv_p, gm, y_mla, x2, mod3, wo, g2, wr_t, br)


def _dispatch_kernel(n_tiles, ztile_ref, na_ref, pos_ref, h2_ref, xs_hbm, zbuf, sem):
    tm = h2_ref.shape[0] // ROWS_PER_TOKEN
    tile_rows = MOE_TILE * ROWS_PER_TOKEN

    @pl.when(pl.program_id(0) == 0)
    def _():
        zbuf[...] = jnp.zeros(zbuf.shape, zbuf.dtype)

        def zero_tile(tile):
            row = pl.multiple_of(tile * tile_rows, tile_rows)
            return pltpu.make_async_copy(zbuf, xs_hbm.at[pl.ds(row, tile_rows), :], sem)
        for e in range(N_EXPERTS):
            zero_tile(ztile_ref[e]).start()
        for e in range(N_EXPERTS):
            zero_tile(ztile_ref[e]).wait()

        def zero_tail(k, carry):
            cp = zero_tile(na_ref[0] + k)
            cp.start()
            cp.wait()
            return carry
        lax.fori_loop(0, n_tiles - na_ref[0], zero_tail, 0)

    def body(c, carry):
        for k in range(DMA_UNROLL):
            t = c * DMA_UNROLL + k
            src = h2_ref.at[pl.ds(pl.multiple_of(t * ROWS_PER_TOKEN, ROWS_PER_TOKEN), ROWS_PER_TOKEN), :]
            for s in range(2):
                dst_row = pl.multiple_of(pos_ref[0, 0, s * tm + t] * ROWS_PER_TOKEN, ROWS_PER_TOKEN)
                pltpu.make_async_copy(src, xs_hbm.at[pl.ds(dst_row, ROWS_PER_TOKEN), :],
                                      sem).start(priority=s)
        return carry
    lax.fori_loop(0, tm // DMA_UNROLL, body, 0)
    for _ in range(2):
        pltpu.make_async_copy(h2_ref, xs_hbm.at[pl.ds(0, tm * ROWS_PER_TOKEN), :], sem).wait()


def _dispatch(ztile, na, pos3, h2r, n_tiles):
    tm = pos3.shape[2] // 2
    n_steps = pos3.shape[0]
    grid_spec = pltpu.PrefetchScalarGridSpec(
        num_scalar_prefetch=2,
        grid=(n_steps,),
        in_specs=[
            pl.BlockSpec((1, 1, 2 * tm), lambda i, z, n: (i, 0, 0), memory_space=pltpu.SMEM),
            pl.BlockSpec((tm * ROWS_PER_TOKEN, LANES), lambda i, z, n: (i, 0)),
        ],
        out_specs=pl.BlockSpec(memory_space=pl.ANY),
        scratch_shapes=[
            pltpu.VMEM((MOE_TILE * ROWS_PER_TOKEN, LANES), F32),
            pltpu.SemaphoreType.DMA(()),
        ],
    )
    return pl.pallas_call(
        functools.partial(_dispatch_kernel, n_tiles),
        grid_spec=grid_spec,
        out_shape=jax.ShapeDtypeStruct((n_tiles * MOE_TILE * ROWS_PER_TOKEN, LANES), F32),
        compiler_params=pltpu.CompilerParams(
            dimension_semantics=("arbitrary",), vmem_limit_bytes=VMEM_LIMIT),
        name="dispatch",
    )(ztile, na, pos3, h2r)


def _moe_kernel(te_ref, na_ref, x_ref, wg_ref, wu_ref, wd_ref, y_ref, wgu_s, wd_s):
    i = pl.program_id(0)
    active = i < na_ref[0]
    new_expert = jnp.logical_or(i == 0, te_ref[i] != te_ref[jnp.maximum(i - 1, 0)])

    @pl.when(jnp.logical_and(active, new_expert))
    def _():
        wgu_s[:, 0:D_EXPERT] = wg_ref[0].astype(BF16)
        wgu_s[:, D_EXPERT:] = wu_ref[0].astype(BF16)
        wd_s[...] = wd_ref[0].astype(BF16)

    @pl.when(active)
    def _():
        x = jnp.concatenate([col.astype(BF16) for col in _load_rows(x_ref, MOE_TILE)], axis=1)
        gu = _dot(x, wgu_s[...])
        g = gu[:, 0:D_EXPERT]
        a = (g * _sigmoid(g)) * gu[:, D_EXPERT:]
        y = _dot(a.astype(BF16), wd_s[...])
        _store_rows(y_ref, y, MOE_TILE)

    @pl.when(jnp.logical_not(active))
    def _():
        y_ref[...] = jnp.zeros(y_ref.shape, y_ref.dtype)


def _moe(te, na, xs, wg, wu, wd):
    nt = te.shape[0]
    tile_rows = MOE_TILE * ROWS_PER_TOKEN
    tile = lambda i, te_r, na_r: (jnp.minimum(i, na_r[0] - 1), 0)
    wspec = lambda shape: pl.BlockSpec((1,) + shape, lambda i, te_r, na_r: (te_r[i], 0, 0))
    grid_spec = pltpu.PrefetchScalarGridSpec(
        num_scalar_prefetch=2,
        grid=(nt,),
        in_specs=[
            pl.BlockSpec((tile_rows, LANES), tile),
            wspec((D_MODEL, D_EXPERT)),
            wspec((D_MODEL, D_EXPERT)),
            wspec((D_EXPERT, D_MODEL)),
        ],
        out_specs=pl.BlockSpec((tile_rows, LANES), lambda i, te_r, na_r: (i, 0)),
        scratch_shapes=[
            pltpu.VMEM((D_MODEL, 2 * D_EXPERT), BF16),
            pltpu.VMEM((D_EXPERT, D_MODEL), BF16),
        ],
    )
    return pl.pallas_call(
        _moe_kernel,
        grid_spec=grid_spec,
        out_shape=jax.ShapeDtypeStruct(xs.shape, xs.dtype),
        compiler_params=pltpu.CompilerParams(
            dimension_semantics=("arbitrary",), vmem_limit_bytes=VMEM_LIMIT),
        name="moe",
    )(te, na, xs, wg, wu, wd)


def _comb_kernel(pos0_ref, posn_ref, x1_ref, cwt_ref, mod_ref, y_hbm, o_ref, ybuf, sem):
    tm = x1_ref.shape[0]
    i = pl.program_id(0)
    slot = i % 2

    def gather(pos_ref, dst_slot):
        def body(c, carry):
            for k in range(DMA_UNROLL):
                r = c * DMA_UNROLL + k
                src_row = pl.multiple_of(pos_ref[0, 0, r] * ROWS_PER_TOKEN, ROWS_PER_TOKEN)
                dst_row = pl.multiple_of(r * ROWS_PER_TOKEN, ROWS_PER_TOKEN)
                pltpu.make_async_copy(y_hbm.at[pl.ds(src_row, ROWS_PER_TOKEN), :],
                                      ybuf.at[dst_slot, pl.ds(dst_row, ROWS_PER_TOKEN), :],
                                      sem.at[dst_slot]).start(priority=k % 2)
            return carry
        lax.fori_loop(0, 2 * tm // DMA_UNROLL, body, 0)

    @pl.when(i == 0)
    def _():
        gather(pos0_ref, 0)

    @pl.when(i + 1 < pl.num_programs(0))
    def _():
        gather(posn_ref, 1 - slot)

    pltpu.make_async_copy(y_hbm.at[pl.ds(0, 2 * tm * ROWS_PER_TOKEN), :], ybuf.at[slot],
                          sem.at[slot]).wait()
    c0 = cwt_ref[:, 0:1]
    c1 = cwt_ref[:, 1:2]
    y0 = _load_rows(ybuf.at[slot], tm)
    y1 = _load_rows(ybuf.at[slot], tm, offset=tm * ROWS_PER_TOKEN)
    for j in range(ROWS_PER_TOKEN):
        cols = slice(j * LANES, (j + 1) * LANES)
        gate2 = mod_ref[0, :, 5 * D_MODEL + j * LANES:5 * D_MODEL + (j + 1) * LANES]
        o_ref[:, cols] = x1_ref[:, cols] + gate2 * (c0 * y0[j] + c1 * y1[j])


def _comb(x1, yr, pos3, cwt, mod3, seq):
    t = x1.shape[0]
    tm = ROW_TILE
    tiles_per_seq = seq // tm
    n_steps = t // tm
    row = lambda i: (i, 0)
    smem_blk = lambda f: pl.BlockSpec((1, 1, 2 * tm), f, memory_space=pltpu.SMEM)
    return pl.pallas_call(
        _comb_kernel,
        grid=(n_steps,),
        in_specs=[
            smem_blk(lambda i: (0, 0, 0)),
            smem_blk(lambda i: (jnp.minimum(i + 1, n_steps - 1), 0, 0)),
            pl.BlockSpec((tm, D_MODEL), row),
            pl.BlockSpec((tm, LANES), row),
            pl.BlockSpec((1, 1, 6 * D_MODEL), lambda i: (i // tiles_per_seq, 0, 0)),
            pl.BlockSpec(memory_space=pl.ANY),
        ],
        out_specs=pl.BlockSpec((tm, D_MODEL), row),
        out_shape=jax.ShapeDtypeStruct((t, D_MODEL), F32),
        scratch_shapes=[
            pltpu.VMEM((2, 2 * tm * ROWS_PER_TOKEN, LANES), F32),
            pltpu.SemaphoreType.DMA((2,)),
        ],
        compiler_params=pltpu.CompilerParams(
            dimension_semantics=("arbitrary",), vmem_limit_bytes=VMEM_LIMIT),
        name="comb",
    )(pos3, pos3, x1, cwt, mod3, yr)


def _route_plan(eid, rank, cnt, n_tok):
    n_tiles = (2 * n_tok) // MOE_TILE + N_EXPERTS
    experts = jnp.arange(N_EXPERTS, dtype=jnp.int32)
    counts = cnt.reshape(N_EXPERTS).astype(jnp.int32)
    ntile = (counts + MOE_TILE - 1) // MOE_TILE
    tend = jnp.cumsum(ntile)
    tstart = tend - ntile
    n_active = tend[-1]
    tj = jnp.arange(n_tiles, dtype=jnp.int32)
    te_raw = jnp.minimum(jnp.sum((tj[:, None] >= tend[None, :]).astype(jnp.int32), axis=1),
                         N_EXPERTS - 1)
    te_last = jnp.sum(jnp.where(tj == n_active - 1, te_raw, 0))
    te = jnp.where(tj < n_active, te_raw, te_last).astype(jnp.int32)
    first_row = jnp.sum(jnp.where(eid[:, :, None] == experts[None, None, :],
                                  (tstart * MOE_TILE)[None, None, :], 0), axis=-1)
    pos = (first_row + rank).astype(jnp.int32)
    ztile = jnp.maximum(tend - 1, 0).astype(jnp.int32)
    pos3 = pos.reshape(2, n_tok // ROW_TILE, ROW_TILE).transpose(1, 0, 2).reshape(
        n_tok // ROW_TILE, 1, 2 * ROW_TILE)
    return te, n_active.reshape(1).astype(jnp.int32), ztile, pos3, n_tiles


def _rotate_half_cols(w):
    half = QK_ROPE // 2
    return jnp.concatenate([w[..., half:], w[..., :half]], axis=-1)


def kernel(x, c, positions, w_ada, b_ada, norm1_g, w_in, conv_w, q_a_norm_g, w_q_b, kv_a_norm_g, w_kv_b, q_norm_g, k_norm_g, w_o, norm2_g, w_router_group, b_router_group, w_router_expert, b_router_expert, w_exp_gate, w_exp_up, w_exp_down):
    nb, seq, d = x.shape
    depth = w_ada.shape[0]
    n_tok = nb * seq
    assert d == D_MODEL and seq % ROW_TILE == 0 and seq % Q_TILE == 0 and Q_TILE % CHUNK == 0
    assert (2 * n_tok) % MOE_TILE == 0

    inv = ROPE_BASE ** (-jnp.arange(0, QK_ROPE, 2, dtype=F32) / QK_ROPE)
    ang = inv[None, :, None] * positions.astype(F32)[:, None, :]
    cos, sin = jnp.cos(ang), jnp.sin(ang)
    cst = jnp.concatenate([cos, cos, -sin, sin], axis=1)

    x2 = x.reshape(n_tok, d)
    for l in range(depth):
        wi = w_in[l]
        o_q = 3 * D_MODEL
        o_kv = o_q + Q_LORA
        o_kr = o_kv + KV_LORA
        o_gc = o_kr + QK_ROPE
        o_gm = o_gc + D_MODEL
        w_kr = wi[:, o_kr:o_gc]
        w_cat = jnp.concatenate(
            [wi[:, 0:o_q], wi[:, o_gc:o_gm], wi[:, o_gm:], wi[:, o_q:o_kv], wi[:, o_kv:o_kr]],
            axis=1).astype(BF16)
        w_krt = jnp.concatenate([w_kr, _rotate_half_cols(w_kr)], axis=1).T.astype(BF16)
        wq3 = w_q_b[l].reshape(Q_LORA, N_HEADS, QK_HEAD)
        wq = jnp.concatenate([wq3, _rotate_half_cols(wq3[..., QK_NOPE:])], axis=-1)
        wqt = wq.transpose(1, 2, 0).astype(BF16)
        wkvt = w_kv_b[l].reshape(KV_LORA, N_HEADS, QK_NOPE + V_HEAD).transpose(1, 2, 0).astype(BF16)
        gq = jnp.concatenate([q_norm_g[l], _rotate_half_cols(q_norm_g[l][QK_NOPE:])]).reshape(-1, 1)
        gk = jnp.concatenate([k_norm_g[l], _rotate_half_cols(k_norm_g[l][QK_NOPE:])]).reshape(-1, 1)
        wr_t = jnp.concatenate(
            [w_router_expert[l].T, w_router_group[l].T,
             jnp.zeros((ROUTER_ROWS - N_EXPERTS - N_GROUPS, d), F32)], axis=0).astype(BF16)
        br = jnp.concatenate(
            [b_router_expert[l], b_router_group[l],
             jnp.zeros((ROUTER_ROWS - N_EXPERTS - N_GROUPS,), F32)]).reshape(ROUTER_ROWS, 1)

        mod3 = _ada(c, w_ada[l], b_ada[l]).reshape(nb, 1, 6 * d)
        conv_p, gm, qn, kvn, krt = _inproj(
            x2, mod3, norm1_g[l].reshape(1, d), w_cat, w_krt, conv_w[l],
            q_a_norm_g[l].reshape(1, -1), kv_a_norm_g[l].reshape(1, -1), seq)
        y_mla = _attn(qn, kvn, krt, cst, wqt, wkvt, gq, gk, nb, seq)
        x1, h2r, eid, rank, cnt, cwt = _oproj(conv_p, gm, y_mla, x2, mod3, w_o[l].astype(BF16),
                                              norm2_g[l].reshape(1, d), wr_t, br, seq)
        te, na, ztile, pos3, n_tiles = _route_plan(eid, rank, cnt, n_tok)
        xs = _dispatch(ztile, na, pos3, h2r, n_tiles)
        yr = _moe(te, na, xs,
                  w_exp_gate[l].reshape(N_EXPERTS, d, D_EXPERT),
                  w_exp_up[l].reshape(N_EXPERTS, d, D_EXPERT),
                  w_exp_down[l].reshape(N_EXPERTS, D_EXPERT, d))
        x2 = _comb(x1, yr, pos3, cwt, mod3, seq)
    return x2.reshape(nb, seq, d)
```

```python
import functools
import math

import jax
import jax.numpy as jnp
from jax import lax
from jax.experimental import pallas as pl
from jax.experimental.pallas import tpu as pltpu

F32 = jnp.float32
BF16 = jnp.bfloat16

D_MODEL = 1024
N_HEADS = 8
QK_NOPE = 128
QK_ROPE = 64
QK_HEAD = QK_NOPE + QK_ROPE
V_HEAD = 128
Q_LORA = 384
KV_LORA = 256
CHUNK = 64
EPS = 1e-6
ROPE_BASE = 10000.0
N_GROUPS = 4
EXPERTS_PER_GROUP = 8
N_EXPERTS = N_GROUPS * EXPERTS_PER_GROUP
D_EXPERT = 256
CONV_K = 3

LANES = 128
SUBLANES = 8
VMEM_LIMIT = 56 * 1024 * 1024

ROW_TILE = 512
Q_TILE = 512
INPROJ_SUB = 2
HEADS_PER_STEP = 2
MOE_TILE = 512
X_SLOTS = 3
ADA_COLS = 1536
ROWS_PER_TOKEN = D_MODEL // LANES
DMA_UNROLL = 8
ROUTER_ROWS = 40


def _sigmoid(v):
    return 1.0 / (1.0 + jnp.exp(-v))


def _dot(a, b):
    return jnp.dot(a, b, preferred_element_type=F32)


def _store_rows(ref, val, n):
    for j in range(ROWS_PER_TOKEN):
        ref[pl.ds(j, n, stride=ROWS_PER_TOKEN), :] = val[:, j * LANES:(j + 1) * LANES]


def _load_rows(ref, n, offset=0):
    return [ref[pl.ds(offset + j, n, stride=ROWS_PER_TOKEN), :] for j in range(ROWS_PER_TOKEN)]


def _dot_nt(a, b):
    return lax.dot_general(a, b, (((1,), (1,)), ((), ())), preferred_element_type=F32)


def _ada_kernel(c_ref, w_ref, b_ref, o_ref):
    c = c_ref[...]
    act = (c * _sigmoid(c)).astype(BF16)
    o_ref[...] = _dot(act, w_ref[...].astype(BF16)) + b_ref[...]


def _ada(c, w_ada, b_ada):
    nb, d = c.shape
    n = w_ada.shape[1]
    return pl.pallas_call(
        _ada_kernel,
        grid=(n // ADA_COLS,),
        in_specs=[
            pl.BlockSpec((nb, d), lambda j: (0, 0)),
            pl.BlockSpec((d, ADA_COLS), lambda j: (0, j)),
            pl.BlockSpec((1, ADA_COLS), lambda j: (0, j)),
        ],
        out_specs=pl.BlockSpec((nb, ADA_COLS), lambda j: (0, j)),
        out_shape=jax.ShapeDtypeStruct((nb, n), F32),
        compiler_params=pltpu.CompilerParams(
            dimension_semantics=("arbitrary",), vmem_limit_bytes=VMEM_LIMIT),
        name="ada",
    )(c, w_ada, b_ada.reshape(1, n))


_C_ZX = 0
_C_ZB = D_MODEL
_C_ZC = 2 * D_MODEL
_C_GC = 3 * D_MODEL
_C_GM = 4 * D_MODEL
_C_QL = 5 * D_MODEL
_C_KV = _C_QL + Q_LORA
_C_END = _C_KV + KV_LORA


def _inproj_kernel(tiles_per_seq, x_ref, mod_ref, g1_ref, w_ref, wkr_ref, cw_ref, gq_ref, gkv_ref,
                   conv_ref, gm_ref, qn_ref, kvn_ref, krt_ref, ubuf):
    tm = x_ref.shape[0] // INPROJ_SUB

    @pl.when(pl.program_id(0) % tiles_per_seq == 0)
    def _():
        ubuf[0:SUBLANES, :] = jnp.zeros((SUBLANES, D_MODEL), F32)

    shift = mod_ref[0, :, 0:D_MODEL]
    scale = mod_ref[0, :, D_MODEL:2 * D_MODEL]
    for sub in range(INPROJ_SUB):
        rows = slice(sub * tm, (sub + 1) * tm)
        x = x_ref[rows, :]
        xn = x * lax.rsqrt(jnp.mean(x * x, axis=-1, keepdims=True) + EPS) * g1_ref[...]
        h = (xn * (1.0 + scale) + shift).astype(BF16)

        def proj(lo, hi):
            return _dot(h, w_ref[:, lo:hi])

        u = proj(_C_ZC, _C_GC) * proj(_C_ZX, _C_ZB)
        ubuf[SUBLANES:SUBLANES + tm, :] = u
        conv = (ubuf[SUBLANES - 2:SUBLANES - 2 + tm, :] * cw_ref[0:1, :]
                + ubuf[SUBLANES - 1:SUBLANES - 1 + tm, :] * cw_ref[1:2, :]
                + u * cw_ref[2:3, :])
        ubuf[0:SUBLANES, :] = ubuf[tm:tm + SUBLANES, :]
        y_conv = proj(_C_ZB, _C_ZC) * conv
        conv_ref[rows, :] = (_sigmoid(proj(_C_GC, _C_GM)) * y_conv).astype(BF16)
        gm_ref[rows, :] = _sigmoid(proj(_C_GM, _C_QL)).astype(BF16)

        ql = proj(_C_QL, _C_KV)
        qn_ref[rows, :] = (ql * lax.rsqrt(jnp.mean(ql * ql, axis=-1, keepdims=True) + EPS)
                           * gq_ref[...]).astype(BF16)
        kl = proj(_C_KV, _C_END)
        kvn_ref[rows, :] = (kl * lax.rsqrt(jnp.mean(kl * kl, axis=-1, keepdims=True) + EPS)
                            * gkv_ref[...]).astype(BF16)
        krt_ref[0, :, rows] = _dot_nt(wkr_ref[...], h)


def _inproj(x2, mod3, g1, w_cat, w_krt, conv_w, gq, gkv, seq):
    t = x2.shape[0]
    nb = t // seq
    tm = ROW_TILE * INPROJ_SUB
    tiles_per_seq = seq // tm
    row = lambda i: (i, 0)
    const = lambda i: (0, 0)
    return pl.pallas_call(
        functools.partial(_inproj_kernel, tiles_per_seq),
        grid=(t // tm,),
        in_specs=[
            pl.BlockSpec((tm, D_MODEL), row),
            pl.BlockSpec((1, 1, 6 * D_MODEL), lambda i: (i // tiles_per_seq, 0, 0)),
            pl.BlockSpec((1, D_MODEL), const),
            pl.BlockSpec((D_MODEL, _C_END), const),
            pl.BlockSpec((2 * QK_ROPE, D_MODEL), const),
            pl.BlockSpec((CONV_K, D_MODEL), const),
            pl.BlockSpec((1, Q_LORA), const),
            pl.BlockSpec((1, KV_LORA), const),
        ],
        out_specs=[
            pl.BlockSpec((tm, D_MODEL), row),
            pl.BlockSpec((tm, D_MODEL), row),
            pl.BlockSpec((tm, Q_LORA), row),
            pl.BlockSpec((tm, KV_LORA), row),
            pl.BlockSpec((1, 2 * QK_ROPE, tm),
                         lambda i: (i // tiles_per_seq, 0, i % tiles_per_seq)),
        ],
        out_shape=[
            jax.ShapeDtypeStruct((t, D_MODEL), BF16),
            jax.ShapeDtypeStruct((t, D_MODEL), BF16),
            jax.ShapeDtypeStruct((t, Q_LORA), BF16),
            jax.ShapeDtypeStruct((t, KV_LORA), BF16),
            jax.ShapeDtypeStruct((nb, 2 * QK_ROPE, seq), F32),
        ],
        scratch_shapes=[pltpu.VMEM((ROW_TILE + SUBLANES, D_MODEL), F32)],
        compiler_params=pltpu.CompilerParams(
            dimension_semantics=("arbitrary",), vmem_limit_bytes=VMEM_LIMIT),
        name="inproj",
    )(x2, mod3, g1, w_cat, w_krt, conv_w, gq, gkv)


def _attn_kernel(qn_ref, kvn_ref, krt_ref, cst_ref, wqt_ref, wkvt_ref, gq_ref, gk_ref,
                 o_ref, *scratch):
    per_head = len(scratch) // HEADS_PER_STEP
    heads = [_attn_head(hh, qn_ref, kvn_ref, krt_ref, cst_ref, wqt_ref, wkvt_ref, gq_ref, gk_ref, o_ref,
                        *scratch[hh * per_head:(hh + 1) * per_head]) for hh in range(HEADS_PER_STEP)]
    nq = qn_ref.shape[0] // Q_TILE
    for scores, _ in heads:
        scores(nq - 1)
    for i in range(nq - 1, -1, -1):
        for scores, finish in heads:
            if i > 0:
                scores(i - 1)
            finish(i)


def _attn_head(hh, qn_ref, kvn_ref, krt_ref, cst_ref, wqt_ref, wkvt_ref, gq_ref, gk_ref,
               o_ref, qt_s, k_s, vt_s, s_buf0, s_buf1):
    seq = qn_ref.shape[0]
    cos_t = cst_ref[0, 0:QK_ROPE, :]
    sin_t = cst_ref[0, QK_ROPE:, :]

    def normed_rope(nope, r, rr, g, extra_scale):
        ss = jnp.sum(nope * nope, axis=0, keepdims=True) + jnp.sum(r * r, axis=0, keepdims=True)
        scale = lax.rsqrt(ss * (1.0 / QK_HEAD) + EPS) * extra_scale
        rope = r * g[QK_NOPE:QK_HEAD] * cos_t + rr * g[QK_HEAD:] * sin_t
        return (nope * g[0:QK_NOPE] * scale).astype(BF16), (rope * scale).astype(BF16)

    qt = _dot_nt(wqt_ref[hh], qn_ref[...])
    q_n, q_r = normed_rope(qt[0:QK_NOPE], qt[QK_NOPE:QK_HEAD], qt[QK_HEAD:], gq_ref[...],
                           QK_HEAD ** -0.5 * math.log2(math.e))
    qt_s[0:QK_NOPE, :] = q_n
    qt_s[QK_NOPE:QK_HEAD, :] = q_r
    qt_s[QK_HEAD:, :] = jnp.zeros((QK_ROPE, seq), BF16)

    kvt = _dot_nt(wkvt_ref[hh], kvn_ref[...])
    krt = krt_ref[0]
    k_n, k_r = normed_rope(kvt[0:QK_NOPE], krt[0:QK_ROPE], krt[QK_ROPE:], gk_ref[...], 1.0)
    kt = jnp.concatenate([k_n, k_r, jnp.zeros((QK_ROPE, seq), BF16)], axis=0)
    k_s[...] = kt.T
    vt_s[0:V_HEAD, :] = kvt[QK_NOPE:].astype(BF16)
    vt_s[V_HEAD:, :] = jnp.ones((vt_s.shape[0] - V_HEAD, seq), BF16)

    tq = Q_TILE
    kchunk = lax.broadcasted_iota(jnp.int32, (tq, tq), 0) // CHUNK
    qchunk = lax.broadcasted_iota(jnp.int32, (tq, tq), 1) // CHUNK
    diag_ok = kchunk <= qchunk
    neg = jnp.finfo(F32).min

    def scores(i):
        q0 = i * tq
        sb = s_buf0 if i % 2 == 0 else s_buf1
        q = qt_s[:, q0:q0 + tq]
        if i > 0:
            sb[0:q0, :] = _dot(k_s[0:q0, :], q)
        sb[q0:q0 + tq, :] = jnp.where(diag_ok, _dot(k_s[q0:q0 + tq, :], q), neg)

    def finish(i):
        q0 = i * tq
        kend = q0 + tq
        sb = s_buf0 if i % 2 == 0 else s_buf1
        m = jnp.max(sb[0:kend, :], axis=0, keepdims=True)
        acc = _dot(vt_s[:, 0:kend], jnp.exp2(sb[0:kend, :] - m).astype(BF16))
        o_t = acc[0:V_HEAD] / acc[V_HEAD:V_HEAD + 1]
        o_ref[q0:q0 + tq, hh * V_HEAD:(hh + 1) * V_HEAD] = o_t.T.astype(o_ref.dtype)

    return scores, finish


def _attn(qn, kvn, krt, cst, wqt, wkvt, gq, gk, nb, seq):
    t = qn.shape[0]
    per_b = lambda b, h: (b, 0)
    per_b3 = lambda b, h: (b, 0, 0)
    per_h = lambda b, h: (h, 0, 0)
    const = lambda b, h: (0, 0)
    qk_rows = QK_NOPE + 2 * QK_ROPE
    return pl.pallas_call(
        _attn_kernel,
        grid=(nb, N_HEADS // HEADS_PER_STEP),
        in_specs=[
            pl.BlockSpec((seq, Q_LORA), per_b),
            pl.BlockSpec((seq, KV_LORA), per_b),
            pl.BlockSpec((1, 2 * QK_ROPE, seq), per_b3),
            pl.BlockSpec((1, 2 * QK_ROPE, seq), per_b3),
            pl.BlockSpec((HEADS_PER_STEP, qk_rows, Q_LORA), per_h),
            pl.BlockSpec((HEADS_PER_STEP, QK_NOPE + V_HEAD, KV_LORA), per_h),
            pl.BlockSpec((qk_rows, 1), const),
            pl.BlockSpec((qk_rows, 1), const),
        ],
        out_specs=pl.BlockSpec((seq, HEADS_PER_STEP * V_HEAD), lambda b, h: (b, h)),
        out_shape=jax.ShapeDtypeStruct((t, N_HEADS * V_HEAD), BF16),
        scratch_shapes=[
            pltpu.VMEM((qk_rows, seq), BF16),
            pltpu.VMEM((seq, qk_rows), BF16),
            pltpu.VMEM((V_HEAD + 2 * SUBLANES, seq), BF16),
            pltpu.VMEM((seq, Q_TILE), F32),
            pltpu.VMEM((seq, Q_TILE), F32),
        ] * HEADS_PER_STEP,
        compiler_params=pltpu.CompilerParams(
            dimension_semantics=("arbitrary", "arbitrary"), vmem_limit_bytes=VMEM_LIMIT),
        name="attn",
    )(qn, kvn, krt, cst, wqt, wkvt, gq, gk)


def _oproj_kernel(conv_ref, gm_ref, y_ref, x_ref, mod_ref, wo_ref, g2_ref, wr_ref, br_ref,
                  x1_ref, h2_ref, eid_ref, rank_ref, cnt_ref, cwt_ref, base):
    tm = x_ref.shape[0]

    @pl.when(pl.program_id(0) == 0)
    def _():
        base[...] = jnp.zeros(base.shape, F32)

    merged = conv_ref[...].astype(F32) + gm_ref[...].astype(F32) * y_ref[...].astype(F32)
    att = _dot(merged.astype(BF16), wo_ref[...])
    gate1 = mod_ref[0, :, 2 * D_MODEL:3 * D_MODEL]
    shift2 = mod_ref[0, :, 3 * D_MODEL:4 * D_MODEL]
    scale2 = mod_ref[0, :, 4 * D_MODEL:5 * D_MODEL]
    x1 = x_ref[...] + gate1 * att
    x1_ref[...] = x1
    xn = x1 * lax.rsqrt(jnp.mean(x1 * x1, axis=-1, keepdims=True) + EPS) * g2_ref[...]
    h2 = xn * (1.0 + scale2) + shift2
    _store_rows(h2_ref, h2, tm)

    lt = _dot_nt(wr_ref[...], h2.astype(BF16)) + br_ref[...]
    gl = [lt[N_EXPERTS + r:N_EXPERTS + r + 1, :] for r in range(N_GROUPS)]
    gmax = jnp.maximum(jnp.maximum(gl[0], gl[1]), jnp.maximum(gl[2], gl[3]))
    gidx = jnp.full(gmax.shape, N_GROUPS - 1, jnp.int32)
    for r in range(N_GROUPS - 2, -1, -1):
        gidx = jnp.where(gl[r] == gmax, r, gidx)
    gsum = jnp.exp(gl[0] - gmax)
    for r in range(1, N_GROUPS):
        gsum = gsum + jnp.exp(gl[r] - gmax)
    p_group = 1.0 / gsum
    es = lt[(N_GROUPS - 1) * EXPERTS_PER_GROUP:N_GROUPS * EXPERTS_PER_GROUP, :]
    for r in range(N_GROUPS - 2, -1, -1):
        es = jnp.where(gidx == r, lt[r * EXPERTS_PER_GROUP:(r + 1) * EXPERTS_PER_GROUP, :], es)
    row = lax.broadcasted_iota(jnp.int32, es.shape, 0)
    m1 = jnp.max(es, axis=0, keepdims=True)
    i1 = jnp.min(jnp.where(es == m1, row, EXPERTS_PER_GROUP), axis=0, keepdims=True)
    es2 = jnp.where(row == i1, -jnp.inf, es)
    m2 = jnp.max(es2, axis=0, keepdims=True)
    i2 = jnp.min(jnp.where(es2 == m2, row, EXPERTS_PER_GROUP), axis=0, keepdims=True)
    e2 = jnp.exp(m2 - m1)
    w1 = p_group / (1.0 + e2)
    w2 = w1 * e2
    eid0 = gidx * EXPERTS_PER_GROUP + i1
    eid1 = gidx * EXPERTS_PER_GROUP + i2
    eid_ref[0:1, :] = eid0
    eid_ref[1:2, :] = eid1
    erow = lax.broadcasted_iota(jnp.int32, (N_EXPERTS, tm), 0)
    oh0 = erow == eid0
    oh1 = erow == eid1
    both = jnp.where(oh0, 1.0, jnp.where(oh1, 1.0, 0.0))
    earlier = (lax.broadcasted_iota(jnp.int32, (tm, tm), 0)
               < lax.broadcasted_iota(jnp.int32, (tm, tm), 1))
    seen = base[...] + _dot(both.astype(BF16), jnp.where(earlier, 1.0, 0.0).astype(BF16))
    rank_ref[0:1, :] = jnp.sum(jnp.where(oh0, seen, 0.0), axis=0, keepdims=True).astype(jnp.int32)
    rank_ref[1:2, :] = jnp.sum(jnp.where(oh1, seen, 0.0), axis=0, keepdims=True).astype(jnp.int32)
    base[...] = base[...] + jnp.sum(both, axis=1, keepdims=True)
    cnt_ref[...] = base[...]
    wrow = lax.broadcasted_iota(jnp.int32, (LANES, tm), 0)
    wmat = jnp.where(wrow == 0, w1, jnp.where(wrow == 1, w2, 0.0))
    cwt_ref[...] = wmat.T


def _oproj(conv_p, gm, y_mla, x2, mod3, wo, g2, wr_t, br, seq):
    t = x2.shape[0]
    tm = ROW_TILE
    tiles_per_seq = seq // tm
    row = lambda i: (i, 0)
    const = lambda i: (0, 0)
    return pl.pallas_call(
        _oproj_kernel,
        grid=(t // tm,),
        in_specs=[
            pl.BlockSpec((tm, D_MODEL), row),
            pl.BlockSpec((tm, D_MODEL), row),
            pl.BlockSpec((tm, D_MODEL), row),
            pl.BlockSpec((tm, D_MODEL), row),
            pl.BlockSpec((1, 1, 6 * D_MODEL), lambda i: (i // tiles_per_seq, 0, 0)),
            pl.BlockSpec((D_MODEL, D_MODEL), const),
            pl.BlockSpec((1, D_MODEL), const),
            pl.BlockSpec((ROUTER_ROWS, D_MODEL), const),
            pl.BlockSpec((ROUTER_ROWS, 1), const),
        ],
        out_specs=[
            pl.BlockSpec((tm, D_MODEL), row),
            pl.BlockSpec((tm * ROWS_PER_TOKEN, LANES), row),
            pl.BlockSpec((2, tm), lambda i: (0, i)),
            pl.BlockSpec((2, tm), lambda i: (0, i)),
            pl.BlockSpec((N_EXPERTS, 1), const),
            pl.BlockSpec((tm, LANES), row),
        ],
        out_shape=[
            jax.ShapeDtypeStruct((t, D_MODEL), F32),
            jax.ShapeDtypeStruct((t * ROWS_PER_TOKEN, LANES), F32),
            jax.ShapeDtypeStruct((2, t), jnp.int32),
            jax.ShapeDtypeStruct((2, t), jnp.int32),
            jax.ShapeDtypeStruct((N_EXPERTS, 1), F32),
            jax.ShapeDtypeStruct((t, LANES), F32),
        ],
        scratch_shapes=[pltpu.VMEM((N_EXPERTS, 1), F32)],
        compiler_params=pltpu.CompilerParams(
            dimension_semantics=("arbitrary",), vmem_limit_bytes=VMEM_LIMIT),
        name="oproj",
    )(conv_p, gm, y_mla, x2, mod3, wo, g2, wr_t, br)


def _dispatch_kernel(n_tiles, ztile_ref, na_ref, pos_ref, h2_ref, xs_hbm, zbuf, sem):
    tm = h2_ref.shape[0] // ROWS_PER_TOKEN
    tile_rows = MOE_TILE * ROWS_PER_TOKEN

    @pl.when(pl.program_id(0) == 0)
    def _():
        zbuf[...] = jnp.zeros(zbuf.shape, zbuf.dtype)

        def zero_tile(tile):
            row = pl.multiple_of(tile * tile_rows, tile_rows)
            return pltpu.make_async_copy(zbuf, xs_hbm.at[pl.ds(row, tile_rows), :], sem)
        for e in range(N_EXPERTS):
            zero_tile(ztile_ref[e]).start()
        for e in range(N_EXPERTS):
            zero_tile(ztile_ref[e]).wait()

        def zero_tail(k, carry):
            cp = zero_tile(na_ref[0] + k)
            cp.start()
            cp.wait()
            return carry
        lax.fori_loop(0, n_tiles - na_ref[0], zero_tail, 0)

    def body(c, carry):
        for k in range(DMA_UNROLL):
            t = c * DMA_UNROLL + k
            src = h2_ref.at[pl.ds(pl.multiple_of(t * ROWS_PER_TOKEN, ROWS_PER_TOKEN), ROWS_PER_TOKEN), :]
            for s in range(2):
                dst_row = pl.multiple_of(pos_ref[0, 0, s * tm + t] * ROWS_PER_TOKEN, ROWS_PER_TOKEN)
                pltpu.make_async_copy(src, xs_hbm.at[pl.ds(dst_row, ROWS_PER_TOKEN), :],
                                      sem).start(priority=s)
        return carry
    lax.fori_loop(0, tm // DMA_UNROLL, body, 0)
    for _ in range(2):
        pltpu.make_async_copy(h2_ref, xs_hbm.at[pl.ds(0, tm * ROWS_PER_TOKEN), :], sem).wait()


def _dispatch(ztile, na, pos3, h2r, n_tiles):
    tm = pos3.shape[2] // 2
    n_steps = pos3.shape[0]
    grid_spec = pltpu.PrefetchScalarGridSpec(
        num_scalar_prefetch=2,
        grid=(n_steps,),
        in_specs=[
            pl.BlockSpec((1, 1, 2 * tm), lambda i, z, n: (i, 0, 0), memory_space=pltpu.SMEM),
            pl.BlockSpec((tm * ROWS_PER_TOKEN, LANES), lambda i, z, n: (i, 0)),
        ],
        out_specs=pl.BlockSpec(memory_space=pl.ANY),
        scratch_shapes=[
            pltpu.VMEM((MOE_TILE * ROWS_PER_TOKEN, LANES), F32),
            pltpu.SemaphoreType.DMA(()),
        ],
    )
    return pl.pallas_call(
        functools.partial(_dispatch_kernel, n_tiles),
        grid_spec=grid_spec,
        out_shape=jax.ShapeDtypeStruct((n_tiles * MOE_TILE * ROWS_PER_TOKEN, LANES), F32),
        compiler_params=pltpu.CompilerParams(
            dimension_semantics=("arbitrary",), vmem_limit_bytes=VMEM_LIMIT),
        name="dispatch",
    )(ztile, na, pos3, h2r)


def _moe_kernel(te_ref, na_ref, x_hbm, wg_ref, wu_ref, wd_ref, y_ref, xbuf, wgu_s, wd_s, sem):
    i = pl.program_id(0)
    n_active = na_ref[0]
    active = i < n_active
    tile_rows = MOE_TILE * ROWS_PER_TOKEN

    def fetch(tile):
        slot = tile % X_SLOTS
        row = pl.multiple_of(tile * tile_rows, tile_rows)
        return pltpu.make_async_copy(x_hbm.at[pl.ds(row, tile_rows), :], xbuf.at[slot], sem.at[slot])

    @pl.when(i == 0)
    def _():
        for k in range(X_SLOTS - 1):
            @pl.when(k < n_active)
            def _():
                fetch(k).start()

    @pl.when(i + X_SLOTS - 1 < n_active)
    def _():
        fetch(i + X_SLOTS - 1).start()

    new_expert = jnp.logical_or(i == 0, te_ref[i] != te_ref[jnp.maximum(i - 1, 0)])

    @pl.when(jnp.logical_and(active, new_expert))
    def _():
        wgu_s[:, 0:D_EXPERT] = wg_ref[0].astype(BF16)
        wgu_s[:, D_EXPERT:] = wu_ref[0].astype(BF16)
        wd_s[...] = wd_ref[0].astype(BF16)

    @pl.when(active)
    def _():
        fetch(i).wait()
        x = jnp.concatenate([col.astype(BF16) for col in _load_rows(xbuf.at[i % X_SLOTS], MOE_TILE)], axis=1)
        gu = _dot(x, wgu_s[...])
        g = gu[:, 0:D_EXPERT]
        a = (g * _sigmoid(g)) * gu[:, D_EXPERT:]
        y = _dot(a.astype(BF16), wd_s[...])
        _store_rows(y_ref, y, MOE_TILE)

    @pl.when(jnp.logical_not(active))
    def _():
        y_ref[...] = jnp.zeros(y_ref.shape, y_ref.dtype)


def _moe(te, na, xs, wg, wu, wd):
    nt = te.shape[0]
    tile_rows = MOE_TILE * ROWS_PER_TOKEN
    wspec = lambda shape: pl.BlockSpec((1,) + shape, lambda i, te_r, na_r: (te_r[i], 0, 0))
    grid_spec = pltpu.PrefetchScalarGridSpec(
        num_scalar_prefetch=2,
        grid=(nt,),
        in_specs=[
            pl.BlockSpec(memory_space=pl.ANY),
            wspec((D_MODEL, D_EXPERT)),
            wspec((D_MODEL, D_EXPERT)),
            wspec((D_EXPERT, D_MODEL)),
        ],
        out_specs=pl.BlockSpec((tile_rows, LANES), lambda i, te_r, na_r: (i, 0)),
        scratch_shapes=[
            pltpu.VMEM((X_SLOTS, tile_rows, LANES), F32),
            pltpu.VMEM((D_MODEL, 2 * D_EXPERT), BF16),
            pltpu.VMEM((D_EXPERT, D_MODEL), BF16),
            pltpu.SemaphoreType.DMA((X_SLOTS,)),
        ],
    )
    return pl.pallas_call(
        _moe_kernel,
        grid_spec=grid_spec,
        out_shape=jax.ShapeDtypeStruct(xs.shape, xs.dtype),
        compiler_params=pltpu.CompilerParams(
            dimension_semantics=("arbitrary",), vmem_limit_bytes=VMEM_LIMIT),
        name="moe",
    )(te, na, xs, wg, wu, wd)


def _comb_kernel(pos0_ref, posn_ref, x1_ref, cwt_ref, mod_ref, y_hbm, o_ref, ybuf, sem):
    tm = x1_ref.shape[0]
    i = pl.program_id(0)
    slot = i % 2

    def gather(pos_ref, dst_slot):
        def body(c, carry):
            for k in range(DMA_UNROLL):
                r = c * DMA_UNROLL + k
                src_row = pl.multiple_of(pos_ref[0, 0, r] * ROWS_PER_TOKEN, ROWS_PER_TOKEN)
                dst_row = pl.multiple_of(r * ROWS_PER_TOKEN, ROWS_PER_TOKEN)
                pltpu.make_async_copy(y_hbm.at[pl.ds(src_row, ROWS_PER_TOKEN), :],
                                      ybuf.at[dst_slot, pl.ds(dst_row, ROWS_PER_TOKEN), :],
                                      sem.at[dst_slot]).start(priority=k % 2)
            return carry
        lax.fori_loop(0, 2 * tm // DMA_UNROLL, body, 0)

    @pl.when(i == 0)
    def _():
        gather(pos0_ref, 0)

    @pl.when(i + 1 < pl.num_programs(0))
    def _():
        gather(posn_ref, 1 - slot)

    pltpu.make_async_copy(y_hbm.at[pl.ds(0, 2 * tm * ROWS_PER_TOKEN), :], ybuf.at[slot],
                          sem.at[slot]).wait()
    c0 = cwt_ref[:, 0:1]
    c1 = cwt_ref[:, 1:2]
    y0 = _load_rows(ybuf.at[slot], tm)
    y1 = _load_rows(ybuf.at[slot], tm, offset=tm * ROWS_PER_TOKEN)
    for j in range(ROWS_PER_TOKEN):
        cols = slice(j * LANES, (j + 1) * LANES)
        gate2 = mod_ref[0, :, 5 * D_MODEL + j * LANES:5 * D_MODEL + (j + 1) * LANES]
        o_ref[:, cols] = x1_ref[:, cols] + gate2 * (c0 * y0[j] + c1 * y1[j])


def _comb(x1, yr, pos3, cwt, mod3, seq):
    t = x1.shape[0]
    tm = ROW_TILE
    tiles_per_seq = seq // tm
    n_steps = t // tm
    row = lambda i: (i, 0)
    smem_blk = lambda f: pl.BlockSpec((1, 1, 2 * tm), f, memory_space=pltpu.SMEM)
    return pl.pallas_call(
        _comb_kernel,
        grid=(n_steps,),
        in_specs=[
            smem_blk(lambda i: (0, 0, 0)),
            smem_blk(lambda i: (jnp.minimum(i + 1, n_steps - 1), 0, 0)),
            pl.BlockSpec((tm, D_MODEL), row),
            pl.BlockSpec((tm, LANES), row),
            pl.BlockSpec((1, 1, 6 * D_MODEL), lambda i: (i // tiles_per_seq, 0, 0)),
            pl.BlockSpec(memory_space=pl.ANY),
        ],
        out_specs=pl.BlockSpec((tm, D_MODEL), row),
        out_shape=jax.ShapeDtypeStruct((t, D_MODEL), F32),
        scratch_shapes=[
            pltpu.VMEM((2, 2 * tm * ROWS_PER_TOKEN, LANES), F32),
            pltpu.SemaphoreType.DMA((2,)),
        ],
        compiler_params=pltpu.CompilerParams(
            dimension_semantics=("arbitrary",), vmem_limit_bytes=VMEM_LIMIT),
        name="comb",
    )(pos3, pos3, x1, cwt, mod3, yr)


def _route_plan(eid, rank, cnt, n_tok):
    n_tiles = (2 * n_tok) // MOE_TILE + N_EXPERTS
    experts = jnp.arange(N_EXPERTS, dtype=jnp.int32)
    counts = cnt.reshape(N_EXPERTS).astype(jnp.int32)
    ntile = (counts + MOE_TILE - 1) // MOE_TILE
    tend = jnp.cumsum(ntile)
    tstart = tend - ntile
    n_active = tend[-1]
    tj = jnp.arange(n_tiles, dtype=jnp.int32)
    te_raw = jnp.minimum(jnp.sum((tj[:, None] >= tend[None, :]).astype(jnp.int32), axis=1),
                         N_EXPERTS - 1)
    te_last = jnp.sum(jnp.where(tj == n_active - 1, te_raw, 0))
    te = jnp.where(tj < n_active, te_raw, te_last).astype(jnp.int32)
    first_row = jnp.sum(jnp.where(eid[:, :, None] == experts[None, None, :],
                                  (tstart * MOE_TILE)[None, None, :], 0), axis=-1)
    pos = (first_row + rank).astype(jnp.int32)
    ztile = jnp.maximum(tend - 1, 0).astype(jnp.int32)
    pos3 = pos.reshape(2, n_tok // ROW_TILE, ROW_TILE).transpose(1, 0, 2).reshape(
        n_tok // ROW_TILE, 1, 2 * ROW_TILE)
    return te, n_active.reshape(1).astype(jnp.int32), ztile, pos3, n_tiles


def _rotate_half_cols(w):
    half = QK_ROPE // 2
    return jnp.concatenate([w[..., half:], w[..., :half]], axis=-1)


def kernel(x, c, positions, w_ada, b_ada, norm1_g, w_in, conv_w, q_a_norm_g, w_q_b, kv_a_norm_g, w_kv_b, q_norm_g, k_norm_g, w_o, norm2_g, w_router_group, b_router_group, w_router_expert, b_router_expert, w_exp_gate, w_exp_up, w_exp_down):
    nb, seq, d = x.shape
    depth = w_ada.shape[0]
    n_tok = nb * seq
    assert d == D_MODEL and seq % ROW_TILE == 0 and seq % Q_TILE == 0 and Q_TILE % CHUNK == 0
    assert (2 * n_tok) % MOE_TILE == 0

    inv = ROPE_BASE ** (-jnp.arange(0, QK_ROPE, 2, dtype=F32) / QK_ROPE)
    ang = inv[None, :, None] * positions.astype(F32)[:, None, :]
    cos, sin = jnp.cos(ang), jnp.sin(ang)
    cst = jnp.concatenate([cos, cos, -sin, sin], axis=1)

    x2 = x.reshape(n_tok, d)
    for l in range(depth):
        wi = w_in[l]
        o_q = 3 * D_MODEL
        o_kv = o_q + Q_LORA
        o_kr = o_kv + KV_LORA
        o_gc = o_kr + QK_ROPE
        o_gm = o_gc + D_MODEL
        w_kr = wi[:, o_kr:o_gc]
        w_cat = jnp.concatenate(
            [wi[:, 0:o_q], wi[:, o_gc:o_gm], wi[:, o_gm:], wi[:, o_q:o_kv], wi[:, o_kv:o_kr]],
            axis=1).astype(BF16)
        w_krt = jnp.concatenate([w_kr, _rotate_half_cols(w_kr)], axis=1).T.astype(BF16)
        wq3 = w_q_b[l].reshape(Q_LORA, N_HEADS, QK_HEAD)
        wq = jnp.concatenate([wq3, _rotate_half_cols(wq3[..., QK_NOPE:])], axis=-1)
        wqt = wq.transpose(1, 2, 0).astype(BF16)
        wkvt = w_kv_b[l].reshape(KV_LORA, N_HEADS, QK_NOPE + V_HEAD).transpose(1, 2, 0).astype(BF16)
        gq = jnp.concatenate([q_norm_g[l], _rotate_half_cols(q_norm_g[l][QK_NOPE:])]).reshape(-1, 1)
        gk = jnp.concatenate([k_norm_g[l], _rotate_half_cols(k_norm_g[l][QK_NOPE:])]).reshape(-1, 1)
        wr_t = jnp.concatenate(
            [w_router_expert[l].T, w_router_group[l].T,
             jnp.zeros((ROUTER_ROWS - N_EXPERTS - N_GROUPS, d), F32)], axis=0).astype(BF16)
        br = jnp.concatenate(
            [b_router_expert[l], b_router_group[l],
             jnp.zeros((ROUTER_ROWS - N_EXPERTS - N_GROUPS,), F32)]).reshape(ROUTER_ROWS, 1)

        mod3 = _ada(c, w_ada[l], b_ada[l]).reshape(nb, 1, 6 * d)
        conv_p, gm, qn, kvn, krt = _inproj(
            x2, mod3, norm1_g[l].reshape(1, d), w_cat, w_krt, conv_w[l],
            q_a_norm_g[l].reshape(1, -1), kv_a_norm_g[l].reshape(1, -1), seq)
        y_mla = _attn(qn, kvn, krt, cst, wqt, wkvt, gq, gk, nb, seq)
        x1, h2r, eid, rank, cnt, cwt = _oproj(conv_p, gm, y_mla, x2, mod3, w_o[l].astype(BF16),
                                              norm2_g[l].reshape(1, d), wr_t, br, seq)
        te, na, ztile, pos3, n_tiles = _route_plan(eid, rank, cnt, n_tok)
        xs = _dispatch(ztile, na, pos3, h2r, n_tiles)
        yr = _moe(te, na, xs,
                  w_exp_gate[l].reshape(N_EXPERTS, d, D_EXPERT),
                  w_exp_up[l].reshape(N_EXPERTS, d, D_EXPERT),
                  w_exp_down[l].reshape(N_EXPERTS, D_EXPERT, d))
        x2 = _comb(x1, yr, pos3, cwt, mod3, seq)
    return x2.reshape(nb, seq, d)
```

```python
import functools
import math

import jax
import jax.numpy as jnp
from jax import lax
from jax.experimental import pallas as pl
from jax.experimental.pallas import tpu as pltpu

F32 = jnp.float32
BF16 = jnp.bfloat16

D_MODEL = 1024
N_HEADS = 8
QK_NOPE = 128
QK_ROPE = 64
QK_HEAD = QK_NOPE + QK_ROPE
V_HEAD = 128
Q_LORA = 384
KV_LORA = 256
CHUNK = 64
EPS = 1e-6
ROPE_BASE = 10000.0
N_GROUPS = 4
EXPERTS_PER_GROUP = 8
N_EXPERTS = N_GROUPS * EXPERTS_PER_GROUP
D_EXPERT = 256
CONV_K = 3

LANES = 128
SUBLANES = 8
VMEM_LIMIT = 56 * 1024 * 1024

ROW_TILE = 512
Q_TILE = 512
INPROJ_SUB = 2
HEADS_PER_STEP = 2
MOE_TILE = 512
X_SLOTS = 3
ADA_COLS = 1536
ROWS_PER_TOKEN = D_MODEL // LANES
DMA_UNROLL = 8
ROUTER_ROWS = 40


def _sigmoid(v):
    return 1.0 / (1.0 + jnp.exp(-v))


def _dot(a, b):
    return jnp.dot(a, b, preferred_element_type=F32)


def _store_rows(ref, val, n):
    for j in range(ROWS_PER_TOKEN):
        ref[pl.ds(j, n, stride=ROWS_PER_TOKEN), :] = val[:, j * LANES:(j + 1) * LANES]


def _load_rows(ref, n, offset=0):
    return [ref[pl.ds(offset + j, n, stride=ROWS_PER_TOKEN), :] for j in range(ROWS_PER_TOKEN)]


def _dot_nt(a, b):
    return lax.dot_general(a, b, (((1,), (1,)), ((), ())), preferred_element_type=F32)


def _ada_kernel(c_ref, w_ref, b_ref, o_ref):
    c = c_ref[...]
    act = (c * _sigmoid(c)).astype(BF16)
    o_ref[...] = _dot(act, w_ref[...].astype(BF16)) + b_ref[...]


def _ada(c, w_ada, b_ada):
    nb, d = c.shape
    n = w_ada.shape[1]
    return pl.pallas_call(
        _ada_kernel,
        grid=(n // ADA_COLS,),
        in_specs=[
            pl.BlockSpec((nb, d), lambda j: (0, 0)),
            pl.BlockSpec((d, ADA_COLS), lambda j: (0, j)),
            pl.BlockSpec((1, ADA_COLS), lambda j: (0, j)),
        ],
        out_specs=pl.BlockSpec((nb, ADA_COLS), lambda j: (0, j)),
        out_shape=jax.ShapeDtypeStruct((nb, n), F32),
        compiler_params=pltpu.CompilerParams(
            dimension_semantics=("arbitrary",), vmem_limit_bytes=VMEM_LIMIT),
        name="ada",
    )(c, w_ada, b_ada.reshape(1, n))


_C_ZX = 0
_C_ZB = D_MODEL
_C_ZC = 2 * D_MODEL
_C_GC = 3 * D_MODEL
_C_GM = 4 * D_MODEL
_C_QL = 5 * D_MODEL
_C_KV = _C_QL + Q_LORA
_C_END = _C_KV + KV_LORA


def _inproj_kernel(tiles_per_seq, x_ref, mod_ref, g1_ref, w_ref, wkr_ref, cw_ref, gq_ref, gkv_ref,
                   conv_ref, gm_ref, qn_ref, kvn_ref, krt_ref, ubuf):
    tm = x_ref.shape[0] // INPROJ_SUB

    @pl.when(pl.program_id(0) % tiles_per_seq == 0)
    def _():
        ubuf[0:SUBLANES, :] = jnp.zeros((SUBLANES, D_MODEL), F32)

    shift = mod_ref[0, :, 0:D_MODEL]
    scale = mod_ref[0, :, D_MODEL:2 * D_MODEL]
    for sub in range(INPROJ_SUB):
        rows = slice(sub * tm, (sub + 1) * tm)
        x = x_ref[rows, :]
        xn = x * lax.rsqrt(jnp.mean(x * x, axis=-1, keepdims=True) + EPS) * g1_ref[...]
        h = (xn * (1.0 + scale) + shift).astype(BF16)

        def proj(lo, hi):
            return _dot(h, w_ref[:, lo:hi])

        u = proj(_C_ZC, _C_GC) * proj(_C_ZX, _C_ZB)
        ubuf[SUBLANES:SUBLANES + tm, :] = u
        conv = (ubuf[SUBLANES - 2:SUBLANES - 2 + tm, :] * cw_ref[0:1, :]
                + ubuf[SUBLANES - 1:SUBLANES - 1 + tm, :] * cw_ref[1:2, :]
                + u * cw_ref[2:3, :])
        ubuf[0:SUBLANES, :] = ubuf[tm:tm + SUBLANES, :]
        y_conv = proj(_C_ZB, _C_ZC) * conv
        conv_ref[rows, :] = (_sigmoid(proj(_C_GC, _C_GM)) * y_conv).astype(BF16)
        gm_ref[rows, :] = _sigmoid(proj(_C_GM, _C_QL)).astype(BF16)

        ql = proj(_C_QL, _C_KV)
        qn_ref[rows, :] = (ql * lax.rsqrt(jnp.mean(ql * ql, axis=-1, keepdims=True) + EPS)
                           * gq_ref[...]).astype(BF16)
        kl = proj(_C_KV, _C_END)
        kvn_ref[rows, :] = (kl * lax.rsqrt(jnp.mean(kl * kl, axis=-1, keepdims=True) + EPS)
                            * gkv_ref[...]).astype(BF16)
        krt_ref[0, :, rows] = _dot_nt(wkr_ref[...], h)


def _inproj(x2, mod3, g1, w_cat, w_krt, conv_w, gq, gkv, seq):
    t = x2.shape[0]
    nb = t // seq
    tm = ROW_TILE * INPROJ_SUB
    tiles_per_seq = seq // tm
    row = lambda i: (i, 0)
    const = lambda i: (0, 0)
    return pl.pallas_call(
        functools.partial(_inproj_kernel, tiles_per_seq),
        grid=(t // tm,),
        in_specs=[
            pl.BlockSpec((tm, D_MODEL), row),
            pl.BlockSpec((1, 1, 6 * D_MODEL), lambda i: (i // tiles_per_seq, 0, 0)),
            pl.BlockSpec((1, D_MODEL), const),
            pl.BlockSpec((D_MODEL, _C_END), const),
            pl.BlockSpec((2 * QK_ROPE, D_MODEL), const),
            pl.BlockSpec((CONV_K, D_MODEL), const),
            pl.BlockSpec((1, Q_LORA), const),
            pl.BlockSpec((1, KV_LORA), const),
        ],
        out_specs=[
            pl.BlockSpec((tm, D_MODEL), row),
            pl.BlockSpec((tm, D_MODEL), row),
            pl.BlockSpec((tm, Q_LORA), row),
            pl.BlockSpec((tm, KV_LORA), row),
            pl.BlockSpec((1, 2 * QK_ROPE, tm),
                         lambda i: (i // tiles_per_seq, 0, i % tiles_per_seq)),
        ],
        out_shape=[
            jax.ShapeDtypeStruct((t, D_MODEL), BF16),
            jax.ShapeDtypeStruct((t, D_MODEL), BF16),
            jax.ShapeDtypeStruct((t, Q_LORA), BF16),
            jax.ShapeDtypeStruct((t, KV_LORA), BF16),
            jax.ShapeDtypeStruct((nb, 2 * QK_ROPE, seq), F32),
        ],
        scratch_shapes=[pltpu.VMEM((ROW_TILE + SUBLANES, D_MODEL), F32)],
        compiler_params=pltpu.CompilerParams(
            dimension_semantics=("arbitrary",), vmem_limit_bytes=VMEM_LIMIT),
        name="inproj",
    )(x2, mod3, g1, w_cat, w_krt, conv_w, gq, gkv)


def _attn_kernel(qn_ref, kvn_ref, krt_ref, cst_ref, wqt_ref, wkvt_ref, gq_ref, gk_ref,
                 o_ref, *scratch):
    per_head = len(scratch) // HEADS_PER_STEP
    heads = [_attn_head(hh, qn_ref, kvn_ref, krt_ref, cst_ref, wqt_ref, wkvt_ref, gq_ref, gk_ref, o_ref,
                        *scratch[hh * per_head:(hh + 1) * per_head]) for hh in range(HEADS_PER_STEP)]
    nq = qn_ref.shape[0] // Q_TILE
    for scores, _ in heads:
        scores(nq - 1)
    for i in range(nq - 1, -1, -1):
        for scores, finish in heads:
            if i > 0:
                scores(i - 1)
            finish(i)


def _attn_head(hh, qn_ref, kvn_ref, krt_ref, cst_ref, wqt_ref, wkvt_ref, gq_ref, gk_ref,
               o_ref, qt_s, k_s, vt_s, s_buf0, s_buf1):
    seq = qn_ref.shape[0]
    cos_t = cst_ref[0, 0:QK_ROPE, :]
    sin_t = cst_ref[0, QK_ROPE:, :]

    def normed_rope(nope, r, rr, g, extra_scale):
        ss = jnp.sum(nope * nope, axis=0, keepdims=True) + jnp.sum(r * r, axis=0, keepdims=True)
        scale = lax.rsqrt(ss * (1.0 / QK_HEAD) + EPS) * extra_scale
        rope = r * g[QK_NOPE:QK_HEAD] * cos_t + rr * g[QK_HEAD:] * sin_t
        return (nope * g[0:QK_NOPE] * scale).astype(BF16), (rope * scale).astype(BF16)

    qt = _dot_nt(wqt_ref[hh], qn_ref[...])
    q_n, q_r = normed_rope(qt[0:QK_NOPE], qt[QK_NOPE:QK_HEAD], qt[QK_HEAD:], gq_ref[...],
                           QK_HEAD ** -0.5 * math.log2(math.e))
    qt_s[0:QK_NOPE, :] = q_n
    qt_s[QK_NOPE:QK_HEAD, :] = q_r
    qt_s[QK_HEAD:, :] = jnp.zeros((QK_ROPE, seq), BF16)

    kvt = _dot_nt(wkvt_ref[hh], kvn_ref[...])
    krt = krt_ref[0]
    k_n, k_r = normed_rope(kvt[0:QK_NOPE], krt[0:QK_ROPE], krt[QK_ROPE:], gk_ref[...], 1.0)
    kt = jnp.concatenate([k_n, k_r, jnp.zeros((QK_ROPE, seq), BF16)], axis=0)
    k_s[...] = kt.T
    vt_s[0:V_HEAD, :] = kvt[QK_NOPE:].astype(BF16)
    vt_s[V_HEAD:, :] = jnp.ones((vt_s.shape[0] - V_HEAD, seq), BF16)

    tq = Q_TILE
    kchunk = lax.broadcasted_iota(jnp.int32, (tq, tq), 0) // CHUNK
    qchunk = lax.broadcasted_iota(jnp.int32, (tq, tq), 1) // CHUNK
    diag_ok = kchunk <= qchunk
    neg = jnp.finfo(F32).min

    def scores(i):
        q0 = i * tq
        sb = s_buf0 if i % 2 == 0 else s_buf1
        q = qt_s[:, q0:q0 + tq]
        if i > 0:
            sb[0:q0, :] = _dot(k_s[0:q0, :], q)
        sb[q0:q0 + tq, :] = jnp.where(diag_ok, _dot(k_s[q0:q0 + tq, :], q), neg)

    def finish(i):
        q0 = i * tq
        kend = q0 + tq
        sb = s_buf0 if i % 2 == 0 else s_buf1
        m = jnp.max(sb[0:kend, :], axis=0, keepdims=True)
        acc = _dot(vt_s[:, 0:kend], jnp.exp2(sb[0:kend, :] - m).astype(BF16))
        o_t = acc[0:V_HEAD] / acc[V_HEAD:V_HEAD + 1]
        o_ref[q0:q0 + tq, hh * V_HEAD:(hh + 1) * V_HEAD] = o_t.T.astype(o_ref.dtype)

    return scores, finish


def _attn(qn, kvn, krt, cst, wqt, wkvt, gq, gk, nb, seq):
    t = qn.shape[0]
    per_b = lambda b, h: (b, 0)
    per_b3 = lambda b, h: (b, 0, 0)
    per_h = lambda b, h: (h, 0, 0)
    const = lambda b, h: (0, 0)
    qk_rows = QK_NOPE + 2 * QK_ROPE
    return pl.pallas_call(
        _attn_kernel,
        grid=(nb, N_HEADS // HEADS_PER_STEP),
        in_specs=[
            pl.BlockSpec((seq, Q_LORA), per_b),
            pl.BlockSpec((seq, KV_LORA), per_b),
            pl.BlockSpec((1, 2 * QK_ROPE, seq), per_b3),
            pl.BlockSpec((1, 2 * QK_ROPE, seq), per_b3),
            pl.BlockSpec((HEADS_PER_STEP, qk_rows, Q_LORA), per_h),
            pl.BlockSpec((HEADS_PER_STEP, QK_NOPE + V_HEAD, KV_LORA), per_h),
            pl.BlockSpec((qk_rows, 1), const),
            pl.BlockSpec((qk_rows, 1), const),
        ],
        out_specs=pl.BlockSpec((seq, HEADS_PER_STEP * V_HEAD), lambda b, h: (b, h)),
        out_shape=jax.ShapeDtypeStruct((t, N_HEADS * V_HEAD), BF16),
        scratch_shapes=[
            pltpu.VMEM((qk_rows, seq), BF16),
            pltpu.VMEM((seq, qk_rows), BF16),
            pltpu.VMEM((V_HEAD + 2 * SUBLANES, seq), BF16),
            pltpu.VMEM((seq, Q_TILE), F32),
            pltpu.VMEM((seq, Q_TILE), F32),
        ] * HEADS_PER_STEP,
        compiler_params=pltpu.CompilerParams(
            dimension_semantics=("arbitrary", "arbitrary"), vmem_limit_bytes=VMEM_LIMIT),
        name="attn",
    )(qn, kvn, krt, cst, wqt, wkvt, gq, gk)


def _oproj_kernel(n_tiles, conv_ref, gm_ref, y_ref, x_ref, mod_ref, wo_ref, g2_ref, wr_ref, br_ref,
                  x1_ref, pos_ref, alloc_ref, cwt_ref, xs_hbm,
                  fill, cur, nfree, h2s, pos_v, pos_s, fin_v, fin_s, zbuf, rsem, psem, fsem):
    tm = x_ref.shape[0]
    i = pl.program_id(0)
    last = pl.num_programs(0) - 1
    slot = i % 2

    def rows_wait(s):
        for _ in range(2):
            pltpu.make_async_copy(h2s.at[s], xs_hbm.at[pl.ds(0, tm * ROWS_PER_TOKEN), :], rsem.at[s]).wait()

    def issue_rows(s):
        def body(c, carry):
            for k in range(DMA_UNROLL):
                t = c * DMA_UNROLL + k
                src = h2s.at[s, pl.ds(pl.multiple_of(t * ROWS_PER_TOKEN, ROWS_PER_TOKEN), ROWS_PER_TOKEN), :]
                for e_slot in range(2):
                    dst_row = pl.multiple_of(pos_s[s, e_slot, t] * ROWS_PER_TOKEN, ROWS_PER_TOKEN)
                    pltpu.make_async_copy(src, xs_hbm.at[pl.ds(dst_row, ROWS_PER_TOKEN), :],
                                          rsem.at[s]).start(priority=e_slot)
            return carry
        lax.fori_loop(0, tm // DMA_UNROLL, body, 0)

    def pos_copy(s):
        return pltpu.make_async_copy(pos_v.at[s], pos_s.at[s], psem.at[s])

    @pl.when(i == 0)
    def _():
        fill[...] = jnp.full(fill.shape, float(MOE_TILE), F32)
        cur[...] = jnp.zeros(cur.shape, F32)
        nfree[...] = jnp.zeros(nfree.shape, F32)
        pos_v[...] = jnp.zeros(pos_v.shape, jnp.int32)

    @pl.when(i >= 2)
    def _():
        rows_wait(slot)

    @pl.when(i >= 1)
    def _():
        pos_copy(1 - slot).wait()
        issue_rows(1 - slot)

    merged = conv_ref[...].astype(F32) + gm_ref[...].astype(F32) * y_ref[...].astype(F32)
    att = _dot(merged.astype(BF16), wo_ref[...])
    gate1 = mod_ref[0, :, 2 * D_MODEL:3 * D_MODEL]
    shift2 = mod_ref[0, :, 3 * D_MODEL:4 * D_MODEL]
    scale2 = mod_ref[0, :, 4 * D_MODEL:5 * D_MODEL]
    x1 = x_ref[...] + gate1 * att
    x1_ref[...] = x1
    xn = x1 * lax.rsqrt(jnp.mean(x1 * x1, axis=-1, keepdims=True) + EPS) * g2_ref[...]
    h2 = xn * (1.0 + scale2) + shift2
    _store_rows(h2s.at[slot], h2, tm)

    lt = _dot_nt(wr_ref[...], h2.astype(BF16)) + br_ref[...]
    gl = [lt[N_EXPERTS + r:N_EXPERTS + r + 1, :] for r in range(N_GROUPS)]
    gmax = jnp.maximum(jnp.maximum(gl[0], gl[1]), jnp.maximum(gl[2], gl[3]))
    gidx = jnp.full(gmax.shape, N_GROUPS - 1, jnp.int32)
    for r in range(N_GROUPS - 2, -1, -1):
        gidx = jnp.where(gl[r] == gmax, r, gidx)
    gsum = jnp.exp(gl[0] - gmax)
    for r in range(1, N_GROUPS):
        gsum = gsum + jnp.exp(gl[r] - gmax)
    p_group = 1.0 / gsum
    es = lt[(N_GROUPS - 1) * EXPERTS_PER_GROUP:N_GROUPS * EXPERTS_PER_GROUP, :]
    for r in range(N_GROUPS - 2, -1, -1):
        es = jnp.where(gidx == r, lt[r * EXPERTS_PER_GROUP:(r + 1) * EXPERTS_PER_GROUP, :], es)
    row = lax.broadcasted_iota(jnp.int32, es.shape, 0)
    m1 = jnp.max(es, axis=0, keepdims=True)
    i1 = jnp.min(jnp.where(es == m1, row, EXPERTS_PER_GROUP), axis=0, keepdims=True)
    es2 = jnp.where(row == i1, -jnp.inf, es)
    m2 = jnp.max(es2, axis=0, keepdims=True)
    i2 = jnp.min(jnp.where(es2 == m2, row, EXPERTS_PER_GROUP), axis=0, keepdims=True)
    e2 = jnp.exp(m2 - m1)
    w1 = p_group / (1.0 + e2)
    w2 = w1 * e2
    eid0 = gidx * EXPERTS_PER_GROUP + i1
    eid1 = gidx * EXPERTS_PER_GROUP + i2
    erow = lax.broadcasted_iota(jnp.int32, (N_EXPERTS, tm), 0)
    oh0 = erow == eid0
    oh1 = erow == eid1
    both = jnp.where(oh0, 1.0, jnp.where(oh1, 1.0, 0.0))
    earlier = (lax.broadcasted_iota(jnp.int32, (tm, tm), 0)
               < lax.broadcasted_iota(jnp.int32, (tm, tm), 1))
    prefix = _dot(both.astype(BF16), jnp.where(earlier, 1.0, 0.0).astype(BF16))
    count = jnp.sum(both, axis=1, keepdims=True)
    need = jnp.where(fill[...] + count > MOE_TILE, 1.0, 0.0)
    lower = (lax.broadcasted_iota(jnp.int32, (N_EXPERTS, N_EXPERTS), 1)
             < lax.broadcasted_iota(jnp.int32, (N_EXPERTS, N_EXPERTS), 0))
    need_b = jnp.broadcast_to(need, (N_EXPERTS, LANES)).astype(BF16)
    before = _dot(jnp.where(lower, 1.0, 0.0).astype(BF16), need_b)[:, 0:1]
    new_id = nfree[...] + before
    row_in_tile = fill[...] + prefix
    posmat = jnp.where(row_in_tile < MOE_TILE,
                       cur[...] * MOE_TILE + row_in_tile,
                       new_id * MOE_TILE + row_in_tile - MOE_TILE)
    pos0 = jnp.sum(jnp.where(oh0, posmat, 0.0), axis=0, keepdims=True).astype(jnp.int32)
    pos1 = jnp.sum(jnp.where(oh1, posmat, 0.0), axis=0, keepdims=True).astype(jnp.int32)
    pos_ref[0:1, :] = pos0
    pos_ref[1:2, :] = pos1
    pos_v[slot, 0:1, :] = pos0
    pos_v[slot, 1:2, :] = pos1
    lane = lax.broadcasted_iota(jnp.int32, (N_EXPERTS, LANES), 1)
    alloc_ref[0] = jnp.where(lane == 0, need, jnp.where(lane == 1, new_id, 0.0))
    took = need > 0.5
    fill[...] = jnp.where(took, fill[...] + count - MOE_TILE, fill[...] + count)
    cur[...] = jnp.where(took, new_id, cur[...])
    nfree[...] = nfree[...] + jnp.sum(need, axis=0, keepdims=True)
    pos_copy(slot).start()
    wrow = lax.broadcasted_iota(jnp.int32, (LANES, tm), 0)
    wmat = jnp.where(wrow == 0, w1, jnp.where(wrow == 1, w2, 0.0))
    cwt_ref[...] = wmat.T

    @pl.when(i == last)
    def _():
        pos_copy(slot).wait()
        issue_rows(slot)

        diag = lax.broadcasted_iota(jnp.int32, (N_EXPERTS, LANES), 0) == lax.broadcasted_iota(
            jnp.int32, (N_EXPERTS, LANES), 1)
        as_row = lambda col: jnp.sum(jnp.where(diag, col, 0.0), axis=0, keepdims=True).astype(jnp.int32)
        fin_v[...] = jnp.zeros(fin_v.shape, jnp.int32)
        fin_v[0:1, :] = as_row(fill[...])
        fin_v[1:2, :] = as_row(cur[...])
        fin_v[2:3, :] = jnp.broadcast_to(nfree[...], (1, LANES)).astype(jnp.int32)
        fin = pltpu.make_async_copy(fin_v, fin_s, fsem)
        fin.start()
        fin.wait()
        zbuf[...] = jnp.zeros(zbuf.shape, F32)

        def zero_rows(first_row, n_rows):
            cp = pltpu.make_async_copy(
                zbuf.at[pl.ds(0, n_rows * ROWS_PER_TOKEN), :],
                xs_hbm.at[pl.ds(pl.multiple_of(first_row * ROWS_PER_TOKEN, ROWS_PER_TOKEN),
                                n_rows * ROWS_PER_TOKEN), :], fsem)
            cp.start()
            cp.wait()

        def expert_tail(e, carry):
            used = fin_s[0, e]
            rem = MOE_TILE - used
            base = fin_s[1, e] * MOE_TILE + used
            chunk = MOE_TILE // 2
            while chunk >= 1:
                @pl.when((rem & chunk) != 0)
                def _(chunk=chunk):
                    zero_rows(base + (rem & ~(2 * chunk - 1)), chunk)
                chunk //= 2
            return carry
        lax.fori_loop(0, N_EXPERTS, expert_tail, 0)

        def free_tile(t, carry):
            zero_rows(t * MOE_TILE, MOE_TILE)
            return carry
        lax.fori_loop(fin_s[2, 0], n_tiles, free_tile, 0)

        @pl.when(i >= 1)
        def _():
            rows_wait(1 - slot)
        rows_wait(slot)


def _oproj(conv_p, gm, y_mla, x2, mod3, wo, g2, wr_t, br, seq):
    t = x2.shape[0]
    tm = ROW_TILE
    assert tm <= MOE_TILE
    tiles_per_seq = seq // tm
    n_steps = t // tm
    n_tiles = (2 * t) // MOE_TILE + N_EXPERTS
    row = lambda i: (i, 0)
    const = lambda i: (0, 0)
    return pl.pallas_call(
        functools.partial(_oproj_kernel, n_tiles),
        grid=(n_steps,),
        in_specs=[
            pl.BlockSpec((tm, D_MODEL), row),
            pl.BlockSpec((tm, D_MODEL), row),
            pl.BlockSpec((tm, D_MODEL), row),
            pl.BlockSpec((tm, D_MODEL), row),
            pl.BlockSpec((1, 1, 6 * D_MODEL), lambda i: (i // tiles_per_seq, 0, 0)),
            pl.BlockSpec((D_MODEL, D_MODEL), const),
            pl.BlockSpec((1, D_MODEL), const),
            pl.BlockSpec((ROUTER_ROWS, D_MODEL), const),
            pl.BlockSpec((ROUTER_ROWS, 1), const),
        ],
        out_specs=[
            pl.BlockSpec((tm, D_MODEL), row),
            pl.BlockSpec((2, tm), lambda i: (0, i)),
            pl.BlockSpec((1, N_EXPERTS, LANES), lambda i: (i, 0, 0)),
            pl.BlockSpec((tm, LANES), row),
            pl.BlockSpec(memory_space=pl.ANY),
        ],
        out_shape=[
            jax.ShapeDtypeStruct((t, D_MODEL), F32),
            jax.ShapeDtypeStruct((2, t), jnp.int32),
            jax.ShapeDtypeStruct((n_steps, N_EXPERTS, LANES), F32),
            jax.ShapeDtypeStruct((t, LANES), F32),
            jax.ShapeDtypeStruct((n_tiles * MOE_TILE * ROWS_PER_TOKEN, LANES), F32),
        ],
        scratch_shapes=[
            pltpu.VMEM((N_EXPERTS, 1), F32),
            pltpu.VMEM((N_EXPERTS, 1), F32),
            pltpu.VMEM((1, 1), F32),
            pltpu.VMEM((2, tm * ROWS_PER_TOKEN, LANES), F32),
            pltpu.VMEM((2, SUBLANES, tm), jnp.int32),
            pltpu.SMEM((2, SUBLANES, tm), jnp.int32),
            pltpu.VMEM((SUBLANES, LANES), jnp.int32),
            pltpu.SMEM((SUBLANES, LANES), jnp.int32),
            pltpu.VMEM((MOE_TILE * ROWS_PER_TOKEN, LANES), F32),
            pltpu.SemaphoreType.DMA((2,)),
            pltpu.SemaphoreType.DMA((2,)),
            pltpu.SemaphoreType.DMA(()),
        ],
        compiler_params=pltpu.CompilerParams(
            dimension_semantics=("arbitrary",), vmem_limit_bytes=VMEM_LIMIT),
        name="oproj",
    )(conv_p, gm, y_mla, x2, mod3, wo, g2, wr_t, br)


def _moe_kernel(te_ref, perm_ref, na_ref, x_hbm, wg_ref, wu_ref, wd_ref, y_ref, xbuf, wgu_s, wd_s, sem):
    i = pl.program_id(0)
    n_active = na_ref[0]
    active = i < n_active
    tile_rows = MOE_TILE * ROWS_PER_TOKEN

    def fetch(step):
        slot = step % X_SLOTS
        row = pl.multiple_of(perm_ref[step] * tile_rows, tile_rows)
        return pltpu.make_async_copy(x_hbm.at[pl.ds(row, tile_rows), :], xbuf.at[slot], sem.at[slot])

    @pl.when(i == 0)
    def _():
        for k in range(X_SLOTS - 1):
            @pl.when(k < n_active)
            def _():
                fetch(k).start()

    @pl.when(i + X_SLOTS - 1 < n_active)
    def _():
        fetch(i + X_SLOTS - 1).start()

    new_expert = jnp.logical_or(i == 0, te_ref[i] != te_ref[jnp.maximum(i - 1, 0)])

    @pl.when(jnp.logical_and(active, new_expert))
    def _():
        wgu_s[:, 0:D_EXPERT] = wg_ref[0].astype(BF16)
        wgu_s[:, D_EXPERT:] = wu_ref[0].astype(BF16)
        wd_s[...] = wd_ref[0].astype(BF16)

    @pl.when(active)
    def _():
        fetch(i).wait()
        x = jnp.concatenate([col.astype(BF16) for col in _load_rows(xbuf.at[i % X_SLOTS], MOE_TILE)], axis=1)
        gu = _dot(x, wgu_s[...])
        g = gu[:, 0:D_EXPERT]
        a = (g * _sigmoid(g)) * gu[:, D_EXPERT:]
        y = _dot(a.astype(BF16), wd_s[...])
        _store_rows(y_ref, y, MOE_TILE)

    @pl.when(jnp.logical_not(active))
    def _():
        y_ref[...] = jnp.zeros(y_ref.shape, y_ref.dtype)


def _moe(te, perm, na, xs, wg, wu, wd):
    nt = te.shape[0]
    tile_rows = MOE_TILE * ROWS_PER_TOKEN
    wspec = lambda shape: pl.BlockSpec((1,) + shape, lambda i, te_r, pm_r, na_r: (te_r[i], 0, 0))
    grid_spec = pltpu.PrefetchScalarGridSpec(
        num_scalar_prefetch=3,
        grid=(nt,),
        in_specs=[
            pl.BlockSpec(memory_space=pl.ANY),
            wspec((D_MODEL, D_EXPERT)),
            wspec((D_MODEL, D_EXPERT)),
            wspec((D_EXPERT, D_MODEL)),
        ],
        out_specs=pl.BlockSpec((tile_rows, LANES), lambda i, te_r, pm_r, na_r: (pm_r[i], 0)),
        scratch_shapes=[
            pltpu.VMEM((X_SLOTS, tile_rows, LANES), F32),
            pltpu.VMEM((D_MODEL, 2 * D_EXPERT), BF16),
            pltpu.VMEM((D_EXPERT, D_MODEL), BF16),
            pltpu.SemaphoreType.DMA((X_SLOTS,)),
        ],
    )
    return pl.pallas_call(
        _moe_kernel,
        grid_spec=grid_spec,
        out_shape=jax.ShapeDtypeStruct(xs.shape, xs.dtype),
        compiler_params=pltpu.CompilerParams(
            dimension_semantics=("arbitrary",), vmem_limit_bytes=VMEM_LIMIT),
        name="moe",
    )(te, perm, na, xs, wg, wu, wd)


def _comb_kernel(pos0_ref, posn_ref, x1_ref, cwt_ref, mod_ref, y_hbm, o_ref, ybuf, sem):
    tm = x1_ref.shape[0]
    i = pl.program_id(0)
    slot = i % 2

    def gather(pos_ref, dst_slot):
        def body(c, carry):
            for k in range(DMA_UNROLL):
                r = c * DMA_UNROLL + k
                src_row = pl.multiple_of(pos_ref[0, 0, r] * ROWS_PER_TOKEN, ROWS_PER_TOKEN)
                dst_row = pl.multiple_of(r * ROWS_PER_TOKEN, ROWS_PER_TOKEN)
                pltpu.make_async_copy(y_hbm.at[pl.ds(src_row, ROWS_PER_TOKEN), :],
                                      ybuf.at[dst_slot, pl.ds(dst_row, ROWS_PER_TOKEN), :],
                                      sem.at[dst_slot]).start(priority=k % 2)
            return carry
        lax.fori_loop(0, 2 * tm // DMA_UNROLL, body, 0)

    @pl.when(i == 0)
    def _():
        gather(pos0_ref, 0)

    @pl.when(i + 1 < pl.num_programs(0))
    def _():
        gather(posn_ref, 1 - slot)

    pltpu.make_async_copy(y_hbm.at[pl.ds(0, 2 * tm * ROWS_PER_TOKEN), :], ybuf.at[slot],
                          sem.at[slot]).wait()
    c0 = cwt_ref[:, 0:1]
    c1 = cwt_ref[:, 1:2]
    y0 = _load_rows(ybuf.at[slot], tm)
    y1 = _load_rows(ybuf.at[slot], tm, offset=tm * ROWS_PER_TOKEN)
    for j in range(ROWS_PER_TOKEN):
        cols = slice(j * LANES, (j + 1) * LANES)
        gate2 = mod_ref[0, :, 5 * D_MODEL + j * LANES:5 * D_MODEL + (j + 1) * LANES]
        o_ref[:, cols] = x1_ref[:, cols] + gate2 * (c0 * y0[j] + c1 * y1[j])


def _comb(x1, yr, pos3, cwt, mod3, seq):
    t = x1.shape[0]
    tm = ROW_TILE
    tiles_per_seq = seq // tm
    n_steps = t // tm
    row = lambda i: (i, 0)
    smem_blk = lambda f: pl.BlockSpec((1, 1, 2 * tm), f, memory_space=pltpu.SMEM)
    return pl.pallas_call(
        _comb_kernel,
        grid=(n_steps,),
        in_specs=[
            smem_blk(lambda i: (0, 0, 0)),
            smem_blk(lambda i: (jnp.minimum(i + 1, n_steps - 1), 0, 0)),
            pl.BlockSpec((tm, D_MODEL), row),
            pl.BlockSpec((tm, LANES), row),
            pl.BlockSpec((1, 1, 6 * D_MODEL), lambda i: (i // tiles_per_seq, 0, 0)),
            pl.BlockSpec(memory_space=pl.ANY),
        ],
        out_specs=pl.BlockSpec((tm, D_MODEL), row),
        out_shape=jax.ShapeDtypeStruct((t, D_MODEL), F32),
        scratch_shapes=[
            pltpu.VMEM((2, 2 * tm * ROWS_PER_TOKEN, LANES), F32),
            pltpu.SemaphoreType.DMA((2,)),
        ],
        compiler_params=pltpu.CompilerParams(
            dimension_semantics=("arbitrary",), vmem_limit_bytes=VMEM_LIMIT),
        name="comb",
    )(pos3, pos3, x1, cwt, mod3, yr)


def _tile_plan(alloc, n_tok):
    n_tiles = (2 * n_tok) // MOE_TILE + N_EXPERTS
    took = alloc[:, :, 0] > 0.5
    tile_id = alloc[:, :, 1].astype(jnp.int32)
    tiles = jnp.arange(n_tiles, dtype=jnp.int32)
    experts = jnp.arange(N_EXPERTS, dtype=jnp.int32)
    hit = jnp.logical_and(took[:, :, None], tile_id[:, :, None] == tiles[None, None, :])
    owner = jnp.sum(jnp.where(hit, experts[None, :, None], 0), axis=(0, 1))
    used = jnp.any(hit, axis=(0, 1))
    key = jnp.where(used, owner, N_EXPERTS).astype(jnp.int32)
    te_sorted, perm = lax.sort((key, tiles), num_keys=1)
    n_active = jnp.sum(used.astype(jnp.int32))
    te_last = jnp.sum(jnp.where(tiles == n_active - 1, te_sorted, 0))
    te = jnp.where(tiles < n_active, te_sorted, te_last).astype(jnp.int32)
    return te, perm.astype(jnp.int32), n_active.reshape(1).astype(jnp.int32)


def _rotate_half_cols(w):
    half = QK_ROPE // 2
    return jnp.concatenate([w[..., half:], w[..., :half]], axis=-1)


def kernel(x, c, positions, w_ada, b_ada, norm1_g, w_in, conv_w, q_a_norm_g, w_q_b, kv_a_norm_g, w_kv_b, q_norm_g, k_norm_g, w_o, norm2_g, w_router_group, b_router_group, w_router_expert, b_router_expert, w_exp_gate, w_exp_up, w_exp_down):
    nb, seq, d = x.shape
    depth = w_ada.shape[0]
    n_tok = nb * seq
    assert d == D_MODEL and seq % ROW_TILE == 0 and seq % Q_TILE == 0 and Q_TILE % CHUNK == 0
    assert (2 * n_tok) % MOE_TILE == 0

    inv = ROPE_BASE ** (-jnp.arange(0, QK_ROPE, 2, dtype=F32) / QK_ROPE)
    ang = inv[None, :, None] * positions.astype(F32)[:, None, :]
    cos, sin = jnp.cos(ang), jnp.sin(ang)
    cst = jnp.concatenate([cos, cos, -sin, sin], axis=1)

    x2 = x.reshape(n_tok, d)
    for l in range(depth):
        wi = w_in[l]
        o_q = 3 * D_MODEL
        o_kv = o_q + Q_LORA
        o_kr = o_kv + KV_LORA
        o_gc = o_kr + QK_ROPE
        o_gm = o_gc + D_MODEL
        w_kr = wi[:, o_kr:o_gc]
        w_cat = jnp.concatenate(
            [wi[:, 0:o_q], wi[:, o_gc:o_gm], wi[:, o_gm:], wi[:, o_q:o_kv], wi[:, o_kv:o_kr]],
            axis=1).astype(BF16)
        w_krt = jnp.concatenate([w_kr, _rotate_half_cols(w_kr)], axis=1).T.astype(BF16)
        wq3 = w_q_b[l].reshape(Q_LORA, N_HEADS, QK_HEAD)
        wq = jnp.concatenate([wq3, _rotate_half_cols(wq3[..., QK_NOPE:])], axis=-1)
        wqt = wq.transpose(1, 2, 0).astype(BF16)
        wkvt = w_kv_b[l].reshape(KV_LORA, N_HEADS, QK_NOPE + V_HEAD).transpose(1, 2, 0).astype(BF16)
        gq = jnp.concatenate([q_norm_g[l], _rotate_half_cols(q_norm_g[l][QK_NOPE:])]).reshape(-1, 1)
        gk = jnp.concatenate([k_norm_g[l], _rotate_half_cols(k_norm_g[l][QK_NOPE:])]).reshape(-1, 1)
        wr_t = jnp.concatenate(
            [w_router_expert[l].T, w_router_group[l].T,
             jnp.zeros((ROUTER_ROWS - N_EXPERTS - N_GROUPS, d), F32)], axis=0).astype(BF16)
        br = jnp.concatenate(
            [b_router_expert[l], b_router_group[l],
             jnp.zeros((ROUTER_ROWS - N_EXPERTS - N_GROUPS,), F32)]).reshape(ROUTER_ROWS, 1)

        mod3 = _ada(c, w_ada[l], b_ada[l]).reshape(nb, 1, 6 * d)
        conv_p, gm, qn, kvn, krt = _inproj(
            x2, mod3, norm1_g[l].reshape(1, d), w_cat, w_krt, conv_w[l],
            q_a_norm_g[l].reshape(1, -1), kv_a_norm_g[l].reshape(1, -1), seq)
        y_mla = _attn(qn, kvn, krt, cst, wqt, wkvt, gq, gk, nb, seq)
        x1, pos, alloc, cwt, xs = _oproj(conv_p, gm, y_mla, x2, mod3, w_o[l].astype(BF16),
                                         norm2_g[l].reshape(1, d), wr_t, br, seq)
        te, perm, na = _tile_plan(alloc, n_tok)
        pos3 = pos.reshape(2, n_tok // ROW_TILE, ROW_TILE).transpose(1, 0, 2).reshape(
            n_tok // ROW_TILE, 1, 2 * ROW_TILE)
        yr = _moe(te, perm, na, xs,
                  w_exp_gate[l].reshape(N_EXPERTS, d, D_EXPERT),
                  w_exp_up[l].reshape(N_EXPERTS, d, D_EXPERT),
                  w_exp_down[l].reshape(N_EXPERTS, D_EXPERT, d))
        x2 = _comb(x1, yr, pos3, cwt, mod3, seq)
    return x2.reshape(nb, seq, d)
```

```python
import functools
import math

import jax
import jax.numpy as jnp
from jax import lax
from jax.experimental import pallas as pl
from jax.experimental.pallas import tpu as pltpu

F32 = jnp.float32
BF16 = jnp.bfloat16

D_MODEL = 1024
N_HEADS = 8
QK_NOPE = 128
QK_ROPE = 64
QK_HEAD = QK_NOPE + QK_ROPE
V_HEAD = 128
Q_LORA = 384
KV_LORA = 256
CHUNK = 64
EPS = 1e-6
ROPE_BASE = 10000.0
N_GROUPS = 4
EXPERTS_PER_GROUP = 8
N_EXPERTS = N_GROUPS * EXPERTS_PER_GROUP
D_EXPERT = 256
CONV_K = 3

LANES = 128
SUBLANES = 8
VMEM_LIMIT = 56 * 1024 * 1024

ROW_TILE = 512
Q_TILE = 512
INPROJ_SUB = 2
HEADS_PER_STEP = 2
MOE_TILE = 512
X_SLOTS = 3
ADA_COLS = 1536
ROWS_PER_TOKEN = D_MODEL // LANES
DMA_UNROLL = 8
ROUTER_ROWS = 40


def _sigmoid(v):
    return 1.0 / (1.0 + jnp.exp(-v))


def _dot(a, b):
    return jnp.dot(a, b, preferred_element_type=F32)


def _store_rows(ref, val, n):
    for j in range(ROWS_PER_TOKEN):
        ref[pl.ds(j, n, stride=ROWS_PER_TOKEN), :] = val[:, j * LANES:(j + 1) * LANES]


def _load_rows(ref, n, offset=0):
    return [ref[pl.ds(offset + j, n, stride=ROWS_PER_TOKEN), :] for j in range(ROWS_PER_TOKEN)]


def _dot_nt(a, b):
    return lax.dot_general(a, b, (((1,), (1,)), ((), ())), preferred_element_type=F32)


def _ada_kernel(c_ref, w_ref, b_ref, o_ref):
    c = c_ref[...]
    act = (c * _sigmoid(c)).astype(BF16)
    o_ref[...] = _dot(act, w_ref[...].astype(BF16)) + b_ref[...]


def _ada(c, w_ada, b_ada):
    nb, d = c.shape
    n = w_ada.shape[1]
    return pl.pallas_call(
        _ada_kernel,
        grid=(n // ADA_COLS,),
        in_specs=[
            pl.BlockSpec((nb, d), lambda j: (0, 0)),
            pl.BlockSpec((d, ADA_COLS), lambda j: (0, j)),
            pl.BlockSpec((1, ADA_COLS), lambda j: (0, j)),
        ],
        out_specs=pl.BlockSpec((nb, ADA_COLS), lambda j: (0, j)),
        out_shape=jax.ShapeDtypeStruct((nb, n), F32),
        compiler_params=pltpu.CompilerParams(
            dimension_semantics=("arbitrary",), vmem_limit_bytes=VMEM_LIMIT),
        name="ada",
    )(c, w_ada, b_ada.reshape(1, n))


def _inproj_kernel(tiles_per_seq, x_ref, mod_ref, g1_ref, wmix_ref, wgate_ref, wlat_ref, wkr_ref,
                   cw_ref, gq_ref, gkv_ref,
                   conv_ref, gm_ref, qn_ref, kvn_ref, krt_ref, ubuf):
    tm = x_ref.shape[0] // INPROJ_SUB

    @pl.when(pl.program_id(0) % tiles_per_seq == 0)
    def _():
        ubuf[0:SUBLANES, :] = jnp.zeros((SUBLANES, D_MODEL), F32)

    shift = mod_ref[0, :, 0:D_MODEL]
    scale = mod_ref[0, :, D_MODEL:2 * D_MODEL]
    for sub in range(INPROJ_SUB):
        rows = slice(sub * tm, (sub + 1) * tm)
        x = x_ref[rows, :]
        xn = x * lax.rsqrt(jnp.mean(x * x, axis=-1, keepdims=True) + EPS) * g1_ref[...]
        h = (xn * (1.0 + scale) + shift).astype(BF16)

        def proj(w_ref, lo, width):
            return _dot(h, w_ref[:, lo:lo + width])

        u = proj(wmix_ref, 2 * D_MODEL, D_MODEL) * proj(wmix_ref, 0, D_MODEL)
        ubuf[SUBLANES:SUBLANES + tm, :] = u
        conv = (ubuf[SUBLANES - 2:SUBLANES - 2 + tm, :] * cw_ref[0:1, :]
                + ubuf[SUBLANES - 1:SUBLANES - 1 + tm, :] * cw_ref[1:2, :]
                + u * cw_ref[2:3, :])
        ubuf[0:SUBLANES, :] = ubuf[tm:tm + SUBLANES, :]
        y_conv = proj(wmix_ref, D_MODEL, D_MODEL) * conv
        conv_ref[rows, :] = (_sigmoid(proj(wgate_ref, 0, D_MODEL)) * y_conv).astype(BF16)
        gm_ref[rows, :] = _sigmoid(proj(wgate_ref, D_MODEL, D_MODEL)).astype(BF16)

        ql = proj(wlat_ref, 0, Q_LORA)
        qn_ref[rows, :] = (ql * lax.rsqrt(jnp.mean(ql * ql, axis=-1, keepdims=True) + EPS)
                           * gq_ref[...]).astype(BF16)
        kl = proj(wlat_ref, Q_LORA, KV_LORA)
        kvn_ref[rows, :] = (kl * lax.rsqrt(jnp.mean(kl * kl, axis=-1, keepdims=True) + EPS)
                            * gkv_ref[...]).astype(BF16)
        krt_ref[0, :, rows] = _dot_nt(wkr_ref[...], h)


def _inproj(x2, mod3, g1, w_mix, w_gate, w_lat, w_krt, conv_w, gq, gkv, seq):
    t = x2.shape[0]
    nb = t // seq
    tm = ROW_TILE * INPROJ_SUB
    tiles_per_seq = seq // tm
    row = lambda i: (i, 0)
    const = lambda i: (0, 0)
    return pl.pallas_call(
        functools.partial(_inproj_kernel, tiles_per_seq),
        grid=(t // tm,),
        in_specs=[
            pl.BlockSpec((tm, D_MODEL), row),
            pl.BlockSpec((1, 1, 6 * D_MODEL), lambda i: (i // tiles_per_seq, 0, 0)),
            pl.BlockSpec((1, D_MODEL), const),
            pl.BlockSpec((D_MODEL, 3 * D_MODEL), const),
            pl.BlockSpec((D_MODEL, 2 * D_MODEL), const),
            pl.BlockSpec((D_MODEL, Q_LORA + KV_LORA), const),
            pl.BlockSpec((2 * QK_ROPE, D_MODEL), const),
            pl.BlockSpec((CONV_K, D_MODEL), const),
            pl.BlockSpec((1, Q_LORA), const),
            pl.BlockSpec((1, KV_LORA), const),
        ],
        out_specs=[
            pl.BlockSpec((tm, D_MODEL), row),
            pl.BlockSpec((tm, D_MODEL), row),
            pl.BlockSpec((tm, Q_LORA), row),
            pl.BlockSpec((tm, KV_LORA), row),
            pl.BlockSpec((1, 2 * QK_ROPE, tm),
                         lambda i: (i // tiles_per_seq, 0, i % tiles_per_seq)),
        ],
        out_shape=[
            jax.ShapeDtypeStruct((t, D_MODEL), BF16),
            jax.ShapeDtypeStruct((t, D_MODEL), BF16),
            jax.ShapeDtypeStruct((t, Q_LORA), BF16),
            jax.ShapeDtypeStruct((t, KV_LORA), BF16),
            jax.ShapeDtypeStruct((nb, 2 * QK_ROPE, seq), F32),
        ],
        scratch_shapes=[pltpu.VMEM((ROW_TILE + SUBLANES, D_MODEL), F32)],
        compiler_params=pltpu.CompilerParams(
            dimension_semantics=("arbitrary",), vmem_limit_bytes=VMEM_LIMIT),
        name="inproj",
    )(x2, mod3, g1, w_mix, w_gate, w_lat, w_krt, conv_w, gq, gkv)


def _attn_kernel(qn_ref, kvn_ref, krt_ref, cst_ref, wqt_ref, wkvt_ref, gq_ref, gk_ref,
                 o_ref, *scratch):
    per_head = len(scratch) // HEADS_PER_STEP
    heads = [_attn_head(hh, qn_ref, kvn_ref, krt_ref, cst_ref, wqt_ref, wkvt_ref, gq_ref, gk_ref, o_ref,
                        *scratch[hh * per_head:(hh + 1) * per_head]) for hh in range(HEADS_PER_STEP)]
    nq = qn_ref.shape[0] // Q_TILE
    for scores, _ in heads:
        scores(nq - 1)
    for i in range(nq - 1, -1, -1):
        for scores, finish in heads:
            if i > 0:
                scores(i - 1)
            finish(i)


def _attn_head(hh, qn_ref, kvn_ref, krt_ref, cst_ref, wqt_ref, wkvt_ref, gq_ref, gk_ref,
               o_ref, qt_s, k_s, vt_s, s_buf0, s_buf1):
    seq = qn_ref.shape[0]
    cos_t = cst_ref[0, 0:QK_ROPE, :]
    sin_t = cst_ref[0, QK_ROPE:, :]

    def normed_rope(nope, r, rr, g, extra_scale):
        ss = jnp.sum(nope * nope, axis=0, keepdims=True) + jnp.sum(r * r, axis=0, keepdims=True)
        scale = lax.rsqrt(ss * (1.0 / QK_HEAD) + EPS) * extra_scale
        rope = r * g[QK_NOPE:QK_HEAD] * cos_t + rr * g[QK_HEAD:] * sin_t
        return (nope * g[0:QK_NOPE] * scale).astype(BF16), (rope * scale).astype(BF16)

    qt = _dot_nt(wqt_ref[hh], qn_ref[...])
    q_n, q_r = normed_rope(qt[0:QK_NOPE], qt[QK_NOPE:QK_HEAD], qt[QK_HEAD:], gq_ref[...],
                           QK_HEAD ** -0.5 * math.log2(math.e))
    qt_s[0:QK_NOPE, :] = q_n
    qt_s[QK_NOPE:QK_HEAD, :] = q_r
    qt_s[QK_HEAD:, :] = jnp.zeros((QK_ROPE, seq), BF16)

    kvt = _dot_nt(wkvt_ref[hh], kvn_ref[...])
    krt = krt_ref[0]
    k_n, k_r = normed_rope(kvt[0:QK_NOPE], krt[0:QK_ROPE], krt[QK_ROPE:], gk_ref[...], 1.0)
    kt = jnp.concatenate([k_n, k_r, jnp.zeros((QK_ROPE, seq), BF16)], axis=0)
    k_s[...] = kt.T
    vt_s[0:V_HEAD, :] = kvt[QK_NOPE:].astype(BF16)
    vt_s[V_HEAD:, :] = jnp.ones((vt_s.shape[0] - V_HEAD, seq), BF16)

    tq = Q_TILE
    kchunk = lax.broadcasted_iota(jnp.int32, (tq, tq), 0) // CHUNK
    qchunk = lax.broadcasted_iota(jnp.int32, (tq, tq), 1) // CHUNK
    diag_ok = kchunk <= qchunk
    neg = jnp.finfo(F32).min

    def scores(i):
        q0 = i * tq
        sb = s_buf0 if i % 2 == 0 else s_buf1
        q = qt_s[:, q0:q0 + tq]
        if i > 0:
            sb[0:q0, :] = _dot(k_s[0:q0, :], q)
        sb[q0:q0 + tq, :] = jnp.where(diag_ok, _dot(k_s[q0:q0 + tq, :], q), neg)

    def finish(i):
        q0 = i * tq
        kend = q0 + tq
        sb = s_buf0 if i % 2 == 0 else s_buf1
        m = jnp.max(sb[0:kend, :], axis=0, keepdims=True)
        acc = _dot(vt_s[:, 0:kend], jnp.exp2(sb[0:kend, :] - m).astype(BF16))
        o_t = acc[0:V_HEAD] / acc[V_HEAD:V_HEAD + 1]
        o_ref[q0:q0 + tq, hh * V_HEAD:(hh + 1) * V_HEAD] = o_t.T.astype(o_ref.dtype)

    return scores, finish


def _attn(qn, kvn, krt, cst, wqt, wkvt, gq, gk, nb, seq):
    t = qn.shape[0]
    per_b = lambda b, h: (b, 0)
    per_b3 = lambda b, h: (b, 0, 0)
    per_h = lambda b, h: (h, 0, 0)
    const = lambda b, h: (0, 0)
    qk_rows = QK_NOPE + 2 * QK_ROPE
    return pl.pallas_call(
        _attn_kernel,
        grid=(nb, N_HEADS // HEADS_PER_STEP),
        in_specs=[
            pl.BlockSpec((seq, Q_LORA), per_b),
            pl.BlockSpec((seq, KV_LORA), per_b),
            pl.BlockSpec((1, 2 * QK_ROPE, seq), per_b3),
            pl.BlockSpec((1, 2 * QK_ROPE, seq), per_b3),
            pl.BlockSpec((HEADS_PER_STEP, qk_rows, Q_LORA), per_h),
            pl.BlockSpec((HEADS_PER_STEP, QK_NOPE + V_HEAD, KV_LORA), per_h),
            pl.BlockSpec((qk_rows, 1), const),
            pl.BlockSpec((qk_rows, 1), const),
        ],
        out_specs=pl.BlockSpec((seq, HEADS_PER_STEP * V_HEAD), lambda b, h: (b, h)),
        out_shape=jax.ShapeDtypeStruct((t, N_HEADS * V_HEAD), BF16),
        scratch_shapes=[
            pltpu.VMEM((qk_rows, seq), BF16),
            pltpu.VMEM((seq, qk_rows), BF16),
            pltpu.VMEM((V_HEAD + 2 * SUBLANES, seq), BF16),
            pltpu.VMEM((seq, Q_TILE), F32),
            pltpu.VMEM((seq, Q_TILE), F32),
        ] * HEADS_PER_STEP,
        compiler_params=pltpu.CompilerParams(
            dimension_semantics=("arbitrary", "arbitrary"), vmem_limit_bytes=VMEM_LIMIT),
        name="attn",
    )(qn, kvn, krt, cst, wqt, wkvt, gq, gk)


def _oproj_kernel(conv_ref, gm_ref, y_ref, x_ref, mod_ref, wo_ref, g2_ref, wr_ref, br_ref,
                  x1_ref, h2_ref, eid_ref, rank_ref, cnt_ref, cwt_ref, base):
    tm = x_ref.shape[0]

    @pl.when(pl.program_id(0) == 0)
    def _():
        base[...] = jnp.zeros(base.shape, F32)

    merged = conv_ref[...].astype(F32) + gm_ref[...].astype(F32) * y_ref[...].astype(F32)
    att = _dot(merged.astype(BF16), wo_ref[...])
    gate1 = mod_ref[0, :, 2 * D_MODEL:3 * D_MODEL]
    shift2 = mod_ref[0, :, 3 * D_MODEL:4 * D_MODEL]
    scale2 = mod_ref[0, :, 4 * D_MODEL:5 * D_MODEL]
    x1 = x_ref[...] + gate1 * att
    x1_ref[...] = x1
    xn = x1 * lax.rsqrt(jnp.mean(x1 * x1, axis=-1, keepdims=True) + EPS) * g2_ref[...]
    h2 = xn * (1.0 + scale2) + shift2
    _store_rows(h2_ref, h2, tm)

    lt = _dot_nt(wr_ref[...], h2.astype(BF16)) + br_ref[...]
    gl = [lt[N_EXPERTS + r:N_EXPERTS + r + 1, :] for r in range(N_GROUPS)]
    gmax = jnp.maximum(jnp.maximum(gl[0], gl[1]), jnp.maximum(gl[2], gl[3]))
    gidx = jnp.full(gmax.shape, N_GROUPS - 1, jnp.int32)
    for r in range(N_GROUPS - 2, -1, -1):
        gidx = jnp.where(gl[r] == gmax, r, gidx)
    gsum = jnp.exp(gl[0] - gmax)
    for r in range(1, N_GROUPS):
        gsum = gsum + jnp.exp(gl[r] - gmax)
    p_group = 1.0 / gsum
    es = lt[(N_GROUPS - 1) * EXPERTS_PER_GROUP:N_GROUPS * EXPERTS_PER_GROUP, :]
    for r in range(N_GROUPS - 2, -1, -1):
        es = jnp.where(gidx == r, lt[r * EXPERTS_PER_GROUP:(r + 1) * EXPERTS_PER_GROUP, :], es)
    row = lax.broadcasted_iota(jnp.int32, es.shape, 0)
    m1 = jnp.max(es, axis=0, keepdims=True)
    i1 = jnp.min(jnp.where(es == m1, row, EXPERTS_PER_GROUP), axis=0, keepdims=True)
    es2 = jnp.where(row == i1, -jnp.inf, es)
    m2 = jnp.max(es2, axis=0, keepdims=True)
    i2 = jnp.min(jnp.where(es2 == m2, row, EXPERTS_PER_GROUP), axis=0, keepdims=True)
    e2 = jnp.exp(m2 - m1)
    w1 = p_group / (1.0 + e2)
    w2 = w1 * e2
    eid0 = gidx * EXPERTS_PER_GROUP + i1
    eid1 = gidx * EXPERTS_PER_GROUP + i2
    eid_ref[0:1, :] = eid0
    eid_ref[1:2, :] = eid1
    erow = lax.broadcasted_iota(jnp.int32, (N_EXPERTS, tm), 0)
    oh0 = erow == eid0
    oh1 = erow == eid1
    both = jnp.where(oh0, 1.0, jnp.where(oh1, 1.0, 0.0))
    earlier = (lax.broadcasted_iota(jnp.int32, (tm, tm), 0)
               < lax.broadcasted_iota(jnp.int32, (tm, tm), 1))
    seen = base[...] + _dot(both.astype(BF16), jnp.where(earlier, 1.0, 0.0).astype(BF16))
    rank_ref[0:1, :] = jnp.sum(jnp.where(oh0, seen, 0.0), axis=0, keepdims=True).astype(jnp.int32)
    rank_ref[1:2, :] = jnp.sum(jnp.where(oh1, seen, 0.0), axis=0, keepdims=True).astype(jnp.int32)
    base[...] = base[...] + jnp.sum(both, axis=1, keepdims=True)
    cnt_ref[...] = base[...]
    wrow = lax.broadcasted_iota(jnp.int32, (LANES, tm), 0)
    wmat = jnp.where(wrow == 0, w1, jnp.where(wrow == 1, w2, 0.0))
    cwt_ref[...] = wmat.T


def _oproj(conv_p, gm, y_mla, x2, mod3, wo, g2, wr_t, br, seq):
    t = x2.shape[0]
    tm = ROW_TILE
    tiles_per_seq = seq // tm
    row = lambda i: (i, 0)
    const = lambda i: (0, 0)
    return pl.pallas_call(
        _oproj_kernel,
        grid=(t // tm,),
        in_specs=[
            pl.BlockSpec((tm, D_MODEL), row),
            pl.BlockSpec((tm, D_MODEL), row),
            pl.BlockSpec((tm, D_MODEL), row),
            pl.BlockSpec((tm, D_MODEL), row),
            pl.BlockSpec((1, 1, 6 * D_MODEL), lambda i: (i // tiles_per_seq, 0, 0)),
            pl.BlockSpec((D_MODEL, D_MODEL), const),
            pl.BlockSpec((1, D_MODEL), const),
            pl.BlockSpec((ROUTER_ROWS, D_MODEL), const),
            pl.BlockSpec((ROUTER_ROWS, 1), const),
        ],
        out_specs=[
            pl.BlockSpec((tm, D_MODEL), row),
            pl.BlockSpec((tm * ROWS_PER_TOKEN, LANES), row),
            pl.BlockSpec((2, tm), lambda i: (0, i)),
            pl.BlockSpec((2, tm), lambda i: (0, i)),
            pl.BlockSpec((N_EXPERTS, 1), const),
            pl.BlockSpec((tm, LANES), row),
        ],
        out_shape=[
            jax.ShapeDtypeStruct((t, D_MODEL), F32),
            jax.ShapeDtypeStruct((t * ROWS_PER_TOKEN, LANES), F32),
            jax.ShapeDtypeStruct((2, t), jnp.int32),
            jax.ShapeDtypeStruct((2, t), jnp.int32),
            jax.ShapeDtypeStruct((N_EXPERTS, 1), F32),
            jax.ShapeDtypeStruct((t, LANES), F32),
        ],
        scratch_shapes=[pltpu.VMEM((N_EXPERTS, 1), F32)],
        compiler_params=pltpu.CompilerParams(
            dimension_semantics=("arbitrary",), vmem_limit_bytes=VMEM_LIMIT),
        name="oproj",
    )(conv_p, gm, y_mla, x2, mod3, wo, g2, wr_t, br)


def _dispatch_kernel(n_tiles, ztile_ref, na_ref, pos_ref, h2_ref, xs_hbm, zbuf, sem):
    tm = h2_ref.shape[0] // ROWS_PER_TOKEN
    tile_rows = MOE_TILE * ROWS_PER_TOKEN

    @pl.when(pl.program_id(0) == 0)
    def _():
        zbuf[...] = jnp.zeros(zbuf.shape, zbuf.dtype)

        def zero_tile(tile):
            row = pl.multiple_of(tile * tile_rows, tile_rows)
            return pltpu.make_async_copy(zbuf, xs_hbm.at[pl.ds(row, tile_rows), :], sem)
        for e in range(N_EXPERTS):
            zero_tile(ztile_ref[e]).start()
        for e in range(N_EXPERTS):
            zero_tile(ztile_ref[e]).wait()

        def zero_tail(k, carry):
            cp = zero_tile(na_ref[0] + k)
            cp.start()
            cp.wait()
            return carry
        lax.fori_loop(0, n_tiles - na_ref[0], zero_tail, 0)

    def body(c, carry):
        for k in range(DMA_UNROLL):
            t = c * DMA_UNROLL + k
            src = h2_ref.at[pl.ds(pl.multiple_of(t * ROWS_PER_TOKEN, ROWS_PER_TOKEN), ROWS_PER_TOKEN), :]
            for s in range(2):
                dst_row = pl.multiple_of(pos_ref[0, 0, s * tm + t] * ROWS_PER_TOKEN, ROWS_PER_TOKEN)
                pltpu.make_async_copy(src, xs_hbm.at[pl.ds(dst_row, ROWS_PER_TOKEN), :],
                                      sem).start(priority=s)
        return carry
    lax.fori_loop(0, tm // DMA_UNROLL, body, 0)
    for _ in range(2):
        pltpu.make_async_copy(h2_ref, xs_hbm.at[pl.ds(0, tm * ROWS_PER_TOKEN), :], sem).wait()


def _dispatch(ztile, na, pos3, h2r, n_tiles):
    tm = pos3.shape[2] // 2
    n_steps = pos3.shape[0]
    grid_spec = pltpu.PrefetchScalarGridSpec(
        num_scalar_prefetch=2,
        grid=(n_steps,),
        in_specs=[
            pl.BlockSpec((1, 1, 2 * tm), lambda i, z, n: (i, 0, 0), memory_space=pltpu.SMEM),
            pl.BlockSpec((tm * ROWS_PER_TOKEN, LANES), lambda i, z, n: (i, 0)),
        ],
        out_specs=pl.BlockSpec(memory_space=pl.ANY),
        scratch_shapes=[
            pltpu.VMEM((MOE_TILE * ROWS_PER_TOKEN, LANES), F32),
            pltpu.SemaphoreType.DMA(()),
        ],
    )
    return pl.pallas_call(
        functools.partial(_dispatch_kernel, n_tiles),
        grid_spec=grid_spec,
        out_shape=jax.ShapeDtypeStruct((n_tiles * MOE_TILE * ROWS_PER_TOKEN, LANES), F32),
        compiler_params=pltpu.CompilerParams(
            dimension_semantics=("arbitrary",), vmem_limit_bytes=VMEM_LIMIT),
        name="dispatch",
    )(ztile, na, pos3, h2r)


def _moe_kernel(te_ref, na_ref, x_hbm, wg_ref, wu_ref, wd_ref, y_ref, xbuf, wgu_s, wd_s, sem):
    i = pl.program_id(0)
    n_active = na_ref[0]
    active = i < n_active
    tile_rows = MOE_TILE * ROWS_PER_TOKEN

    def fetch(tile):
        slot = tile % X_SLOTS
        row = pl.multiple_of(tile * tile_rows, tile_rows)
        return pltpu.make_async_copy(x_hbm.at[pl.ds(row, tile_rows), :], xbuf.at[slot], sem.at[slot])

    @pl.when(i == 0)
    def _():
        for k in range(X_SLOTS - 1):
            @pl.when(k < n_active)
            def _():
                fetch(k).start()

    @pl.when(i + X_SLOTS - 1 < n_active)
    def _():
        fetch(i + X_SLOTS - 1).start()

    new_expert = jnp.logical_or(i == 0, te_ref[i] != te_ref[jnp.maximum(i - 1, 0)])

    @pl.when(jnp.logical_and(active, new_expert))
    def _():
        wgu_s[:, 0:D_EXPERT] = wg_ref[0].astype(BF16)
        wgu_s[:, D_EXPERT:] = wu_ref[0].astype(BF16)
        wd_s[...] = wd_ref[0].astype(BF16)

    @pl.when(active)
    def _():
        fetch(i).wait()
        x = jnp.concatenate([col.astype(BF16) for col in _load_rows(xbuf.at[i % X_SLOTS], MOE_TILE)], axis=1)
        gu = _dot(x, wgu_s[...])
        g = gu[:, 0:D_EXPERT]
        a = (g * _sigmoid(g)) * gu[:, D_EXPERT:]
        y = _dot(a.astype(BF16), wd_s[...])
        _store_rows(y_ref, y, MOE_TILE)

    @pl.when(jnp.logical_not(active))
    def _():
        y_ref[...] = jnp.zeros(y_ref.shape, y_ref.dtype)


def _moe(te, na, xs, wg, wu, wd):
    nt = te.shape[0]
    tile_rows = MOE_TILE * ROWS_PER_TOKEN
    wspec = lambda shape: pl.BlockSpec((1,) + shape, lambda i, te_r, na_r: (te_r[i], 0, 0))
    grid_spec = pltpu.PrefetchScalarGridSpec(
        num_scalar_prefetch=2,
        grid=(nt,),
        in_specs=[
            pl.BlockSpec(memory_space=pl.ANY),
            wspec((D_MODEL, D_EXPERT)),
            wspec((D_MODEL, D_EXPERT)),
            wspec((D_EXPERT, D_MODEL)),
        ],
        out_specs=pl.BlockSpec((tile_rows, LANES), lambda i, te_r, na_r: (i, 0)),
        scratch_shapes=[
            pltpu.VMEM((X_SLOTS, tile_rows, LANES), F32),
            pltpu.VMEM((D_MODEL, 2 * D_EXPERT), BF16),
            pltpu.VMEM((D_EXPERT, D_MODEL), BF16),
            pltpu.SemaphoreType.DMA((X_SLOTS,)),
        ],
    )
    return pl.pallas_call(
        _moe_kernel,
        grid_spec=grid_spec,
        out_shape=jax.ShapeDtypeStruct(xs.shape, xs.dtype),
        compiler_params=pltpu.CompilerParams(
            dimension_semantics=("arbitrary",), vmem_limit_bytes=VMEM_LIMIT),
        name="moe",
    )(te, na, xs, wg, wu, wd)


def _comb_kernel(pos0_ref, posn_ref, x1_ref, cwt_ref, mod_ref, y_hbm, o_ref, ybuf, sem):
    tm = x1_ref.shape[0]
    i = pl.program_id(0)
    slot = i % 2

    def gather(pos_ref, dst_slot):
        def body(c, carry):
            for k in range(DMA_UNROLL):
                r = c * DMA_UNROLL + k
                src_row = pl.multiple_of(pos_ref[0, 0, r] * ROWS_PER_TOKEN, ROWS_PER_TOKEN)
                dst_row = pl.multiple_of(r * ROWS_PER_TOKEN, ROWS_PER_TOKEN)
                pltpu.make_async_copy(y_hbm.at[pl.ds(src_row, ROWS_PER_TOKEN), :],
                                      ybuf.at[dst_slot, pl.ds(dst_row, ROWS_PER_TOKEN), :],
                                      sem.at[dst_slot]).start(priority=k % 2)
            return carry
        lax.fori_loop(0, 2 * tm // DMA_UNROLL, body, 0)

    @pl.when(i == 0)
    def _():
        gather(pos0_ref, 0)

    @pl.when(i + 1 < pl.num_programs(0))
    def _():
        gather(posn_ref, 1 - slot)

    pltpu.make_async_copy(y_hbm.at[pl.ds(0, 2 * tm * ROWS_PER_TOKEN), :], ybuf.at[slot],
                          sem.at[slot]).wait()
    c0 = cwt_ref[:, 0:1]
    c1 = cwt_ref[:, 1:2]
    y0 = _load_rows(ybuf.at[slot], tm)
    y1 = _load_rows(ybuf.at[slot], tm, offset=tm * ROWS_PER_TOKEN)
    for j in range(ROWS_PER_TOKEN):
        cols = slice(j * LANES, (j + 1) * LANES)
        gate2 = mod_ref[0, :, 5 * D_MODEL + j * LANES:5 * D_MODEL + (j + 1) * LANES]
        o_ref[:, cols] = x1_ref[:, cols] + gate2 * (c0 * y0[j] + c1 * y1[j])


def _comb(x1, yr, pos3, cwt, mod3, seq):
    t = x1.shape[0]
    tm = ROW_TILE
    tiles_per_seq = seq // tm
    n_steps = t // tm
    row = lambda i: (i, 0)
    smem_blk = lambda f: pl.BlockSpec((1, 1, 2 * tm), f, memory_space=pltpu.SMEM)
    return pl.pallas_call(
        _comb_kernel,
        grid=(n_steps,),
        in_specs=[
            smem_blk(lambda i: (0, 0, 0)),
            smem_blk(lambda i: (jnp.minimum(i + 1, n_steps - 1), 0, 0)),
            pl.BlockSpec((tm, D_MODEL), row),
            pl.BlockSpec((tm, LANES), row),
            pl.BlockSpec((1, 1, 6 * D_MODEL), lambda i: (i // tiles_per_seq, 0, 0)),
            pl.BlockSpec(memory_space=pl.ANY),
        ],
        out_specs=pl.BlockSpec((tm, D_MODEL), row),
        out_shape=jax.ShapeDtypeStruct((t, D_MODEL), F32),
        scratch_shapes=[
            pltpu.VMEM((2, 2 * tm * ROWS_PER_TOKEN, LANES), F32),
            pltpu.SemaphoreType.DMA((2,)),
        ],
        compiler_params=pltpu.CompilerParams(
            dimension_semantics=("arbitrary",), vmem_limit_bytes=VMEM_LIMIT),
        name="comb",
    )(pos3, pos3, x1, cwt, mod3, yr)


def _route_plan(eid, rank, cnt, n_tok):
    n_tiles = (2 * n_tok) // MOE_TILE + N_EXPERTS
    experts = jnp.arange(N_EXPERTS, dtype=jnp.int32)
    counts = cnt.reshape(N_EXPERTS).astype(jnp.int32)
    ntile = (counts + MOE_TILE - 1) // MOE_TILE
    tend = jnp.cumsum(ntile)
    tstart = tend - ntile
    n_active = tend[-1]
    tj = jnp.arange(n_tiles, dtype=jnp.int32)
    te_raw = jnp.minimum(jnp.sum((tj[:, None] >= tend[None, :]).astype(jnp.int32), axis=1),
                         N_EXPERTS - 1)
    te_last = jnp.sum(jnp.where(tj == n_active - 1, te_raw, 0))
    te = jnp.where(tj < n_active, te_raw, te_last).astype(jnp.int32)
    first_row = jnp.sum(jnp.where(eid[:, :, None] == experts[None, None, :],
                                  (tstart * MOE_TILE)[None, None, :], 0), axis=-1)
    pos = (first_row + rank).astype(jnp.int32)
    ztile = jnp.maximum(tend - 1, 0).astype(jnp.int32)
    pos3 = pos.reshape(2, n_tok // ROW_TILE, ROW_TILE).transpose(1, 0, 2).reshape(
        n_tok // ROW_TILE, 1, 2 * ROW_TILE)
    return te, n_active.reshape(1).astype(jnp.int32), ztile, pos3, n_tiles


def _rotate_half_cols(w):
    half = QK_ROPE // 2
    return jnp.concatenate([w[..., half:], w[..., :half]], axis=-1)


def kernel(x, c, positions, w_ada, b_ada, norm1_g, w_in, conv_w, q_a_norm_g, w_q_b, kv_a_norm_g, w_kv_b, q_norm_g, k_norm_g, w_o, norm2_g, w_router_group, b_router_group, w_router_expert, b_router_expert, w_exp_gate, w_exp_up, w_exp_down):
    nb, seq, d = x.shape
    depth = w_ada.shape[0]
    n_tok = nb * seq
    assert d == D_MODEL and seq % ROW_TILE == 0 and seq % Q_TILE == 0 and Q_TILE % CHUNK == 0
    assert (2 * n_tok) % MOE_TILE == 0

    inv = ROPE_BASE ** (-jnp.arange(0, QK_ROPE, 2, dtype=F32) / QK_ROPE)
    ang = inv[None, :, None] * positions.astype(F32)[:, None, :]
    cos, sin = jnp.cos(ang), jnp.sin(ang)
    cst = jnp.concatenate([cos, cos, -sin, sin], axis=1)

    x2 = x.reshape(n_tok, d)
    for l in range(depth):
        wi = w_in[l]
        o_q = 3 * D_MODEL
        o_kv = o_q + Q_LORA
        o_kr = o_kv + KV_LORA
        o_gc = o_kr + QK_ROPE
        w_kr = wi[:, o_kr:o_gc]
        w_mix = wi[:, 0:o_q].astype(BF16)
        w_lat = wi[:, o_q:o_kr].astype(BF16)
        w_gate = wi[:, o_gc:].astype(BF16)
        w_krt = jnp.concatenate([w_kr, _rotate_half_cols(w_kr)], axis=1).T.astype(BF16)
        wq3 = w_q_b[l].reshape(Q_LORA, N_HEADS, QK_HEAD)
        wq = jnp.concatenate([wq3, _rotate_half_cols(wq3[..., QK_NOPE:])], axis=-1)
        wqt = wq.transpose(1, 2, 0).astype(BF16)
        wkvt = w_kv_b[l].reshape(KV_LORA, N_HEADS, QK_NOPE + V_HEAD).transpose(1, 2, 0).astype(BF16)
        gq = jnp.concatenate([q_norm_g[l], _rotate_half_cols(q_norm_g[l][QK_NOPE:])]).reshape(-1, 1)
        gk = jnp.concatenate([k_norm_g[l], _rotate_half_cols(k_norm_g[l][QK_NOPE:])]).reshape(-1, 1)
        wr_t = jnp.concatenate(
            [w_router_expert[l].T, w_router_group[l].T,
             jnp.zeros((ROUTER_ROWS - N_EXPERTS - N_GROUPS, d), F32)], axis=0).astype(BF16)
        br = jnp.concatenate(
            [b_router_expert[l], b_router_group[l],
             jnp.zeros((ROUTER_ROWS - N_EXPERTS - N_GROUPS,), F32)]).reshape(ROUTER_ROWS, 1)

        mod3 = _ada(c, w_ada[l], b_ada[l]).reshape(nb, 1, 6 * d)
        conv_p, gm, qn, kvn, krt = _inproj(
            x2, mod3, norm1_g[l].reshape(1, d), w_mix, w_gate, w_lat, w_krt, conv_w[l],
            q_a_norm_g[l].reshape(1, -1), kv_a_norm_g[l].reshape(1, -1), seq)
        y_mla = _attn(qn, kvn, krt, cst, wqt, wkvt, gq, gk, nb, seq)
        x1, h2r, eid, rank, cnt, cwt = _oproj(conv_p, gm, y_mla, x2, mod3, w_o[l].astype(BF16),
                                              norm2_g[l].reshape(1, d), wr_t, br, seq)
        te, na, ztile, pos3, n_tiles = _route_plan(eid, rank, cnt, n_tok)
        xs = _dispatch(ztile, na, pos3, h2r, n_tiles)
        yr = _moe(te, na, xs,
                  w_exp_gate[l].reshape(N_EXPERTS, d, D_EXPERT),
                  w_exp_up[l].reshape(N_EXPERTS, d, D_EXPERT),
                  w_exp_down[l].reshape(N_EXPERTS, D_EXPERT, d))
        x2 = _comb(x1, yr, pos3, cwt, mod3, seq)
    return x2.reshape(nb, seq, d)
```

```python
import functools
import math

import jax
import jax.numpy as jnp
from jax import lax
from jax.experimental import pallas as pl
from jax.experimental.pallas import tpu as pltpu

F32 = jnp.float32
BF16 = jnp.bfloat16

D_MODEL = 1024
N_HEADS = 8
QK_NOPE = 128
QK_ROPE = 64
QK_HEAD = QK_NOPE + QK_ROPE
V_HEAD = 128
Q_LORA = 384
KV_LORA = 256
CHUNK = 64
EPS = 1e-6
ROPE_BASE = 10000.0
N_GROUPS = 4
EXPERTS_PER_GROUP = 8
N_EXPERTS = N_GROUPS * EXPERTS_PER_GROUP
D_EXPERT = 256
CONV_K = 3

LANES = 128
SUBLANES = 8
VMEM_LIMIT = 56 * 1024 * 1024

ROW_TILE = 512
Q_TILE = 512
INPROJ_SUB = 2
HEADS_PER_STEP = 2
MOE_TILE = 512
X_SLOTS = 4
ADA_COLS = 1536
ROWS_PER_TOKEN = D_MODEL // LANES
DMA_UNROLL = 8
ROUTER_ROWS = 40


def _sigmoid(v):
    return 1.0 / (1.0 + jnp.exp(-v))


def _dot(a, b):
    return jnp.dot(a, b, preferred_element_type=F32)


def _store_rows(ref, val, n):
    for j in range(ROWS_PER_TOKEN):
        ref[pl.ds(j, n, stride=ROWS_PER_TOKEN), :] = val[:, j * LANES:(j + 1) * LANES]


def _load_rows(ref, n, offset=0):
    return [ref[pl.ds(offset + j, n, stride=ROWS_PER_TOKEN), :] for j in range(ROWS_PER_TOKEN)]


def _dot_nt(a, b):
    return lax.dot_general(a, b, (((1,), (1,)), ((), ())), preferred_element_type=F32)


def _ada_kernel(c_ref, w_ref, b_ref, o_ref):
    c = c_ref[...]
    act = (c * _sigmoid(c)).astype(BF16)
    o_ref[...] = _dot(act, w_ref[...].astype(BF16)) + b_ref[...]


def _ada(c, w_ada, b_ada):
    nb, d = c.shape
    n = w_ada.shape[1]
    return pl.pallas_call(
        _ada_kernel,
        grid=(n // ADA_COLS,),
        in_specs=[
            pl.BlockSpec((nb, d), lambda j: (0, 0)),
            pl.BlockSpec((d, ADA_COLS), lambda j: (0, j)),
            pl.BlockSpec((1, ADA_COLS), lambda j: (0, j)),
        ],
        out_specs=pl.BlockSpec((nb, ADA_COLS), lambda j: (0, j)),
        out_shape=jax.ShapeDtypeStruct((nb, n), F32),
        compiler_params=pltpu.CompilerParams(
            dimension_semantics=("arbitrary",), vmem_limit_bytes=VMEM_LIMIT),
        name="ada",
    )(c, w_ada, b_ada.reshape(1, n))


def _inproj_kernel(tiles_per_seq, x_ref, mod_ref, g1_ref, wmix_ref, wgate_ref, wlat_ref, wkr_ref,
                   cw_ref, gq_ref, gkv_ref,
                   conv_ref, gm_ref, qn_ref, kvn_ref, krt_ref, ubuf):
    tm = x_ref.shape[0] // INPROJ_SUB

    @pl.when(pl.program_id(0) % tiles_per_seq == 0)
    def _():
        ubuf[0:SUBLANES, :] = jnp.zeros((SUBLANES, D_MODEL), F32)

    shift = mod_ref[0, :, 0:D_MODEL]
    scale = mod_ref[0, :, D_MODEL:2 * D_MODEL]
    for sub in range(INPROJ_SUB):
        rows = slice(sub * tm, (sub + 1) * tm)
        x = x_ref[rows, :]
        xn = x * lax.rsqrt(jnp.mean(x * x, axis=-1, keepdims=True) + EPS) * g1_ref[...]
        h = (xn * (1.0 + scale) + shift).astype(BF16)

        def proj(w_ref, lo, width):
            return _dot(h, w_ref[:, lo:lo + width])

        u = proj(wmix_ref, 2 * D_MODEL, D_MODEL) * proj(wmix_ref, 0, D_MODEL)
        ubuf[SUBLANES:SUBLANES + tm, :] = u
        conv = (ubuf[SUBLANES - 2:SUBLANES - 2 + tm, :] * cw_ref[0:1, :]
                + ubuf[SUBLANES - 1:SUBLANES - 1 + tm, :] * cw_ref[1:2, :]
                + u * cw_ref[2:3, :])
        ubuf[0:SUBLANES, :] = ubuf[tm:tm + SUBLANES, :]
        y_conv = proj(wmix_ref, D_MODEL, D_MODEL) * conv
        conv_ref[rows, :] = (_sigmoid(proj(wgate_ref, 0, D_MODEL)) * y_conv).astype(BF16)
        gm_ref[rows, :] = _sigmoid(proj(wgate_ref, D_MODEL, D_MODEL)).astype(BF16)

        ql = proj(wlat_ref, 0, Q_LORA)
        qn_ref[rows, :] = (ql * lax.rsqrt(jnp.mean(ql * ql, axis=-1, keepdims=True) + EPS)
                           * gq_ref[...]).astype(BF16)
        kl = proj(wlat_ref, Q_LORA, KV_LORA)
        kvn_ref[rows, :] = (kl * lax.rsqrt(jnp.mean(kl * kl, axis=-1, keepdims=True) + EPS)
                            * gkv_ref[...]).astype(BF16)
        krt_ref[0, :, rows] = _dot_nt(wkr_ref[...], h)


def _inproj(x2, mod3, g1, w_mix, w_gate, w_lat, w_krt, conv_w, gq, gkv, seq):
    t = x2.shape[0]
    nb = t // seq
    tm = ROW_TILE * INPROJ_SUB
    tiles_per_seq = seq // tm
    row = lambda i: (i, 0)
    const = lambda i: (0, 0)
    return pl.pallas_call(
        functools.partial(_inproj_kernel, tiles_per_seq),
        grid=(t // tm,),
        in_specs=[
            pl.BlockSpec((tm, D_MODEL), row),
            pl.BlockSpec((1, 1, 6 * D_MODEL), lambda i: (i // tiles_per_seq, 0, 0)),
            pl.BlockSpec((1, D_MODEL), const),
            pl.BlockSpec((D_MODEL, 3 * D_MODEL), const),
            pl.BlockSpec((D_MODEL, 2 * D_MODEL), const),
            pl.BlockSpec((D_MODEL, Q_LORA + KV_LORA), const),
            pl.BlockSpec((2 * QK_ROPE, D_MODEL), const),
            pl.BlockSpec((CONV_K, D_MODEL), const),
            pl.BlockSpec((1, Q_LORA), const),
            pl.BlockSpec((1, KV_LORA), const),
        ],
        out_specs=[
            pl.BlockSpec((tm, D_MODEL), row),
            pl.BlockSpec((tm, D_MODEL), row),
            pl.BlockSpec((tm, Q_LORA), row),
            pl.BlockSpec((tm, KV_LORA), row),
            pl.BlockSpec((1, 2 * QK_ROPE, tm),
                         lambda i: (i // tiles_per_seq, 0, i % tiles_per_seq)),
        ],
        out_shape=[
            jax.ShapeDtypeStruct((t, D_MODEL), BF16),
            jax.ShapeDtypeStruct((t, D_MODEL), BF16),
            jax.ShapeDtypeStruct((t, Q_LORA), BF16),
            jax.ShapeDtypeStruct((t, KV_LORA), BF16),
            jax.ShapeDtypeStruct((nb, 2 * QK_ROPE, seq), F32),
        ],
        scratch_shapes=[pltpu.VMEM((ROW_TILE + SUBLANES, D_MODEL), F32)],
        compiler_params=pltpu.CompilerParams(
            dimension_semantics=("arbitrary",), vmem_limit_bytes=VMEM_LIMIT),
        name="inproj",
    )(x2, mod3, g1, w_mix, w_gate, w_lat, w_krt, conv_w, gq, gkv)


def _attn_kernel(qn_ref, kvn_ref, krt_ref, cst_ref, wqt_ref, wkvt_ref, gq_ref, gk_ref,
                 o_ref, *scratch):
    per_head = len(scratch) // HEADS_PER_STEP
    heads = [_attn_head(hh, qn_ref, kvn_ref, krt_ref, cst_ref, wqt_ref, wkvt_ref, gq_ref, gk_ref, o_ref,
                        *scratch[hh * per_head:(hh + 1) * per_head]) for hh in range(HEADS_PER_STEP)]
    nq = qn_ref.shape[0] // Q_TILE
    for scores, _ in heads:
        scores(0)
    for i in range(nq):
        for scores, finish in heads:
            if i + 1 < nq:
                scores(i + 1)
            finish(i)


def _attn_head(hh, qn_ref, kvn_ref, krt_ref, cst_ref, wqt_ref, wkvt_ref, gq_ref, gk_ref,
               o_ref, qt_s, k_s, vt_s, s_buf0, s_buf1):
    seq = qn_ref.shape[0]
    cos_t = cst_ref[0, 0:QK_ROPE, :]
    sin_t = cst_ref[0, QK_ROPE:, :]

    def normed_rope(nope, r, rr, g, extra_scale):
        ss = jnp.sum(nope * nope, axis=0, keepdims=True) + jnp.sum(r * r, axis=0, keepdims=True)
        scale = lax.rsqrt(ss * (1.0 / QK_HEAD) + EPS) * extra_scale
        rope = r * g[QK_NOPE:QK_HEAD] * cos_t + rr * g[QK_HEAD:] * sin_t
        return (nope * g[0:QK_NOPE] * scale).astype(BF16), (rope * scale).astype(BF16)

    qt = _dot_nt(wqt_ref[hh], qn_ref[...])
    q_n, q_r = normed_rope(qt[0:QK_NOPE], qt[QK_NOPE:QK_HEAD], qt[QK_HEAD:], gq_ref[...],
                           QK_HEAD ** -0.5 * math.log2(math.e))
    qt_s[0:QK_NOPE, :] = q_n
    qt_s[QK_NOPE:QK_HEAD, :] = q_r
    qt_s[QK_HEAD:, :] = jnp.zeros((QK_ROPE, seq), BF16)

    kvt = _dot_nt(wkvt_ref[hh], kvn_ref[...])
    krt = krt_ref[0]
    k_n, k_r = normed_rope(kvt[0:QK_NOPE], krt[0:QK_ROPE], krt[QK_ROPE:], gk_ref[...], 1.0)
    kt = jnp.concatenate([k_n, k_r, jnp.zeros((QK_ROPE, seq), BF16)], axis=0)
    k_s[...] = kt.T
    vt_s[0:V_HEAD, :] = kvt[QK_NOPE:].astype(BF16)
    vt_s[V_HEAD:, :] = jnp.ones((vt_s.shape[0] - V_HEAD, seq), BF16)

    tq = Q_TILE
    kchunk = lax.broadcasted_iota(jnp.int32, (tq, tq), 0) // CHUNK
    qchunk = lax.broadcasted_iota(jnp.int32, (tq, tq), 1) // CHUNK
    diag_ok = kchunk <= qchunk
    neg = jnp.finfo(F32).min

    def scores(i):
        q0 = i * tq
        sb = s_buf0 if i % 2 == 0 else s_buf1
        q = qt_s[:, q0:q0 + tq]
        if i > 0:
            sb[0:q0, :] = _dot(k_s[0:q0, :], q)
        sb[q0:q0 + tq, :] = jnp.where(diag_ok, _dot(k_s[q0:q0 + tq, :], q), neg)

    def finish(i):
        q0 = i * tq
        kend = q0 + tq
        sb = s_buf0 if i % 2 == 0 else s_buf1
        m = jnp.max(sb[0:kend, :], axis=0, keepdims=True)
        acc = _dot(vt_s[:, 0:kend], jnp.exp2(sb[0:kend, :] - m).astype(BF16))
        o_t = acc[0:V_HEAD] / acc[V_HEAD:V_HEAD + 1]
        o_ref[q0:q0 + tq, hh * V_HEAD:(hh + 1) * V_HEAD] = o_t.T.astype(o_ref.dtype)

    return scores, finish


def _attn(qn, kvn, krt, cst, wqt, wkvt, gq, gk, nb, seq):
    t = qn.shape[0]
    per_b = lambda b, h: (b, 0)
    per_b3 = lambda b, h: (b, 0, 0)
    per_h = lambda b, h: (h, 0, 0)
    const = lambda b, h: (0, 0)
    qk_rows = QK_NOPE + 2 * QK_ROPE
    return pl.pallas_call(
        _attn_kernel,
        grid=(nb, N_HEADS // HEADS_PER_STEP),
        in_specs=[
            pl.BlockSpec((seq, Q_LORA), per_b),
            pl.BlockSpec((seq, KV_LORA), per_b),
            pl.BlockSpec((1, 2 * QK_ROPE, seq), per_b3),
            pl.BlockSpec((1, 2 * QK_ROPE, seq), per_b3),
            pl.BlockSpec((HEADS_PER_STEP, qk_rows, Q_LORA), per_h),
            pl.BlockSpec((HEADS_PER_STEP, QK_NOPE + V_HEAD, KV_LORA), per_h),
            pl.BlockSpec((qk_rows, 1), const),
            pl.BlockSpec((qk_rows, 1), const),
        ],
        out_specs=pl.BlockSpec((seq, HEADS_PER_STEP * V_HEAD), lambda b, h: (b, h)),
        out_shape=jax.ShapeDtypeStruct((t, N_HEADS * V_HEAD), BF16),
        scratch_shapes=[
            pltpu.VMEM((qk_rows, seq), BF16),
            pltpu.VMEM((seq, qk_rows), BF16),
            pltpu.VMEM((V_HEAD + 2 * SUBLANES, seq), BF16),
            pltpu.VMEM((seq, Q_TILE), F32),
            pltpu.VMEM((seq, Q_TILE), F32),
        ] * HEADS_PER_STEP,
        compiler_params=pltpu.CompilerParams(
            dimension_semantics=("arbitrary", "arbitrary"), vmem_limit_bytes=VMEM_LIMIT),
        name="attn",
    )(qn, kvn, krt, cst, wqt, wkvt, gq, gk)


def _oproj_kernel(conv_ref, gm_ref, y_ref, x_ref, mod_ref, wo_ref, g2_ref, wr_ref, br_ref,
                  x1_ref, h2_ref, eid_ref, rank_ref, cnt_ref, cwt_ref, base):
    tm = x_ref.shape[0]

    @pl.when(pl.program_id(0) == 0)
    def _():
        base[...] = jnp.zeros(base.shape, F32)

    merged = conv_ref[...].astype(F32) + gm_ref[...].astype(F32) * y_ref[...].astype(F32)
    att = _dot(merged.astype(BF16), wo_ref[...])
    gate1 = mod_ref[0, :, 2 * D_MODEL:3 * D_MODEL]
    shift2 = mod_ref[0, :, 3 * D_MODEL:4 * D_MODEL]
    scale2 = mod_ref[0, :, 4 * D_MODEL:5 * D_MODEL]
    x1 = x_ref[...] + gate1 * att
    x1_ref[...] = x1
    xn = x1 * lax.rsqrt(jnp.mean(x1 * x1, axis=-1, keepdims=True) + EPS) * g2_ref[...]
    h2 = xn * (1.0 + scale2) + shift2
    _store_rows(h2_ref, h2, tm)

    lt = _dot_nt(wr_ref[...], h2.astype(BF16)) + br_ref[...]
    gl = [lt[N_EXPERTS + r:N_EXPERTS + r + 1, :] for r in range(N_GROUPS)]
    gmax = jnp.maximum(jnp.maximum(gl[0], gl[1]), jnp.maximum(gl[2], gl[3]))
    gidx = jnp.full(gmax.shape, N_GROUPS - 1, jnp.int32)
    for r in range(N_GROUPS - 2, -1, -1):
        gidx = jnp.where(gl[r] == gmax, r, gidx)
    gsum = jnp.exp(gl[0] - gmax)
    for r in range(1, N_GROUPS):
        gsum = gsum + jnp.exp(gl[r] - gmax)
    p_group = 1.0 / gsum
    es = lt[(N_GROUPS - 1) * EXPERTS_PER_GROUP:N_GROUPS * EXPERTS_PER_GROUP, :]
    for r in range(N_GROUPS - 2, -1, -1):
        es = jnp.where(gidx == r, lt[r * EXPERTS_PER_GROUP:(r + 1) * EXPERTS_PER_GROUP, :], es)
    row = lax.broadcasted_iota(jnp.int32, es.shape, 0)
    m1 = jnp.max(es, axis=0, keepdims=True)
    i1 = jnp.min(jnp.where(es == m1, row, EXPERTS_PER_GROUP), axis=0, keepdims=True)
    es2 = jnp.where(row == i1, -jnp.inf, es)
    m2 = jnp.max(es2, axis=0, keepdims=True)
    i2 = jnp.min(jnp.where(es2 == m2, row, EXPERTS_PER_GROUP), axis=0, keepdims=True)
    e2 = jnp.exp(m2 - m1)
    w1 = p_group / (1.0 + e2)
    w2 = w1 * e2
    eid0 = gidx * EXPERTS_PER_GROUP + i1
    eid1 = gidx * EXPERTS_PER_GROUP + i2
    eid_ref[0:1, :] = eid0
    eid_ref[1:2, :] = eid1
    erow = lax.broadcasted_iota(jnp.int32, (N_EXPERTS, tm), 0)
    oh0 = erow == eid0
    oh1 = erow == eid1
    both = jnp.where(oh0, 1.0, jnp.where(oh1, 1.0, 0.0))
    earlier = (lax.broadcasted_iota(jnp.int32, (tm, tm), 0)
               < lax.broadcasted_iota(jnp.int32, (tm, tm), 1))
    seen = base[...] + _dot(both.astype(BF16), jnp.where(earlier, 1.0, 0.0).astype(BF16))
    rank_ref[0:1, :] = jnp.sum(jnp.where(oh0, seen, 0.0), axis=0, keepdims=True).astype(jnp.int32)
    rank_ref[1:2, :] = jnp.sum(jnp.where(oh1, seen, 0.0), axis=0, keepdims=True).astype(jnp.int32)
    base[...] = base[...] + jnp.sum(both, axis=1, keepdims=True)
    cnt_ref[...] = base[...]
    wrow = lax.broadcasted_iota(jnp.int32, (LANES, tm), 0)
    wmat = jnp.where(wrow == 0, w1, jnp.where(wrow == 1, w2, 0.0))
    cwt_ref[...] = wmat.T


def _oproj(conv_p, gm, y_mla, x2, mod3, wo, g2, wr_t, br, seq):
    t = x2.shape[0]
    tm = ROW_TILE
    tiles_per_seq = seq // tm
    row = lambda i: (i, 0)
    const = lambda i: (0, 0)
    return pl.pallas_call(
        _oproj_kernel,
        grid=(t // tm,),
        in_specs=[
            pl.BlockSpec((tm, D_MODEL), row),
            pl.BlockSpec((tm, D_MODEL), row),
            pl.BlockSpec((tm, D_MODEL), row),
            pl.BlockSpec((tm, D_MODEL), row),
            pl.BlockSpec((1, 1, 6 * D_MODEL), lambda i: (i // tiles_per_seq, 0, 0)),
            pl.BlockSpec((D_MODEL, D_MODEL), const),
            pl.BlockSpec((1, D_MODEL), const),
            pl.BlockSpec((ROUTER_ROWS, D_MODEL), const),
            pl.BlockSpec((ROUTER_ROWS, 1), const),
        ],
        out_specs=[
            pl.BlockSpec((tm, D_MODEL), row),
            pl.BlockSpec((tm * ROWS_PER_TOKEN, LANES), row),
            pl.BlockSpec((2, tm), lambda i: (0, i)),
            pl.BlockSpec((2, tm), lambda i: (0, i)),
            pl.BlockSpec((N_EXPERTS, 1), const),
            pl.BlockSpec((tm, LANES), row),
        ],
        out_shape=[
            jax.ShapeDtypeStruct((t, D_MODEL), F32),
            jax.ShapeDtypeStruct((t * ROWS_PER_TOKEN, LANES), F32),
            jax.ShapeDtypeStruct((2, t), jnp.int32),
            jax.ShapeDtypeStruct((2, t), jnp.int32),
            jax.ShapeDtypeStruct((N_EXPERTS, 1), F32),
            jax.ShapeDtypeStruct((t, LANES), F32),
        ],
        scratch_shapes=[pltpu.VMEM((N_EXPERTS, 1), F32)],
        compiler_params=pltpu.CompilerParams(
            dimension_semantics=("arbitrary",), vmem_limit_bytes=VMEM_LIMIT),
        name="oproj",
    )(conv_p, gm, y_mla, x2, mod3, wo, g2, wr_t, br)


def _dispatch_kernel(n_tiles, ztile_ref, na_ref, pos_ref, h2_ref, xs_hbm, zbuf, sem, zsem):
    tm = h2_ref.shape[0] // ROWS_PER_TOKEN
    tile_rows = MOE_TILE * ROWS_PER_TOKEN

    def zero_tile(tile, on_sem):
        row = pl.multiple_of(tile * tile_rows, tile_rows)
        return pltpu.make_async_copy(zbuf, xs_hbm.at[pl.ds(row, tile_rows), :], on_sem)

    def unused_tiles(action):
        def body(k, carry):
            action(zero_tile(na_ref[0] + k, zsem))
            return carry
        lax.fori_loop(0, n_tiles - na_ref[0], body, 0)

    @pl.when(pl.program_id(0) == 0)
    def _():
        zbuf[...] = jnp.zeros(zbuf.shape, zbuf.dtype)
        for e in range(N_EXPERTS):
            zero_tile(ztile_ref[e], sem).start()
        for e in range(N_EXPERTS):
            zero_tile(ztile_ref[e], sem).wait()
        unused_tiles(lambda cp: cp.start())

    def body(c, carry):
        for k in range(DMA_UNROLL):
            t = c * DMA_UNROLL + k
            src = h2_ref.at[pl.ds(pl.multiple_of(t * ROWS_PER_TOKEN, ROWS_PER_TOKEN), ROWS_PER_TOKEN), :]
            for s in range(2):
                dst_row = pl.multiple_of(pos_ref[0, 0, s * tm + t] * ROWS_PER_TOKEN, ROWS_PER_TOKEN)
                pltpu.make_async_copy(src, xs_hbm.at[pl.ds(dst_row, ROWS_PER_TOKEN), :],
                                      sem).start(priority=s)
        return carry
    lax.fori_loop(0, tm // DMA_UNROLL, body, 0)
    for _ in range(2):
        pltpu.make_async_copy(h2_ref, xs_hbm.at[pl.ds(0, tm * ROWS_PER_TOKEN), :], sem).wait()

    @pl.when(pl.program_id(0) == pl.num_programs(0) - 1)
    def _():
        unused_tiles(lambda cp: cp.wait())


def _dispatch(ztile, na, pos3, h2r, n_tiles):
    tm = pos3.shape[2] // 2
    n_steps = pos3.shape[0]
    grid_spec = pltpu.PrefetchScalarGridSpec(
        num_scalar_prefetch=2,
        grid=(n_steps,),
        in_specs=[
            pl.BlockSpec((1, 1, 2 * tm), lambda i, z, n: (i, 0, 0), memory_space=pltpu.SMEM),
            pl.BlockSpec((tm * ROWS_PER_TOKEN, LANES), lambda i, z, n: (i, 0)),
        ],
        out_specs=pl.BlockSpec(memory_space=pl.ANY),
        scratch_shapes=[
            pltpu.VMEM((MOE_TILE * ROWS_PER_TOKEN, LANES), F32),
            pltpu.SemaphoreType.DMA(()),
            pltpu.SemaphoreType.DMA(()),
        ],
    )
    return pl.pallas_call(
        functools.partial(_dispatch_kernel, n_tiles),
        grid_spec=grid_spec,
        out_shape=jax.ShapeDtypeStruct((n_tiles * MOE_TILE * ROWS_PER_TOKEN, LANES), F32),
        compiler_params=pltpu.CompilerParams(
            dimension_semantics=("arbitrary",), vmem_limit_bytes=VMEM_LIMIT),
        name="dispatch",
    )(ztile, na, pos3, h2r)


def _moe_kernel(te_ref, na_ref, x_hbm, wg_ref, wu_ref, wd_ref, y_ref, xbuf, wgu_s, wd_s, sem):
    i = pl.program_id(0)
    n_active = na_ref[0]
    active = i < n_active
    tile_rows = MOE_TILE * ROWS_PER_TOKEN

    def fetch(tile):
        slot = tile % X_SLOTS
        row = pl.multiple_of(tile * tile_rows, tile_rows)
        return pltpu.make_async_copy(x_hbm.at[pl.ds(row, tile_rows), :], xbuf.at[slot], sem.at[slot])

    @pl.when(i == 0)
    def _():
        for k in range(X_SLOTS - 1):
            @pl.when(k < n_active)
            def _():
                fetch(k).start()

    @pl.when(i + X_SLOTS - 1 < n_active)
    def _():
        fetch(i + X_SLOTS - 1).start()

    new_expert = jnp.logical_or(i == 0, te_ref[i] != te_ref[jnp.maximum(i - 1, 0)])

    @pl.when(jnp.logical_and(active, new_expert))
    def _():
        wgu_s[:, 0:D_EXPERT] = wg_ref[0].astype(BF16)
        wgu_s[:, D_EXPERT:] = wu_ref[0].astype(BF16)
        wd_s[...] = wd_ref[0].astype(BF16)

    @pl.when(active)
    def _():
        fetch(i).wait()
        x = jnp.concatenate([col.astype(BF16) for col in _load_rows(xbuf.at[i % X_SLOTS], MOE_TILE)], axis=1)
        gu = _dot(x, wgu_s[...])
        g = gu[:, 0:D_EXPERT]
        a = (g * _sigmoid(g)) * gu[:, D_EXPERT:]
        y = _dot(a.astype(BF16), wd_s[...])
        _store_rows(y_ref, y, MOE_TILE)

    @pl.when(jnp.logical_not(active))
    def _():
        y_ref[...] = jnp.zeros(y_ref.shape, y_ref.dtype)


def _moe(te, na, xs, wg, wu, wd):
    nt = te.shape[0]
    tile_rows = MOE_TILE * ROWS_PER_TOKEN
    wspec = lambda shape: pl.BlockSpec((1,) + shape, lambda i, te_r, na_r: (te_r[i], 0, 0))
    grid_spec = pltpu.PrefetchScalarGridSpec(
        num_scalar_prefetch=2,
        grid=(nt,),
        in_specs=[
            pl.BlockSpec(memory_space=pl.ANY),
            wspec((D_MODEL, D_EXPERT)),
            wspec((D_MODEL, D_EXPERT)),
            wspec((D_EXPERT, D_MODEL)),
        ],
        out_specs=pl.BlockSpec((tile_rows, LANES), lambda i, te_r, na_r: (i, 0)),
        scratch_shapes=[
            pltpu.VMEM((X_SLOTS, tile_rows, LANES), F32),
            pltpu.VMEM((D_MODEL, 2 * D_EXPERT), BF16),
            pltpu.VMEM((D_EXPERT, D_MODEL), BF16),
            pltpu.SemaphoreType.DMA((X_SLOTS,)),
        ],
    )
    return pl.pallas_call(
        _moe_kernel,
        grid_spec=grid_spec,
        out_shape=jax.ShapeDtypeStruct(xs.shape, xs.dtype),
        compiler_params=pltpu.CompilerParams(
            dimension_semantics=("arbitrary",), vmem_limit_bytes=VMEM_LIMIT),
        name="moe",
    )(te, na, xs, wg, wu, wd)


def _comb_kernel(pos0_ref, posn_ref, x1_ref, cwt_ref, mod_ref, y_hbm, o_ref, ybuf, sem):
    tm = x1_ref.shape[0]
    i = pl.program_id(0)
    slot = i % 2

    def gather(pos_ref, dst_slot):
        def body(c, carry):
            for k in range(DMA_UNROLL):
                r = c * DMA_UNROLL + k
                src_row = pl.multiple_of(pos_ref[0, 0, r] * ROWS_PER_TOKEN, ROWS_PER_TOKEN)
                dst_row = pl.multiple_of(r * ROWS_PER_TOKEN, ROWS_PER_TOKEN)
                pltpu.make_async_copy(y_hbm.at[pl.ds(src_row, ROWS_PER_TOKEN), :],
                                      ybuf.at[dst_slot, pl.ds(dst_row, ROWS_PER_TOKEN), :],
                                      sem.at[dst_slot]).start(priority=k % 2)
            return carry
        lax.fori_loop(0, 2 * tm // DMA_UNROLL, body, 0)

    @pl.when(i == 0)
    def _():
        gather(pos0_ref, 0)

    @pl.when(i + 1 < pl.num_programs(0))
    def _():
        gather(posn_ref, 1 - slot)

    pltpu.make_async_copy(y_hbm.at[pl.ds(0, 2 * tm * ROWS_PER_TOKEN), :], ybuf.at[slot],
                          sem.at[slot]).wait()
    c0 = cwt_ref[:, 0:1]
    c1 = cwt_ref[:, 1:2]
    y0 = _load_rows(ybuf.at[slot], tm)
    y1 = _load_rows(ybuf.at[slot], tm, offset=tm * ROWS_PER_TOKEN)
    for j in range(ROWS_PER_TOKEN):
        cols = slice(j * LANES, (j + 1) * LANES)
        gate2 = mod_ref[0, :, 5 * D_MODEL + j * LANES:5 * D_MODEL + (j + 1) * LANES]
        o_ref[:, cols] = x1_ref[:, cols] + gate2 * (c0 * y0[j] + c1 * y1[j])


def _comb(x1, yr, pos3, cwt, mod3, seq):
    t = x1.shape[0]
    tm = ROW_TILE
    tiles_per_seq = seq // tm
    n_steps = t // tm
    row = lambda i: (i, 0)
    smem_blk = lambda f: pl.BlockSpec((1, 1, 2 * tm), f, memory_space=pltpu.SMEM)
    return pl.pallas_call(
        _comb_kernel,
        grid=(n_steps,),
        in_specs=[
            smem_blk(lambda i: (0, 0, 0)),
            smem_blk(lambda i: (jnp.minimum(i + 1, n_steps - 1), 0, 0)),
            pl.BlockSpec((tm, D_MODEL), row),
            pl.BlockSpec((tm, LANES), row),
            pl.BlockSpec((1, 1, 6 * D_MODEL), lambda i: (i // tiles_per_seq, 0, 0)),
            pl.BlockSpec(memory_space=pl.ANY),
        ],
        out_specs=pl.BlockSpec((tm, D_MODEL), row),
        out_shape=jax.ShapeDtypeStruct((t, D_MODEL), F32),
        scratch_shapes=[
            pltpu.VMEM((2, 2 * tm * ROWS_PER_TOKEN, LANES), F32),
            pltpu.SemaphoreType.DMA((2,)),
        ],
        compiler_params=pltpu.CompilerParams(
            dimension_semantics=("arbitrary",), vmem_limit_bytes=VMEM_LIMIT),
        name="comb",
    )(pos3, pos3, x1, cwt, mod3, yr)


def _route_plan(eid, rank, cnt, n_tok):
    n_tiles = (2 * n_tok) // MOE_TILE + N_EXPERTS
    experts = jnp.arange(N_EXPERTS, dtype=jnp.int32)
    counts = cnt.reshape(N_EXPERTS).astype(jnp.int32)
    ntile = (counts + MOE_TILE - 1) // MOE_TILE
    tend = jnp.cumsum(ntile)
    tstart = tend - ntile
    n_active = tend[-1]
    tj = jnp.arange(n_tiles, dtype=jnp.int32)
    te_raw = jnp.minimum(jnp.sum((tj[:, None] >= tend[None, :]).astype(jnp.int32), axis=1),
                         N_EXPERTS - 1)
    te_last = jnp.sum(jnp.where(tj == n_active - 1, te_raw, 0))
    te = jnp.where(tj < n_active, te_raw, te_last).astype(jnp.int32)
    first_row = jnp.sum(jnp.where(eid[:, :, None] == experts[None, None, :],
                                  (tstart * MOE_TILE)[None, None, :], 0), axis=-1)
    pos = (first_row + rank).astype(jnp.int32)
    ztile = jnp.maximum(tend - 1, 0).astype(jnp.int32)
    pos3 = pos.reshape(2, n_tok // ROW_TILE, ROW_TILE).transpose(1, 0, 2).reshape(
        n_tok // ROW_TILE, 1, 2 * ROW_TILE)
    return te, n_active.reshape(1).astype(jnp.int32), ztile, pos3, n_tiles


def _rotate_half_cols(w):
    half = QK_ROPE // 2
    return jnp.concatenate([w[..., half:], w[..., :half]], axis=-1)


def kernel(x, c, positions, w_ada, b_ada, norm1_g, w_in, conv_w, q_a_norm_g, w_q_b, kv_a_norm_g, w_kv_b, q_norm_g, k_norm_g, w_o, norm2_g, w_router_group, b_router_group, w_router_expert, b_router_expert, w_exp_gate, w_exp_up, w_exp_down):
    nb, seq, d = x.shape
    depth = w_ada.shape[0]
    n_tok = nb * seq
    assert d == D_MODEL and seq % ROW_TILE == 0 and seq % Q_TILE == 0 and Q_TILE % CHUNK == 0
    assert (2 * n_tok) % MOE_TILE == 0

    inv = ROPE_BASE ** (-jnp.arange(0, QK_ROPE, 2, dtype=F32) / QK_ROPE)
    ang = inv[None, :, None] * positions.astype(F32)[:, None, :]
    cos, sin = jnp.cos(ang), jnp.sin(ang)
    cst = jnp.concatenate([cos, cos, -sin, sin], axis=1)

    x2 = x.reshape(n_tok, d)
    for l in range(depth):
        wi = w_in[l]
        o_q = 3 * D_MODEL
        o_kv = o_q + Q_LORA
        o_kr = o_kv + KV_LORA
        o_gc = o_kr + QK_ROPE
        w_kr = wi[:, o_kr:o_gc]
        w_mix = wi[:, 0:o_q].astype(BF16)
        w_lat = wi[:, o_q:o_kr].astype(BF16)
        w_gate = wi[:, o_gc:].astype(BF16)
        w_krt = jnp.concatenate([w_kr, _rotate_half_cols(w_kr)], axis=1).T.astype(BF16)
        wq3 = w_q_b[l].reshape(Q_LORA, N_HEADS, QK_HEAD)
        wq = jnp.concatenate([wq3, _rotate_half_cols(wq3[..., QK_NOPE:])], axis=-1)
        wqt = wq.transpose(1, 2, 0).astype(BF16)
        wkvt = w_kv_b[l].reshape(KV_LORA, N_HEADS, QK_NOPE + V_HEAD).transpose(1, 2, 0).astype(BF16)
        gq = jnp.concatenate([q_norm_g[l], _rotate_half_cols(q_norm_g[l][QK_NOPE:])]).reshape(-1, 1)
        gk = jnp.concatenate([k_norm_g[l], _rotate_half_cols(k_norm_g[l][QK_NOPE:])]).reshape(-1, 1)
        wr_t = jnp.concatenate(
            [w_router_expert[l].T, w_router_group[l].T,
             jnp.zeros((ROUTER_ROWS - N_EXPERTS - N_GROUPS, d), F32)], axis=0).astype(BF16)
        br = jnp.concatenate(
            [b_router_expert[l], b_router_group[l],
             jnp.zeros((ROUTER_ROWS - N_EXPERTS - N_GROUPS,), F32)]).reshape(ROUTER_ROWS, 1)

        mod3 = _ada(c, w_ada[l], b_ada[l]).reshape(nb, 1, 6 * d)
        conv_p, gm, qn, kvn, krt = _inproj(
            x2, mod3, norm1_g[l].reshape(1, d), w_mix, w_gate, w_lat, w_krt, conv_w[l],
            q_a_norm_g[l].reshape(1, -1), kv_a_norm_g[l].reshape(1, -1), seq)
        y_mla = _attn(qn, kvn, krt, cst, wqt, wkvt, gq, gk, nb, seq)
        x1, h2r, eid, rank, cnt, cwt = _oproj(conv_p, gm, y_mla, x2, mod3, w_o[l].astype(BF16),
                                              norm2_g[l].reshape(1, d), wr_t, br, seq)
        te, na, ztile, pos3, n_tiles = _route_plan(eid, rank, cnt, n_tok)
        xs = _dispatch(ztile, na, pos3, h2r, n_tiles)
        yr = _moe(te, na, xs,
                  w_exp_gate[l].reshape(N_EXPERTS, d, D_EXPERT),
                  w_exp_up[l].reshape(N_EXPERTS, d, D_EXPERT),
                  w_exp_down[l].reshape(N_EXPERTS, D_EXPERT, d))
        x2 = _comb(x1, yr, pos3, cwt, mod3, seq)
    return x2.reshape(nb, seq, d)
```

```python
import functools
import math

import jax
import jax.numpy as jnp
from jax import lax
from jax.experimental import pallas as pl
from jax.experimental.pallas import tpu as pltpu

F32 = jnp.float32
BF16 = jnp.bfloat16

D_MODEL = 1024
N_HEADS = 8
QK_NOPE = 128
QK_ROPE = 64
QK_HEAD = QK_NOPE + QK_ROPE
V_HEAD = 128
Q_LORA = 384
KV_LORA = 256
CHUNK = 64
EPS = 1e-6
ROPE_BASE = 10000.0
N_GROUPS = 4
EXPERTS_PER_GROUP = 8
N_EXPERTS = N_GROUPS * EXPERTS_PER_GROUP
D_EXPERT = 256
CONV_K = 3

LANES = 128
SUBLANES = 8
VMEM_LIMIT = 56 * 1024 * 1024

ROW_TILE = 512
Q_TILE = 512
INPROJ_SUB = 2
HEADS_PER_STEP = 2
MOE_TILE = 512
X_SLOTS = 4
ADA_COLS = 1536
ROWS_PER_TOKEN = D_MODEL // LANES
DMA_UNROLL = 8
ROUTER_ROWS = 40


def _sigmoid(v):
    return 1.0 / (1.0 + jnp.exp(-v))


def _dot(a, b):
    return jnp.dot(a, b, preferred_element_type=F32)


def _store_rows(ref, val, n):
    for j in range(ROWS_PER_TOKEN):
        ref[pl.ds(j, n, stride=ROWS_PER_TOKEN), :] = val[:, j * LANES:(j + 1) * LANES]


def _load_rows(ref, n, offset=0):
    return [ref[pl.ds(offset + j, n, stride=ROWS_PER_TOKEN), :] for j in range(ROWS_PER_TOKEN)]


def _dot_nt(a, b):
    return lax.dot_general(a, b, (((1,), (1,)), ((), ())), preferred_element_type=F32)


def _ada_kernel(c_ref, w_ref, b_ref, o_ref):
    c = c_ref[...]
    act = (c * _sigmoid(c)).astype(BF16)
    o_ref[...] = _dot(act, w_ref[...].astype(BF16)) + b_ref[...]


def _ada(c, w_ada, b_ada):
    nb, d = c.shape
    n = w_ada.shape[1]
    return pl.pallas_call(
        _ada_kernel,
        grid=(n // ADA_COLS,),
        in_specs=[
            pl.BlockSpec((nb, d), lambda j: (0, 0)),
            pl.BlockSpec((d, ADA_COLS), lambda j: (0, j)),
            pl.BlockSpec((1, ADA_COLS), lambda j: (0, j)),
        ],
        out_specs=pl.BlockSpec((nb, ADA_COLS), lambda j: (0, j)),
        out_shape=jax.ShapeDtypeStruct((nb, n), F32),
        compiler_params=pltpu.CompilerParams(
            dimension_semantics=("arbitrary",), vmem_limit_bytes=VMEM_LIMIT),
        name="ada",
    )(c, w_ada, b_ada.reshape(1, n))


def _inproj_kernel(tiles_per_seq, x_ref, mod_ref, g1_ref, wmix_ref, wgate_ref, wlat_ref, wkr_ref,
                   cw_ref, gq_ref, gkv_ref,
                   conv_ref, gm_ref, qn_ref, kvn_ref, krt_ref, ubuf):
    tm = x_ref.shape[0] // INPROJ_SUB

    @pl.when(pl.program_id(0) % tiles_per_seq == 0)
    def _():
        ubuf[0:SUBLANES, :] = jnp.zeros((SUBLANES, D_MODEL), F32)

    shift = mod_ref[0, :, 0:D_MODEL]
    scale = mod_ref[0, :, D_MODEL:2 * D_MODEL]
    for sub in range(INPROJ_SUB):
        rows = slice(sub * tm, (sub + 1) * tm)
        x = x_ref[rows, :]
        xn = x * lax.rsqrt(jnp.mean(x * x, axis=-1, keepdims=True) + EPS) * g1_ref[...]
        h = (xn * (1.0 + scale) + shift).astype(BF16)

        def proj(w_ref, lo, width):
            return _dot(h, w_ref[:, lo:lo + width])

        u = proj(wmix_ref, 2 * D_MODEL, D_MODEL) * proj(wmix_ref, 0, D_MODEL)
        ubuf[SUBLANES:SUBLANES + tm, :] = u
        conv = (ubuf[SUBLANES - 2:SUBLANES - 2 + tm, :] * cw_ref[0:1, :]
                + ubuf[SUBLANES - 1:SUBLANES - 1 + tm, :] * cw_ref[1:2, :]
                + u * cw_ref[2:3, :])
        ubuf[0:SUBLANES, :] = ubuf[tm:tm + SUBLANES, :]
        y_conv = proj(wmix_ref, D_MODEL, D_MODEL) * conv
        conv_ref[rows, :] = (_sigmoid(proj(wgate_ref, 0, D_MODEL)) * y_conv).astype(BF16)
        gm_ref[rows, :] = _sigmoid(proj(wgate_ref, D_MODEL, D_MODEL)).astype(BF16)

        ql = proj(wlat_ref, 0, Q_LORA)
        qn_ref[rows, :] = (ql * lax.rsqrt(jnp.mean(ql * ql, axis=-1, keepdims=True) + EPS)
                           * gq_ref[...]).astype(BF16)
        kl = proj(wlat_ref, Q_LORA, KV_LORA)
        kvn_ref[rows, :] = (kl * lax.rsqrt(jnp.mean(kl * kl, axis=-1, keepdims=True) + EPS)
                            * gkv_ref[...]).astype(BF16)
        krt_ref[0, :, rows] = _dot_nt(wkr_ref[...], h)


def _inproj(x2, mod3, g1, w_mix, w_gate, w_lat, w_krt, conv_w, gq, gkv, seq):
    t = x2.shape[0]
    nb = t // seq
    tm = ROW_TILE * INPROJ_SUB
    tiles_per_seq = seq // tm
    row = lambda i: (i, 0)
    const = lambda i: (0, 0)
    return pl.pallas_call(
        functools.partial(_inproj_kernel, tiles_per_seq),
        grid=(t // tm,),
        in_specs=[
            pl.BlockSpec((tm, D_MODEL), row),
            pl.BlockSpec((1, 1, 6 * D_MODEL), lambda i: (i // tiles_per_seq, 0, 0)),
            pl.BlockSpec((1, D_MODEL), const),
            pl.BlockSpec((D_MODEL, 3 * D_MODEL), const),
            pl.BlockSpec((D_MODEL, 2 * D_MODEL), const),
            pl.BlockSpec((D_MODEL, Q_LORA + KV_LORA), const),
            pl.BlockSpec((2 * QK_ROPE, D_MODEL), const),
            pl.BlockSpec((CONV_K, D_MODEL), const),
            pl.BlockSpec((1, Q_LORA), const),
            pl.BlockSpec((1, KV_LORA), const),
        ],
        out_specs=[
            pl.BlockSpec((tm, D_MODEL), row),
            pl.BlockSpec((tm, D_MODEL), row),
            pl.BlockSpec((tm, Q_LORA), row),
            pl.BlockSpec((tm, KV_LORA), row),
            pl.BlockSpec((1, 2 * QK_ROPE, tm),
                         lambda i: (i // tiles_per_seq, 0, i % tiles_per_seq)),
        ],
        out_shape=[
            jax.ShapeDtypeStruct((t, D_MODEL), BF16),
            jax.ShapeDtypeStruct((t, D_MODEL), BF16),
            jax.ShapeDtypeStruct((t, Q_LORA), BF16),
            jax.ShapeDtypeStruct((t, KV_LORA), BF16),
            jax.ShapeDtypeStruct((nb, 2 * QK_ROPE, seq), F32),
        ],
        scratch_shapes=[pltpu.VMEM((ROW_TILE + SUBLANES, D_MODEL), F32)],
        compiler_params=pltpu.CompilerParams(
            dimension_semantics=("arbitrary",), vmem_limit_bytes=VMEM_LIMIT),
        name="inproj",
    )(x2, mod3, g1, w_mix, w_gate, w_lat, w_krt, conv_w, gq, gkv)


def _attn_kernel(qn_ref, kvn_ref, krt_ref, cst_ref, wqt_ref, wkvt_ref, gq_ref, gk_ref, conv_ref, gm_ref,
                 o_ref, *scratch):
    per_head = len(scratch) // HEADS_PER_STEP
    heads = [_attn_head(hh, qn_ref, kvn_ref, krt_ref, cst_ref, wqt_ref, wkvt_ref, gq_ref, gk_ref,
                        conv_ref, gm_ref, o_ref, *scratch[hh * per_head:(hh + 1) * per_head])
             for hh in range(HEADS_PER_STEP)]
    nq = qn_ref.shape[0] // Q_TILE
    for scores, _ in heads:
        scores(0)
    for i in range(nq):
        for scores, finish in heads:
            if i + 1 < nq:
                scores(i + 1)
            finish(i)


def _attn_head(hh, qn_ref, kvn_ref, krt_ref, cst_ref, wqt_ref, wkvt_ref, gq_ref, gk_ref,
               conv_ref, gm_ref, o_ref, qt_s, k_s, vt_s, s_buf0, s_buf1):
    seq = qn_ref.shape[0]
    cos_t = cst_ref[0, 0:QK_ROPE, :]
    sin_t = cst_ref[0, QK_ROPE:, :]

    def normed_rope(nope, r, rr, g, extra_scale):
        ss = jnp.sum(nope * nope, axis=0, keepdims=True) + jnp.sum(r * r, axis=0, keepdims=True)
        scale = lax.rsqrt(ss * (1.0 / QK_HEAD) + EPS) * extra_scale
        rope = r * g[QK_NOPE:QK_HEAD] * cos_t + rr * g[QK_HEAD:] * sin_t
        return (nope * g[0:QK_NOPE] * scale).astype(BF16), (rope * scale).astype(BF16)

    qt = _dot_nt(wqt_ref[hh], qn_ref[...])
    q_n, q_r = normed_rope(qt[0:QK_NOPE], qt[QK_NOPE:QK_HEAD], qt[QK_HEAD:], gq_ref[...],
                           QK_HEAD ** -0.5 * math.log2(math.e))
    qt_s[0:QK_NOPE, :] = q_n
    qt_s[QK_NOPE:QK_HEAD, :] = q_r
    qt_s[QK_HEAD:, :] = jnp.zeros((QK_ROPE, seq), BF16)

    kvt = _dot_nt(wkvt_ref[hh], kvn_ref[...])
    krt = krt_ref[0]
    k_n, k_r = normed_rope(kvt[0:QK_NOPE], krt[0:QK_ROPE], krt[QK_ROPE:], gk_ref[...], 1.0)
    kt = jnp.concatenate([k_n, k_r, jnp.zeros((QK_ROPE, seq), BF16)], axis=0)
    k_s[...] = kt.T
    vt_s[0:V_HEAD, :] = kvt[QK_NOPE:].astype(BF16)
    vt_s[V_HEAD:, :] = jnp.ones((vt_s.shape[0] - V_HEAD, seq), BF16)

    tq = Q_TILE
    kchunk = lax.broadcasted_iota(jnp.int32, (tq, tq), 0) // CHUNK
    qchunk = lax.broadcasted_iota(jnp.int32, (tq, tq), 1) // CHUNK
    diag_ok = kchunk <= qchunk
    neg = jnp.finfo(F32).min

    def scores(i):
        q0 = i * tq
        sb = s_buf0 if i % 2 == 0 else s_buf1
        q = qt_s[:, q0:q0 + tq]
        if i > 0:
            sb[0:q0, :] = _dot(k_s[0:q0, :], q)
        sb[q0:q0 + tq, :] = jnp.where(diag_ok, _dot(k_s[q0:q0 + tq, :], q), neg)

    def finish(i):
        q0 = i * tq
        kend = q0 + tq
        sb = s_buf0 if i % 2 == 0 else s_buf1
        m = jnp.max(sb[0:kend, :], axis=0, keepdims=True)
        acc = _dot(vt_s[:, 0:kend], jnp.exp2(sb[0:kend, :] - m).astype(BF16))
        o_t = acc[0:V_HEAD] / acc[V_HEAD:V_HEAD + 1]
        rows, cols = slice(q0, q0 + tq), slice(hh * V_HEAD, (hh + 1) * V_HEAD)
        y_mla = o_t.T.astype(BF16).astype(F32)
        merged = conv_ref[rows, cols].astype(F32) + gm_ref[rows, cols].astype(F32) * y_mla
        o_ref[rows, cols] = merged.astype(o_ref.dtype)

    return scores, finish


def _attn(qn, kvn, krt, cst, wqt, wkvt, gq, gk, conv_p, gm, nb, seq):
    t = qn.shape[0]
    per_b = lambda b, h: (b, 0)
    per_b3 = lambda b, h: (b, 0, 0)
    per_h = lambda b, h: (h, 0, 0)
    const = lambda b, h: (0, 0)
    qk_rows = QK_NOPE + 2 * QK_ROPE
    return pl.pallas_call(
        _attn_kernel,
        grid=(nb, N_HEADS // HEADS_PER_STEP),
        in_specs=[
            pl.BlockSpec((seq, Q_LORA), per_b),
            pl.BlockSpec((seq, KV_LORA), per_b),
            pl.BlockSpec((1, 2 * QK_ROPE, seq), per_b3),
            pl.BlockSpec((1, 2 * QK_ROPE, seq), per_b3),
            pl.BlockSpec((HEADS_PER_STEP, qk_rows, Q_LORA), per_h),
            pl.BlockSpec((HEADS_PER_STEP, QK_NOPE + V_HEAD, KV_LORA), per_h),
            pl.BlockSpec((qk_rows, 1), const),
            pl.BlockSpec((qk_rows, 1), const),
            pl.BlockSpec((seq, HEADS_PER_STEP * V_HEAD), lambda b, h: (b, h)),
            pl.BlockSpec((seq, HEADS_PER_STEP * V_HEAD), lambda b, h: (b, h)),
        ],
        out_specs=pl.BlockSpec((seq, HEADS_PER_STEP * V_HEAD), lambda b, h: (b, h)),
        out_shape=jax.ShapeDtypeStruct((t, N_HEADS * V_HEAD), BF16),
        scratch_shapes=[
            pltpu.VMEM((qk_rows, seq), BF16),
            pltpu.VMEM((seq, qk_rows), BF16),
            pltpu.VMEM((V_HEAD + 2 * SUBLANES, seq), BF16),
            pltpu.VMEM((seq, Q_TILE), F32),
            pltpu.VMEM((seq, Q_TILE), F32),
        ] * HEADS_PER_STEP,
        compiler_params=pltpu.CompilerParams(
            dimension_semantics=("arbitrary", "arbitrary"), vmem_limit_bytes=VMEM_LIMIT),
        name="attn",
    )(qn, kvn, krt, cst, wqt, wkvt, gq, gk, conv_p, gm)


def _oproj_kernel(merged_ref, x_ref, mod_ref, wo_ref, g2_ref, wr_ref, br_ref,
                  x1_ref, h2_ref, eid_ref, rank_ref, cnt_ref, cwt_ref, base):
    tm = x_ref.shape[0]

    @pl.when(pl.program_id(0) == 0)
    def _():
        base[...] = jnp.zeros(base.shape, F32)

    att = _dot(merged_ref[...], wo_ref[...])
    gate1 = mod_ref[0, :, 2 * D_MODEL:3 * D_MODEL]
    shift2 = mod_ref[0, :, 3 * D_MODEL:4 * D_MODEL]
    scale2 = mod_ref[0, :, 4 * D_MODEL:5 * D_MODEL]
    x1 = x_ref[...] + gate1 * att
    x1_ref[...] = x1
    xn = x1 * lax.rsqrt(jnp.mean(x1 * x1, axis=-1, keepdims=True) + EPS) * g2_ref[...]
    h2 = xn * (1.0 + scale2) + shift2
    _store_rows(h2_ref, h2, tm)

    lt = _dot_nt(wr_ref[...], h2.astype(BF16)) + br_ref[...]
    gl = [lt[N_EXPERTS + r:N_EXPERTS + r + 1, :] for r in range(N_GROUPS)]
    gmax = jnp.maximum(jnp.maximum(gl[0], gl[1]), jnp.maximum(gl[2], gl[3]))
    gidx = jnp.full(gmax.shape, N_GROUPS - 1, jnp.int32)
    for r in range(N_GROUPS - 2, -1, -1):
        gidx = jnp.where(gl[r] == gmax, r, gidx)
    gsum = jnp.exp(gl[0] - gmax)
    for r in range(1, N_GROUPS):
        gsum = gsum + jnp.exp(gl[r] - gmax)
    p_group = 1.0 / gsum
    es = lt[(N_GROUPS - 1) * EXPERTS_PER_GROUP:N_GROUPS * EXPERTS_PER_GROUP, :]
    for r in range(N_GROUPS - 2, -1, -1):
        es = jnp.where(gidx == r, lt[r * EXPERTS_PER_GROUP:(r + 1) * EXPERTS_PER_GROUP, :], es)
    row = lax.broadcasted_iota(jnp.int32, es.shape, 0)
    m1 = jnp.max(es, axis=0, keepdims=True)
    i1 = jnp.min(jnp.where(es == m1, row, EXPERTS_PER_GROUP), axis=0, keepdims=True)
    es2 = jnp.where(row == i1, -jnp.inf, es)
    m2 = jnp.max(es2, axis=0, keepdims=True)
    i2 = jnp.min(jnp.where(es2 == m2, row, EXPERTS_PER_GROUP), axis=0, keepdims=True)
    e2 = jnp.exp(m2 - m1)
    w1 = p_group / (1.0 + e2)
    w2 = w1 * e2
    eid0 = gidx * EXPERTS_PER_GROUP + i1
    eid1 = gidx * EXPERTS_PER_GROUP + i2
    eid_ref[0:1, :] = eid0
    eid_ref[1:2, :] = eid1
    erow = lax.broadcasted_iota(jnp.int32, (N_EXPERTS, tm), 0)
    oh0 = erow == eid0
    oh1 = erow == eid1
    both = jnp.where(oh0, 1.0, jnp.where(oh1, 1.0, 0.0))
    earlier = (lax.broadcasted_iota(jnp.int32, (tm, tm), 0)
               < lax.broadcasted_iota(jnp.int32, (tm, tm), 1))
    seen = base[...] + _dot(both.astype(BF16), jnp.where(earlier, 1.0, 0.0).astype(BF16))
    rank_ref[0:1, :] = jnp.sum(jnp.where(oh0, seen, 0.0), axis=0, keepdims=True).astype(jnp.int32)
    rank_ref[1:2, :] = jnp.sum(jnp.where(oh1, seen, 0.0), axis=0, keepdims=True).astype(jnp.int32)
    base[...] = base[...] + jnp.sum(both, axis=1, keepdims=True)
    cnt_ref[...] = base[...]
    wrow = lax.broadcasted_iota(jnp.int32, (LANES, tm), 0)
    wmat = jnp.where(wrow == 0, w1, jnp.where(wrow == 1, w2, 0.0))
    cwt_ref[...] = wmat.T


def _oproj(merged, x2, mod3, wo, g2, wr_t, br, seq):
    t = x2.shape[0]
    tm = ROW_TILE
    tiles_per_seq = seq // tm
    row = lambda i: (i, 0)
    const = lambda i: (0, 0)
    return pl.pallas_call(
        _oproj_kernel,
        grid=(t // tm,),
        in_specs=[
            pl.BlockSpec((tm, D_MODEL), row),
            pl.BlockSpec((tm, D_MODEL), row),
            pl.BlockSpec((1, 1, 6 * D_MODEL), lambda i: (i // tiles_per_seq, 0, 0)),
            pl.BlockSpec((D_MODEL, D_MODEL), const),
            pl.BlockSpec((1, D_MODEL), const),
            pl.BlockSpec((ROUTER_ROWS, D_MODEL), const),
            pl.BlockSpec((ROUTER_ROWS, 1), const),
        ],
        out_specs=[
            pl.BlockSpec((tm, D_MODEL), row),
            pl.BlockSpec((tm * ROWS_PER_TOKEN, LANES), row),
            pl.BlockSpec((2, tm), lambda i: (0, i)),
            pl.BlockSpec((2, tm), lambda i: (0, i)),
            pl.BlockSpec((N_EXPERTS, 1), const),
            pl.BlockSpec((tm, LANES), row),
        ],
        out_shape=[
            jax.ShapeDtypeStruct((t, D_MODEL), F32),
            jax.ShapeDtypeStruct((t * ROWS_PER_TOKEN, LANES), F32),
            jax.ShapeDtypeStruct((2, t), jnp.int32),
            jax.ShapeDtypeStruct((2, t), jnp.int32),
            jax.ShapeDtypeStruct((N_EXPERTS, 1), F32),
            jax.ShapeDtypeStruct((t, LANES), F32),
        ],
        scratch_shapes=[pltpu.VMEM((N_EXPERTS, 1), F32)],
        compiler_params=pltpu.CompilerParams(
            dimension_semantics=("arbitrary",), vmem_limit_bytes=VMEM_LIMIT),
        name="oproj",
    )(merged, x2, mod3, wo, g2, wr_t, br)


def _dispatch_kernel(n_tiles, ztile_ref, na_ref, pos_ref, h2_ref, xs_hbm, zbuf, sem, zsem):
    tm = h2_ref.shape[0] // ROWS_PER_TOKEN
    tile_rows = MOE_TILE * ROWS_PER_TOKEN

    def zero_tile(tile, on_sem):
        row = pl.multiple_of(tile * tile_rows, tile_rows)
        return pltpu.make_async_copy(zbuf, xs_hbm.at[pl.ds(row, tile_rows), :], on_sem)

    def unused_tiles(action):
        def body(k, carry):
            action(zero_tile(na_ref[0] + k, zsem))
            return carry
        lax.fori_loop(0, n_tiles - na_ref[0], body, 0)

    @pl.when(pl.program_id(0) == 0)
    def _():
        zbuf[...] = jnp.zeros(zbuf.shape, zbuf.dtype)
        for e in range(N_EXPERTS):
            zero_tile(ztile_ref[e], sem).start()
        for e in range(N_EXPERTS):
            zero_tile(ztile_ref[e], sem).wait()
        unused_tiles(lambda cp: cp.start())

    def body(c, carry):
        for k in range(DMA_UNROLL):
            t = c * DMA_UNROLL + k
            src = h2_ref.at[pl.ds(pl.multiple_of(t * ROWS_PER_TOKEN, ROWS_PER_TOKEN), ROWS_PER_TOKEN), :]
            for s in range(2):
                dst_row = pl.multiple_of(pos_ref[0, 0, s * tm + t] * ROWS_PER_TOKEN, ROWS_PER_TOKEN)
                pltpu.make_async_copy(src, xs_hbm.at[pl.ds(dst_row, ROWS_PER_TOKEN), :],
                                      sem).start(priority=s)
        return carry
    lax.fori_loop(0, tm // DMA_UNROLL, body, 0)
    for _ in range(2):
        pltpu.make_async_copy(h2_ref, xs_hbm.at[pl.ds(0, tm * ROWS_PER_TOKEN), :], sem).wait()

    @pl.when(pl.program_id(0) == pl.num_programs(0) - 1)
    def _():
        unused_tiles(lambda cp: cp.wait())


def _dispatch(ztile, na, pos3, h2r, n_tiles):
    tm = pos3.shape[2] // 2
    n_steps = pos3.shape[0]
    grid_spec = pltpu.PrefetchScalarGridSpec(
        num_scalar_prefetch=2,
        grid=(n_steps,),
        in_specs=[
            pl.BlockSpec((1, 1, 2 * tm), lambda i, z, n: (i, 0, 0), memory_space=pltpu.SMEM),
            pl.BlockSpec((tm * ROWS_PER_TOKEN, LANES), lambda i, z, n: (i, 0)),
        ],
        out_specs=pl.BlockSpec(memory_space=pl.ANY),
        scratch_shapes=[
            pltpu.VMEM((MOE_TILE * ROWS_PER_TOKEN, LANES), F32),
            pltpu.SemaphoreType.DMA(()),
            pltpu.SemaphoreType.DMA(()),
        ],
    )
    return pl.pallas_call(
        functools.partial(_dispatch_kernel, n_tiles),
        grid_spec=grid_spec,
        out_shape=jax.ShapeDtypeStruct((n_tiles * MOE_TILE * ROWS_PER_TOKEN, LANES), F32),
        compiler_params=pltpu.CompilerParams(
            dimension_semantics=("arbitrary",), vmem_limit_bytes=VMEM_LIMIT),
        name="dispatch",
    )(ztile, na, pos3, h2r)


def _moe_kernel(te_ref, na_ref, x_hbm, wg_ref, wu_ref, wd_ref, y_ref, xbuf, wgu_s, wd_s, sem):
    i = pl.program_id(0)
    n_active = na_ref[0]
    active = i < n_active
    tile_rows = MOE_TILE * ROWS_PER_TOKEN

    def fetch(tile):
        slot = tile % X_SLOTS
        row = pl.multiple_of(tile * tile_rows, tile_rows)
        return pltpu.make_async_copy(x_hbm.at[pl.ds(row, tile_rows), :], xbuf.at[slot], sem.at[slot])

    @pl.when(i == 0)
    def _():
        for k in range(X_SLOTS - 1):
            @pl.when(k < n_active)
            def _():
                fetch(k).start()

    @pl.when(i + X_SLOTS - 1 < n_active)
    def _():
        fetch(i + X_SLOTS - 1).start()

    new_expert = jnp.logical_or(i == 0, te_ref[i] != te_ref[jnp.maximum(i - 1, 0)])

    @pl.when(jnp.logical_and(active, new_expert))
    def _():
        wgu_s[:, 0:D_EXPERT] = wg_ref[0].astype(BF16)
        wgu_s[:, D_EXPERT:] = wu_ref[0].astype(BF16)
        wd_s[...] = wd_ref[0].astype(BF16)

    @pl.when(active)
    def _():
        fetch(i).wait()
        x = jnp.concatenate([col.astype(BF16) for col in _load_rows(xbuf.at[i % X_SLOTS], MOE_TILE)], axis=1)
        gu = _dot(x, wgu_s[...])
        g = gu[:, 0:D_EXPERT]
        a = (g * _sigmoid(g)) * gu[:, D_EXPERT:]
        y = _dot(a.astype(BF16), wd_s[...])
        _store_rows(y_ref, y, MOE_TILE)

    @pl.when(jnp.logical_not(active))
    def _():
        y_ref[...] = jnp.zeros(y_ref.shape, y_ref.dtype)


def _moe(te, na, xs, wg, wu, wd):
    nt = te.shape[0]
    tile_rows = MOE_TILE * ROWS_PER_TOKEN
    wspec = lambda shape: pl.BlockSpec((1,) + shape, lambda i, te_r, na_r: (te_r[i], 0, 0))
    grid_spec = pltpu.PrefetchScalarGridSpec(
        num_scalar_prefetch=2,
        grid=(nt,),
        in_specs=[
            pl.BlockSpec(memory_space=pl.ANY),
            wspec((D_MODEL, D_EXPERT)),
            wspec((D_MODEL, D_EXPERT)),
            wspec((D_EXPERT, D_MODEL)),
        ],
        out_specs=pl.BlockSpec((tile_rows, LANES), lambda i, te_r, na_r: (i, 0)),
        scratch_shapes=[
            pltpu.VMEM((X_SLOTS, tile_rows, LANES), F32),
            pltpu.VMEM((D_MODEL, 2 * D_EXPERT), BF16),
            pltpu.VMEM((D_EXPERT, D_MODEL), BF16),
            pltpu.SemaphoreType.DMA((X_SLOTS,)),
        ],
    )
    return pl.pallas_call(
        _moe_kernel,
        grid_spec=grid_spec,
        out_shape=jax.ShapeDtypeStruct(xs.shape, xs.dtype),
        compiler_params=pltpu.CompilerParams(
            dimension_semantics=("arbitrary",), vmem_limit_bytes=VMEM_LIMIT),
        name="moe",
    )(te, na, xs, wg, wu, wd)


def _comb_kernel(pos0_ref, posn_ref, x1_ref, cwt_ref, mod_ref, y_hbm, o_ref, ybuf, sem):
    tm = x1_ref.shape[0]
    i = pl.program_id(0)
    slot = i % 2

    def gather(pos_ref, dst_slot):
        def body(c, carry):
            for k in range(DMA_UNROLL):
                r = c * DMA_UNROLL + k
                src_row = pl.multiple_of(pos_ref[0, 0, r] * ROWS_PER_TOKEN, ROWS_PER_TOKEN)
                dst_row = pl.multiple_of(r * ROWS_PER_TOKEN, ROWS_PER_TOKEN)
                pltpu.make_async_copy(y_hbm.at[pl.ds(src_row, ROWS_PER_TOKEN), :],
                                      ybuf.at[dst_slot, pl.ds(dst_row, ROWS_PER_TOKEN), :],
                                      sem.at[dst_slot]).start(priority=k % 2)
            return carry
        lax.fori_loop(0, 2 * tm // DMA_UNROLL, body, 0)

    @pl.when(i == 0)
    def _():
        gather(pos0_ref, 0)

    @pl.when(i + 1 < pl.num_programs(0))
    def _():
        gather(posn_ref, 1 - slot)

    pltpu.make_async_copy(y_hbm.at[pl.ds(0, 2 * tm * ROWS_PER_TOKEN), :], ybuf.at[slot],
                          sem.at[slot]).wait()
    c0 = cwt_ref[:, 0:1]
    c1 = cwt_ref[:, 1:2]
    y0 = _load_rows(ybuf.at[slot], tm)
    y1 = _load_rows(ybuf.at[slot], tm, offset=tm * ROWS_PER_TOKEN)
    for j in range(ROWS_PER_TOKEN):
        cols = slice(j * LANES, (j + 1) * LANES)
        gate2 = mod_ref[0, :, 5 * D_MODEL + j * LANES:5 * D_MODEL + (j + 1) * LANES]
        o_ref[:, cols] = x1_ref[:, cols] + gate2 * (c0 * y0[j] + c1 * y1[j])


def _comb(x1, yr, pos3, cwt, mod3, seq):
    t = x1.shape[0]
    tm = ROW_TILE
    tiles_per_seq = seq // tm
    n_steps = t // tm
    row = lambda i: (i, 0)
    smem_blk = lambda f: pl.BlockSpec((1, 1, 2 * tm), f, memory_space=pltpu.SMEM)
    return pl.pallas_call(
        _comb_kernel,
        grid=(n_steps,),
        in_specs=[
            smem_blk(lambda i: (0, 0, 0)),
            smem_blk(lambda i: (jnp.minimum(i + 1, n_steps - 1), 0, 0)),
            pl.BlockSpec((tm, D_MODEL), row),
            pl.BlockSpec((tm, LANES), row),
            pl.BlockSpec((1, 1, 6 * D_MODEL), lambda i: (i // tiles_per_seq, 0, 0)),
            pl.BlockSpec(memory_space=pl.ANY),
        ],
        out_specs=pl.BlockSpec((tm, D_MODEL), row),
        out_shape=jax.ShapeDtypeStruct((t, D_MODEL), F32),
        scratch_shapes=[
            pltpu.VMEM((2, 2 * tm * ROWS_PER_TOKEN, LANES), F32),
            pltpu.SemaphoreType.DMA((2,)),
        ],
        compiler_params=pltpu.CompilerParams(
            dimension_semantics=("arbitrary",), vmem_limit_bytes=VMEM_LIMIT),
        name="comb",
    )(pos3, pos3, x1, cwt, mod3, yr)


def _route_plan(eid, rank, cnt, n_tok):
    n_tiles = (2 * n_tok) // MOE_TILE + N_EXPERTS
    experts = jnp.arange(N_EXPERTS, dtype=jnp.int32)
    counts = cnt.reshape(N_EXPERTS).astype(jnp.int32)
    ntile = (counts + MOE_TILE - 1) // MOE_TILE
    tend = jnp.cumsum(ntile)
    tstart = tend - ntile
    n_active = tend[-1]
    tj = jnp.arange(n_tiles, dtype=jnp.int32)
    te_raw = jnp.minimum(jnp.sum((tj[:, None] >= tend[None, :]).astype(jnp.int32), axis=1),
                         N_EXPERTS - 1)
    te_last = jnp.sum(jnp.where(tj == n_active - 1, te_raw, 0))
    te = jnp.where(tj < n_active, te_raw, te_last).astype(jnp.int32)
    first_row = jnp.sum(jnp.where(eid[:, :, None] == experts[None, None, :],
                                  (tstart * MOE_TILE)[None, None, :], 0), axis=-1)
    pos = (first_row + rank).astype(jnp.int32)
    ztile = jnp.maximum(tend - 1, 0).astype(jnp.int32)
    pos3 = pos.reshape(2, n_tok // ROW_TILE, ROW_TILE).transpose(1, 0, 2).reshape(
        n_tok // ROW_TILE, 1, 2 * ROW_TILE)
    return te, n_active.reshape(1).astype(jnp.int32), ztile, pos3, n_tiles


def _rotate_half_cols(w):
    half = QK_ROPE // 2
    return jnp.concatenate([w[..., half:], w[..., :half]], axis=-1)


def kernel(x, c, positions, w_ada, b_ada, norm1_g, w_in, conv_w, q_a_norm_g, w_q_b, kv_a_norm_g, w_kv_b, q_norm_g, k_norm_g, w_o, norm2_g, w_router_group, b_router_group, w_router_expert, b_router_expert, w_exp_gate, w_exp_up, w_exp_down):
    nb, seq, d = x.shape
    depth = w_ada.shape[0]
    n_tok = nb * seq
    assert d == D_MODEL and seq % ROW_TILE == 0 and seq % Q_TILE == 0 and Q_TILE % CHUNK == 0
    assert (2 * n_tok) % MOE_TILE == 0

    inv = ROPE_BASE ** (-jnp.arange(0, QK_ROPE, 2, dtype=F32) / QK_ROPE)
    ang = inv[None, :, None] * positions.astype(F32)[:, None, :]
    cos, sin = jnp.cos(ang), jnp.sin(ang)
    cst = jnp.concatenate([cos, cos, -sin, sin], axis=1)

    x2 = x.reshape(n_tok, d)
    for l in range(depth):
        wi = w_in[l]
        o_q = 3 * D_MODEL
        o_kv = o_q + Q_LORA
        o_kr = o_kv + KV_LORA
        o_gc = o_kr + QK_ROPE
        w_kr = wi[:, o_kr:o_gc]
        w_mix = wi[:, 0:o_q].astype(BF16)
        w_lat = wi[:, o_q:o_kr].astype(BF16)
        w_gate = wi[:, o_gc:].astype(BF16)
        w_krt = jnp.concatenate([w_kr, _rotate_half_cols(w_kr)], axis=1).T.astype(BF16)
        wq3 = w_q_b[l].reshape(Q_LORA, N_HEADS, QK_HEAD)
        wq = jnp.concatenate([wq3, _rotate_half_cols(wq3[..., QK_NOPE:])], axis=-1)
        wqt = wq.transpose(1, 2, 0).astype(BF16)
        wkvt = w_kv_b[l].reshape(KV_LORA, N_HEADS, QK_NOPE + V_HEAD).transpose(1, 2, 0).astype(BF16)
        gq = jnp.concatenate([q_norm_g[l], _rotate_half_cols(q_norm_g[l][QK_NOPE:])]).reshape(-1, 1)
        gk = jnp.concatenate([k_norm_g[l], _rotate_half_cols(k_norm_g[l][QK_NOPE:])]).reshape(-1, 1)
        wr_t = jnp.concatenate(
            [w_router_expert[l].T, w_router_group[l].T,
             jnp.zeros((ROUTER_ROWS - N_EXPERTS - N_GROUPS, d), F32)], axis=0).astype(BF16)
        br = jnp.concatenate(
            [b_router_expert[l], b_router_group[l],
             jnp.zeros((ROUTER_ROWS - N_EXPERTS - N_GROUPS,), F32)]).reshape(ROUTER_ROWS, 1)

        mod3 = _ada(c, w_ada[l], b_ada[l]).reshape(nb, 1, 6 * d)
        conv_p, gm, qn, kvn, krt = _inproj(
            x2, mod3, norm1_g[l].reshape(1, d), w_mix, w_gate, w_lat, w_krt, conv_w[l],
            q_a_norm_g[l].reshape(1, -1), kv_a_norm_g[l].reshape(1, -1), seq)
        merged = _attn(qn, kvn, krt, cst, wqt, wkvt, gq, gk, conv_p, gm, nb, seq)
        x1, h2r, eid, rank, cnt, cwt = _oproj(merged, x2, mod3, w_o[l].astype(BF16),
                                              norm2_g[l].reshape(1, d), wr_t, br, seq)
        te, na, ztile, pos3, n_tiles = _route_plan(eid, rank, cnt, n_tok)
        xs = _dispatch(ztile, na, pos3, h2r, n_tiles)
        yr = _moe(te, na, xs,
                  w_exp_gate[l].reshape(N_EXPERTS, d, D_EXPERT),
                  w_exp_up[l].reshape(N_EXPERTS, d, D_EXPERT),
                  w_exp_down[l].reshape(N_EXPERTS, D_EXPERT, d))
        x2 = _comb(x1, yr, pos3, cwt, mod3, seq)
    return x2.reshape(nb, seq, d)
```

```python
import functools
import math

import jax
import jax.numpy as jnp
from jax import lax
from jax.experimental import pallas as pl
from jax.experimental.pallas import tpu as pltpu

F32 = jnp.float32
BF16 = jnp.bfloat16

D_MODEL = 1024
N_HEADS = 8
QK_NOPE = 128
QK_ROPE = 64
QK_HEAD = QK_NOPE + QK_ROPE
V_HEAD = 128
Q_LORA = 384
KV_LORA = 256
CHUNK = 64
EPS = 1e-6
ROPE_BASE = 10000.0
N_GROUPS = 4
EXPERTS_PER_GROUP = 8
N_EXPERTS = N_GROUPS * EXPERTS_PER_GROUP
D_EXPERT = 256
CONV_K = 3

LANES = 128
SUBLANES = 8
VMEM_LIMIT = 56 * 1024 * 1024

ROW_TILE = 512
Q_TILE = 512
INPROJ_SUB = 2
HEADS_PER_STEP = 2
MOE_TILE = 512
X_SLOTS = 4
ADA_COLS = 1536
ROWS_PER_TOKEN = D_MODEL // LANES
DMA_UNROLL = 8
ROUTER_ROWS = 40


def _sigmoid(v):
    return 1.0 / (1.0 + jnp.exp(-v))


def _dot(a, b):
    return jnp.dot(a, b, preferred_element_type=F32)


def _store_rows(ref, val, n):
    for j in range(ROWS_PER_TOKEN):
        ref[pl.ds(j, n, stride=ROWS_PER_TOKEN), :] = val[:, j * LANES:(j + 1) * LANES]


def _load_rows(ref, n, offset=0):
    return [ref[pl.ds(offset + j, n, stride=ROWS_PER_TOKEN), :] for j in range(ROWS_PER_TOKEN)]


def _dot_nt(a, b):
    return lax.dot_general(a, b, (((1,), (1,)), ((), ())), preferred_element_type=F32)


def _ada_kernel(c_ref, w_ref, b_ref, o_ref):
    c = c_ref[...]
    act = (c * _sigmoid(c)).astype(BF16)
    o_ref[...] = _dot(act, w_ref[...].astype(BF16)) + b_ref[...]


def _ada(c, w_ada, b_ada):
    nb, d = c.shape
    n = w_ada.shape[1]
    return pl.pallas_call(
        _ada_kernel,
        grid=(n // ADA_COLS,),
        in_specs=[
            pl.BlockSpec((nb, d), lambda j: (0, 0)),
            pl.BlockSpec((d, ADA_COLS), lambda j: (0, j)),
            pl.BlockSpec((1, ADA_COLS), lambda j: (0, j)),
        ],
        out_specs=pl.BlockSpec((nb, ADA_COLS), lambda j: (0, j)),
        out_shape=jax.ShapeDtypeStruct((nb, n), F32),
        compiler_params=pltpu.CompilerParams(
            dimension_semantics=("arbitrary",), vmem_limit_bytes=VMEM_LIMIT),
        name="ada",
    )(c, w_ada, b_ada.reshape(1, n))


def _inproj_kernel(tiles_per_seq, x_ref, mod_ref, g1_ref, wmix_ref, wgate_ref, wlat_ref, wkr_ref,
                   cw_ref, gq_ref, gkv_ref,
                   conv_ref, gm_ref, qn_ref, kvn_ref, krt_ref, ubuf):
    tm = x_ref.shape[0] // INPROJ_SUB

    @pl.when(pl.program_id(0) % tiles_per_seq == 0)
    def _():
        ubuf[0:SUBLANES, :] = jnp.zeros((SUBLANES, D_MODEL), F32)

    shift = mod_ref[0, :, 0:D_MODEL]
    scale = mod_ref[0, :, D_MODEL:2 * D_MODEL]
    for sub in range(INPROJ_SUB):
        rows = slice(sub * tm, (sub + 1) * tm)
        x = x_ref[rows, :]
        xn = x * lax.rsqrt(jnp.mean(x * x, axis=-1, keepdims=True) + EPS) * g1_ref[...]
        h = (xn * (1.0 + scale) + shift).astype(BF16)

        def proj(w_ref, lo, width):
            return _dot(h, w_ref[:, lo:lo + width])

        u = proj(wmix_ref, 2 * D_MODEL, D_MODEL) * proj(wmix_ref, 0, D_MODEL)
        ubuf[SUBLANES:SUBLANES + tm, :] = u
        conv = (ubuf[SUBLANES - 2:SUBLANES - 2 + tm, :] * cw_ref[0:1, :]
                + ubuf[SUBLANES - 1:SUBLANES - 1 + tm, :] * cw_ref[1:2, :]
                + u * cw_ref[2:3, :])
        ubuf[0:SUBLANES, :] = ubuf[tm:tm + SUBLANES, :]
        y_conv = proj(wmix_ref, D_MODEL, D_MODEL) * conv
        conv_ref[rows, :] = (_sigmoid(proj(wgate_ref, 0, D_MODEL)) * y_conv).astype(BF16)
        gm_ref[rows, :] = _sigmoid(proj(wgate_ref, D_MODEL, D_MODEL)).astype(BF16)

        ql = proj(wlat_ref, 0, Q_LORA)
        qn_ref[rows, :] = (ql * lax.rsqrt(jnp.mean(ql * ql, axis=-1, keepdims=True) + EPS)
                           * gq_ref[...]).astype(BF16)
        kl = proj(wlat_ref, Q_LORA, KV_LORA)
        kvn_ref[rows, :] = (kl * lax.rsqrt(jnp.mean(kl * kl, axis=-1, keepdims=True) + EPS)
                            * gkv_ref[...]).astype(BF16)
        krt_ref[0, :, rows] = _dot_nt(wkr_ref[...], h)


def _inproj(x2, mod3, g1, w_mix, w_gate, w_lat, w_krt, conv_w, gq, gkv, seq):
    t = x2.shape[0]
    nb = t // seq
    tm = ROW_TILE * INPROJ_SUB
    tiles_per_seq = seq // tm
    row = lambda i: (i, 0)
    const = lambda i: (0, 0)
    return pl.pallas_call(
        functools.partial(_inproj_kernel, tiles_per_seq),
        grid=(t // tm,),
        in_specs=[
            pl.BlockSpec((tm, D_MODEL), row),
            pl.BlockSpec((1, 1, 6 * D_MODEL), lambda i: (i // tiles_per_seq, 0, 0)),
            pl.BlockSpec((1, D_MODEL), const),
            pl.BlockSpec((D_MODEL, 3 * D_MODEL), const),
            pl.BlockSpec((D_MODEL, 2 * D_MODEL), const),
            pl.BlockSpec((D_MODEL, Q_LORA + KV_LORA), const),
            pl.BlockSpec((2 * QK_ROPE, D_MODEL), const),
            pl.BlockSpec((CONV_K, D_MODEL), const),
            pl.BlockSpec((1, Q_LORA), const),
            pl.BlockSpec((1, KV_LORA), const),
        ],
        out_specs=[
            pl.BlockSpec((tm, D_MODEL), row),
            pl.BlockSpec((tm, D_MODEL), row),
            pl.BlockSpec((tm, Q_LORA), row),
            pl.BlockSpec((tm, KV_LORA), row),
            pl.BlockSpec((1, 2 * QK_ROPE, tm),
                         lambda i: (i // tiles_per_seq, 0, i % tiles_per_seq)),
        ],
        out_shape=[
            jax.ShapeDtypeStruct((t, D_MODEL), BF16),
            jax.ShapeDtypeStruct((t, D_MODEL), BF16),
            jax.ShapeDtypeStruct((t, Q_LORA), BF16),
            jax.ShapeDtypeStruct((t, KV_LORA), BF16),
            jax.ShapeDtypeStruct((nb, 2 * QK_ROPE, seq), F32),
        ],
        scratch_shapes=[pltpu.VMEM((ROW_TILE + SUBLANES, D_MODEL), F32)],
        compiler_params=pltpu.CompilerParams(
            dimension_semantics=("arbitrary",), vmem_limit_bytes=VMEM_LIMIT),
        name="inproj",
    )(x2, mod3, g1, w_mix, w_gate, w_lat, w_krt, conv_w, gq, gkv)


def _attn_kernel(qn_ref, kvn_ref, krt_ref, cst_ref, wqt_ref, wkvt_ref, gq_ref, gk_ref, conv_ref, gm_ref,
                 o_ref, *scratch):
    per_head = len(scratch) // HEADS_PER_STEP
    heads = [_attn_head(hh, qn_ref, kvn_ref, krt_ref, cst_ref, wqt_ref, wkvt_ref, gq_ref, gk_ref,
                        conv_ref, gm_ref, o_ref, *scratch[hh * per_head:(hh + 1) * per_head])
             for hh in range(HEADS_PER_STEP)]
    nq = qn_ref.shape[0] // Q_TILE
    for scores, _ in heads:
        scores(0)
    for i in range(nq):
        for scores, finish in heads:
            if i + 1 < nq:
                scores(i + 1)
            finish(i)


def _attn_head(hh, qn_ref, kvn_ref, krt_ref, cst_ref, wqt_ref, wkvt_ref, gq_ref, gk_ref,
               conv_ref, gm_ref, o_ref, qt_s, k_s, vt_s, s_buf0, s_buf1):
    seq = qn_ref.shape[0]
    cos_t = cst_ref[0, 0:QK_ROPE, :]
    sin_t = cst_ref[0, QK_ROPE:, :]

    def normed_rope(nope, r, rr, g, extra_scale):
        ss = jnp.sum(nope * nope, axis=0, keepdims=True) + jnp.sum(r * r, axis=0, keepdims=True)
        scale = lax.rsqrt(ss * (1.0 / QK_HEAD) + EPS) * extra_scale
        rope = r * g[QK_NOPE:QK_HEAD] * cos_t + rr * g[QK_HEAD:] * sin_t
        return (nope * g[0:QK_NOPE] * scale).astype(BF16), (rope * scale).astype(BF16)

    qt = _dot_nt(wqt_ref[hh], qn_ref[...])
    q_n, q_r = normed_rope(qt[0:QK_NOPE], qt[QK_NOPE:QK_HEAD], qt[QK_HEAD:], gq_ref[...],
                           QK_HEAD ** -0.5 * math.log2(math.e))
    qt_s[0:QK_NOPE, :] = q_n
    qt_s[QK_NOPE:QK_HEAD, :] = q_r
    qt_s[QK_HEAD:, :] = jnp.zeros((QK_ROPE, seq), BF16)

    kvt = _dot_nt(wkvt_ref[hh], kvn_ref[...])
    krt = krt_ref[0]
    k_n, k_r = normed_rope(kvt[0:QK_NOPE], krt[0:QK_ROPE], krt[QK_ROPE:], gk_ref[...], 1.0)
    kt = jnp.concatenate([k_n, k_r, jnp.zeros((QK_ROPE, seq), BF16)], axis=0)
    k_s[...] = kt.T
    vt_s[0:V_HEAD, :] = kvt[QK_NOPE:].astype(BF16)
    vt_s[V_HEAD:, :] = jnp.ones((vt_s.shape[0] - V_HEAD, seq), BF16)

    tq = Q_TILE
    hq = tq // 2
    kchunk = lax.broadcasted_iota(jnp.int32, (hq, tq), 0) // CHUNK
    qchunk = lax.broadcasted_iota(jnp.int32, (hq, tq), 1) // CHUNK
    head_ok = kchunk <= qchunk
    tail_ok = head_ok[:, 0:hq]
    neg = jnp.finfo(F32).min

    def scores(i):
        q0 = i * tq
        sb = s_buf0 if i % 2 == 0 else s_buf1
        q = qt_s[:, q0:q0 + tq]
        if i > 0:
            sb[0:q0, :] = _dot(k_s[0:q0, :], q)
        sb[q0:q0 + hq, :] = jnp.where(head_ok, _dot(k_s[q0:q0 + hq, :], q), neg)
        sb[q0 + hq:q0 + tq, hq:] = jnp.where(tail_ok, _dot(k_s[q0 + hq:q0 + tq, :], q[:, hq:]), neg)

    def finish(i):
        q0 = i * tq
        kmain = q0 + hq
        sb = s_buf0 if i % 2 == 0 else s_buf1
        tail = sb[kmain:kmain + hq, hq:]
        m_main = jnp.max(sb[0:kmain, :], axis=0, keepdims=True)
        m_tail = jnp.maximum(m_main[:, hq:], jnp.max(tail, axis=0, keepdims=True))
        m = jnp.concatenate([m_main[:, 0:hq], m_tail], axis=1)
        acc = _dot(vt_s[:, 0:kmain], jnp.exp2(sb[0:kmain, :] - m).astype(BF16))
        acc_tail = _dot(vt_s[:, kmain:kmain + hq], jnp.exp2(tail - m_tail).astype(BF16))
        acc = jnp.concatenate([acc[:, 0:hq], acc[:, hq:] + acc_tail], axis=1)
        o_t = acc[0:V_HEAD] / acc[V_HEAD:V_HEAD + 1]
        rows, cols = slice(q0, q0 + tq), slice(hh * V_HEAD, (hh + 1) * V_HEAD)
        y_mla = o_t.T.astype(BF16).astype(F32)
        merged = conv_ref[rows, cols].astype(F32) + gm_ref[rows, cols].astype(F32) * y_mla
        o_ref[rows, cols] = merged.astype(o_ref.dtype)

    return scores, finish


def _attn(qn, kvn, krt, cst, wqt, wkvt, gq, gk, conv_p, gm, nb, seq):
    t = qn.shape[0]
    per_b = lambda b, h: (b, 0)
    per_b3 = lambda b, h: (b, 0, 0)
    per_h = lambda b, h: (h, 0, 0)
    const = lambda b, h: (0, 0)
    qk_rows = QK_NOPE + 2 * QK_ROPE
    return pl.pallas_call(
        _attn_kernel,
        grid=(nb, N_HEADS // HEADS_PER_STEP),
        in_specs=[
            pl.BlockSpec((seq, Q_LORA), per_b),
            pl.BlockSpec((seq, KV_LORA), per_b),
            pl.BlockSpec((1, 2 * QK_ROPE, seq), per_b3),
            pl.BlockSpec((1, 2 * QK_ROPE, seq), per_b3),
            pl.BlockSpec((HEADS_PER_STEP, qk_rows, Q_LORA), per_h),
            pl.BlockSpec((HEADS_PER_STEP, QK_NOPE + V_HEAD, KV_LORA), per_h),
            pl.BlockSpec((qk_rows, 1), const),
            pl.BlockSpec((qk_rows, 1), const),
            pl.BlockSpec((seq, HEADS_PER_STEP * V_HEAD), lambda b, h: (b, h)),
            pl.BlockSpec((seq, HEADS_PER_STEP * V_HEAD), lambda b, h: (b, h)),
        ],
        out_specs=pl.BlockSpec((seq, HEADS_PER_STEP * V_HEAD), lambda b, h: (b, h)),
        out_shape=jax.ShapeDtypeStruct((t, N_HEADS * V_HEAD), BF16),
        scratch_shapes=[
            pltpu.VMEM((qk_rows, seq), BF16),
            pltpu.VMEM((seq, qk_rows), BF16),
            pltpu.VMEM((V_HEAD + 2 * SUBLANES, seq), BF16),
            pltpu.VMEM((seq, Q_TILE), F32),
            pltpu.VMEM((seq, Q_TILE), F32),
        ] * HEADS_PER_STEP,
        compiler_params=pltpu.CompilerParams(
            dimension_semantics=("arbitrary", "arbitrary"), vmem_limit_bytes=VMEM_LIMIT),
        name="attn",
    )(qn, kvn, krt, cst, wqt, wkvt, gq, gk, conv_p, gm)


def _oproj_kernel(merged_ref, x_ref, mod_ref, wo_ref, g2_ref, wr_ref, br_ref,
                  x1_ref, h2_ref, eid_ref, rank_ref, cnt_ref, cwt_ref, base):
    tm = x_ref.shape[0]

    @pl.when(pl.program_id(0) == 0)
    def _():
        base[...] = jnp.zeros(base.shape, F32)

    att = _dot(merged_ref[...], wo_ref[...])
    gate1 = mod_ref[0, :, 2 * D_MODEL:3 * D_MODEL]
    shift2 = mod_ref[0, :, 3 * D_MODEL:4 * D_MODEL]
    scale2 = mod_ref[0, :, 4 * D_MODEL:5 * D_MODEL]
    x1 = x_ref[...] + gate1 * att
    x1_ref[...] = x1
    xn = x1 * lax.rsqrt(jnp.mean(x1 * x1, axis=-1, keepdims=True) + EPS) * g2_ref[...]
    h2 = xn * (1.0 + scale2) + shift2
    _store_rows(h2_ref, h2, tm)

    lt = _dot_nt(wr_ref[...], h2.astype(BF16)) + br_ref[...]
    gl = [lt[N_EXPERTS + r:N_EXPERTS + r + 1, :] for r in range(N_GROUPS)]
    gmax = jnp.maximum(jnp.maximum(gl[0], gl[1]), jnp.maximum(gl[2], gl[3]))
    gidx = jnp.full(gmax.shape, N_GROUPS - 1, jnp.int32)
    for r in range(N_GROUPS - 2, -1, -1):
        gidx = jnp.where(gl[r] == gmax, r, gidx)
    gsum = jnp.exp(gl[0] - gmax)
    for r in range(1, N_GROUPS):
        gsum = gsum + jnp.exp(gl[r] - gmax)
    p_group = 1.0 / gsum
    es = lt[(N_GROUPS - 1) * EXPERTS_PER_GROUP:N_GROUPS * EXPERTS_PER_GROUP, :]
    for r in range(N_GROUPS - 2, -1, -1):
        es = jnp.where(gidx == r, lt[r * EXPERTS_PER_GROUP:(r + 1) * EXPERTS_PER_GROUP, :], es)
    row = lax.broadcasted_iota(jnp.int32, es.shape, 0)
    m1 = jnp.max(es, axis=0, keepdims=True)
    i1 = jnp.min(jnp.where(es == m1, row, EXPERTS_PER_GROUP), axis=0, keepdims=True)
    es2 = jnp.where(row == i1, -jnp.inf, es)
    m2 = jnp.max(es2, axis=0, keepdims=True)
    i2 = jnp.min(jnp.where(es2 == m2, row, EXPERTS_PER_GROUP), axis=0, keepdims=True)
    e2 = jnp.exp(m2 - m1)
    w1 = p_group / (1.0 + e2)
    w2 = w1 * e2
    eid0 = gidx * EXPERTS_PER_GROUP + i1
    eid1 = gidx * EXPERTS_PER_GROUP + i2
    eid_ref[0:1, :] = eid0
    eid_ref[1:2, :] = eid1
    erow = lax.broadcasted_iota(jnp.int32, (N_EXPERTS, tm), 0)
    oh0 = erow == eid0
    oh1 = erow == eid1
    both = jnp.where(oh0, 1.0, jnp.where(oh1, 1.0, 0.0))
    earlier = (lax.broadcasted_iota(jnp.int32, (tm, tm), 0)
               < lax.broadcasted_iota(jnp.int32, (tm, tm), 1))
    seen = base[...] + _dot(both.astype(BF16), jnp.where(earlier, 1.0, 0.0).astype(BF16))
    rank_ref[0:1, :] = jnp.sum(jnp.where(oh0, seen, 0.0), axis=0, keepdims=True).astype(jnp.int32)
    rank_ref[1:2, :] = jnp.sum(jnp.where(oh1, seen, 0.0), axis=0, keepdims=True).astype(jnp.int32)
    base[...] = base[...] + jnp.sum(both, axis=1, keepdims=True)
    cnt_ref[...] = base[...]
    wrow = lax.broadcasted_iota(jnp.int32, (LANES, tm), 0)
    wmat = jnp.where(wrow == 0, w1, jnp.where(wrow == 1, w2, 0.0))
    cwt_ref[...] = wmat.T


def _oproj(merged, x2, mod3, wo, g2, wr_t, br, seq):
    t = x2.shape[0]
    tm = ROW_TILE
    tiles_per_seq = seq // tm
    row = lambda i: (i, 0)
    const = lambda i: (0, 0)
    return pl.pallas_call(
        _oproj_kernel,
        grid=(t // tm,),
        in_specs=[
            pl.BlockSpec((tm, D_MODEL), row),
            pl.BlockSpec((tm, D_MODEL), row),
            pl.BlockSpec((1, 1, 6 * D_MODEL), lambda i: (i // tiles_per_seq, 0, 0)),
            pl.BlockSpec((D_MODEL, D_MODEL), const),
            pl.BlockSpec((1, D_MODEL), const),
            pl.BlockSpec((ROUTER_ROWS, D_MODEL), const),
            pl.BlockSpec((ROUTER_ROWS, 1), const),
        ],
        out_specs=[
            pl.BlockSpec((tm, D_MODEL), row),
            pl.BlockSpec((tm * ROWS_PER_TOKEN, LANES), row),
            pl.BlockSpec((2, tm), lambda i: (0, i)),
            pl.BlockSpec((2, tm), lambda i: (0, i)),
            pl.BlockSpec((N_EXPERTS, 1), const),
            pl.BlockSpec((tm, LANES), row),
        ],
        out_shape=[
            jax.ShapeDtypeStruct((t, D_MODEL), F32),
            jax.ShapeDtypeStruct((t * ROWS_PER_TOKEN, LANES), F32),
            jax.ShapeDtypeStruct((2, t), jnp.int32),
            jax.ShapeDtypeStruct((2, t), jnp.int32),
            jax.ShapeDtypeStruct((N_EXPERTS, 1), F32),
            jax.ShapeDtypeStruct((t, LANES), F32),
        ],
        scratch_shapes=[pltpu.VMEM((N_EXPERTS, 1), F32)],
        compiler_params=pltpu.CompilerParams(
            dimension_semantics=("arbitrary",), vmem_limit_bytes=VMEM_LIMIT),
        name="oproj",
    )(merged, x2, mod3, wo, g2, wr_t, br)


def _dispatch_kernel(n_tiles, ztile_ref, na_ref, pos_ref, h2_ref, xs_hbm, zbuf, sem, zsem):
    tm = h2_ref.shape[0] // ROWS_PER_TOKEN
    tile_rows = MOE_TILE * ROWS_PER_TOKEN

    def zero_tile(tile, on_sem):
        row = pl.multiple_of(tile * tile_rows, tile_rows)
        return pltpu.make_async_copy(zbuf, xs_hbm.at[pl.ds(row, tile_rows), :], on_sem)

    def unused_tiles(action):
        def body(k, carry):
            action(zero_tile(na_ref[0] + k, zsem))
            return carry
        lax.fori_loop(0, n_tiles - na_ref[0], body, 0)

    @pl.when(pl.program_id(0) == 0)
    def _():
        zbuf[...] = jnp.zeros(zbuf.shape, zbuf.dtype)
        for e in range(N_EXPERTS):
            zero_tile(ztile_ref[e], sem).start()
        for e in range(N_EXPERTS):
            zero_tile(ztile_ref[e], sem).wait()
        unused_tiles(lambda cp: cp.start())

    def body(c, carry):
        for k in range(DMA_UNROLL):
            t = c * DMA_UNROLL + k
            src = h2_ref.at[pl.ds(pl.multiple_of(t * ROWS_PER_TOKEN, ROWS_PER_TOKEN), ROWS_PER_TOKEN), :]
            for s in range(2):
                dst_row = pl.multiple_of(pos_ref[0, 0, s * tm + t] * ROWS_PER_TOKEN, ROWS_PER_TOKEN)
                pltpu.make_async_copy(src, xs_hbm.at[pl.ds(dst_row, ROWS_PER_TOKEN), :],
                                      sem).start(priority=s)
        return carry
    lax.fori_loop(0, tm // DMA_UNROLL, body, 0)
    for _ in range(2):
        pltpu.make_async_copy(h2_ref, xs_hbm.at[pl.ds(0, tm * ROWS_PER_TOKEN), :], sem).wait()

    @pl.when(pl.program_id(0) == pl.num_programs(0) - 1)
    def _():
        unused_tiles(lambda cp: cp.wait())


def _dispatch(ztile, na, pos3, h2r, n_tiles):
    tm = pos3.shape[2] // 2
    n_steps = pos3.shape[0]
    grid_spec = pltpu.PrefetchScalarGridSpec(
        num_scalar_prefetch=2,
        grid=(n_steps,),
        in_specs=[
            pl.BlockSpec((1, 1, 2 * tm), lambda i, z, n: (i, 0, 0), memory_space=pltpu.SMEM),
            pl.BlockSpec((tm * ROWS_PER_TOKEN, LANES), lambda i, z, n: (i, 0)),
        ],
        out_specs=pl.BlockSpec(memory_space=pl.ANY),
        scratch_shapes=[
            pltpu.VMEM((MOE_TILE * ROWS_PER_TOKEN, LANES), F32),
            pltpu.SemaphoreType.DMA(()),
            pltpu.SemaphoreType.DMA(()),
        ],
    )
    return pl.pallas_call(
        functools.partial(_dispatch_kernel, n_tiles),
        grid_spec=grid_spec,
        out_shape=jax.ShapeDtypeStruct((n_tiles * MOE_TILE * ROWS_PER_TOKEN, LANES), F32),
        compiler_params=pltpu.CompilerParams(
            dimension_semantics=("arbitrary",), vmem_limit_bytes=VMEM_LIMIT),
        name="dispatch",
    )(ztile, na, pos3, h2r)


def _moe_kernel(te_ref, na_ref, x_hbm, wg_ref, wu_ref, wd_ref, y_ref, xbuf, wgu_s, wd_s, sem):
    i = pl.program_id(0)
    n_active = na_ref[0]
    active = i < n_active
    tile_rows = MOE_TILE * ROWS_PER_TOKEN

    def fetch(tile):
        slot = tile % X_SLOTS
        row = pl.multiple_of(tile * tile_rows, tile_rows)
        return pltpu.make_async_copy(x_hbm.at[pl.ds(row, tile_rows), :], xbuf.at[slot], sem.at[slot])

    @pl.when(i == 0)
    def _():
        for k in range(X_SLOTS - 1):
            @pl.when(k < n_active)
            def _():
                fetch(k).start()

    @pl.when(i + X_SLOTS - 1 < n_active)
    def _():
        fetch(i + X_SLOTS - 1).start()

    new_expert = jnp.logical_or(i == 0, te_ref[i] != te_ref[jnp.maximum(i - 1, 0)])

    @pl.when(jnp.logical_and(active, new_expert))
    def _():
        wgu_s[:, 0:D_EXPERT] = wg_ref[0].astype(BF16)
        wgu_s[:, D_EXPERT:] = wu_ref[0].astype(BF16)
        wd_s[...] = wd_ref[0].astype(BF16)

    @pl.when(active)
    def _():
        fetch(i).wait()
        x = jnp.concatenate([col.astype(BF16) for col in _load_rows(xbuf.at[i % X_SLOTS], MOE_TILE)], axis=1)
        gu = _dot(x, wgu_s[...])
        g = gu[:, 0:D_EXPERT]
        a = (g * _sigmoid(g)) * gu[:, D_EXPERT:]
        y = _dot(a.astype(BF16), wd_s[...])
        _store_rows(y_ref, y, MOE_TILE)

    @pl.when(jnp.logical_not(active))
    def _():
        y_ref[...] = jnp.zeros(y_ref.shape, y_ref.dtype)


def _moe(te, na, xs, wg, wu, wd):
    nt = te.shape[0]
    tile_rows = MOE_TILE * ROWS_PER_TOKEN
    wspec = lambda shape: pl.BlockSpec((1,) + shape, lambda i, te_r, na_r: (te_r[i], 0, 0))
    grid_spec = pltpu.PrefetchScalarGridSpec(
        num_scalar_prefetch=2,
        grid=(nt,),
        in_specs=[
            pl.BlockSpec(memory_space=pl.ANY),
            wspec((D_MODEL, D_EXPERT)),
            wspec((D_MODEL, D_EXPERT)),
            wspec((D_EXPERT, D_MODEL)),
        ],
        out_specs=pl.BlockSpec((tile_rows, LANES), lambda i, te_r, na_r: (i, 0)),
        scratch_shapes=[
            pltpu.VMEM((X_SLOTS, tile_rows, LANES), F32),
            pltpu.VMEM((D_MODEL, 2 * D_EXPERT), BF16),
            pltpu.VMEM((D_EXPERT, D_MODEL), BF16),
            pltpu.SemaphoreType.DMA((X_SLOTS,)),
        ],
    )
    return pl.pallas_call(
        _moe_kernel,
        grid_spec=grid_spec,
        out_shape=jax.ShapeDtypeStruct(xs.shape, xs.dtype),
        compiler_params=pltpu.CompilerParams(
            dimension_semantics=("arbitrary",), vmem_limit_bytes=VMEM_LIMIT),
        name="moe",
    )(te, na, xs, wg, wu, wd)


def _comb_kernel(pos0_ref, posn_ref, x1_ref, cwt_ref, mod_ref, y_hbm, o_ref, ybuf, sem):
    tm = x1_ref.shape[0]
    i = pl.program_id(0)
    slot = i % 2

    def gather(pos_ref, dst_slot):
        def body(c, carry):
            for k in range(DMA_UNROLL):
                r = c * DMA_UNROLL + k
                src_row = pl.multiple_of(pos_ref[0, 0, r] * ROWS_PER_TOKEN, ROWS_PER_TOKEN)
                dst_row = pl.multiple_of(r * ROWS_PER_TOKEN, ROWS_PER_TOKEN)
                pltpu.make_async_copy(y_hbm.at[pl.ds(src_row, ROWS_PER_TOKEN), :],
                                      ybuf.at[dst_slot, pl.ds(dst_row, ROWS_PER_TOKEN), :],
                                      sem.at[dst_slot]).start(priority=k % 2)
            return carry
        lax.fori_loop(0, 2 * tm // DMA_UNROLL, body, 0)

    @pl.when(i == 0)
    def _():
        gather(pos0_ref, 0)

    @pl.when(i + 1 < pl.num_programs(0))
    def _():
        gather(posn_ref, 1 - slot)

    pltpu.make_async_copy(y_hbm.at[pl.ds(0, 2 * tm * ROWS_PER_TOKEN), :], ybuf.at[slot],
                          sem.at[slot]).wait()
    c0 = cwt_ref[:, 0:1]
    c1 = cwt_ref[:, 1:2]
    y0 = _load_rows(ybuf.at[slot], tm)
    y1 = _load_rows(ybuf.at[slot], tm, offset=tm * ROWS_PER_TOKEN)
    for j in range(ROWS_PER_TOKEN):
        cols = slice(j * LANES, (j + 1) * LANES)
        gate2 = mod_ref[0, :, 5 * D_MODEL + j * LANES:5 * D_MODEL + (j + 1) * LANES]
        o_ref[:, cols] = x1_ref[:, cols] + gate2 * (c0 * y0[j] + c1 * y1[j])


def _comb(x1, yr, pos3, cwt, mod3, seq):
    t = x1.shape[0]
    tm = ROW_TILE
    tiles_per_seq = seq // tm
    n_steps = t // tm
    row = lambda i: (i, 0)
    smem_blk = lambda f: pl.BlockSpec((1, 1, 2 * tm), f, memory_space=pltpu.SMEM)
    return pl.pallas_call(
        _comb_kernel,
        grid=(n_steps,),
        in_specs=[
            smem_blk(lambda i: (0, 0, 0)),
            smem_blk(lambda i: (jnp.minimum(i + 1, n_steps - 1), 0, 0)),
            pl.BlockSpec((tm, D_MODEL), row),
            pl.BlockSpec((tm, LANES), row),
            pl.BlockSpec((1, 1, 6 * D_MODEL), lambda i: (i // tiles_per_seq, 0, 0)),
            pl.BlockSpec(memory_space=pl.ANY),
        ],
        out_specs=pl.BlockSpec((tm, D_MODEL), row),
        out_shape=jax.ShapeDtypeStruct((t, D_MODEL), F32),
        scratch_shapes=[
            pltpu.VMEM((2, 2 * tm * ROWS_PER_TOKEN, LANES), F32),
            pltpu.SemaphoreType.DMA((2,)),
        ],
        compiler_params=pltpu.CompilerParams(
            dimension_semantics=("arbitrary",), vmem_limit_bytes=VMEM_LIMIT),
        name="comb",
    )(pos3, pos3, x1, cwt, mod3, yr)


def _route_plan(eid, rank, cnt, n_tok):
    n_tiles = (2 * n_tok) // MOE_TILE + N_EXPERTS
    experts = jnp.arange(N_EXPERTS, dtype=jnp.int32)
    counts = cnt.reshape(N_EXPERTS).astype(jnp.int32)
    ntile = (counts + MOE_TILE - 1) // MOE_TILE
    tend = jnp.cumsum(ntile)
    tstart = tend - ntile
    n_active = tend[-1]
    tj = jnp.arange(n_tiles, dtype=jnp.int32)
    te_raw = jnp.minimum(jnp.sum((tj[:, None] >= tend[None, :]).astype(jnp.int32), axis=1),
                         N_EXPERTS - 1)
    te_last = jnp.sum(jnp.where(tj == n_active - 1, te_raw, 0))
    te = jnp.where(tj < n_active, te_raw, te_last).astype(jnp.int32)
    first_row = jnp.sum(jnp.where(eid[:, :, None] == experts[None, None, :],
                                  (tstart * MOE_TILE)[None, None, :], 0), axis=-1)
    pos = (first_row + rank).astype(jnp.int32)
    ztile = jnp.maximum(tend - 1, 0).astype(jnp.int32)
    pos3 = pos.reshape(2, n_tok // ROW_TILE, ROW_TILE).transpose(1, 0, 2).reshape(
        n_tok // ROW_TILE, 1, 2 * ROW_TILE)
    return te, n_active.reshape(1).astype(jnp.int32), ztile, pos3, n_tiles


def _rotate_half_cols(w):
    half = QK_ROPE // 2
    return jnp.concatenate([w[..., half:], w[..., :half]], axis=-1)


def kernel(x, c, positions, w_ada, b_ada, norm1_g, w_in, conv_w, q_a_norm_g, w_q_b, kv_a_norm_g, w_kv_b, q_norm_g, k_norm_g, w_o, norm2_g, w_router_group, b_router_group, w_router_expert, b_router_expert, w_exp_gate, w_exp_up, w_exp_down):
    nb, seq, d = x.shape
    depth = w_ada.shape[0]
    n_tok = nb * seq
    assert d == D_MODEL and seq % ROW_TILE == 0 and seq % Q_TILE == 0 and Q_TILE % CHUNK == 0
    assert (2 * n_tok) % MOE_TILE == 0

    inv = ROPE_BASE ** (-jnp.arange(0, QK_ROPE, 2, dtype=F32) / QK_ROPE)
    ang = inv[None, :, None] * positions.astype(F32)[:, None, :]
    cos, sin = jnp.cos(ang), jnp.sin(ang)
    cst = jnp.concatenate([cos, cos, -sin, sin], axis=1)

    x2 = x.reshape(n_tok, d)
    for l in range(depth):
        wi = w_in[l]
        o_q = 3 * D_MODEL
        o_kv = o_q + Q_LORA
        o_kr = o_kv + KV_LORA
        o_gc = o_kr + QK_ROPE
        w_kr = wi[:, o_kr:o_gc]
        w_mix = wi[:, 0:o_q].astype(BF16)
        w_lat = wi[:, o_q:o_kr].astype(BF16)
        w_gate = wi[:, o_gc:].astype(BF16)
        w_krt = jnp.concatenate([w_kr, _rotate_half_cols(w_kr)], axis=1).T.astype(BF16)
        wq3 = w_q_b[l].reshape(Q_LORA, N_HEADS, QK_HEAD)
        wq = jnp.concatenate([wq3, _rotate_half_cols(wq3[..., QK_NOPE:])], axis=-1)
        wqt = wq.transpose(1, 2, 0).astype(BF16)
        wkvt = w_kv_b[l].reshape(KV_LORA, N_HEADS, QK_NOPE + V_HEAD).transpose(1, 2, 0).astype(BF16)
        gq = jnp.concatenate([q_norm_g[l], _rotate_half_cols(q_norm_g[l][QK_NOPE:])]).reshape(-1, 1)
        gk = jnp.concatenate([k_norm_g[l], _rotate_half_cols(k_norm_g[l][QK_NOPE:])]).reshape(-1, 1)
        wr_t = jnp.concatenate(
            [w_router_expert[l].T, w_router_group[l].T,
             jnp.zeros((ROUTER_ROWS - N_EXPERTS - N_GROUPS, d), F32)], axis=0).astype(BF16)
        br = jnp.concatenate(
            [b_router_expert[l], b_router_group[l],
             jnp.zeros((ROUTER_ROWS - N_EXPERTS - N_GROUPS,), F32)]).reshape(ROUTER_ROWS, 1)

        mod3 = _ada(c, w_ada[l], b_ada[l]).reshape(nb, 1, 6 * d)
        conv_p, gm, qn, kvn, krt = _inproj(
            x2, mod3, norm1_g[l].reshape(1, d), w_mix, w_gate, w_lat, w_krt, conv_w[l],
            q_a_norm_g[l].reshape(1, -1), kv_a_norm_g[l].reshape(1, -1), seq)
        merged = _attn(qn, kvn, krt, cst, wqt, wkvt, gq, gk, conv_p, gm, nb, seq)
        x1, h2r, eid, rank, cnt, cwt = _oproj(merged, x2, mod3, w_o[l].astype(BF16),
                                              norm2_g[l].reshape(1, d), wr_t, br, seq)
        te, na, ztile, pos3, n_tiles = _route_plan(eid, rank, cnt, n_tok)
        xs = _dispatch(ztile, na, pos3, h2r, n_tiles)
        yr = _moe(te, na, xs,
                  w_exp_gate[l].reshape(N_EXPERTS, d, D_EXPERT),
                  w_exp_up[l].reshape(N_EXPERTS, d, D_EXPERT),
                  w_exp_down[l].reshape(N_EXPERTS, D_EXPERT, d))
        x2 = _comb(x1, yr, pos3, cwt, mod3, seq)
    return x2.reshape(nb, seq, d)
```

```python
import functools
import math

import jax
import jax.numpy as jnp
from jax import lax
from jax.experimental import pallas as pl
from jax.experimental.pallas import tpu as pltpu

F32 = jnp.float32
BF16 = jnp.bfloat16

D_MODEL = 1024
N_HEADS = 8
QK_NOPE = 128
QK_ROPE = 64
QK_HEAD = QK_NOPE + QK_ROPE
V_HEAD = 128
Q_LORA = 384
KV_LORA = 256
CHUNK = 64
EPS = 1e-6
ROPE_BASE = 10000.0
N_GROUPS = 4
EXPERTS_PER_GROUP = 8
N_EXPERTS = N_GROUPS * EXPERTS_PER_GROUP
D_EXPERT = 256
CONV_K = 3

LANES = 128
SUBLANES = 8
VMEM_LIMIT = 56 * 1024 * 1024

ROW_TILE = 512
Q_TILE = 512
INPROJ_SUB = 2
HEADS_PER_STEP = 2
MOE_TILE = 512
X_SLOTS = 4
ADA_COLS = 1536
ROWS_PER_TOKEN = D_MODEL // LANES
DMA_UNROLL = 8
ROUTER_ROWS = 40


def _sigmoid(v):
    return 1.0 / (1.0 + jnp.exp(-v))


def _dot(a, b):
    return jnp.dot(a, b, preferred_element_type=F32)


def _store_rows(ref, val, n):
    for j in range(ROWS_PER_TOKEN):
        ref[pl.ds(j, n, stride=ROWS_PER_TOKEN), :] = val[:, j * LANES:(j + 1) * LANES]


def _load_rows(ref, n, offset=0):
    return [ref[pl.ds(offset + j, n, stride=ROWS_PER_TOKEN), :] for j in range(ROWS_PER_TOKEN)]


def _dot_nt(a, b):
    return lax.dot_general(a, b, (((1,), (1,)), ((), ())), preferred_element_type=F32)


def _ada_kernel(c_ref, w_ref, b_ref, o_ref):
    c = c_ref[...]
    act = (c * _sigmoid(c)).astype(BF16)
    o_ref[...] = _dot(act, w_ref[...].astype(BF16)) + b_ref[...]


def _ada(c, w_ada, b_ada):
    nb, d = c.shape
    n = w_ada.shape[1]
    return pl.pallas_call(
        _ada_kernel,
        grid=(n // ADA_COLS,),
        in_specs=[
            pl.BlockSpec((nb, d), lambda j: (0, 0)),
            pl.BlockSpec((d, ADA_COLS), lambda j: (0, j)),
            pl.BlockSpec((1, ADA_COLS), lambda j: (0, j)),
        ],
        out_specs=pl.BlockSpec((nb, ADA_COLS), lambda j: (0, j)),
        out_shape=jax.ShapeDtypeStruct((nb, n), F32),
        compiler_params=pltpu.CompilerParams(
            dimension_semantics=("arbitrary",), vmem_limit_bytes=VMEM_LIMIT),
        name="ada",
    )(c, w_ada, b_ada.reshape(1, n))


O_LAT = 3 * D_MODEL
O_KR = O_LAT + Q_LORA + KV_LORA
O_GATE = O_KR + QK_ROPE


def _inproj_kernel(tiles_per_seq, x_ref, mod_ref, g1_ref, w_ref, cw_ref, gq_ref, gkv_ref,
                   conv_ref, gm_ref, qn_ref, kvn_ref, krt_ref, ubuf):
    tm = x_ref.shape[0] // INPROJ_SUB

    @pl.when(pl.program_id(0) % tiles_per_seq == 0)
    def _():
        ubuf[0:SUBLANES, :] = jnp.zeros((SUBLANES, D_MODEL), F32)

    shift = mod_ref[0, :, 0:D_MODEL]
    scale = mod_ref[0, :, D_MODEL:2 * D_MODEL]
    for sub in range(INPROJ_SUB):
        rows = slice(sub * tm, (sub + 1) * tm)
        x = x_ref[rows, :]
        xn = x * lax.rsqrt(jnp.mean(x * x, axis=-1, keepdims=True) + EPS) * g1_ref[...]
        h = (xn * (1.0 + scale) + shift).astype(BF16)

        def proj(lo, width):
            return _dot_nt(h, w_ref[lo:lo + width, :])

        u = proj(2 * D_MODEL, D_MODEL) * proj(0, D_MODEL)
        ubuf[SUBLANES:SUBLANES + tm, :] = u
        conv = (ubuf[SUBLANES - 2:SUBLANES - 2 + tm, :] * cw_ref[0:1, :]
                + ubuf[SUBLANES - 1:SUBLANES - 1 + tm, :] * cw_ref[1:2, :]
                + u * cw_ref[2:3, :])
        ubuf[0:SUBLANES, :] = ubuf[tm:tm + SUBLANES, :]
        y_conv = proj(D_MODEL, D_MODEL) * conv
        conv_ref[rows, :] = (_sigmoid(proj(O_GATE, D_MODEL)) * y_conv).astype(BF16)
        gm_ref[rows, :] = _sigmoid(proj(O_GATE + D_MODEL, D_MODEL)).astype(BF16)

        ql = proj(O_LAT, Q_LORA)
        qn_ref[rows, :] = (ql * lax.rsqrt(jnp.mean(ql * ql, axis=-1, keepdims=True) + EPS)
                           * gq_ref[...]).astype(BF16)
        kl = proj(O_LAT + Q_LORA, KV_LORA)
        kvn_ref[rows, :] = (kl * lax.rsqrt(jnp.mean(kl * kl, axis=-1, keepdims=True) + EPS)
                            * gkv_ref[...]).astype(BF16)
        kr = _dot_nt(w_ref[O_KR:O_GATE, :], h)
        half = QK_ROPE // 2
        krt_ref[0, 0:QK_ROPE, rows] = kr
        krt_ref[0, QK_ROPE:QK_ROPE + half, rows] = kr[half:]
        krt_ref[0, QK_ROPE + half:, rows] = kr[:half]


def _inproj(x2, mod3, g1, w_t, conv_w, gq, gkv, seq):
    t = x2.shape[0]
    nb = t // seq
    tm = ROW_TILE * INPROJ_SUB
    tiles_per_seq = seq // tm
    row = lambda i: (i, 0)
    const = lambda i: (0, 0)
    return pl.pallas_call(
        functools.partial(_inproj_kernel, tiles_per_seq),
        grid=(t // tm,),
        in_specs=[
            pl.BlockSpec((tm, D_MODEL), row),
            pl.BlockSpec((1, 1, 6 * D_MODEL), lambda i: (i // tiles_per_seq, 0, 0)),
            pl.BlockSpec((1, D_MODEL), const),
            pl.BlockSpec(w_t.shape, const),
            pl.BlockSpec((CONV_K, D_MODEL), const),
            pl.BlockSpec((1, Q_LORA), const),
            pl.BlockSpec((1, KV_LORA), const),
        ],
        out_specs=[
            pl.BlockSpec((tm, D_MODEL), row),
            pl.BlockSpec((tm, D_MODEL), row),
            pl.BlockSpec((tm, Q_LORA), row),
            pl.BlockSpec((tm, KV_LORA), row),
            pl.BlockSpec((1, 2 * QK_ROPE, tm),
                         lambda i: (i // tiles_per_seq, 0, i % tiles_per_seq)),
        ],
        out_shape=[
            jax.ShapeDtypeStruct((t, D_MODEL), BF16),
            jax.ShapeDtypeStruct((t, D_MODEL), BF16),
            jax.ShapeDtypeStruct((t, Q_LORA), BF16),
            jax.ShapeDtypeStruct((t, KV_LORA), BF16),
            jax.ShapeDtypeStruct((nb, 2 * QK_ROPE, seq), F32),
        ],
        scratch_shapes=[pltpu.VMEM((ROW_TILE + SUBLANES, D_MODEL), F32)],
        compiler_params=pltpu.CompilerParams(
            dimension_semantics=("arbitrary",), vmem_limit_bytes=VMEM_LIMIT),
        name="inproj",
    )(x2, mod3, g1, w_t, conv_w, gq, gkv)


def _attn_kernel(qn_ref, kvn_ref, krt_ref, cos_ref, sin_ref, wqt_ref, wkvt_ref, gq_ref, gk_ref, conv_ref, gm_ref,
                 o_ref, *scratch):
    per_head = len(scratch) // HEADS_PER_STEP
    heads = [_attn_head(hh, qn_ref, kvn_ref, krt_ref, cos_ref, sin_ref, wqt_ref, wkvt_ref, gq_ref, gk_ref,
                        conv_ref, gm_ref, o_ref, *scratch[hh * per_head:(hh + 1) * per_head])
             for hh in range(HEADS_PER_STEP)]
    nq = qn_ref.shape[0] // Q_TILE
    for scores, _ in heads:
        scores(0)
    for i in range(nq):
        for scores, finish in heads:
            if i + 1 < nq:
                scores(i + 1)
            finish(i)


def _attn_head(hh, qn_ref, kvn_ref, krt_ref, cos_ref, sin_ref, wqt_ref, wkvt_ref, gq_ref, gk_ref,
               conv_ref, gm_ref, o_ref, qt_s, k_s, vt_s, s_buf0, s_buf1):
    seq = qn_ref.shape[0]
    cos_t = jnp.concatenate([cos_ref[0], cos_ref[0]], axis=0)
    sin_t = jnp.concatenate([-sin_ref[0], sin_ref[0]], axis=0)

    def normed_rope(nope, r, rr, g, extra_scale):
        ss = jnp.sum(nope * nope, axis=0, keepdims=True) + jnp.sum(r * r, axis=0, keepdims=True)
        scale = lax.rsqrt(ss * (1.0 / QK_HEAD) + EPS) * extra_scale
        rope = r * g[QK_NOPE:QK_HEAD] * cos_t + rr * g[QK_HEAD:] * sin_t
        return (nope * g[0:QK_NOPE] * scale).astype(BF16), (rope * scale).astype(BF16)

    qt = _dot_nt(wqt_ref[hh], qn_ref[...])
    q_n, q_r = normed_rope(qt[0:QK_NOPE], qt[QK_NOPE:QK_HEAD], qt[QK_HEAD:], gq_ref[...],
                           QK_HEAD ** -0.5 * math.log2(math.e))
    qt_s[0:QK_NOPE, :] = q_n
    qt_s[QK_NOPE:QK_HEAD, :] = q_r
    qt_s[QK_HEAD:, :] = jnp.zeros((QK_ROPE, seq), BF16)

    kvt = _dot_nt(wkvt_ref[hh], kvn_ref[...])
    krt = krt_ref[0]
    k_n, k_r = normed_rope(kvt[0:QK_NOPE], krt[0:QK_ROPE], krt[QK_ROPE:], gk_ref[...], 1.0)
    kt = jnp.concatenate([k_n, k_r, jnp.zeros((QK_ROPE, seq), BF16)], axis=0)
    k_s[...] = kt.T
    vt_s[0:V_HEAD, :] = kvt[QK_NOPE:].astype(BF16)
    vt_s[V_HEAD:, :] = jnp.ones((vt_s.shape[0] - V_HEAD, seq), BF16)

    tq = Q_TILE
    hq = tq // 2
    kchunk = lax.broadcasted_iota(jnp.int32, (hq, tq), 0) // CHUNK
    qchunk = lax.broadcasted_iota(jnp.int32, (hq, tq), 1) // CHUNK
    head_ok = kchunk <= qchunk
    tail_ok = head_ok[:, 0:hq]
    neg = jnp.finfo(F32).min

    def scores(i):
        q0 = i * tq
        sb = s_buf0 if i % 2 == 0 else s_buf1
        q = qt_s[:, q0:q0 + tq]
        if i > 0:
            sb[0:q0, :] = _dot(k_s[0:q0, :], q)
        sb[q0:q0 + hq, :] = jnp.where(head_ok, _dot(k_s[q0:q0 + hq, :], q), neg)
        sb[q0 + hq:q0 + tq, hq:] = jnp.where(tail_ok, _dot(k_s[q0 + hq:q0 + tq, :], q[:, hq:]), neg)

    def finish(i):
        q0 = i * tq
        kmain = q0 + hq
        sb = s_buf0 if i % 2 == 0 else s_buf1
        tail = sb[kmain:kmain + hq, hq:]
        m_main = jnp.max(sb[0:kmain, :], axis=0, keepdims=True)
        m_tail = jnp.maximum(m_main[:, hq:], jnp.max(tail, axis=0, keepdims=True))
        m = jnp.concatenate([m_main[:, 0:hq], m_tail], axis=1)
        acc = _dot(vt_s[:, 0:kmain], jnp.exp2(sb[0:kmain, :] - m).astype(BF16))
        acc_tail = _dot(vt_s[:, kmain:kmain + hq], jnp.exp2(tail - m_tail).astype(BF16))
        acc = jnp.concatenate([acc[:, 0:hq], acc[:, hq:] + acc_tail], axis=1)
        o_t = acc[0:V_HEAD] / acc[V_HEAD:V_HEAD + 1]
        rows, cols = slice(q0, q0 + tq), slice(hh * V_HEAD, (hh + 1) * V_HEAD)
        y_mla = o_t.T.astype(BF16).astype(F32)
        merged = conv_ref[rows, cols].astype(F32) + gm_ref[rows, cols].astype(F32) * y_mla
        o_ref[rows, cols] = merged.astype(o_ref.dtype)

    return scores, finish


def _attn(qn, kvn, krt, cos, sin, wqt, wkvt, gq, gk, conv_p, gm, nb, seq):
    t = qn.shape[0]
    per_b = lambda b, h: (b, 0)
    per_b3 = lambda b, h: (b, 0, 0)
    per_h = lambda b, h: (h, 0, 0)
    const = lambda b, h: (0, 0)
    qk_rows = QK_NOPE + 2 * QK_ROPE
    return pl.pallas_call(
        _attn_kernel,
        grid=(nb, N_HEADS // HEADS_PER_STEP),
        in_specs=[
            pl.BlockSpec((seq, Q_LORA), per_b),
            pl.BlockSpec((seq, KV_LORA), per_b),
            pl.BlockSpec((1, 2 * QK_ROPE, seq), per_b3),
            pl.BlockSpec((1, QK_ROPE // 2, seq), per_b3),
            pl.BlockSpec((1, QK_ROPE // 2, seq), per_b3),
            pl.BlockSpec((HEADS_PER_STEP, qk_rows, Q_LORA), per_h),
            pl.BlockSpec((HEADS_PER_STEP, QK_NOPE + V_HEAD, KV_LORA), per_h),
            pl.BlockSpec((qk_rows, 1), const),
            pl.BlockSpec((qk_rows, 1), const),
            pl.BlockSpec((seq, HEADS_PER_STEP * V_HEAD), lambda b, h: (b, h)),
            pl.BlockSpec((seq, HEADS_PER_STEP * V_HEAD), lambda b, h: (b, h)),
        ],
        out_specs=pl.BlockSpec((seq, HEADS_PER_STEP * V_HEAD), lambda b, h: (b, h)),
        out_shape=jax.ShapeDtypeStruct((t, N_HEADS * V_HEAD), BF16),
        scratch_shapes=[
            pltpu.VMEM((qk_rows, seq), BF16),
            pltpu.VMEM((seq, qk_rows), BF16),
            pltpu.VMEM((V_HEAD + 2 * SUBLANES, seq), BF16),
            pltpu.VMEM((seq, Q_TILE), F32),
            pltpu.VMEM((seq, Q_TILE), F32),
        ] * HEADS_PER_STEP,
        compiler_params=pltpu.CompilerParams(
            dimension_semantics=("arbitrary", "arbitrary"), vmem_limit_bytes=VMEM_LIMIT),
        name="attn",
    )(qn, kvn, krt, cos, sin, wqt, wkvt, gq, gk, conv_p, gm)


def _oproj_kernel(merged_ref, x_ref, mod_ref, wo_ref, g2_ref, wr_ref, br_ref,
                  x1_ref, h2_ref, eid_ref, rank_ref, cnt_ref, cwt_ref, base):
    tm = x_ref.shape[0]

    @pl.when(pl.program_id(0) == 0)
    def _():
        base[...] = jnp.zeros(base.shape, F32)

    att = _dot(merged_ref[...], wo_ref[...])
    gate1 = mod_ref[0, :, 2 * D_MODEL:3 * D_MODEL]
    shift2 = mod_ref[0, :, 3 * D_MODEL:4 * D_MODEL]
    scale2 = mod_ref[0, :, 4 * D_MODEL:5 * D_MODEL]
    x1 = x_ref[...] + gate1 * att
    x1_ref[...] = x1
    xn = x1 * lax.rsqrt(jnp.mean(x1 * x1, axis=-1, keepdims=True) + EPS) * g2_ref[...]
    h2 = xn * (1.0 + scale2) + shift2
    _store_rows(h2_ref, h2, tm)

    lt = _dot_nt(wr_ref[...], h2.astype(BF16)) + br_ref[...]
    gl = [lt[N_EXPERTS + r:N_EXPERTS + r + 1, :] for r in range(N_GROUPS)]
    gmax = jnp.maximum(jnp.maximum(gl[0], gl[1]), jnp.maximum(gl[2], gl[3]))
    gidx = jnp.full(gmax.shape, N_GROUPS - 1, jnp.int32)
    for r in range(N_GROUPS - 2, -1, -1):
        gidx = jnp.where(gl[r] == gmax, r, gidx)
    gsum = jnp.exp(gl[0] - gmax)
    for r in range(1, N_GROUPS):
        gsum = gsum + jnp.exp(gl[r] - gmax)
    p_group = 1.0 / gsum
    es = lt[(N_GROUPS - 1) * EXPERTS_PER_GROUP:N_GROUPS * EXPERTS_PER_GROUP, :]
    for r in range(N_GROUPS - 2, -1, -1):
        es = jnp.where(gidx == r, lt[r * EXPERTS_PER_GROUP:(r + 1) * EXPERTS_PER_GROUP, :], es)
    row = lax.broadcasted_iota(jnp.int32, es.shape, 0)
    m1 = jnp.max(es, axis=0, keepdims=True)
    i1 = jnp.min(jnp.where(es == m1, row, EXPERTS_PER_GROUP), axis=0, keepdims=True)
    es2 = jnp.where(row == i1, -jnp.inf, es)
    m2 = jnp.max(es2, axis=0, keepdims=True)
    i2 = jnp.min(jnp.where(es2 == m2, row, EXPERTS_PER_GROUP), axis=0, keepdims=True)
    e2 = jnp.exp(m2 - m1)
    w1 = p_group / (1.0 + e2)
    w2 = w1 * e2
    eid0 = gidx * EXPERTS_PER_GROUP + i1
    eid1 = gidx * EXPERTS_PER_GROUP + i2
    eid_ref[0:1, :] = eid0
    eid_ref[1:2, :] = eid1
    erow = lax.broadcasted_iota(jnp.int32, (N_EXPERTS, tm), 0)
    oh0 = erow == eid0
    oh1 = erow == eid1
    both = jnp.where(oh0, 1.0, jnp.where(oh1, 1.0, 0.0))
    earlier = (lax.broadcasted_iota(jnp.int32, (tm, tm), 0)
               < lax.broadcasted_iota(jnp.int32, (tm, tm), 1))
    seen = base[...] + _dot(both.astype(BF16), jnp.where(earlier, 1.0, 0.0).astype(BF16))
    rank_ref[0:1, :] = jnp.sum(jnp.where(oh0, seen, 0.0), axis=0, keepdims=True).astype(jnp.int32)
    rank_ref[1:2, :] = jnp.sum(jnp.where(oh1, seen, 0.0), axis=0, keepdims=True).astype(jnp.int32)
    base[...] = base[...] + jnp.sum(both, axis=1, keepdims=True)
    cnt_ref[...] = base[...]
    wrow = lax.broadcasted_iota(jnp.int32, (LANES, tm), 0)
    wmat = jnp.where(wrow == 0, w1, jnp.where(wrow == 1, w2, 0.0))
    cwt_ref[...] = wmat.T


def _oproj(merged, x2, mod3, wo, g2, wr_t, br, seq):
    t = x2.shape[0]
    tm = ROW_TILE
    tiles_per_seq = seq // tm
    row = lambda i: (i, 0)
    const = lambda i: (0, 0)
    return pl.pallas_call(
        _oproj_kernel,
        grid=(t // tm,),
        in_specs=[
            pl.BlockSpec((tm, D_MODEL), row),
            pl.BlockSpec((tm, D_MODEL), row),
            pl.BlockSpec((1, 1, 6 * D_MODEL), lambda i: (i // tiles_per_seq, 0, 0)),
            pl.BlockSpec((D_MODEL, D_MODEL), const),
            pl.BlockSpec((1, D_MODEL), const),
            pl.BlockSpec((ROUTER_ROWS, D_MODEL), const),
            pl.BlockSpec((ROUTER_ROWS, 1), const),
        ],
        out_specs=[
            pl.BlockSpec((tm, D_MODEL), row),
            pl.BlockSpec((tm * ROWS_PER_TOKEN, LANES), row),
            pl.BlockSpec((2, tm), lambda i: (0, i)),
            pl.BlockSpec((2, tm), lambda i: (0, i)),
            pl.BlockSpec((N_EXPERTS, 1), const),
            pl.BlockSpec((tm, LANES), row),
        ],
        out_shape=[
            jax.ShapeDtypeStruct((t, D_MODEL), F32),
            jax.ShapeDtypeStruct((t * ROWS_PER_TOKEN, LANES), F32),
            jax.ShapeDtypeStruct((2, t), jnp.int32),
            jax.ShapeDtypeStruct((2, t), jnp.int32),
            jax.ShapeDtypeStruct((N_EXPERTS, 1), F32),
            jax.ShapeDtypeStruct((t, LANES), F32),
        ],
        scratch_shapes=[pltpu.VMEM((N_EXPERTS, 1), F32)],
        compiler_params=pltpu.CompilerParams(
            dimension_semantics=("arbitrary",), vmem_limit_bytes=VMEM_LIMIT),
        name="oproj",
    )(merged, x2, mod3, wo, g2, wr_t, br)


def _dispatch_kernel(n_tiles, ztile_ref, na_ref, pos_ref, h2_ref, xs_hbm, zbuf, sem, zsem):
    tm = h2_ref.shape[0] // ROWS_PER_TOKEN
    tile_rows = MOE_TILE * ROWS_PER_TOKEN

    def zero_tile(tile, on_sem):
        row = pl.multiple_of(tile * tile_rows, tile_rows)
        return pltpu.make_async_copy(zbuf, xs_hbm.at[pl.ds(row, tile_rows), :], on_sem)

    def unused_tiles(action):
        def body(k, carry):
            action(zero_tile(na_ref[0] + k, zsem))
            return carry
        lax.fori_loop(0, n_tiles - na_ref[0], body, 0)

    @pl.when(pl.program_id(0) == 0)
    def _():
        zbuf[...] = jnp.zeros(zbuf.shape, zbuf.dtype)
        for e in range(N_EXPERTS):
            zero_tile(ztile_ref[e], sem).start()
        for e in range(N_EXPERTS):
            zero_tile(ztile_ref[e], sem).wait()
        unused_tiles(lambda cp: cp.start())

    def body(c, carry):
        for k in range(DMA_UNROLL):
            t = c * DMA_UNROLL + k
            src = h2_ref.at[pl.ds(pl.multiple_of(t * ROWS_PER_TOKEN, ROWS_PER_TOKEN), ROWS_PER_TOKEN), :]
            for s in range(2):
                dst_row = pl.multiple_of(pos_ref[0, 0, s * tm + t] * ROWS_PER_TOKEN, ROWS_PER_TOKEN)
                pltpu.make_async_copy(src, xs_hbm.at[pl.ds(dst_row, ROWS_PER_TOKEN), :],
                                      sem).start(priority=s)
        return carry
    lax.fori_loop(0, tm // DMA_UNROLL, body, 0)
    for _ in range(2):
        pltpu.make_async_copy(h2_ref, xs_hbm.at[pl.ds(0, tm * ROWS_PER_TOKEN), :], sem).wait()

    @pl.when(pl.program_id(0) == pl.num_programs(0) - 1)
    def _():
        unused_tiles(lambda cp: cp.wait())


def _dispatch(ztile, na, pos3, h2r, n_tiles):
    tm = pos3.shape[2] // 2
    n_steps = pos3.shape[0]
    grid_spec = pltpu.PrefetchScalarGridSpec(
        num_scalar_prefetch=2,
        grid=(n_steps,),
        in_specs=[
            pl.BlockSpec((1, 1, 2 * tm), lambda i, z, n: (i, 0, 0), memory_space=pltpu.SMEM),
            pl.BlockSpec((tm * ROWS_PER_TOKEN, LANES), lambda i, z, n: (i, 0)),
        ],
        out_specs=pl.BlockSpec(memory_space=pl.ANY),
        scratch_shapes=[
            pltpu.VMEM((MOE_TILE * ROWS_PER_TOKEN, LANES), F32),
            pltpu.SemaphoreType.DMA(()),
            pltpu.SemaphoreType.DMA(()),
        ],
    )
    return pl.pallas_call(
        functools.partial(_dispatch_kernel, n_tiles),
        grid_spec=grid_spec,
        out_shape=jax.ShapeDtypeStruct((n_tiles * MOE_TILE * ROWS_PER_TOKEN, LANES), F32),
        compiler_params=pltpu.CompilerParams(
            dimension_semantics=("arbitrary",), vmem_limit_bytes=VMEM_LIMIT),
        name="dispatch",
    )(ztile, na, pos3, h2r)


def _moe_kernel(te_ref, na_ref, x_hbm, wg_ref, wu_ref, wd_ref, y_ref, xbuf, wgu_s, wd_s, sem):
    i = pl.program_id(0)
    n_active = na_ref[0]
    active = i < n_active
    tile_rows = MOE_TILE * ROWS_PER_TOKEN

    def fetch(tile):
        slot = tile % X_SLOTS
        row = pl.multiple_of(tile * tile_rows, tile_rows)
        return pltpu.make_async_copy(x_hbm.at[pl.ds(row, tile_rows), :], xbuf.at[slot], sem.at[slot])

    @pl.when(i == 0)
    def _():
        for k in range(X_SLOTS - 1):
            @pl.when(k < n_active)
            def _():
                fetch(k).start()

    @pl.when(i + X_SLOTS - 1 < n_active)
    def _():
        fetch(i + X_SLOTS - 1).start()

    new_expert = jnp.logical_or(i == 0, te_ref[i] != te_ref[jnp.maximum(i - 1, 0)])

    @pl.when(jnp.logical_and(active, new_expert))
    def _():
        wgu_s[:, 0:D_EXPERT] = wg_ref[0].astype(BF16)
        wgu_s[:, D_EXPERT:] = wu_ref[0].astype(BF16)
        wd_s[...] = wd_ref[0].astype(BF16)

    @pl.when(active)
    def _():
        fetch(i).wait()
        x = jnp.concatenate([col.astype(BF16) for col in _load_rows(xbuf.at[i % X_SLOTS], MOE_TILE)], axis=1)
        gu = _dot(x, wgu_s[...])
        g = gu[:, 0:D_EXPERT]
        a = (g * _sigmoid(g)) * gu[:, D_EXPERT:]
        y = _dot(a.astype(BF16), wd_s[...])
        _store_rows(y_ref, y, MOE_TILE)

    @pl.when(jnp.logical_not(active))
    def _():
        y_ref[...] = jnp.zeros(y_ref.shape, y_ref.dtype)


def _moe(te, na, xs, wg, wu, wd):
    nt = te.shape[0]
    tile_rows = MOE_TILE * ROWS_PER_TOKEN
    wspec = lambda shape: pl.BlockSpec((1,) + shape, lambda i, te_r, na_r: (te_r[i], 0, 0))
    grid_spec = pltpu.PrefetchScalarGridSpec(
        num_scalar_prefetch=2,
        grid=(nt,),
        in_specs=[
            pl.BlockSpec(memory_space=pl.ANY),
            wspec((D_MODEL, D_EXPERT)),
            wspec((D_MODEL, D_EXPERT)),
            wspec((D_EXPERT, D_MODEL)),
        ],
        out_specs=pl.BlockSpec((tile_rows, LANES), lambda i, te_r, na_r: (i, 0)),
        scratch_shapes=[
            pltpu.VMEM((X_SLOTS, tile_rows, LANES), F32),
            pltpu.VMEM((D_MODEL, 2 * D_EXPERT), BF16),
            pltpu.VMEM((D_EXPERT, D_MODEL), BF16),
            pltpu.SemaphoreType.DMA((X_SLOTS,)),
        ],
    )
    return pl.pallas_call(
        _moe_kernel,
        grid_spec=grid_spec,
        out_shape=jax.ShapeDtypeStruct(xs.shape, xs.dtype),
        compiler_params=pltpu.CompilerParams(
            dimension_semantics=("arbitrary",), vmem_limit_bytes=VMEM_LIMIT),
        name="moe",
    )(te, na, xs, wg, wu, wd)


def _comb_kernel(pos0_ref, posn_ref, x1_ref, cwt_ref, mod_ref, y_hbm, o_ref, ybuf, sem):
    tm = x1_ref.shape[0]
    i = pl.program_id(0)
    slot = i % 2

    def gather(pos_ref, dst_slot):
        def body(c, carry):
            for k in range(DMA_UNROLL):
                r = c * DMA_UNROLL + k
                src_row = pl.multiple_of(pos_ref[0, 0, r] * ROWS_PER_TOKEN, ROWS_PER_TOKEN)
                dst_row = pl.multiple_of(r * ROWS_PER_TOKEN, ROWS_PER_TOKEN)
                pltpu.make_async_copy(y_hbm.at[pl.ds(src_row, ROWS_PER_TOKEN), :],
                                      ybuf.at[dst_slot, pl.ds(dst_row, ROWS_PER_TOKEN), :],
                                      sem.at[dst_slot]).start(priority=k % 2)
            return carry
        lax.fori_loop(0, 2 * tm // DMA_UNROLL, body, 0)

    @pl.when(i == 0)
    def _():
        gather(pos0_ref, 0)

    @pl.when(i + 1 < pl.num_programs(0))
    def _():
        gather(posn_ref, 1 - slot)

    pltpu.make_async_copy(y_hbm.at[pl.ds(0, 2 * tm * ROWS_PER_TOKEN), :], ybuf.at[slot],
                          sem.at[slot]).wait()
    c0 = cwt_ref[:, 0:1]
    c1 = cwt_ref[:, 1:2]
    y0 = _load_rows(ybuf.at[slot], tm)
    y1 = _load_rows(ybuf.at[slot], tm, offset=tm * ROWS_PER_TOKEN)
    for j in range(ROWS_PER_TOKEN):
        cols = slice(j * LANES, (j + 1) * LANES)
        gate2 = mod_ref[0, :, 5 * D_MODEL + j * LANES:5 * D_MODEL + (j + 1) * LANES]
        o_ref[:, cols] = x1_ref[:, cols] + gate2 * (c0 * y0[j] + c1 * y1[j])


def _comb(x1, yr, pos3, cwt, mod3, seq):
    t = x1.shape[0]
    tm = ROW_TILE
    tiles_per_seq = seq // tm
    n_steps = t // tm
    row = lambda i: (i, 0)
    smem_blk = lambda f: pl.BlockSpec((1, 1, 2 * tm), f, memory_space=pltpu.SMEM)
    return pl.pallas_call(
        _comb_kernel,
        grid=(n_steps,),
        in_specs=[
            smem_blk(lambda i: (0, 0, 0)),
            smem_blk(lambda i: (jnp.minimum(i + 1, n_steps - 1), 0, 0)),
            pl.BlockSpec((tm, D_MODEL), row),
            pl.BlockSpec((tm, LANES), row),
            pl.BlockSpec((1, 1, 6 * D_MODEL), lambda i: (i // tiles_per_seq, 0, 0)),
            pl.BlockSpec(memory_space=pl.ANY),
        ],
        out_specs=pl.BlockSpec((tm, D_MODEL), row),
        out_shape=jax.ShapeDtypeStruct((t, D_MODEL), F32),
        scratch_shapes=[
            pltpu.VMEM((2, 2 * tm * ROWS_PER_TOKEN, LANES), F32),
            pltpu.SemaphoreType.DMA((2,)),
        ],
        compiler_params=pltpu.CompilerParams(
            dimension_semantics=("arbitrary",), vmem_limit_bytes=VMEM_LIMIT),
        name="comb",
    )(pos3, pos3, x1, cwt, mod3, yr)


def _route_plan(eid, rank, cnt, n_tok):
    n_tiles = (2 * n_tok) // MOE_TILE + N_EXPERTS
    experts = jnp.arange(N_EXPERTS, dtype=jnp.int32)
    counts = cnt.reshape(N_EXPERTS).astype(jnp.int32)
    ntile = (counts + MOE_TILE - 1) // MOE_TILE
    tend = jnp.cumsum(ntile)
    tstart = tend - ntile
    n_active = tend[-1]
    tj = jnp.arange(n_tiles, dtype=jnp.int32)
    te_raw = jnp.minimum(jnp.sum((tj[:, None] >= tend[None, :]).astype(jnp.int32), axis=1),
                         N_EXPERTS - 1)
    te_last = jnp.sum(jnp.where(tj == n_active - 1, te_raw, 0))
    te = jnp.where(tj < n_active, te_raw, te_last).astype(jnp.int32)
    first_row = jnp.sum(jnp.where(eid[:, :, None] == experts[None, None, :],
                                  (tstart * MOE_TILE)[None, None, :], 0), axis=-1)
    pos = (first_row + rank).astype(jnp.int32)
    ztile = jnp.maximum(tend - 1, 0).astype(jnp.int32)
    pos3 = pos.reshape(2, n_tok // ROW_TILE, ROW_TILE).transpose(1, 0, 2).reshape(
        n_tok // ROW_TILE, 1, 2 * ROW_TILE)
    return te, n_active.reshape(1).astype(jnp.int32), ztile, pos3, n_tiles


def _rotate_half_cols(w):
    half = QK_ROPE // 2
    return jnp.concatenate([w[..., half:], w[..., :half]], axis=-1)


def kernel(x, c, positions, w_ada, b_ada, norm1_g, w_in, conv_w, q_a_norm_g, w_q_b, kv_a_norm_g, w_kv_b, q_norm_g, k_norm_g, w_o, norm2_g, w_router_group, b_router_group, w_router_expert, b_router_expert, w_exp_gate, w_exp_up, w_exp_down):
    nb, seq, d = x.shape
    depth = w_ada.shape[0]
    n_tok = nb * seq
    assert d == D_MODEL and seq % ROW_TILE == 0 and seq % Q_TILE == 0 and Q_TILE % CHUNK == 0
    assert (2 * n_tok) % MOE_TILE == 0

    inv = ROPE_BASE ** (-jnp.arange(0, QK_ROPE, 2, dtype=F32) / QK_ROPE)
    ang = inv[None, :, None] * positions.astype(F32)[:, None, :]
    cos, sin = jnp.cos(ang), jnp.sin(ang)

    x2 = x.reshape(n_tok, d)
    for l in range(depth):
        w_t = w_in[l].T.astype(BF16)
        wq3 = w_q_b[l].reshape(Q_LORA, N_HEADS, QK_HEAD)
        wq = jnp.concatenate([wq3, _rotate_half_cols(wq3[..., QK_NOPE:])], axis=-1)
        wqt = wq.transpose(1, 2, 0).astype(BF16)
        wkvt = w_kv_b[l].reshape(KV_LORA, N_HEADS, QK_NOPE + V_HEAD).transpose(1, 2, 0).astype(BF16)
        gq = jnp.concatenate([q_norm_g[l], _rotate_half_cols(q_norm_g[l][QK_NOPE:])]).reshape(-1, 1)
        gk = jnp.concatenate([k_norm_g[l], _rotate_half_cols(k_norm_g[l][QK_NOPE:])]).reshape(-1, 1)
        wr_t = jnp.concatenate(
            [w_router_expert[l].T, w_router_group[l].T,
             jnp.zeros((ROUTER_ROWS - N_EXPERTS - N_GROUPS, d), F32)], axis=0).astype(BF16)
        br = jnp.concatenate(
            [b_router_expert[l], b_router_group[l],
             jnp.zeros((ROUTER_ROWS - N_EXPERTS - N_GROUPS,), F32)]).reshape(ROUTER_ROWS, 1)

        mod3 = _ada(c, w_ada[l], b_ada[l]).reshape(nb, 1, 6 * d)
        conv_p, gm, qn, kvn, krt = _inproj(
            x2, mod3, norm1_g[l].reshape(1, d), w_t, conv_w[l],
            q_a_norm_g[l].reshape(1, -1), kv_a_norm_g[l].reshape(1, -1), seq)
        merged = _attn(qn, kvn, krt, cos, sin, wqt, wkvt, gq, gk, conv_p, gm, nb, seq)
        x1, h2r, eid, rank, cnt, cwt = _oproj(merged, x2, mod3, w_o[l].astype(BF16),
                                              norm2_g[l].reshape(1, d), wr_t, br, seq)
        te, na, ztile, pos3, n_tiles = _route_plan(eid, rank, cnt, n_tok)
        xs = _dispatch(ztile, na, pos3, h2r, n_tiles)
        yr = _moe(te, na, xs,
                  w_exp_gate[l].reshape(N_EXPERTS, d, D_EXPERT),
                  w_exp_up[l].reshape(N_EXPERTS, d, D_EXPERT),
                  w_exp_down[l].reshape(N_EXPERTS, D_EXPERT, d))
        x2 = _comb(x1, yr, pos3, cwt, mod3, seq)
    return x2.reshape(nb, seq, d)
```

```python
import functools
import math

import jax
import jax.numpy as jnp
from jax import lax
from jax.experimental import pallas as pl
from jax.experimental.pallas import tpu as pltpu

F32 = jnp.float32
BF16 = jnp.bfloat16

D_MODEL = 1024
N_HEADS = 8
QK_NOPE = 128
QK_ROPE = 64
QK_HEAD = QK_NOPE + QK_ROPE
V_HEAD = 128
Q_LORA = 384
KV_LORA = 256
CHUNK = 64
EPS = 1e-6
ROPE_BASE = 10000.0
N_GROUPS = 4
EXPERTS_PER_GROUP = 8
N_EXPERTS = N_GROUPS * EXPERTS_PER_GROUP
D_EXPERT = 256
CONV_K = 3

LANES = 128
SUBLANES = 8
VMEM_LIMIT = 56 * 1024 * 1024

ROW_TILE = 512
Q_TILE = 512
INPROJ_SUB = 2
OPROJ_TILE = 1024
HEADS_PER_STEP = 2
MOE_TILE = 512
X_SLOTS = 4
ADA_COLS = 1536
ROWS_PER_TOKEN = D_MODEL // LANES
DMA_UNROLL = 8
ROUTER_ROWS = 40


def _sigmoid(v):
    return 1.0 / (1.0 + jnp.exp(-v))


def _dot(a, b):
    return jnp.dot(a, b, preferred_element_type=F32)


def _store_rows(ref, val, n):
    for j in range(ROWS_PER_TOKEN):
        ref[pl.ds(j, n, stride=ROWS_PER_TOKEN), :] = val[:, j * LANES:(j + 1) * LANES]


def _load_rows(ref, n, offset=0):
    return [ref[pl.ds(offset + j, n, stride=ROWS_PER_TOKEN), :] for j in range(ROWS_PER_TOKEN)]


def _dot_nt(a, b):
    return lax.dot_general(a, b, (((1,), (1,)), ((), ())), preferred_element_type=F32)


def _ada_kernel(c_ref, w_ref, b_ref, o_ref):
    c = c_ref[...]
    act = (c * _sigmoid(c)).astype(BF16)
    o_ref[...] = _dot(act, w_ref[...].astype(BF16)) + b_ref[...]


def _ada(c, w_ada, b_ada):
    nb, d = c.shape
    n = w_ada.shape[1]
    return pl.pallas_call(
        _ada_kernel,
        grid=(n // ADA_COLS,),
        in_specs=[
            pl.BlockSpec((nb, d), lambda j: (0, 0)),
            pl.BlockSpec((d, ADA_COLS), lambda j: (0, j)),
            pl.BlockSpec((1, ADA_COLS), lambda j: (0, j)),
        ],
        out_specs=pl.BlockSpec((nb, ADA_COLS), lambda j: (0, j)),
        out_shape=jax.ShapeDtypeStruct((nb, n), F32),
        compiler_params=pltpu.CompilerParams(
            dimension_semantics=("arbitrary",), vmem_limit_bytes=VMEM_LIMIT),
        name="ada",
    )(c, w_ada, b_ada.reshape(1, n))


O_LAT = 3 * D_MODEL
O_KR = O_LAT + Q_LORA + KV_LORA
O_GATE = O_KR + QK_ROPE


def _inproj_kernel(tiles_per_seq, x_ref, mod_ref, g1_ref, w_ref, cw_ref, gq_ref, gkv_ref,
                   conv_ref, gm_ref, qn_ref, kvn_ref, krt_ref, ubuf):
    tm = x_ref.shape[0] // INPROJ_SUB

    @pl.when(pl.program_id(0) % tiles_per_seq == 0)
    def _():
        ubuf[0:SUBLANES, :] = jnp.zeros((SUBLANES, D_MODEL), F32)

    shift = mod_ref[0, :, 0:D_MODEL]
    scale = mod_ref[0, :, D_MODEL:2 * D_MODEL]
    for sub in range(INPROJ_SUB):
        rows = slice(sub * tm, (sub + 1) * tm)
        x = x_ref[rows, :]
        xn = x * lax.rsqrt(jnp.mean(x * x, axis=-1, keepdims=True) + EPS) * g1_ref[...]
        h = (xn * (1.0 + scale) + shift).astype(BF16)

        def proj(lo, width):
            return _dot_nt(h, w_ref[lo:lo + width, :])

        u = proj(2 * D_MODEL, D_MODEL) * proj(0, D_MODEL)
        ubuf[SUBLANES:SUBLANES + tm, :] = u
        conv = (ubuf[SUBLANES - 2:SUBLANES - 2 + tm, :] * cw_ref[0:1, :]
                + ubuf[SUBLANES - 1:SUBLANES - 1 + tm, :] * cw_ref[1:2, :]
                + u * cw_ref[2:3, :])
        ubuf[0:SUBLANES, :] = ubuf[tm:tm + SUBLANES, :]
        y_conv = proj(D_MODEL, D_MODEL) * conv
        conv_ref[rows, :] = (_sigmoid(proj(O_GATE, D_MODEL)) * y_conv).astype(BF16)
        gm_ref[rows, :] = _sigmoid(proj(O_GATE + D_MODEL, D_MODEL)).astype(BF16)

        ql = proj(O_LAT, Q_LORA)
        qn_ref[rows, :] = (ql * lax.rsqrt(jnp.mean(ql * ql, axis=-1, keepdims=True) + EPS)
                           * gq_ref[...]).astype(BF16)
        kl = proj(O_LAT + Q_LORA, KV_LORA)
        kvn_ref[rows, :] = (kl * lax.rsqrt(jnp.mean(kl * kl, axis=-1, keepdims=True) + EPS)
                            * gkv_ref[...]).astype(BF16)
        kr = _dot_nt(w_ref[O_KR:O_GATE, :], h)
        half = QK_ROPE // 2
        krt_ref[0, 0:QK_ROPE, rows] = kr
        krt_ref[0, QK_ROPE:QK_ROPE + half, rows] = kr[half:]
        krt_ref[0, QK_ROPE + half:, rows] = kr[:half]


def _inproj(x2, mod3, g1, w_t, conv_w, gq, gkv, seq):
    t = x2.shape[0]
    nb = t // seq
    tm = ROW_TILE * INPROJ_SUB
    tiles_per_seq = seq // tm
    row = lambda i: (i, 0)
    const = lambda i: (0, 0)
    return pl.pallas_call(
        functools.partial(_inproj_kernel, tiles_per_seq),
        grid=(t // tm,),
        in_specs=[
            pl.BlockSpec((tm, D_MODEL), row),
            pl.BlockSpec((1, 1, 6 * D_MODEL), lambda i: (i // tiles_per_seq, 0, 0)),
            pl.BlockSpec((1, D_MODEL), const),
            pl.BlockSpec(w_t.shape, const),
            pl.BlockSpec((CONV_K, D_MODEL), const),
            pl.BlockSpec((1, Q_LORA), const),
            pl.BlockSpec((1, KV_LORA), const),
        ],
        out_specs=[
            pl.BlockSpec((tm, D_MODEL), row),
            pl.BlockSpec((tm, D_MODEL), row),
            pl.BlockSpec((tm, Q_LORA), row),
            pl.BlockSpec((tm, KV_LORA), row),
            pl.BlockSpec((1, 2 * QK_ROPE, tm),
                         lambda i: (i // tiles_per_seq, 0, i % tiles_per_seq)),
        ],
        out_shape=[
            jax.ShapeDtypeStruct((t, D_MODEL), BF16),
            jax.ShapeDtypeStruct((t, D_MODEL), BF16),
            jax.ShapeDtypeStruct((t, Q_LORA), BF16),
            jax.ShapeDtypeStruct((t, KV_LORA), BF16),
            jax.ShapeDtypeStruct((nb, 2 * QK_ROPE, seq), F32),
        ],
        scratch_shapes=[pltpu.VMEM((ROW_TILE + SUBLANES, D_MODEL), F32)],
        compiler_params=pltpu.CompilerParams(
            dimension_semantics=("arbitrary",), vmem_limit_bytes=VMEM_LIMIT),
        name="inproj",
    )(x2, mod3, g1, w_t, conv_w, gq, gkv)


def _attn_kernel(qn_ref, kvn_ref, krt_ref, cos_ref, sin_ref, wqt_ref, wkvt_ref, gq_ref, gk_ref, conv_ref, gm_ref,
                 o_ref, *scratch):
    per_head = len(scratch) // HEADS_PER_STEP
    heads = [_attn_head(hh, qn_ref, kvn_ref, krt_ref, cos_ref, sin_ref, wqt_ref, wkvt_ref, gq_ref, gk_ref,
                        conv_ref, gm_ref, o_ref, *scratch[hh * per_head:(hh + 1) * per_head])
             for hh in range(HEADS_PER_STEP)]
    nq = qn_ref.shape[0] // Q_TILE
    for scores, _ in heads:
        scores(0)
    for i in range(nq):
        for scores, finish in heads:
            if i + 1 < nq:
                scores(i + 1)
            finish(i)


def _attn_head(hh, qn_ref, kvn_ref, krt_ref, cos_ref, sin_ref, wqt_ref, wkvt_ref, gq_ref, gk_ref,
               conv_ref, gm_ref, o_ref, qt_s, k_s, vt_s, s_buf0, s_buf1):
    seq = qn_ref.shape[0]
    cos_t = jnp.concatenate([cos_ref[0], cos_ref[0]], axis=0)
    sin_t = jnp.concatenate([-sin_ref[0], sin_ref[0]], axis=0)

    def normed_rope(nope, r, rr, g, extra_scale):
        ss = jnp.sum(nope * nope, axis=0, keepdims=True) + jnp.sum(r * r, axis=0, keepdims=True)
        scale = lax.rsqrt(ss * (1.0 / QK_HEAD) + EPS) * extra_scale
        rope = r * g[QK_NOPE:QK_HEAD] * cos_t + rr * g[QK_HEAD:] * sin_t
        return (nope * g[0:QK_NOPE] * scale).astype(BF16), (rope * scale).astype(BF16)

    qt = _dot_nt(wqt_ref[hh], qn_ref[...])
    q_n, q_r = normed_rope(qt[0:QK_NOPE], qt[QK_NOPE:QK_HEAD], qt[QK_HEAD:], gq_ref[...],
                           QK_HEAD ** -0.5 * math.log2(math.e))
    qt_s[0:QK_NOPE, :] = q_n
    qt_s[QK_NOPE:QK_HEAD, :] = q_r
    qt_s[QK_HEAD:, :] = jnp.zeros((QK_ROPE, seq), BF16)

    kvt = _dot_nt(wkvt_ref[hh], kvn_ref[...])
    krt = krt_ref[0]
    k_n, k_r = normed_rope(kvt[0:QK_NOPE], krt[0:QK_ROPE], krt[QK_ROPE:], gk_ref[...], 1.0)
    kt = jnp.concatenate([k_n, k_r, jnp.zeros((QK_ROPE, seq), BF16)], axis=0)
    k_s[...] = kt.T
    vt_s[0:V_HEAD, :] = kvt[QK_NOPE:].astype(BF16)
    vt_s[V_HEAD:, :] = jnp.ones((vt_s.shape[0] - V_HEAD, seq), BF16)

    tq = Q_TILE
    hq = tq // 2
    kchunk = lax.broadcasted_iota(jnp.int32, (hq, tq), 0) // CHUNK
    qchunk = lax.broadcasted_iota(jnp.int32, (hq, tq), 1) // CHUNK
    head_ok = kchunk <= qchunk
    tail_ok = head_ok[:, 0:hq]
    neg = jnp.finfo(F32).min

    def scores(i):
        q0 = i * tq
        sb = s_buf0 if i % 2 == 0 else s_buf1
        q = qt_s[:, q0:q0 + tq]
        if i > 0:
            sb[0:q0, :] = _dot(k_s[0:q0, :], q)
        sb[q0:q0 + hq, :] = jnp.where(head_ok, _dot(k_s[q0:q0 + hq, :], q), neg)
        sb[q0 + hq:q0 + tq, hq:] = jnp.where(tail_ok, _dot(k_s[q0 + hq:q0 + tq, :], q[:, hq:]), neg)

    def finish(i):
        q0 = i * tq
        kmain = q0 + hq
        sb = s_buf0 if i % 2 == 0 else s_buf1
        tail = sb[kmain:kmain + hq, hq:]
        m_main = jnp.max(sb[0:kmain, :], axis=0, keepdims=True)
        m_tail = jnp.maximum(m_main[:, hq:], jnp.max(tail, axis=0, keepdims=True))
        m = jnp.concatenate([m_main[:, 0:hq], m_tail], axis=1)
        acc = _dot(vt_s[:, 0:kmain], jnp.exp2(sb[0:kmain, :] - m).astype(BF16))
        acc_tail = _dot(vt_s[:, kmain:kmain + hq], jnp.exp2(tail - m_tail).astype(BF16))
        acc = jnp.concatenate([acc[:, 0:hq], acc[:, hq:] + acc_tail], axis=1)
        o_t = acc[0:V_HEAD] / acc[V_HEAD:V_HEAD + 1]
        rows, cols = slice(q0, q0 + tq), slice(hh * V_HEAD, (hh + 1) * V_HEAD)
        y_mla = o_t.T.astype(BF16).astype(F32)
        merged = conv_ref[rows, cols].astype(F32) + gm_ref[rows, cols].astype(F32) * y_mla
        o_ref[rows, cols] = merged.astype(o_ref.dtype)

    return scores, finish


def _attn(qn, kvn, krt, cos, sin, wqt, wkvt, gq, gk, conv_p, gm, nb, seq):
    t = qn.shape[0]
    per_b = lambda b, h: (b, 0)
    per_b3 = lambda b, h: (b, 0, 0)
    per_h = lambda b, h: (h, 0, 0)
    const = lambda b, h: (0, 0)
    qk_rows = QK_NOPE + 2 * QK_ROPE
    return pl.pallas_call(
        _attn_kernel,
        grid=(nb, N_HEADS // HEADS_PER_STEP),
        in_specs=[
            pl.BlockSpec((seq, Q_LORA), per_b),
            pl.BlockSpec((seq, KV_LORA), per_b),
            pl.BlockSpec((1, 2 * QK_ROPE, seq), per_b3),
            pl.BlockSpec((1, QK_ROPE // 2, seq), per_b3),
            pl.BlockSpec((1, QK_ROPE // 2, seq), per_b3),
            pl.BlockSpec((HEADS_PER_STEP, qk_rows, Q_LORA), per_h),
            pl.BlockSpec((HEADS_PER_STEP, QK_NOPE + V_HEAD, KV_LORA), per_h),
            pl.BlockSpec((qk_rows, 1), const),
            pl.BlockSpec((qk_rows, 1), const),
            pl.BlockSpec((seq, HEADS_PER_STEP * V_HEAD), lambda b, h: (b, h)),
            pl.BlockSpec((seq, HEADS_PER_STEP * V_HEAD), lambda b, h: (b, h)),
        ],
        out_specs=pl.BlockSpec((seq, HEADS_PER_STEP * V_HEAD), lambda b, h: (b, h)),
        out_shape=jax.ShapeDtypeStruct((t, N_HEADS * V_HEAD), BF16),
        scratch_shapes=[
            pltpu.VMEM((qk_rows, seq), BF16),
            pltpu.VMEM((seq, qk_rows), BF16),
            pltpu.VMEM((V_HEAD + 2 * SUBLANES, seq), BF16),
            pltpu.VMEM((seq, Q_TILE), F32),
            pltpu.VMEM((seq, Q_TILE), F32),
        ] * HEADS_PER_STEP,
        compiler_params=pltpu.CompilerParams(
            dimension_semantics=("arbitrary", "arbitrary"), vmem_limit_bytes=VMEM_LIMIT),
        name="attn",
    )(qn, kvn, krt, cos, sin, wqt, wkvt, gq, gk, conv_p, gm)


def _oproj_kernel(merged_ref, x_ref, mod_ref, wo_ref, g2_ref, wr_ref, br_ref,
                  x1_ref, h2_ref, eid_ref, rank_ref, cnt_ref, cwt_ref, base):
    tm = x_ref.shape[0]

    @pl.when(pl.program_id(0) == 0)
    def _():
        base[...] = jnp.zeros(base.shape, F32)

    att = _dot(merged_ref[...], wo_ref[...])
    gate1 = mod_ref[0, :, 2 * D_MODEL:3 * D_MODEL]
    shift2 = mod_ref[0, :, 3 * D_MODEL:4 * D_MODEL]
    scale2 = mod_ref[0, :, 4 * D_MODEL:5 * D_MODEL]
    x1 = x_ref[...] + gate1 * att
    x1_ref[...] = x1
    xn = x1 * lax.rsqrt(jnp.mean(x1 * x1, axis=-1, keepdims=True) + EPS) * g2_ref[...]
    h2 = xn * (1.0 + scale2) + shift2
    _store_rows(h2_ref, h2, tm)

    lt = _dot_nt(wr_ref[...], h2.astype(BF16)) + br_ref[...]
    gl = [lt[N_EXPERTS + r:N_EXPERTS + r + 1, :] for r in range(N_GROUPS)]
    gmax = jnp.maximum(jnp.maximum(gl[0], gl[1]), jnp.maximum(gl[2], gl[3]))
    gidx = jnp.full(gmax.shape, N_GROUPS - 1, jnp.int32)
    for r in range(N_GROUPS - 2, -1, -1):
        gidx = jnp.where(gl[r] == gmax, r, gidx)
    gsum = jnp.exp(gl[0] - gmax)
    for r in range(1, N_GROUPS):
        gsum = gsum + jnp.exp(gl[r] - gmax)
    p_group = 1.0 / gsum
    es = lt[(N_GROUPS - 1) * EXPERTS_PER_GROUP:N_GROUPS * EXPERTS_PER_GROUP, :]
    for r in range(N_GROUPS - 2, -1, -1):
        es = jnp.where(gidx == r, lt[r * EXPERTS_PER_GROUP:(r + 1) * EXPERTS_PER_GROUP, :], es)
    row = lax.broadcasted_iota(jnp.int32, es.shape, 0)
    m1 = jnp.max(es, axis=0, keepdims=True)
    i1 = jnp.min(jnp.where(es == m1, row, EXPERTS_PER_GROUP), axis=0, keepdims=True)
    es2 = jnp.where(row == i1, -jnp.inf, es)
    m2 = jnp.max(es2, axis=0, keepdims=True)
    i2 = jnp.min(jnp.where(es2 == m2, row, EXPERTS_PER_GROUP), axis=0, keepdims=True)
    e2 = jnp.exp(m2 - m1)
    w1 = p_group / (1.0 + e2)
    w2 = w1 * e2
    eid0 = gidx * EXPERTS_PER_GROUP + i1
    eid1 = gidx * EXPERTS_PER_GROUP + i2
    eid_ref[0:1, :] = eid0
    eid_ref[1:2, :] = eid1
    erow = lax.broadcasted_iota(jnp.int32, (N_EXPERTS, tm), 0)
    oh0 = erow == eid0
    oh1 = erow == eid1
    both = jnp.where(oh0, 1.0, jnp.where(oh1, 1.0, 0.0))
    earlier = (lax.broadcasted_iota(jnp.int32, (tm, tm), 0)
               < lax.broadcasted_iota(jnp.int32, (tm, tm), 1))
    seen = base[...] + _dot(both.astype(BF16), jnp.where(earlier, 1.0, 0.0).astype(BF16))
    rank_ref[0:1, :] = jnp.sum(jnp.where(oh0, seen, 0.0), axis=0, keepdims=True).astype(jnp.int32)
    rank_ref[1:2, :] = jnp.sum(jnp.where(oh1, seen, 0.0), axis=0, keepdims=True).astype(jnp.int32)
    base[...] = base[...] + jnp.sum(both, axis=1, keepdims=True)
    cnt_ref[...] = base[...]
    wrow = lax.broadcasted_iota(jnp.int32, (LANES, tm), 0)
    wmat = jnp.where(wrow == 0, w1, jnp.where(wrow == 1, w2, 0.0))
    cwt_ref[...] = wmat.T


def _oproj(merged, x2, mod3, wo, g2, wr_t, br, seq):
    t = x2.shape[0]
    tm = OPROJ_TILE
    tiles_per_seq = seq // tm
    row = lambda i: (i, 0)
    const = lambda i: (0, 0)
    return pl.pallas_call(
        _oproj_kernel,
        grid=(t // tm,),
        in_specs=[
            pl.BlockSpec((tm, D_MODEL), row),
            pl.BlockSpec((tm, D_MODEL), row),
            pl.BlockSpec((1, 1, 6 * D_MODEL), lambda i: (i // tiles_per_seq, 0, 0)),
            pl.BlockSpec((D_MODEL, D_MODEL), const),
            pl.BlockSpec((1, D_MODEL), const),
            pl.BlockSpec((ROUTER_ROWS, D_MODEL), const),
            pl.BlockSpec((ROUTER_ROWS, 1), const),
        ],
        out_specs=[
            pl.BlockSpec((tm, D_MODEL), row),
            pl.BlockSpec((tm * ROWS_PER_TOKEN, LANES), row),
            pl.BlockSpec((2, tm), lambda i: (0, i)),
            pl.BlockSpec((2, tm), lambda i: (0, i)),
            pl.BlockSpec((N_EXPERTS, 1), const),
            pl.BlockSpec((tm, LANES), row),
        ],
        out_shape=[
            jax.ShapeDtypeStruct((t, D_MODEL), F32),
            jax.ShapeDtypeStruct((t * ROWS_PER_TOKEN, LANES), F32),
            jax.ShapeDtypeStruct((2, t), jnp.int32),
            jax.ShapeDtypeStruct((2, t), jnp.int32),
            jax.ShapeDtypeStruct((N_EXPERTS, 1), F32),
            jax.ShapeDtypeStruct((t, LANES), F32),
        ],
        scratch_shapes=[pltpu.VMEM((N_EXPERTS, 1), F32)],
        compiler_params=pltpu.CompilerParams(
            dimension_semantics=("arbitrary",), vmem_limit_bytes=VMEM_LIMIT),
        name="oproj",
    )(merged, x2, mod3, wo, g2, wr_t, br)


def _dispatch_kernel(n_tiles, ztile_ref, na_ref, pos_ref, h2_ref, xs_hbm, zbuf, sem, zsem):
    tm = h2_ref.shape[0] // ROWS_PER_TOKEN
    tile_rows = MOE_TILE * ROWS_PER_TOKEN

    def zero_tile(tile, on_sem):
        row = pl.multiple_of(tile * tile_rows, tile_rows)
        return pltpu.make_async_copy(zbuf, xs_hbm.at[pl.ds(row, tile_rows), :], on_sem)

    def unused_tiles(action):
        def body(k, carry):
            action(zero_tile(na_ref[0] + k, zsem))
            return carry
        lax.fori_loop(0, n_tiles - na_ref[0], body, 0)

    @pl.when(pl.program_id(0) == 0)
    def _():
        zbuf[...] = jnp.zeros(zbuf.shape, zbuf.dtype)
        for e in range(N_EXPERTS):
            zero_tile(ztile_ref[e], sem).start()
        for e in range(N_EXPERTS):
            zero_tile(ztile_ref[e], sem).wait()
        unused_tiles(lambda cp: cp.start())

    def body(c, carry):
        for k in range(DMA_UNROLL):
            t = c * DMA_UNROLL + k
            src = h2_ref.at[pl.ds(pl.multiple_of(t * ROWS_PER_TOKEN, ROWS_PER_TOKEN), ROWS_PER_TOKEN), :]
            for s in range(2):
                dst_row = pl.multiple_of(pos_ref[0, 0, s * tm + t] * ROWS_PER_TOKEN, ROWS_PER_TOKEN)
                pltpu.make_async_copy(src, xs_hbm.at[pl.ds(dst_row, ROWS_PER_TOKEN), :],
                                      sem).start(priority=s)
        return carry
    lax.fori_loop(0, tm // DMA_UNROLL, body, 0)
    for _ in range(2):
        pltpu.make_async_copy(h2_ref, xs_hbm.at[pl.ds(0, tm * ROWS_PER_TOKEN), :], sem).wait()

    @pl.when(pl.program_id(0) == pl.num_programs(0) - 1)
    def _():
        unused_tiles(lambda cp: cp.wait())


def _dispatch(ztile, na, pos3, h2r, n_tiles):
    tm = pos3.shape[2] // 2
    n_steps = pos3.shape[0]
    grid_spec = pltpu.PrefetchScalarGridSpec(
        num_scalar_prefetch=2,
        grid=(n_steps,),
        in_specs=[
            pl.BlockSpec((1, 1, 2 * tm), lambda i, z, n: (i, 0, 0), memory_space=pltpu.SMEM),
            pl.BlockSpec((tm * ROWS_PER_TOKEN, LANES), lambda i, z, n: (i, 0)),
        ],
        out_specs=pl.BlockSpec(memory_space=pl.ANY),
        scratch_shapes=[
            pltpu.VMEM((MOE_TILE * ROWS_PER_TOKEN, LANES), F32),
            pltpu.SemaphoreType.DMA(()),
            pltpu.SemaphoreType.DMA(()),
        ],
    )
    return pl.pallas_call(
        functools.partial(_dispatch_kernel, n_tiles),
        grid_spec=grid_spec,
        out_shape=jax.ShapeDtypeStruct((n_tiles * MOE_TILE * ROWS_PER_TOKEN, LANES), F32),
        compiler_params=pltpu.CompilerParams(
            dimension_semantics=("arbitrary",), vmem_limit_bytes=VMEM_LIMIT),
        name="dispatch",
    )(ztile, na, pos3, h2r)


def _moe_kernel(te_ref, na_ref, x_hbm, wg_ref, wu_ref, wd_ref, y_ref, xbuf, wgu_s, wd_s, sem):
    i = pl.program_id(0)
    n_active = na_ref[0]
    active = i < n_active
    tile_rows = MOE_TILE * ROWS_PER_TOKEN

    def fetch(tile):
        slot = tile % X_SLOTS
        row = pl.multiple_of(tile * tile_rows, tile_rows)
        return pltpu.make_async_copy(x_hbm.at[pl.ds(row, tile_rows), :], xbuf.at[slot], sem.at[slot])

    @pl.when(i == 0)
    def _():
        for k in range(X_SLOTS - 1):
            @pl.when(k < n_active)
            def _():
                fetch(k).start()

    @pl.when(i + X_SLOTS - 1 < n_active)
    def _():
        fetch(i + X_SLOTS - 1).start()

    new_expert = jnp.logical_or(i == 0, te_ref[i] != te_ref[jnp.maximum(i - 1, 0)])

    @pl.when(jnp.logical_and(active, new_expert))
    def _():
        wgu_s[:, 0:D_EXPERT] = wg_ref[0].astype(BF16)
        wgu_s[:, D_EXPERT:] = wu_ref[0].astype(BF16)
        wd_s[...] = wd_ref[0].astype(BF16)

    @pl.when(active)
    def _():
        fetch(i).wait()
        x = jnp.concatenate([col.astype(BF16) for col in _load_rows(xbuf.at[i % X_SLOTS], MOE_TILE)], axis=1)
        gu = _dot(x, wgu_s[...])
        g = gu[:, 0:D_EXPERT]
        a = (g * _sigmoid(g)) * gu[:, D_EXPERT:]
        y = _dot(a.astype(BF16), wd_s[...])
        _store_rows(y_ref, y, MOE_TILE)

    @pl.when(jnp.logical_not(active))
    def _():
        y_ref[...] = jnp.zeros(y_ref.shape, y_ref.dtype)


def _moe(te, na, xs, wg, wu, wd):
    nt = te.shape[0]
    tile_rows = MOE_TILE * ROWS_PER_TOKEN
    wspec = lambda shape: pl.BlockSpec((1,) + shape, lambda i, te_r, na_r: (te_r[i], 0, 0))
    grid_spec = pltpu.PrefetchScalarGridSpec(
        num_scalar_prefetch=2,
        grid=(nt,),
        in_specs=[
            pl.BlockSpec(memory_space=pl.ANY),
            wspec((D_MODEL, D_EXPERT)),
            wspec((D_MODEL, D_EXPERT)),
            wspec((D_EXPERT, D_MODEL)),
        ],
        out_specs=pl.BlockSpec((tile_rows, LANES), lambda i, te_r, na_r: (i, 0)),
        scratch_shapes=[
            pltpu.VMEM((X_SLOTS, tile_rows, LANES), F32),
            pltpu.VMEM((D_MODEL, 2 * D_EXPERT), BF16),
            pltpu.VMEM((D_EXPERT, D_MODEL), BF16),
            pltpu.SemaphoreType.DMA((X_SLOTS,)),
        ],
    )
    return pl.pallas_call(
        _moe_kernel,
        grid_spec=grid_spec,
        out_shape=jax.ShapeDtypeStruct(xs.shape, xs.dtype),
        compiler_params=pltpu.CompilerParams(
            dimension_semantics=("arbitrary",), vmem_limit_bytes=VMEM_LIMIT),
        name="moe",
    )(te, na, xs, wg, wu, wd)


def _comb_kernel(pos0_ref, posn_ref, x1_ref, cwt_ref, mod_ref, y_hbm, o_ref, ybuf, sem):
    tm = x1_ref.shape[0]
    i = pl.program_id(0)
    slot = i % 2

    def gather(pos_ref, dst_slot):
        def body(c, carry):
            for k in range(DMA_UNROLL):
                r = c * DMA_UNROLL + k
                src_row = pl.multiple_of(pos_ref[0, 0, r] * ROWS_PER_TOKEN, ROWS_PER_TOKEN)
                dst_row = pl.multiple_of(r * ROWS_PER_TOKEN, ROWS_PER_TOKEN)
                pltpu.make_async_copy(y_hbm.at[pl.ds(src_row, ROWS_PER_TOKEN), :],
                                      ybuf.at[dst_slot, pl.ds(dst_row, ROWS_PER_TOKEN), :],
                                      sem.at[dst_slot]).start(priority=k % 2)
            return carry
        lax.fori_loop(0, 2 * tm // DMA_UNROLL, body, 0)

    @pl.when(i == 0)
    def _():
        gather(pos0_ref, 0)

    @pl.when(i + 1 < pl.num_programs(0))
    def _():
        gather(posn_ref, 1 - slot)

    pltpu.make_async_copy(y_hbm.at[pl.ds(0, 2 * tm * ROWS_PER_TOKEN), :], ybuf.at[slot],
                          sem.at[slot]).wait()
    c0 = cwt_ref[:, 0:1]
    c1 = cwt_ref[:, 1:2]
    y0 = _load_rows(ybuf.at[slot], tm)
    y1 = _load_rows(ybuf.at[slot], tm, offset=tm * ROWS_PER_TOKEN)
    for j in range(ROWS_PER_TOKEN):
        cols = slice(j * LANES, (j + 1) * LANES)
        gate2 = mod_ref[0, :, 5 * D_MODEL + j * LANES:5 * D_MODEL + (j + 1) * LANES]
        o_ref[:, cols] = x1_ref[:, cols] + gate2 * (c0 * y0[j] + c1 * y1[j])


def _comb(x1, yr, pos3, cwt, mod3, seq):
    t = x1.shape[0]
    tm = ROW_TILE
    tiles_per_seq = seq // tm
    n_steps = t // tm
    row = lambda i: (i, 0)
    smem_blk = lambda f: pl.BlockSpec((1, 1, 2 * tm), f, memory_space=pltpu.SMEM)
    return pl.pallas_call(
        _comb_kernel,
        grid=(n_steps,),
        in_specs=[
            smem_blk(lambda i: (0, 0, 0)),
            smem_blk(lambda i: (jnp.minimum(i + 1, n_steps - 1), 0, 0)),
            pl.BlockSpec((tm, D_MODEL), row),
            pl.BlockSpec((tm, LANES), row),
            pl.BlockSpec((1, 1, 6 * D_MODEL), lambda i: (i // tiles_per_seq, 0, 0)),
            pl.BlockSpec(memory_space=pl.ANY),
        ],
        out_specs=pl.BlockSpec((tm, D_MODEL), row),
        out_shape=jax.ShapeDtypeStruct((t, D_MODEL), F32),
        scratch_shapes=[
            pltpu.VMEM((2, 2 * tm * ROWS_PER_TOKEN, LANES), F32),
            pltpu.SemaphoreType.DMA((2,)),
        ],
        compiler_params=pltpu.CompilerParams(
            dimension_semantics=("arbitrary",), vmem_limit_bytes=VMEM_LIMIT),
        name="comb",
    )(pos3, pos3, x1, cwt, mod3, yr)


def _route_plan(eid, rank, cnt, n_tok):
    n_tiles = (2 * n_tok) // MOE_TILE + N_EXPERTS
    experts = jnp.arange(N_EXPERTS, dtype=jnp.int32)
    counts = cnt.reshape(N_EXPERTS).astype(jnp.int32)
    ntile = (counts + MOE_TILE - 1) // MOE_TILE
    tend = jnp.cumsum(ntile)
    tstart = tend - ntile
    n_active = tend[-1]
    tj = jnp.arange(n_tiles, dtype=jnp.int32)
    te_raw = jnp.minimum(jnp.sum((tj[:, None] >= tend[None, :]).astype(jnp.int32), axis=1),
                         N_EXPERTS - 1)
    te_last = jnp.sum(jnp.where(tj == n_active - 1, te_raw, 0))
    te = jnp.where(tj < n_active, te_raw, te_last).astype(jnp.int32)
    first_row = jnp.sum(jnp.where(eid[:, :, None] == experts[None, None, :],
                                  (tstart * MOE_TILE)[None, None, :], 0), axis=-1)
    pos = (first_row + rank).astype(jnp.int32)
    ztile = jnp.maximum(tend - 1, 0).astype(jnp.int32)
    pos3 = pos.reshape(2, n_tok // ROW_TILE, ROW_TILE).transpose(1, 0, 2).reshape(
        n_tok // ROW_TILE, 1, 2 * ROW_TILE)
    return te, n_active.reshape(1).astype(jnp.int32), ztile, pos3, n_tiles


def _rotate_half_cols(w):
    half = QK_ROPE // 2
    return jnp.concatenate([w[..., half:], w[..., :half]], axis=-1)


def kernel(x, c, positions, w_ada, b_ada, norm1_g, w_in, conv_w, q_a_norm_g, w_q_b, kv_a_norm_g, w_kv_b, q_norm_g, k_norm_g, w_o, norm2_g, w_router_group, b_router_group, w_router_expert, b_router_expert, w_exp_gate, w_exp_up, w_exp_down):
    nb, seq, d = x.shape
    depth = w_ada.shape[0]
    n_tok = nb * seq
    assert d == D_MODEL and seq % ROW_TILE == 0 and seq % Q_TILE == 0 and Q_TILE % CHUNK == 0
    assert seq % OPROJ_TILE == 0
    assert (2 * n_tok) % MOE_TILE == 0

    inv = ROPE_BASE ** (-jnp.arange(0, QK_ROPE, 2, dtype=F32) / QK_ROPE)
    ang = inv[None, :, None] * positions.astype(F32)[:, None, :]
    cos, sin = jnp.cos(ang), jnp.sin(ang)

    x2 = x.reshape(n_tok, d)
    for l in range(depth):
        w_t = w_in[l].T.astype(BF16)
        wq3 = w_q_b[l].reshape(Q_LORA, N_HEADS, QK_HEAD)
        wq = jnp.concatenate([wq3, _rotate_half_cols(wq3[..., QK_NOPE:])], axis=-1)
        wqt = wq.transpose(1, 2, 0).astype(BF16)
        wkvt = w_kv_b[l].reshape(KV_LORA, N_HEADS, QK_NOPE + V_HEAD).transpose(1, 2, 0).astype(BF16)
        gq = jnp.concatenate([q_norm_g[l], _rotate_half_cols(q_norm_g[l][QK_NOPE:])]).reshape(-1, 1)
        gk = jnp.concatenate([k_norm_g[l], _rotate_half_cols(k_norm_g[l][QK_NOPE:])]).reshape(-1, 1)
        wr_t = jnp.concatenate(
            [w_router_expert[l].T, w_router_group[l].T,
             jnp.zeros((ROUTER_ROWS - N_EXPERTS - N_GROUPS, d), F32)], axis=0).astype(BF16)
        br = jnp.concatenate(
            [b_router_expert[l], b_router_group[l],
             jnp.zeros((ROUTER_ROWS - N_EXPERTS - N_GROUPS,), F32)]).reshape(ROUTER_ROWS, 1)

        mod3 = _ada(c, w_ada[l], b_ada[l]).reshape(nb, 1, 6 * d)
        conv_p, gm, qn, kvn, krt = _inproj(
            x2, mod3, norm1_g[l].reshape(1, d), w_t, conv_w[l],
            q_a_norm_g[l].reshape(1, -1), kv_a_norm_g[l].reshape(1, -1), seq)
        merged = _attn(qn, kvn, krt, cos, sin, wqt, wkvt, gq, gk, conv_p, gm, nb, seq)
        x1, h2r, eid, rank, cnt, cwt = _oproj(merged, x2, mod3, w_o[l].astype(BF16),
                                              norm2_g[l].reshape(1, d), wr_t, br, seq)
        te, na, ztile, pos3, n_tiles = _route_plan(eid, rank, cnt, n_tok)
        xs = _dispatch(ztile, na, pos3, h2r, n_tiles)
        yr = _moe(te, na, xs,
                  w_exp_gate[l].reshape(N_EXPERTS, d, D_EXPERT),
                  w_exp_up[l].reshape(N_EXPERTS, d, D_EXPERT),
                  w_exp_down[l].reshape(N_EXPERTS, D_EXPERT, d))
        x2 = _comb(x1, yr, pos3, cwt, mod3, seq)
    return x2.reshape(nb, seq, d)
```

```python
import functools
import math

import jax
import jax.numpy as jnp
from jax import lax
from jax.experimental import pallas as pl
from jax.experimental.pallas import tpu as pltpu

F32 = jnp.float32
BF16 = jnp.bfloat16

D_MODEL = 1024
N_HEADS = 8
QK_NOPE = 128
QK_ROPE = 64
QK_HEAD = QK_NOPE + QK_ROPE
V_HEAD = 128
Q_LORA = 384
KV_LORA = 256
CHUNK = 64
EPS = 1e-6
ROPE_BASE = 10000.0
N_GROUPS = 4
EXPERTS_PER_GROUP = 8
N_EXPERTS = N_GROUPS * EXPERTS_PER_GROUP
D_EXPERT = 256
CONV_K = 3

LANES = 128
SUBLANES = 8
VMEM_LIMIT = 56 * 1024 * 1024

ROW_TILE = 512
Q_TILE = 512
INPROJ_SUB = 2
OPROJ_TILE = 1024
HEADS_PER_STEP = 2
MOE_TILE = 512
X_SLOTS = 4
ADA_COLS = 1536
ROWS_PER_TOKEN = D_MODEL // LANES
DMA_UNROLL = 8
ROUTER_ROWS = 40


def _sigmoid(v):
    return 1.0 / (1.0 + jnp.exp(-v))


def _dot(a, b):
    return jnp.dot(a, b, preferred_element_type=F32)


def _store_rows(ref, val, n):
    for j in range(ROWS_PER_TOKEN):
        ref[pl.ds(j, n, stride=ROWS_PER_TOKEN), :] = val[:, j * LANES:(j + 1) * LANES]


def _load_rows(ref, n, offset=0):
    return [ref[pl.ds(offset + j, n, stride=ROWS_PER_TOKEN), :] for j in range(ROWS_PER_TOKEN)]


def _dot_nt(a, b):
    return lax.dot_general(a, b, (((1,), (1,)), ((), ())), preferred_element_type=F32)


def _ada_kernel(c_ref, w_ref, b_ref, o_ref):
    c = c_ref[...]
    act = (c * _sigmoid(c)).astype(BF16)
    o_ref[...] = _dot(act, w_ref[...].astype(BF16)) + b_ref[...]


def _ada(c, w_ada, b_ada):
    nb, d = c.shape
    n = w_ada.shape[1]
    return pl.pallas_call(
        _ada_kernel,
        grid=(n // ADA_COLS,),
        in_specs=[
            pl.BlockSpec((nb, d), lambda j: (0, 0)),
            pl.BlockSpec((d, ADA_COLS), lambda j: (0, j)),
            pl.BlockSpec((1, ADA_COLS), lambda j: (0, j)),
        ],
        out_specs=pl.BlockSpec((nb, ADA_COLS), lambda j: (0, j)),
        out_shape=jax.ShapeDtypeStruct((nb, n), F32),
        compiler_params=pltpu.CompilerParams(
            dimension_semantics=("arbitrary",), vmem_limit_bytes=VMEM_LIMIT),
        name="ada",
    )(c, w_ada, b_ada.reshape(1, n))


O_LAT = 3 * D_MODEL
O_KR = O_LAT + Q_LORA + KV_LORA
O_GATE = O_KR + QK_ROPE


def _inproj_kernel(tiles_per_seq, x_ref, mod_ref, g1_ref, w_ref, cw_ref, gq_ref, gkv_ref,
                   conv_ref, gm_ref, qn_ref, kvn_ref, krt_ref, ubuf):
    tm = x_ref.shape[0] // INPROJ_SUB

    @pl.when(pl.program_id(0) % tiles_per_seq == 0)
    def _():
        ubuf[0:SUBLANES, :] = jnp.zeros((SUBLANES, D_MODEL), F32)

    shift = mod_ref[0, :, 0:D_MODEL]
    scale = mod_ref[0, :, D_MODEL:2 * D_MODEL]
    for sub in range(INPROJ_SUB):
        rows = slice(sub * tm, (sub + 1) * tm)
        x = x_ref[rows, :]
        xn = x * lax.rsqrt(jnp.mean(x * x, axis=-1, keepdims=True) + EPS) * g1_ref[...]
        h = (xn * (1.0 + scale) + shift).astype(BF16)

        def proj(lo, width):
            return _dot_nt(h, w_ref[lo:lo + width, :])

        u = proj(2 * D_MODEL, D_MODEL) * proj(0, D_MODEL)
        ubuf[SUBLANES:SUBLANES + tm, :] = u
        conv = (ubuf[SUBLANES - 2:SUBLANES - 2 + tm, :] * cw_ref[0:1, :]
                + ubuf[SUBLANES - 1:SUBLANES - 1 + tm, :] * cw_ref[1:2, :]
                + u * cw_ref[2:3, :])
        ubuf[0:SUBLANES, :] = ubuf[tm:tm + SUBLANES, :]
        y_conv = proj(D_MODEL, D_MODEL) * conv
        conv_ref[rows, :] = (_sigmoid(proj(O_GATE, D_MODEL)) * y_conv).astype(BF16)
        gm_ref[rows, :] = _sigmoid(proj(O_GATE + D_MODEL, D_MODEL)).astype(BF16)

        ql = proj(O_LAT, Q_LORA)
        qn_ref[rows, :] = (ql * lax.rsqrt(jnp.mean(ql * ql, axis=-1, keepdims=True) + EPS)
                           * gq_ref[...]).astype(BF16)
        kl = proj(O_LAT + Q_LORA, KV_LORA)
        kvn_ref[rows, :] = (kl * lax.rsqrt(jnp.mean(kl * kl, axis=-1, keepdims=True) + EPS)
                            * gkv_ref[...]).astype(BF16)
        kr = _dot_nt(w_ref[O_KR:O_GATE, :], h)
        half = QK_ROPE // 2
        krt_ref[0, 0:QK_ROPE, rows] = kr
        krt_ref[0, QK_ROPE:QK_ROPE + half, rows] = kr[half:]
        krt_ref[0, QK_ROPE + half:, rows] = kr[:half]


def _inproj(x2, mod3, g1, w_t, conv_w, gq, gkv, seq):
    t = x2.shape[0]
    nb = t // seq
    tm = ROW_TILE * INPROJ_SUB
    tiles_per_seq = seq // tm
    row = lambda i: (i, 0)
    const = lambda i: (0, 0)
    return pl.pallas_call(
        functools.partial(_inproj_kernel, tiles_per_seq),
        grid=(t // tm,),
        in_specs=[
            pl.BlockSpec((tm, D_MODEL), row),
            pl.BlockSpec((1, 1, 6 * D_MODEL), lambda i: (i // tiles_per_seq, 0, 0)),
            pl.BlockSpec((1, D_MODEL), const),
            pl.BlockSpec(w_t.shape, const),
            pl.BlockSpec((CONV_K, D_MODEL), const),
            pl.BlockSpec((1, Q_LORA), const),
            pl.BlockSpec((1, KV_LORA), const),
        ],
        out_specs=[
            pl.BlockSpec((tm, D_MODEL), row),
            pl.BlockSpec((tm, D_MODEL), row),
            pl.BlockSpec((tm, Q_LORA), row),
            pl.BlockSpec((tm, KV_LORA), row),
            pl.BlockSpec((1, 2 * QK_ROPE, tm),
                         lambda i: (i // tiles_per_seq, 0, i % tiles_per_seq)),
        ],
        out_shape=[
            jax.ShapeDtypeStruct((t, D_MODEL), BF16),
            jax.ShapeDtypeStruct((t, D_MODEL), BF16),
            jax.ShapeDtypeStruct((t, Q_LORA), BF16),
            jax.ShapeDtypeStruct((t, KV_LORA), BF16),
            jax.ShapeDtypeStruct((nb, 2 * QK_ROPE, seq), F32),
        ],
        scratch_shapes=[pltpu.VMEM((ROW_TILE + SUBLANES, D_MODEL), F32)],
        compiler_params=pltpu.CompilerParams(
            dimension_semantics=("arbitrary",), vmem_limit_bytes=VMEM_LIMIT),
        name="inproj",
    )(x2, mod3, g1, w_t, conv_w, gq, gkv)


def _attn_kernel(qn_ref, kvn_ref, krt_ref, cos_ref, sin_ref, wqt_ref, wkvt_ref, gq_ref, gk_ref, conv_ref, gm_ref,
                 o_ref, *scratch):
    per_head = len(scratch) // HEADS_PER_STEP
    heads = [_attn_head(hh, qn_ref, kvn_ref, krt_ref, cos_ref, sin_ref, wqt_ref, wkvt_ref, gq_ref, gk_ref,
                        conv_ref, gm_ref, o_ref, *scratch[hh * per_head:(hh + 1) * per_head])
             for hh in range(HEADS_PER_STEP)]
    nq = qn_ref.shape[0] // Q_TILE
    for scores, _ in heads:
        scores(0)
    for i in range(nq):
        for scores, finish in heads:
            if i + 1 < nq:
                scores(i + 1)
            finish(i)


def _attn_head(hh, qn_ref, kvn_ref, krt_ref, cos_ref, sin_ref, wqt_ref, wkvt_ref, gq_ref, gk_ref,
               conv_ref, gm_ref, o_ref, qt_s, k_s, vt_s, s_buf0, s_buf1):
    seq = qn_ref.shape[0]
    cos_t = jnp.concatenate([cos_ref[0], cos_ref[0]], axis=0)
    sin_t = jnp.concatenate([-sin_ref[0], sin_ref[0]], axis=0)

    def normed_rope(nope, r, rr, g, extra_scale):
        ss = jnp.sum(nope * nope, axis=0, keepdims=True) + jnp.sum(r * r, axis=0, keepdims=True)
        scale = lax.rsqrt(ss * (1.0 / QK_HEAD) + EPS) * extra_scale
        rope = r * g[QK_NOPE:QK_HEAD] * cos_t + rr * g[QK_HEAD:] * sin_t
        return (nope * g[0:QK_NOPE] * scale).astype(BF16), (rope * scale).astype(BF16)

    qt = _dot_nt(wqt_ref[hh], qn_ref[...])
    q_n, q_r = normed_rope(qt[0:QK_NOPE], qt[QK_NOPE:QK_HEAD], qt[QK_HEAD:], gq_ref[...],
                           QK_HEAD ** -0.5 * math.log2(math.e))
    qt_s[0:QK_NOPE, :] = q_n
    qt_s[QK_NOPE:QK_HEAD, :] = q_r
    qt_s[QK_HEAD:, :] = jnp.zeros((QK_ROPE, seq), BF16)

    kvt = _dot_nt(wkvt_ref[hh], kvn_ref[...])
    krt = krt_ref[0]
    k_n, k_r = normed_rope(kvt[0:QK_NOPE], krt[0:QK_ROPE], krt[QK_ROPE:], gk_ref[...], 1.0)
    kt = jnp.concatenate([k_n, k_r, jnp.zeros((QK_ROPE, seq), BF16)], axis=0)
    k_s[...] = kt.T
    vt_s[0:V_HEAD, :] = kvt[QK_NOPE:].astype(BF16)
    vt_s[V_HEAD:, :] = jnp.ones((vt_s.shape[0] - V_HEAD, seq), BF16)

    tq = Q_TILE
    hq = tq // 2
    kchunk = lax.broadcasted_iota(jnp.int32, (hq, tq), 0) // CHUNK
    qchunk = lax.broadcasted_iota(jnp.int32, (hq, tq), 1) // CHUNK
    head_ok = kchunk <= qchunk
    tail_ok = head_ok[:, 0:hq]
    neg = jnp.finfo(F32).min

    def scores(i):
        q0 = i * tq
        sb = s_buf0 if i % 2 == 0 else s_buf1
        q = qt_s[:, q0:q0 + tq]
        if i > 0:
            sb[0:q0, :] = _dot(k_s[0:q0, :], q)
        sb[q0:q0 + hq, :] = jnp.where(head_ok, _dot(k_s[q0:q0 + hq, :], q), neg)
        sb[q0 + hq:q0 + tq, hq:] = jnp.where(tail_ok, _dot(k_s[q0 + hq:q0 + tq, :], q[:, hq:]), neg)

    def finish(i):
        q0 = i * tq
        kmain = q0 + hq
        sb = s_buf0 if i % 2 == 0 else s_buf1
        tail = sb[kmain:kmain + hq, hq:]
        m_main = jnp.max(sb[0:kmain, :], axis=0, keepdims=True)
        m_tail = jnp.maximum(m_main[:, hq:], jnp.max(tail, axis=0, keepdims=True))
        m = jnp.concatenate([m_main[:, 0:hq], m_tail], axis=1)
        acc = _dot(vt_s[:, 0:kmain], jnp.exp2(sb[0:kmain, :] - m).astype(BF16))
        acc_tail = _dot(vt_s[:, kmain:kmain + hq], jnp.exp2(tail - m_tail).astype(BF16))
        acc = jnp.concatenate([acc[:, 0:hq], acc[:, hq:] + acc_tail], axis=1)
        o_t = acc[0:V_HEAD] / acc[V_HEAD:V_HEAD + 1]
        rows, cols = slice(q0, q0 + tq), slice(hh * V_HEAD, (hh + 1) * V_HEAD)
        y_mla = o_t.T.astype(BF16).astype(F32)
        merged = conv_ref[rows, cols].astype(F32) + gm_ref[rows, cols].astype(F32) * y_mla
        o_ref[rows, cols] = merged.astype(o_ref.dtype)

    return scores, finish


def _attn(qn, kvn, krt, cos, sin, wqt, wkvt, gq, gk, conv_p, gm, nb, seq):
    t = qn.shape[0]
    per_b = lambda b, h: (b, 0)
    per_b3 = lambda b, h: (b, 0, 0)
    per_h = lambda b, h: (h, 0, 0)
    const = lambda b, h: (0, 0)
    qk_rows = QK_NOPE + 2 * QK_ROPE
    return pl.pallas_call(
        _attn_kernel,
        grid=(nb, N_HEADS // HEADS_PER_STEP),
        in_specs=[
            pl.BlockSpec((seq, Q_LORA), per_b),
            pl.BlockSpec((seq, KV_LORA), per_b),
            pl.BlockSpec((1, 2 * QK_ROPE, seq), per_b3),
            pl.BlockSpec((1, QK_ROPE // 2, seq), per_b3),
            pl.BlockSpec((1, QK_ROPE // 2, seq), per_b3),
            pl.BlockSpec((HEADS_PER_STEP, qk_rows, Q_LORA), per_h),
            pl.BlockSpec((HEADS_PER_STEP, QK_NOPE + V_HEAD, KV_LORA), per_h),
            pl.BlockSpec((qk_rows, 1), const),
            pl.BlockSpec((qk_rows, 1), const),
            pl.BlockSpec((seq, HEADS_PER_STEP * V_HEAD), lambda b, h: (b, h)),
            pl.BlockSpec((seq, HEADS_PER_STEP * V_HEAD), lambda b, h: (b, h)),
        ],
        out_specs=pl.BlockSpec((seq, HEADS_PER_STEP * V_HEAD), lambda b, h: (b, h)),
        out_shape=jax.ShapeDtypeStruct((t, N_HEADS * V_HEAD), BF16),
        scratch_shapes=[
            pltpu.VMEM((qk_rows, seq), BF16),
            pltpu.VMEM((seq, qk_rows), BF16),
            pltpu.VMEM((V_HEAD + 2 * SUBLANES, seq), BF16),
            pltpu.VMEM((seq, Q_TILE), F32),
            pltpu.VMEM((seq, Q_TILE), F32),
        ] * HEADS_PER_STEP,
        compiler_params=pltpu.CompilerParams(
            dimension_semantics=("arbitrary", "arbitrary"), vmem_limit_bytes=VMEM_LIMIT),
        name="attn",
    )(qn, kvn, krt, cos, sin, wqt, wkvt, gq, gk, conv_p, gm)


def _oproj_kernel(merged_ref, x_ref, mod_ref, wo_ref, g2_ref, wr_ref, br_ref,
                  x1_ref, h2_ref, eid_ref, rank_ref, cnt_ref, cw_ref, base):
    tm = x_ref.shape[0]

    @pl.when(pl.program_id(0) == 0)
    def _():
        base[...] = jnp.zeros(base.shape, F32)

    att = _dot(merged_ref[...], wo_ref[...])
    gate1 = mod_ref[0, :, 2 * D_MODEL:3 * D_MODEL]
    shift2 = mod_ref[0, :, 3 * D_MODEL:4 * D_MODEL]
    scale2 = mod_ref[0, :, 4 * D_MODEL:5 * D_MODEL]
    x1 = x_ref[...] + gate1 * att
    x1_ref[...] = x1
    xn = x1 * lax.rsqrt(jnp.mean(x1 * x1, axis=-1, keepdims=True) + EPS) * g2_ref[...]
    h2 = xn * (1.0 + scale2) + shift2
    _store_rows(h2_ref, h2, tm)

    lt = _dot_nt(wr_ref[...], h2.astype(BF16)) + br_ref[...]
    gl = [lt[N_EXPERTS + r:N_EXPERTS + r + 1, :] for r in range(N_GROUPS)]
    gmax = jnp.maximum(jnp.maximum(gl[0], gl[1]), jnp.maximum(gl[2], gl[3]))
    gidx = jnp.full(gmax.shape, N_GROUPS - 1, jnp.int32)
    for r in range(N_GROUPS - 2, -1, -1):
        gidx = jnp.where(gl[r] == gmax, r, gidx)
    gsum = jnp.exp(gl[0] - gmax)
    for r in range(1, N_GROUPS):
        gsum = gsum + jnp.exp(gl[r] - gmax)
    p_group = 1.0 / gsum
    es = lt[(N_GROUPS - 1) * EXPERTS_PER_GROUP:N_GROUPS * EXPERTS_PER_GROUP, :]
    for r in range(N_GROUPS - 2, -1, -1):
        es = jnp.where(gidx == r, lt[r * EXPERTS_PER_GROUP:(r + 1) * EXPERTS_PER_GROUP, :], es)
    row = lax.broadcasted_iota(jnp.int32, es.shape, 0)
    m1 = jnp.max(es, axis=0, keepdims=True)
    i1 = jnp.min(jnp.where(es == m1, row, EXPERTS_PER_GROUP), axis=0, keepdims=True)
    es2 = jnp.where(row == i1, -jnp.inf, es)
    m2 = jnp.max(es2, axis=0, keepdims=True)
    i2 = jnp.min(jnp.where(es2 == m2, row, EXPERTS_PER_GROUP), axis=0, keepdims=True)
    e2 = jnp.exp(m2 - m1)
    w1 = p_group / (1.0 + e2)
    w2 = w1 * e2
    eid0 = gidx * EXPERTS_PER_GROUP + i1
    eid1 = gidx * EXPERTS_PER_GROUP + i2
    eid_ref[0:1, :] = eid0
    eid_ref[1:2, :] = eid1
    erow = lax.broadcasted_iota(jnp.int32, (N_EXPERTS, tm), 0)
    oh0 = erow == eid0
    oh1 = erow == eid1
    both = jnp.where(oh0, 1.0, jnp.where(oh1, 1.0, 0.0))
    earlier = (lax.broadcasted_iota(jnp.int32, (tm, tm), 0)
               < lax.broadcasted_iota(jnp.int32, (tm, tm), 1))
    seen = base[...] + _dot(both.astype(BF16), jnp.where(earlier, 1.0, 0.0).astype(BF16))
    rank_ref[0:1, :] = jnp.sum(jnp.where(oh0, seen, 0.0), axis=0, keepdims=True).astype(jnp.int32)
    rank_ref[1:2, :] = jnp.sum(jnp.where(oh1, seen, 0.0), axis=0, keepdims=True).astype(jnp.int32)
    base[...] = base[...] + jnp.sum(both, axis=1, keepdims=True)
    cnt_ref[...] = base[...]
    cw_ref[0:1, :] = w1
    cw_ref[1:2, :] = w2


def _oproj(merged, x2, mod3, wo, g2, wr_t, br, seq):
    t = x2.shape[0]
    tm = OPROJ_TILE
    tiles_per_seq = seq // tm
    row = lambda i: (i, 0)
    const = lambda i: (0, 0)
    return pl.pallas_call(
        _oproj_kernel,
        grid=(t // tm,),
        in_specs=[
            pl.BlockSpec((tm, D_MODEL), row),
            pl.BlockSpec((tm, D_MODEL), row),
            pl.BlockSpec((1, 1, 6 * D_MODEL), lambda i: (i // tiles_per_seq, 0, 0)),
            pl.BlockSpec((D_MODEL, D_MODEL), const),
            pl.BlockSpec((1, D_MODEL), const),
            pl.BlockSpec((ROUTER_ROWS, D_MODEL), const),
            pl.BlockSpec((ROUTER_ROWS, 1), const),
        ],
        out_specs=[
            pl.BlockSpec((tm, D_MODEL), row),
            pl.BlockSpec((tm * ROWS_PER_TOKEN, LANES), row),
            pl.BlockSpec((2, tm), lambda i: (0, i)),
            pl.BlockSpec((2, tm), lambda i: (0, i)),
            pl.BlockSpec((N_EXPERTS, 1), const),
            pl.BlockSpec((2, tm), lambda i: (0, i)),
        ],
        out_shape=[
            jax.ShapeDtypeStruct((t, D_MODEL), F32),
            jax.ShapeDtypeStruct((t * ROWS_PER_TOKEN, LANES), F32),
            jax.ShapeDtypeStruct((2, t), jnp.int32),
            jax.ShapeDtypeStruct((2, t), jnp.int32),
            jax.ShapeDtypeStruct((N_EXPERTS, 1), F32),
            jax.ShapeDtypeStruct((2, t), F32),
        ],
        scratch_shapes=[pltpu.VMEM((N_EXPERTS, 1), F32)],
        compiler_params=pltpu.CompilerParams(
            dimension_semantics=("arbitrary",), vmem_limit_bytes=VMEM_LIMIT),
        name="oproj",
    )(merged, x2, mod3, wo, g2, wr_t, br)


def _dispatch_kernel(n_tiles, ztile_ref, na_ref, pos_ref, h2_ref, xs_hbm, zbuf, sem, zsem):
    tm = h2_ref.shape[0] // ROWS_PER_TOKEN
    tile_rows = MOE_TILE * ROWS_PER_TOKEN

    def zero_tile(tile, on_sem):
        row = pl.multiple_of(tile * tile_rows, tile_rows)
        return pltpu.make_async_copy(zbuf, xs_hbm.at[pl.ds(row, tile_rows), :], on_sem)

    def unused_tiles(action):
        def body(k, carry):
            action(zero_tile(na_ref[0] + k, zsem))
            return carry
        lax.fori_loop(0, n_tiles - na_ref[0], body, 0)

    @pl.when(pl.program_id(0) == 0)
    def _():
        zbuf[...] = jnp.zeros(zbuf.shape, zbuf.dtype)
        for e in range(N_EXPERTS):
            zero_tile(ztile_ref[e], sem).start()
        for e in range(N_EXPERTS):
            zero_tile(ztile_ref[e], sem).wait()
        unused_tiles(lambda cp: cp.start())

    def body(c, carry):
        for k in range(DMA_UNROLL):
            t = c * DMA_UNROLL + k
            src = h2_ref.at[pl.ds(pl.multiple_of(t * ROWS_PER_TOKEN, ROWS_PER_TOKEN), ROWS_PER_TOKEN), :]
            for s in range(2):
                dst_row = pl.multiple_of(pos_ref[0, 0, s * tm + t] * ROWS_PER_TOKEN, ROWS_PER_TOKEN)
                pltpu.make_async_copy(src, xs_hbm.at[pl.ds(dst_row, ROWS_PER_TOKEN), :],
                                      sem).start(priority=s)
        return carry
    lax.fori_loop(0, tm // DMA_UNROLL, body, 0)
    for _ in range(2):
        pltpu.make_async_copy(h2_ref, xs_hbm.at[pl.ds(0, tm * ROWS_PER_TOKEN), :], sem).wait()

    @pl.when(pl.program_id(0) == pl.num_programs(0) - 1)
    def _():
        unused_tiles(lambda cp: cp.wait())


def _dispatch(ztile, na, pos3, h2r, n_tiles):
    tm = pos3.shape[2] // 2
    n_steps = pos3.shape[0]
    grid_spec = pltpu.PrefetchScalarGridSpec(
        num_scalar_prefetch=2,
        grid=(n_steps,),
        in_specs=[
            pl.BlockSpec((1, 1, 2 * tm), lambda i, z, n: (i, 0, 0), memory_space=pltpu.SMEM),
            pl.BlockSpec((tm * ROWS_PER_TOKEN, LANES), lambda i, z, n: (i, 0)),
        ],
        out_specs=pl.BlockSpec(memory_space=pl.ANY),
        scratch_shapes=[
            pltpu.VMEM((MOE_TILE * ROWS_PER_TOKEN, LANES), F32),
            pltpu.SemaphoreType.DMA(()),
            pltpu.SemaphoreType.DMA(()),
        ],
    )
    return pl.pallas_call(
        functools.partial(_dispatch_kernel, n_tiles),
        grid_spec=grid_spec,
        out_shape=jax.ShapeDtypeStruct((n_tiles * MOE_TILE * ROWS_PER_TOKEN, LANES), F32),
        compiler_params=pltpu.CompilerParams(
            dimension_semantics=("arbitrary",), vmem_limit_bytes=VMEM_LIMIT),
        name="dispatch",
    )(ztile, na, pos3, h2r)


def _moe_kernel(te_ref, na_ref, x_hbm, wg_ref, wu_ref, wd_ref, y_ref, xbuf, wgu_s, wd_s, sem):
    i = pl.program_id(0)
    n_active = na_ref[0]
    active = i < n_active
    tile_rows = MOE_TILE * ROWS_PER_TOKEN

    def fetch(tile):
        slot = tile % X_SLOTS
        row = pl.multiple_of(tile * tile_rows, tile_rows)
        return pltpu.make_async_copy(x_hbm.at[pl.ds(row, tile_rows), :], xbuf.at[slot], sem.at[slot])

    @pl.when(i == 0)
    def _():
        for k in range(X_SLOTS - 1):
            @pl.when(k < n_active)
            def _():
                fetch(k).start()

    @pl.when(i + X_SLOTS - 1 < n_active)
    def _():
        fetch(i + X_SLOTS - 1).start()

    new_expert = jnp.logical_or(i == 0, te_ref[i] != te_ref[jnp.maximum(i - 1, 0)])

    @pl.when(jnp.logical_and(active, new_expert))
    def _():
        wgu_s[:, 0:D_EXPERT] = wg_ref[0].astype(BF16)
        wgu_s[:, D_EXPERT:] = wu_ref[0].astype(BF16)
        wd_s[...] = wd_ref[0].astype(BF16)

    @pl.when(active)
    def _():
        fetch(i).wait()
        x = jnp.concatenate([col.astype(BF16) for col in _load_rows(xbuf.at[i % X_SLOTS], MOE_TILE)], axis=1)
        gu = _dot(x, wgu_s[...])
        g = gu[:, 0:D_EXPERT]
        a = (g * _sigmoid(g)) * gu[:, D_EXPERT:]
        y = _dot(a.astype(BF16), wd_s[...])
        _store_rows(y_ref, y, MOE_TILE)

    @pl.when(jnp.logical_not(active))
    def _():
        y_ref[...] = jnp.zeros(y_ref.shape, y_ref.dtype)


def _moe(te, na, xs, wg, wu, wd):
    nt = te.shape[0]
    tile_rows = MOE_TILE * ROWS_PER_TOKEN
    wspec = lambda shape: pl.BlockSpec((1,) + shape, lambda i, te_r, na_r: (te_r[i], 0, 0))
    grid_spec = pltpu.PrefetchScalarGridSpec(
        num_scalar_prefetch=2,
        grid=(nt,),
        in_specs=[
            pl.BlockSpec(memory_space=pl.ANY),
            wspec((D_MODEL, D_EXPERT)),
            wspec((D_MODEL, D_EXPERT)),
            wspec((D_EXPERT, D_MODEL)),
        ],
        out_specs=pl.BlockSpec((tile_rows, LANES), lambda i, te_r, na_r: (i, 0)),
        scratch_shapes=[
            pltpu.VMEM((X_SLOTS, tile_rows, LANES), F32),
            pltpu.VMEM((D_MODEL, 2 * D_EXPERT), BF16),
            pltpu.VMEM((D_EXPERT, D_MODEL), BF16),
            pltpu.SemaphoreType.DMA((X_SLOTS,)),
        ],
    )
    return pl.pallas_call(
        _moe_kernel,
        grid_spec=grid_spec,
        out_shape=jax.ShapeDtypeStruct(xs.shape, xs.dtype),
        compiler_params=pltpu.CompilerParams(
            dimension_semantics=("arbitrary",), vmem_limit_bytes=VMEM_LIMIT),
        name="moe",
    )(te, na, xs, wg, wu, wd)


def _comb_kernel(pos0_ref, posn_ref, x1_ref, cw_ref, mod_ref, y_hbm, o_ref, ybuf, sem):
    tm = x1_ref.shape[0]
    i = pl.program_id(0)
    slot = i % 2

    def gather(pos_ref, dst_slot):
        def body(c, carry):
            for k in range(DMA_UNROLL):
                r = c * DMA_UNROLL + k
                src_row = pl.multiple_of(pos_ref[0, 0, r] * ROWS_PER_TOKEN, ROWS_PER_TOKEN)
                dst_row = pl.multiple_of(r * ROWS_PER_TOKEN, ROWS_PER_TOKEN)
                pltpu.make_async_copy(y_hbm.at[pl.ds(src_row, ROWS_PER_TOKEN), :],
                                      ybuf.at[dst_slot, pl.ds(dst_row, ROWS_PER_TOKEN), :],
                                      sem.at[dst_slot]).start(priority=k % 2)
            return carry
        lax.fori_loop(0, 2 * tm // DMA_UNROLL, body, 0)

    @pl.when(i == 0)
    def _():
        gather(pos0_ref, 0)

    @pl.when(i + 1 < pl.num_programs(0))
    def _():
        gather(posn_ref, 1 - slot)

    pltpu.make_async_copy(y_hbm.at[pl.ds(0, 2 * tm * ROWS_PER_TOKEN), :], ybuf.at[slot],
                          sem.at[slot]).wait()
    wrow = lax.broadcasted_iota(jnp.int32, (LANES, tm), 0)
    cwt = jnp.where(wrow == 0, cw_ref[0:1, :], jnp.where(wrow == 1, cw_ref[1:2, :], 0.0)).T
    c0 = cwt[:, 0:1]
    c1 = cwt[:, 1:2]
    y0 = _load_rows(ybuf.at[slot], tm)
    y1 = _load_rows(ybuf.at[slot], tm, offset=tm * ROWS_PER_TOKEN)
    for j in range(ROWS_PER_TOKEN):
        cols = slice(j * LANES, (j + 1) * LANES)
        gate2 = mod_ref[0, :, 5 * D_MODEL + j * LANES:5 * D_MODEL + (j + 1) * LANES]
        o_ref[:, cols] = x1_ref[:, cols] + gate2 * (c0 * y0[j] + c1 * y1[j])


def _comb(x1, yr, pos3, cwt, mod3, seq):
    t = x1.shape[0]
    tm = ROW_TILE
    tiles_per_seq = seq // tm
    n_steps = t // tm
    row = lambda i: (i, 0)
    smem_blk = lambda f: pl.BlockSpec((1, 1, 2 * tm), f, memory_space=pltpu.SMEM)
    return pl.pallas_call(
        _comb_kernel,
        grid=(n_steps,),
        in_specs=[
            smem_blk(lambda i: (0, 0, 0)),
            smem_blk(lambda i: (jnp.minimum(i + 1, n_steps - 1), 0, 0)),
            pl.BlockSpec((tm, D_MODEL), row),
            pl.BlockSpec((2, tm), lambda i: (0, i)),
            pl.BlockSpec((1, 1, 6 * D_MODEL), lambda i: (i // tiles_per_seq, 0, 0)),
            pl.BlockSpec(memory_space=pl.ANY),
        ],
        out_specs=pl.BlockSpec((tm, D_MODEL), row),
        out_shape=jax.ShapeDtypeStruct((t, D_MODEL), F32),
        scratch_shapes=[
            pltpu.VMEM((2, 2 * tm * ROWS_PER_TOKEN, LANES), F32),
            pltpu.SemaphoreType.DMA((2,)),
        ],
        compiler_params=pltpu.CompilerParams(
            dimension_semantics=("arbitrary",), vmem_limit_bytes=VMEM_LIMIT),
        name="comb",
    )(pos3, pos3, x1, cwt, mod3, yr)


def _route_plan(eid, rank, cnt, n_tok):
    n_tiles = (2 * n_tok) // MOE_TILE + N_EXPERTS
    experts = jnp.arange(N_EXPERTS, dtype=jnp.int32)
    counts = cnt.reshape(N_EXPERTS).astype(jnp.int32)
    ntile = (counts + MOE_TILE - 1) // MOE_TILE
    tend = jnp.cumsum(ntile)
    tstart = tend - ntile
    n_active = tend[-1]
    tj = jnp.arange(n_tiles, dtype=jnp.int32)
    te_raw = jnp.minimum(jnp.sum((tj[:, None] >= tend[None, :]).astype(jnp.int32), axis=1),
                         N_EXPERTS - 1)
    te_last = jnp.sum(jnp.where(tj == n_active - 1, te_raw, 0))
    te = jnp.where(tj < n_active, te_raw, te_last).astype(jnp.int32)
    first_row = jnp.sum(jnp.where(eid[:, :, None] == experts[None, None, :],
                                  (tstart * MOE_TILE)[None, None, :], 0), axis=-1)
    pos = (first_row + rank).astype(jnp.int32)
    ztile = jnp.maximum(tend - 1, 0).astype(jnp.int32)
    pos3 = pos.reshape(2, n_tok // ROW_TILE, ROW_TILE).transpose(1, 0, 2).reshape(
        n_tok // ROW_TILE, 1, 2 * ROW_TILE)
    return te, n_active.reshape(1).astype(jnp.int32), ztile, pos3, n_tiles


def _rotate_half_cols(w):
    half = QK_ROPE // 2
    return jnp.concatenate([w[..., half:], w[..., :half]], axis=-1)


def kernel(x, c, positions, w_ada, b_ada, norm1_g, w_in, conv_w, q_a_norm_g, w_q_b, kv_a_norm_g, w_kv_b, q_norm_g, k_norm_g, w_o, norm2_g, w_router_group, b_router_group, w_router_expert, b_router_expert, w_exp_gate, w_exp_up, w_exp_down):
    nb, seq, d = x.shape
    depth = w_ada.shape[0]
    n_tok = nb * seq
    assert d == D_MODEL and seq % ROW_TILE == 0 and seq % Q_TILE == 0 and Q_TILE % CHUNK == 0
    assert seq % OPROJ_TILE == 0
    assert (2 * n_tok) % MOE_TILE == 0

    inv = ROPE_BASE ** (-jnp.arange(0, QK_ROPE, 2, dtype=F32) / QK_ROPE)
    ang = inv[None, :, None] * positions.astype(F32)[:, None, :]
    cos, sin = jnp.cos(ang), jnp.sin(ang)

    x2 = x.reshape(n_tok, d)
    for l in range(depth):
        w_t = w_in[l].T.astype(BF16)
        wq3 = w_q_b[l].reshape(Q_LORA, N_HEADS, QK_HEAD)
        wq = jnp.concatenate([wq3, _rotate_half_cols(wq3[..., QK_NOPE:])], axis=-1)
        wqt = wq.transpose(1, 2, 0).astype(BF16)
        wkvt = w_kv_b[l].reshape(KV_LORA, N_HEADS, QK_NOPE + V_HEAD).transpose(1, 2, 0).astype(BF16)
        gq = jnp.concatenate([q_norm_g[l], _rotate_half_cols(q_norm_g[l][QK_NOPE:])]).reshape(-1, 1)
        gk = jnp.concatenate([k_norm_g[l], _rotate_half_cols(k_norm_g[l][QK_NOPE:])]).reshape(-1, 1)
        wr_t = jnp.concatenate(
            [w_router_expert[l].T, w_router_group[l].T,
             jnp.zeros((ROUTER_ROWS - N_EXPERTS - N_GROUPS, d), F32)], axis=0).astype(BF16)
        br = jnp.concatenate(
            [b_router_expert[l], b_router_group[l],
             jnp.zeros((ROUTER_ROWS - N_EXPERTS - N_GROUPS,), F32)]).reshape(ROUTER_ROWS, 1)

        mod3 = _ada(c, w_ada[l], b_ada[l]).reshape(nb, 1, 6 * d)
        conv_p, gm, qn, kvn, krt = _inproj(
            x2, mod3, norm1_g[l].reshape(1, d), w_t, conv_w[l],
            q_a_norm_g[l].reshape(1, -1), kv_a_norm_g[l].reshape(1, -1), seq)
        merged = _attn(qn, kvn, krt, cos, sin, wqt, wkvt, gq, gk, conv_p, gm, nb, seq)
        x1, h2r, eid, rank, cnt, cwt = _oproj(merged, x2, mod3, w_o[l].astype(BF16),
                                              norm2_g[l].reshape(1, d), wr_t, br, seq)
        te, na, ztile, pos3, n_tiles = _route_plan(eid, rank, cnt, n_tok)
        xs = _dispatch(ztile, na, pos3, h2r, n_tiles)
        yr = _moe(te, na, xs,
                  w_exp_gate[l].reshape(N_EXPERTS, d, D_EXPERT),
                  w_exp_up[l].reshape(N_EXPERTS, d, D_EXPERT),
                  w_exp_down[l].reshape(N_EXPERTS, D_EXPERT, d))
        x2 = _comb(x1, yr, pos3, cwt, mod3, seq)
    return x2.reshape(nb, seq, d)
```

```python
import functools
import math

import jax
import jax.numpy as jnp
from jax import lax
from jax.experimental import pallas as pl
from jax.experimental.pallas import tpu as pltpu

F32 = jnp.float32
BF16 = jnp.bfloat16

D_MODEL = 1024
N_HEADS = 8
QK_NOPE = 128
QK_ROPE = 64
QK_HEAD = QK_NOPE + QK_ROPE
V_HEAD = 128
Q_LORA = 384
KV_LORA = 256
CHUNK = 64
EPS = 1e-6
ROPE_BASE = 10000.0
N_GROUPS = 4
EXPERTS_PER_GROUP = 8
N_EXPERTS = N_GROUPS * EXPERTS_PER_GROUP
D_EXPERT = 256
CONV_K = 3

LANES = 128
SUBLANES = 8
VMEM_LIMIT = 56 * 1024 * 1024

ROW_TILE = 512
Q_TILE = 512
INPROJ_SUB = 2
OPROJ_TILE = 1024
OPROJ_SLOTS = 3
HEADS_PER_STEP = 2
MOE_TILE = 512
X_SLOTS = 4
ADA_COLS = 1536
ROWS_PER_TOKEN = D_MODEL // LANES
DMA_UNROLL = 8
ROUTER_ROWS = 40


def _sigmoid(v):
    return 1.0 / (1.0 + jnp.exp(-v))


def _dot(a, b):
    return jnp.dot(a, b, preferred_element_type=F32)


def _store_rows(ref, val, n):
    for j in range(ROWS_PER_TOKEN):
        ref[pl.ds(j, n, stride=ROWS_PER_TOKEN), :] = val[:, j * LANES:(j + 1) * LANES]


def _load_rows(ref, n, offset=0):
    return [ref[pl.ds(offset + j, n, stride=ROWS_PER_TOKEN), :] for j in range(ROWS_PER_TOKEN)]


def _dot_nt(a, b):
    return lax.dot_general(a, b, (((1,), (1,)), ((), ())), preferred_element_type=F32)


def _ada_kernel(c_ref, w_ref, b_ref, o_ref):
    c = c_ref[...]
    act = (c * _sigmoid(c)).astype(BF16)
    o_ref[...] = _dot(act, w_ref[...].astype(BF16)) + b_ref[...]


def _ada(c, w_ada, b_ada):
    nb, d = c.shape
    n = w_ada.shape[1]
    return pl.pallas_call(
        _ada_kernel,
        grid=(n // ADA_COLS,),
        in_specs=[
            pl.BlockSpec((nb, d), lambda j: (0, 0)),
            pl.BlockSpec((d, ADA_COLS), lambda j: (0, j)),
            pl.BlockSpec((1, ADA_COLS), lambda j: (0, j)),
        ],
        out_specs=pl.BlockSpec((nb, ADA_COLS), lambda j: (0, j)),
        out_shape=jax.ShapeDtypeStruct((nb, n), F32),
        compiler_params=pltpu.CompilerParams(
            dimension_semantics=("arbitrary",), vmem_limit_bytes=VMEM_LIMIT),
        name="ada",
    )(c, w_ada, b_ada.reshape(1, n))


O_LAT = 3 * D_MODEL
O_KR = O_LAT + Q_LORA + KV_LORA
O_GATE = O_KR + QK_ROPE


def _inproj_kernel(tiles_per_seq, x_ref, mod_ref, g1_ref, w_ref, cw_ref, gq_ref, gkv_ref,
                   conv_ref, gm_ref, qn_ref, kvn_ref, krt_ref, ubuf):
    tm = x_ref.shape[0] // INPROJ_SUB

    @pl.when(pl.program_id(0) % tiles_per_seq == 0)
    def _():
        ubuf[0:SUBLANES, :] = jnp.zeros((SUBLANES, D_MODEL), F32)

    shift = mod_ref[0, :, 0:D_MODEL]
    scale = mod_ref[0, :, D_MODEL:2 * D_MODEL]
    for sub in range(INPROJ_SUB):
        rows = slice(sub * tm, (sub + 1) * tm)
        x = x_ref[rows, :]
        xn = x * lax.rsqrt(jnp.mean(x * x, axis=-1, keepdims=True) + EPS) * g1_ref[...]
        h = (xn * (1.0 + scale) + shift).astype(BF16)

        def proj(lo, width):
            return _dot_nt(h, w_ref[lo:lo + width, :])

        u = proj(2 * D_MODEL, D_MODEL) * proj(0, D_MODEL)
        ubuf[SUBLANES:SUBLANES + tm, :] = u
        conv = (ubuf[SUBLANES - 2:SUBLANES - 2 + tm, :] * cw_ref[0:1, :]
                + ubuf[SUBLANES - 1:SUBLANES - 1 + tm, :] * cw_ref[1:2, :]
                + u * cw_ref[2:3, :])
        ubuf[0:SUBLANES, :] = ubuf[tm:tm + SUBLANES, :]
        y_conv = proj(D_MODEL, D_MODEL) * conv
        conv_ref[rows, :] = (_sigmoid(proj(O_GATE, D_MODEL)) * y_conv).astype(BF16)
        gm_ref[rows, :] = _sigmoid(proj(O_GATE + D_MODEL, D_MODEL)).astype(BF16)

        ql = proj(O_LAT, Q_LORA)
        qn_ref[rows, :] = (ql * lax.rsqrt(jnp.mean(ql * ql, axis=-1, keepdims=True) + EPS)
                           * gq_ref[...]).astype(BF16)
        kl = proj(O_LAT + Q_LORA, KV_LORA)
        kvn_ref[rows, :] = (kl * lax.rsqrt(jnp.mean(kl * kl, axis=-1, keepdims=True) + EPS)
                            * gkv_ref[...]).astype(BF16)
        kr = _dot_nt(w_ref[O_KR:O_GATE, :], h)
        half = QK_ROPE // 2
        krt_ref[0, 0:QK_ROPE, rows] = kr
        krt_ref[0, QK_ROPE:QK_ROPE + half, rows] = kr[half:]
        krt_ref[0, QK_ROPE + half:, rows] = kr[:half]


def _inproj(x2, mod3, g1, w_t, conv_w, gq, gkv, seq):
    t = x2.shape[0]
    nb = t // seq
    tm = ROW_TILE * INPROJ_SUB
    tiles_per_seq = seq // tm
    row = lambda i: (i, 0)
    const = lambda i: (0, 0)
    return pl.pallas_call(
        functools.partial(_inproj_kernel, tiles_per_seq),
        grid=(t // tm,),
        in_specs=[
            pl.BlockSpec((tm, D_MODEL), row),
            pl.BlockSpec((1, 1, 6 * D_MODEL), lambda i: (i // tiles_per_seq, 0, 0)),
            pl.BlockSpec((1, D_MODEL), const),
            pl.BlockSpec(w_t.shape, const),
            pl.BlockSpec((CONV_K, D_MODEL), const),
            pl.BlockSpec((1, Q_LORA), const),
            pl.BlockSpec((1, KV_LORA), const),
        ],
        out_specs=[
            pl.BlockSpec((tm, D_MODEL), row),
            pl.BlockSpec((tm, D_MODEL), row),
            pl.BlockSpec((tm, Q_LORA), row),
            pl.BlockSpec((tm, KV_LORA), row),
            pl.BlockSpec((1, 2 * QK_ROPE, tm),
                         lambda i: (i // tiles_per_seq, 0, i % tiles_per_seq)),
        ],
        out_shape=[
            jax.ShapeDtypeStruct((t, D_MODEL), BF16),
            jax.ShapeDtypeStruct((t, D_MODEL), BF16),
            jax.ShapeDtypeStruct((t, Q_LORA), BF16),
            jax.ShapeDtypeStruct((t, KV_LORA), BF16),
            jax.ShapeDtypeStruct((nb, 2 * QK_ROPE, seq), F32),
        ],
        scratch_shapes=[pltpu.VMEM((ROW_TILE + SUBLANES, D_MODEL), F32)],
        compiler_params=pltpu.CompilerParams(
            dimension_semantics=("arbitrary",), vmem_limit_bytes=VMEM_LIMIT),
        name="inproj",
    )(x2, mod3, g1, w_t, conv_w, gq, gkv)


def _attn_kernel(qn_ref, kvn_ref, krt_ref, cos_ref, sin_ref, wqt_ref, wkvt_ref, gq_ref, gk_ref, conv_ref, gm_ref,
                 o_ref, *scratch):
    per_head = len(scratch) // HEADS_PER_STEP
    heads = [_attn_head(hh, qn_ref, kvn_ref, krt_ref, cos_ref, sin_ref, wqt_ref, wkvt_ref, gq_ref, gk_ref,
                        conv_ref, gm_ref, o_ref, *scratch[hh * per_head:(hh + 1) * per_head])
             for hh in range(HEADS_PER_STEP)]
    nq = qn_ref.shape[0] // Q_TILE
    for scores, _ in heads:
        scores(0)
    for i in range(nq):
        for scores, finish in heads:
            if i + 1 < nq:
                scores(i + 1)
            finish(i)


def _attn_head(hh, qn_ref, kvn_ref, krt_ref, cos_ref, sin_ref, wqt_ref, wkvt_ref, gq_ref, gk_ref,
               conv_ref, gm_ref, o_ref, qt_s, k_s, vt_s, s_buf0, s_buf1):
    seq = qn_ref.shape[0]
    cos_t = jnp.concatenate([cos_ref[0], cos_ref[0]], axis=0)
    sin_t = jnp.concatenate([-sin_ref[0], sin_ref[0]], axis=0)

    def normed_rope(nope, r, rr, g, extra_scale):
        ss = jnp.sum(nope * nope, axis=0, keepdims=True) + jnp.sum(r * r, axis=0, keepdims=True)
        scale = lax.rsqrt(ss * (1.0 / QK_HEAD) + EPS) * extra_scale
        rope = r * g[QK_NOPE:QK_HEAD] * cos_t + rr * g[QK_HEAD:] * sin_t
        return (nope * g[0:QK_NOPE] * scale).astype(BF16), (rope * scale).astype(BF16)

    qt = _dot_nt(wqt_ref[hh], qn_ref[...])
    q_n, q_r = normed_rope(qt[0:QK_NOPE], qt[QK_NOPE:QK_HEAD], qt[QK_HEAD:], gq_ref[...],
                           QK_HEAD ** -0.5 * math.log2(math.e))
    qt_s[0:QK_NOPE, :] = q_n
    qt_s[QK_NOPE:QK_HEAD, :] = q_r
    qt_s[QK_HEAD:, :] = jnp.zeros((QK_ROPE, seq), BF16)

    kvt = _dot_nt(wkvt_ref[hh], kvn_ref[...])
    krt = krt_ref[0]
    k_n, k_r = normed_rope(kvt[0:QK_NOPE], krt[0:QK_ROPE], krt[QK_ROPE:], gk_ref[...], 1.0)
    kt = jnp.concatenate([k_n, k_r, jnp.zeros((QK_ROPE, seq), BF16)], axis=0)
    k_s[...] = kt.T
    vt_s[0:V_HEAD, :] = kvt[QK_NOPE:].astype(BF16)
    vt_s[V_HEAD:, :] = jnp.ones((vt_s.shape[0] - V_HEAD, seq), BF16)

    tq = Q_TILE
    hq = tq // 2
    kchunk = lax.broadcasted_iota(jnp.int32, (hq, tq), 0) // CHUNK
    qchunk = lax.broadcasted_iota(jnp.int32, (hq, tq), 1) // CHUNK
    head_ok = kchunk <= qchunk
    tail_ok = head_ok[:, 0:hq]
    neg = jnp.finfo(F32).min

    def scores(i):
        q0 = i * tq
        sb = s_buf0 if i % 2 == 0 else s_buf1
        q = qt_s[:, q0:q0 + tq]
        if i > 0:
            sb[0:q0, :] = _dot(k_s[0:q0, :], q)
        sb[q0:q0 + hq, :] = jnp.where(head_ok, _dot(k_s[q0:q0 + hq, :], q), neg)
        sb[q0 + hq:q0 + tq, hq:] = jnp.where(tail_ok, _dot(k_s[q0 + hq:q0 + tq, :], q[:, hq:]), neg)

    def finish(i):
        q0 = i * tq
        kmain = q0 + hq
        sb = s_buf0 if i % 2 == 0 else s_buf1
        tail = sb[kmain:kmain + hq, hq:]
        m_main = jnp.max(sb[0:kmain, :], axis=0, keepdims=True)
        m_tail = jnp.maximum(m_main[:, hq:], jnp.max(tail, axis=0, keepdims=True))
        m = jnp.concatenate([m_main[:, 0:hq], m_tail], axis=1)
        acc = _dot(vt_s[:, 0:kmain], jnp.exp2(sb[0:kmain, :] - m).astype(BF16))
        acc_tail = _dot(vt_s[:, kmain:kmain + hq], jnp.exp2(tail - m_tail).astype(BF16))
        acc = jnp.concatenate([acc[:, 0:hq], acc[:, hq:] + acc_tail], axis=1)
        o_t = acc[0:V_HEAD] / acc[V_HEAD:V_HEAD + 1]
        rows, cols = slice(q0, q0 + tq), slice(hh * V_HEAD, (hh + 1) * V_HEAD)
        y_mla = o_t.T.astype(BF16).astype(F32)
        merged = conv_ref[rows, cols].astype(F32) + gm_ref[rows, cols].astype(F32) * y_mla
        o_ref[rows, cols] = merged.astype(o_ref.dtype)

    return scores, finish


def _attn(qn, kvn, krt, cos, sin, wqt, wkvt, gq, gk, conv_p, gm, nb, seq):
    t = qn.shape[0]
    per_b = lambda b, h: (b, 0)
    per_b3 = lambda b, h: (b, 0, 0)
    per_h = lambda b, h: (h, 0, 0)
    const = lambda b, h: (0, 0)
    qk_rows = QK_NOPE + 2 * QK_ROPE
    return pl.pallas_call(
        _attn_kernel,
        grid=(nb, N_HEADS // HEADS_PER_STEP),
        in_specs=[
            pl.BlockSpec((seq, Q_LORA), per_b),
            pl.BlockSpec((seq, KV_LORA), per_b),
            pl.BlockSpec((1, 2 * QK_ROPE, seq), per_b3),
            pl.BlockSpec((1, QK_ROPE // 2, seq), per_b3),
            pl.BlockSpec((1, QK_ROPE // 2, seq), per_b3),
            pl.BlockSpec((HEADS_PER_STEP, qk_rows, Q_LORA), per_h),
            pl.BlockSpec((HEADS_PER_STEP, QK_NOPE + V_HEAD, KV_LORA), per_h),
            pl.BlockSpec((qk_rows, 1), const),
            pl.BlockSpec((qk_rows, 1), const),
            pl.BlockSpec((seq, HEADS_PER_STEP * V_HEAD), lambda b, h: (b, h)),
            pl.BlockSpec((seq, HEADS_PER_STEP * V_HEAD), lambda b, h: (b, h)),
        ],
        out_specs=pl.BlockSpec((seq, HEADS_PER_STEP * V_HEAD), lambda b, h: (b, h)),
        out_shape=jax.ShapeDtypeStruct((t, N_HEADS * V_HEAD), BF16),
        scratch_shapes=[
            pltpu.VMEM((qk_rows, seq), BF16),
            pltpu.VMEM((seq, qk_rows), BF16),
            pltpu.VMEM((V_HEAD + 2 * SUBLANES, seq), BF16),
            pltpu.VMEM((seq, Q_TILE), F32),
            pltpu.VMEM((seq, Q_TILE), F32),
        ] * HEADS_PER_STEP,
        compiler_params=pltpu.CompilerParams(
            dimension_semantics=("arbitrary", "arbitrary"), vmem_limit_bytes=VMEM_LIMIT),
        name="attn",
    )(qn, kvn, krt, cos, sin, wqt, wkvt, gq, gk, conv_p, gm)


def _oproj_kernel(n_steps, merged_hbm, x_hbm, mod_ref, wo_ref, g2_ref, wr_ref, br_ref,
                  x1_ref, h2_ref, eid_ref, rank_ref, cnt_ref, cw_ref, base, mbuf, xbuf, sem):
    tm = x1_ref.shape[0]
    i = pl.program_id(0)
    ahead = OPROJ_SLOTS - 1

    def copies(step, slot):
        r0 = step * tm if isinstance(step, int) else pl.multiple_of(step * tm, tm)
        return (pltpu.make_async_copy(merged_hbm.at[pl.ds(r0, tm), :], mbuf.at[slot], sem.at[0, slot]),
                pltpu.make_async_copy(x_hbm.at[pl.ds(r0, tm), :], xbuf.at[slot], sem.at[1, slot]))

    def fetch(step):
        for cp in copies(step, step % OPROJ_SLOTS):
            cp.start()

    @pl.when(i == 0)
    def _():
        base[...] = jnp.zeros(base.shape, F32)
        for s in range(min(ahead, n_steps)):
            fetch(s)

    @pl.when(i + ahead < n_steps)
    def _():
        fetch(i + ahead)

    slot = i % OPROJ_SLOTS
    for cp in copies(i, slot):
        cp.wait()
    merged_ref = mbuf.at[slot]
    x_ref = xbuf.at[slot]

    att = _dot(merged_ref[...], wo_ref[...])
    gate1 = mod_ref[0, :, 2 * D_MODEL:3 * D_MODEL]
    shift2 = mod_ref[0, :, 3 * D_MODEL:4 * D_MODEL]
    scale2 = mod_ref[0, :, 4 * D_MODEL:5 * D_MODEL]
    x1 = x_ref[...] + gate1 * att
    x1_ref[...] = x1
    xn = x1 * lax.rsqrt(jnp.mean(x1 * x1, axis=-1, keepdims=True) + EPS) * g2_ref[...]
    h2 = xn * (1.0 + scale2) + shift2
    _store_rows(h2_ref, h2, tm)

    lt = _dot_nt(wr_ref[...], h2.astype(BF16)) + br_ref[...]
    gl = [lt[N_EXPERTS + r:N_EXPERTS + r + 1, :] for r in range(N_GROUPS)]
    gmax = jnp.maximum(jnp.maximum(gl[0], gl[1]), jnp.maximum(gl[2], gl[3]))
    gidx = jnp.full(gmax.shape, N_GROUPS - 1, jnp.int32)
    for r in range(N_GROUPS - 2, -1, -1):
        gidx = jnp.where(gl[r] == gmax, r, gidx)
    gsum = jnp.exp(gl[0] - gmax)
    for r in range(1, N_GROUPS):
        gsum = gsum + jnp.exp(gl[r] - gmax)
    p_group = 1.0 / gsum
    es = lt[(N_GROUPS - 1) * EXPERTS_PER_GROUP:N_GROUPS * EXPERTS_PER_GROUP, :]
    for r in range(N_GROUPS - 2, -1, -1):
        es = jnp.where(gidx == r, lt[r * EXPERTS_PER_GROUP:(r + 1) * EXPERTS_PER_GROUP, :], es)
    row = lax.broadcasted_iota(jnp.int32, es.shape, 0)
    m1 = jnp.max(es, axis=0, keepdims=True)
    i1 = jnp.min(jnp.where(es == m1, row, EXPERTS_PER_GROUP), axis=0, keepdims=True)
    es2 = jnp.where(row == i1, -jnp.inf, es)
    m2 = jnp.max(es2, axis=0, keepdims=True)
    i2 = jnp.min(jnp.where(es2 == m2, row, EXPERTS_PER_GROUP), axis=0, keepdims=True)
    e2 = jnp.exp(m2 - m1)
    w1 = p_group / (1.0 + e2)
    w2 = w1 * e2
    eid0 = gidx * EXPERTS_PER_GROUP + i1
    eid1 = gidx * EXPERTS_PER_GROUP + i2
    eid_ref[0:1, :] = eid0
    eid_ref[1:2, :] = eid1
    erow = lax.broadcasted_iota(jnp.int32, (N_EXPERTS, tm), 0)
    oh0 = erow == eid0
    oh1 = erow == eid1
    both = jnp.where(oh0, 1.0, jnp.where(oh1, 1.0, 0.0))
    earlier = (lax.broadcasted_iota(jnp.int32, (tm, tm), 0)
               < lax.broadcasted_iota(jnp.int32, (tm, tm), 1))
    seen = base[...] + _dot(both.astype(BF16), jnp.where(earlier, 1.0, 0.0).astype(BF16))
    rank_ref[0:1, :] = jnp.sum(jnp.where(oh0, seen, 0.0), axis=0, keepdims=True).astype(jnp.int32)
    rank_ref[1:2, :] = jnp.sum(jnp.where(oh1, seen, 0.0), axis=0, keepdims=True).astype(jnp.int32)
    base[...] = base[...] + jnp.sum(both, axis=1, keepdims=True)
    cnt_ref[...] = base[...]
    cw_ref[0:1, :] = w1
    cw_ref[1:2, :] = w2


def _oproj(merged, x2, mod3, wo, g2, wr_t, br, seq):
    t = x2.shape[0]
    tm = OPROJ_TILE
    tiles_per_seq = seq // tm
    row = lambda i: (i, 0)
    const = lambda i: (0, 0)
    return pl.pallas_call(
        functools.partial(_oproj_kernel, t // tm),
        grid=(t // tm,),
        in_specs=[
            pl.BlockSpec(memory_space=pl.ANY),
            pl.BlockSpec(memory_space=pl.ANY),
            pl.BlockSpec((1, 1, 6 * D_MODEL), lambda i: (i // tiles_per_seq, 0, 0)),
            pl.BlockSpec((D_MODEL, D_MODEL), const),
            pl.BlockSpec((1, D_MODEL), const),
            pl.BlockSpec((ROUTER_ROWS, D_MODEL), const),
            pl.BlockSpec((ROUTER_ROWS, 1), const),
        ],
        out_specs=[
            pl.BlockSpec((tm, D_MODEL), row),
            pl.BlockSpec((tm * ROWS_PER_TOKEN, LANES), row),
            pl.BlockSpec((2, tm), lambda i: (0, i)),
            pl.BlockSpec((2, tm), lambda i: (0, i)),
            pl.BlockSpec((N_EXPERTS, 1), const),
            pl.BlockSpec((2, tm), lambda i: (0, i)),
        ],
        out_shape=[
            jax.ShapeDtypeStruct((t, D_MODEL), F32),
            jax.ShapeDtypeStruct((t * ROWS_PER_TOKEN, LANES), F32),
            jax.ShapeDtypeStruct((2, t), jnp.int32),
            jax.ShapeDtypeStruct((2, t), jnp.int32),
            jax.ShapeDtypeStruct((N_EXPERTS, 1), F32),
            jax.ShapeDtypeStruct((2, t), F32),
        ],
        scratch_shapes=[
            pltpu.VMEM((N_EXPERTS, 1), F32),
            pltpu.VMEM((OPROJ_SLOTS, tm, D_MODEL), BF16),
            pltpu.VMEM((OPROJ_SLOTS, tm, D_MODEL), F32),
            pltpu.SemaphoreType.DMA((2, OPROJ_SLOTS)),
        ],
        compiler_params=pltpu.CompilerParams(
            dimension_semantics=("arbitrary",), vmem_limit_bytes=VMEM_LIMIT),
        name="oproj",
    )(merged, x2, mod3, wo, g2, wr_t, br)


def _dispatch_kernel(n_tiles, ztile_ref, na_ref, pos_ref, h2_ref, xs_hbm, zbuf, sem, zsem):
    tm = h2_ref.shape[0] // ROWS_PER_TOKEN
    tile_rows = MOE_TILE * ROWS_PER_TOKEN

    def zero_tile(tile, on_sem):
        row = pl.multiple_of(tile * tile_rows, tile_rows)
        return pltpu.make_async_copy(zbuf, xs_hbm.at[pl.ds(row, tile_rows), :], on_sem)

    def unused_tiles(action):
        def body(k, carry):
            action(zero_tile(na_ref[0] + k, zsem))
            return carry
        lax.fori_loop(0, n_tiles - na_ref[0], body, 0)

    @pl.when(pl.program_id(0) == 0)
    def _():
        zbuf[...] = jnp.zeros(zbuf.shape, zbuf.dtype)
        for e in range(N_EXPERTS):
            zero_tile(ztile_ref[e], sem).start()
        for e in range(N_EXPERTS):
            zero_tile(ztile_ref[e], sem).wait()
        unused_tiles(lambda cp: cp.start())

    def body(c, carry):
        for k in range(DMA_UNROLL):
            t = c * DMA_UNROLL + k
            src = h2_ref.at[pl.ds(pl.multiple_of(t * ROWS_PER_TOKEN, ROWS_PER_TOKEN), ROWS_PER_TOKEN), :]
            for s in range(2):
                dst_row = pl.multiple_of(pos_ref[0, 0, s * tm + t] * ROWS_PER_TOKEN, ROWS_PER_TOKEN)
                pltpu.make_async_copy(src, xs_hbm.at[pl.ds(dst_row, ROWS_PER_TOKEN), :],
                                      sem).start(priority=s)
        return carry
    lax.fori_loop(0, tm // DMA_UNROLL, body, 0)
    for _ in range(2):
        pltpu.make_async_copy(h2_ref, xs_hbm.at[pl.ds(0, tm * ROWS_PER_TOKEN), :], sem).wait()

    @pl.when(pl.program_id(0) == pl.num_programs(0) - 1)
    def _():
        unused_tiles(lambda cp: cp.wait())


def _dispatch(ztile, na, pos3, h2r, n_tiles):
    tm = pos3.shape[2] // 2
    n_steps = pos3.shape[0]
    grid_spec = pltpu.PrefetchScalarGridSpec(
        num_scalar_prefetch=2,
        grid=(n_steps,),
        in_specs=[
            pl.BlockSpec((1, 1, 2 * tm), lambda i, z, n: (i, 0, 0), memory_space=pltpu.SMEM),
            pl.BlockSpec((tm * ROWS_PER_TOKEN, LANES), lambda i, z, n: (i, 0)),
        ],
        out_specs=pl.BlockSpec(memory_space=pl.ANY),
        scratch_shapes=[
            pltpu.VMEM((MOE_TILE * ROWS_PER_TOKEN, LANES), F32),
            pltpu.SemaphoreType.DMA(()),
            pltpu.SemaphoreType.DMA(()),
        ],
    )
    return pl.pallas_call(
        functools.partial(_dispatch_kernel, n_tiles),
        grid_spec=grid_spec,
        out_shape=jax.ShapeDtypeStruct((n_tiles * MOE_TILE * ROWS_PER_TOKEN, LANES), F32),
        compiler_params=pltpu.CompilerParams(
            dimension_semantics=("arbitrary",), vmem_limit_bytes=VMEM_LIMIT),
        name="dispatch",
    )(ztile, na, pos3, h2r)


def _moe_kernel(te_ref, na_ref, x_hbm, wg_ref, wu_ref, wd_ref, y_ref, xbuf, wgu_s, wd_s, sem):
    i = pl.program_id(0)
    n_active = na_ref[0]
    active = i < n_active
    tile_rows = MOE_TILE * ROWS_PER_TOKEN

    def fetch(tile):
        slot = tile % X_SLOTS
        row = pl.multiple_of(tile * tile_rows, tile_rows)
        return pltpu.make_async_copy(x_hbm.at[pl.ds(row, tile_rows), :], xbuf.at[slot], sem.at[slot])

    @pl.when(i == 0)
    def _():
        for k in range(X_SLOTS - 1):
            @pl.when(k < n_active)
            def _():
                fetch(k).start()

    @pl.when(i + X_SLOTS - 1 < n_active)
    def _():
        fetch(i + X_SLOTS - 1).start()

    new_expert = jnp.logical_or(i == 0, te_ref[i] != te_ref[jnp.maximum(i - 1, 0)])

    @pl.when(jnp.logical_and(active, new_expert))
    def _():
        wgu_s[:, 0:D_EXPERT] = wg_ref[0].astype(BF16)
        wgu_s[:, D_EXPERT:] = wu_ref[0].astype(BF16)
        wd_s[...] = wd_ref[0].astype(BF16)

    @pl.when(active)
    def _():
        fetch(i).wait()
        x = jnp.concatenate([col.astype(BF16) for col in _load_rows(xbuf.at[i % X_SLOTS], MOE_TILE)], axis=1)
        gu = _dot(x, wgu_s[...])
        g = gu[:, 0:D_EXPERT]
        a = (g * _sigmoid(g)) * gu[:, D_EXPERT:]
        y = _dot(a.astype(BF16), wd_s[...])
        _store_rows(y_ref, y, MOE_TILE)

    @pl.when(jnp.logical_not(active))
    def _():
        y_ref[...] = jnp.zeros(y_ref.shape, y_ref.dtype)


def _moe(te, na, xs, wg, wu, wd):
    nt = te.shape[0]
    tile_rows = MOE_TILE * ROWS_PER_TOKEN
    wspec = lambda shape: pl.BlockSpec((1,) + shape, lambda i, te_r, na_r: (te_r[i], 0, 0))
    grid_spec = pltpu.PrefetchScalarGridSpec(
        num_scalar_prefetch=2,
        grid=(nt,),
        in_specs=[
            pl.BlockSpec(memory_space=pl.ANY),
            wspec((D_MODEL, D_EXPERT)),
            wspec((D_MODEL, D_EXPERT)),
            wspec((D_EXPERT, D_MODEL)),
        ],
        out_specs=pl.BlockSpec((tile_rows, LANES), lambda i, te_r, na_r: (i, 0)),
        scratch_shapes=[
            pltpu.VMEM((X_SLOTS, tile_rows, LANES), F32),
            pltpu.VMEM((D_MODEL, 2 * D_EXPERT), BF16),
            pltpu.VMEM((D_EXPERT, D_MODEL), BF16),
            pltpu.SemaphoreType.DMA((X_SLOTS,)),
        ],
    )
    return pl.pallas_call(
        _moe_kernel,
        grid_spec=grid_spec,
        out_shape=jax.ShapeDtypeStruct(xs.shape, xs.dtype),
        compiler_params=pltpu.CompilerParams(
            dimension_semantics=("arbitrary",), vmem_limit_bytes=VMEM_LIMIT),
        name="moe",
    )(te, na, xs, wg, wu, wd)


def _comb_kernel(pos0_ref, posn_ref, x1_ref, cw_ref, mod_ref, y_hbm, o_ref, ybuf, sem):
    tm = x1_ref.shape[0]
    i = pl.program_id(0)
    slot = i % 2

    def gather(pos_ref, dst_slot):
        def body(c, carry):
            for k in range(DMA_UNROLL):
                r = c * DMA_UNROLL + k
                src_row = pl.multiple_of(pos_ref[0, 0, r] * ROWS_PER_TOKEN, ROWS_PER_TOKEN)
                dst_row = pl.multiple_of(r * ROWS_PER_TOKEN, ROWS_PER_TOKEN)
                pltpu.make_async_copy(y_hbm.at[pl.ds(src_row, ROWS_PER_TOKEN), :],
                                      ybuf.at[dst_slot, pl.ds(dst_row, ROWS_PER_TOKEN), :],
                                      sem.at[dst_slot]).start(priority=k % 2)
            return carry
        lax.fori_loop(0, 2 * tm // DMA_UNROLL, body, 0)

    @pl.when(i == 0)
    def _():
        gather(pos0_ref, 0)

    @pl.when(i + 1 < pl.num_programs(0))
    def _():
        gather(posn_ref, 1 - slot)

    pltpu.make_async_copy(y_hbm.at[pl.ds(0, 2 * tm * ROWS_PER_TOKEN), :], ybuf.at[slot],
                          sem.at[slot]).wait()
    wrow = lax.broadcasted_iota(jnp.int32, (LANES, tm), 0)
    cwt = jnp.where(wrow == 0, cw_ref[0:1, :], jnp.where(wrow == 1, cw_ref[1:2, :], 0.0)).T
    c0 = cwt[:, 0:1]
    c1 = cwt[:, 1:2]
    y0 = _load_rows(ybuf.at[slot], tm)
    y1 = _load_rows(ybuf.at[slot], tm, offset=tm * ROWS_PER_TOKEN)
    for j in range(ROWS_PER_TOKEN):
        cols = slice(j * LANES, (j + 1) * LANES)
        gate2 = mod_ref[0, :, 5 * D_MODEL + j * LANES:5 * D_MODEL + (j + 1) * LANES]
        o_ref[:, cols] = x1_ref[:, cols] + gate2 * (c0 * y0[j] + c1 * y1[j])


def _comb(x1, yr, pos3, cwt, mod3, seq):
    t = x1.shape[0]
    tm = ROW_TILE
    tiles_per_seq = seq // tm
    n_steps = t // tm
    row = lambda i: (i, 0)
    smem_blk = lambda f: pl.BlockSpec((1, 1, 2 * tm), f, memory_space=pltpu.SMEM)
    return pl.pallas_call(
        _comb_kernel,
        grid=(n_steps,),
        in_specs=[
            smem_blk(lambda i: (0, 0, 0)),
            smem_blk(lambda i: (jnp.minimum(i + 1, n_steps - 1), 0, 0)),
            pl.BlockSpec((tm, D_MODEL), row),
            pl.BlockSpec((2, tm), lambda i: (0, i)),
            pl.BlockSpec((1, 1, 6 * D_MODEL), lambda i: (i // tiles_per_seq, 0, 0)),
            pl.BlockSpec(memory_space=pl.ANY),
        ],
        out_specs=pl.BlockSpec((tm, D_MODEL), row),
        out_shape=jax.ShapeDtypeStruct((t, D_MODEL), F32),
        scratch_shapes=[
            pltpu.VMEM((2, 2 * tm * ROWS_PER_TOKEN, LANES), F32),
            pltpu.SemaphoreType.DMA((2,)),
        ],
        compiler_params=pltpu.CompilerParams(
            dimension_semantics=("arbitrary",), vmem_limit_bytes=VMEM_LIMIT),
        name="comb",
    )(pos3, pos3, x1, cwt, mod3, yr)


def _route_plan(eid, rank, cnt, n_tok):
    n_tiles = (2 * n_tok) // MOE_TILE + N_EXPERTS
    experts = jnp.arange(N_EXPERTS, dtype=jnp.int32)
    counts = cnt.reshape(N_EXPERTS).astype(jnp.int32)
    ntile = (counts + MOE_TILE - 1) // MOE_TILE
    tend = jnp.cumsum(ntile)
    tstart = tend - ntile
    n_active = tend[-1]
    tj = jnp.arange(n_tiles, dtype=jnp.int32)
    te_raw = jnp.minimum(jnp.sum((tj[:, None] >= tend[None, :]).astype(jnp.int32), axis=1),
                         N_EXPERTS - 1)
    te_last = jnp.sum(jnp.where(tj == n_active - 1, te_raw, 0))
    te = jnp.where(tj < n_active, te_raw, te_last).astype(jnp.int32)
    first_row = jnp.sum(jnp.where(eid[:, :, None] == experts[None, None, :],
                                  (tstart * MOE_TILE)[None, None, :], 0), axis=-1)
    pos = (first_row + rank).astype(jnp.int32)
    ztile = jnp.maximum(tend - 1, 0).astype(jnp.int32)
    pos3 = pos.reshape(2, n_tok // ROW_TILE, ROW_TILE).transpose(1, 0, 2).reshape(
        n_tok // ROW_TILE, 1, 2 * ROW_TILE)
    return te, n_active.reshape(1).astype(jnp.int32), ztile, pos3, n_tiles


def _rotate_half_cols(w):
    half = QK_ROPE // 2
    return jnp.concatenate([w[..., half:], w[..., :half]], axis=-1)


def kernel(x, c, positions, w_ada, b_ada, norm1_g, w_in, conv_w, q_a_norm_g, w_q_b, kv_a_norm_g, w_kv_b, q_norm_g, k_norm_g, w_o, norm2_g, w_router_group, b_router_group, w_router_expert, b_router_expert, w_exp_gate, w_exp_up, w_exp_down):
    nb, seq, d = x.shape
    depth = w_ada.shape[0]
    n_tok = nb * seq
    assert d == D_MODEL and seq % ROW_TILE == 0 and seq % Q_TILE == 0 and Q_TILE % CHUNK == 0
    assert seq % OPROJ_TILE == 0
    assert (2 * n_tok) % MOE_TILE == 0

    inv = ROPE_BASE ** (-jnp.arange(0, QK_ROPE, 2, dtype=F32) / QK_ROPE)
    ang = inv[None, :, None] * positions.astype(F32)[:, None, :]
    cos, sin = jnp.cos(ang), jnp.sin(ang)

    x2 = x.reshape(n_tok, d)
    for l in range(depth):
        w_t = w_in[l].T.astype(BF16)
        wq3 = w_q_b[l].reshape(Q_LORA, N_HEADS, QK_HEAD)
        wq = jnp.concatenate([wq3, _rotate_half_cols(wq3[..., QK_NOPE:])], axis=-1)
        wqt = wq.transpose(1, 2, 0).astype(BF16)
        wkvt = w_kv_b[l].reshape(KV_LORA, N_HEADS, QK_NOPE + V_HEAD).transpose(1, 2, 0).astype(BF16)
        gq = jnp.concatenate([q_norm_g[l], _rotate_half_cols(q_norm_g[l][QK_NOPE:])]).reshape(-1, 1)
        gk = jnp.concatenate([k_norm_g[l], _rotate_half_cols(k_norm_g[l][QK_NOPE:])]).reshape(-1, 1)
        wr_t = jnp.concatenate(
            [w_router_expert[l].T, w_router_group[l].T,
             jnp.zeros((ROUTER_ROWS - N_EXPERTS - N_GROUPS, d), F32)], axis=0).astype(BF16)
        br = jnp.concatenate(
            [b_router_expert[l], b_router_group[l],
             jnp.zeros((ROUTER_ROWS - N_EXPERTS - N_GROUPS,), F32)]).reshape(ROUTER_ROWS, 1)

        mod3 = _ada(c, w_ada[l], b_ada[l]).reshape(nb, 1, 6 * d)
        conv_p, gm, qn, kvn, krt = _inproj(
            x2, mod3, norm1_g[l].reshape(1, d), w_t, conv_w[l],
            q_a_norm_g[l].reshape(1, -1), kv_a_norm_g[l].reshape(1, -1), seq)
        merged = _attn(qn, kvn, krt, cos, sin, wqt, wkvt, gq, gk, conv_p, gm, nb, seq)
        x1, h2r, eid, rank, cnt, cwt = _oproj(merged, x2, mod3, w_o[l].astype(BF16),
                                              norm2_g[l].reshape(1, d), wr_t, br, seq)
        te, na, ztile, pos3, n_tiles = _route_plan(eid, rank, cnt, n_tok)
        xs = _dispatch(ztile, na, pos3, h2r, n_tiles)
        yr = _moe(te, na, xs,
                  w_exp_gate[l].reshape(N_EXPERTS, d, D_EXPERT),
                  w_exp_up[l].reshape(N_EXPERTS, d, D_EXPERT),
                  w_exp_down[l].reshape(N_EXPERTS, D_EXPERT, d))
        x2 = _comb(x1, yr, pos3, cwt, mod3, seq)
    return x2.reshape(nb, seq, d)
```
